```python
import math
import jax
import jax.numpy as jnp
from jax import lax
import numpy as np

D_MODEL = 1024
BATCH = 32
SEQ = 2048
DEPTH = 1

CHUNK = 64
Q_BLOCK = 128

POOL_DIM = D_MODEL // 2
POOL_WINDOWS = (2, 4, 8, 16)
POOL_GROUPS = len(POOL_WINDOWS)
POOL_GROUP_DIM = POOL_DIM // POOL_GROUPS
ATTN_HEADS = 4
ATTN_HEAD_DIM = 64
QK_DIM = ATTN_HEADS * 2 * ATTN_HEAD_DIM
V_DIM = ATTN_HEADS * 2 * ATTN_HEAD_DIM
IN_DIM = POOL_DIM + 2 * QK_DIM + V_DIM
MIX_DIM = POOL_DIM + V_DIM

NUM_BUCKETS = 32
MAX_DISTANCE = 128

N_EXPERTS = 256
TOP_K = 8
N_GROUPS = 8
TOP_K_GROUPS = 4
EXPERT_DIM = 256
SHARED_DIM = 256
ROUTED_SCALE = 2.5
DISPATCH_BLOCK = 256

ALPHA = (2.0 * DEPTH) ** 0.25
BETA = (8.0 * DEPTH) ** -0.25
LN_EPS = 1e-5

kernel_name = "hybrid_pool_diffattn_moe_deepnorm_adaln"


def layer_norm(x, g, b):
    xf = x.astype(jnp.float32)
    mu = jnp.mean(xf, axis=-1, keepdims=True)
    var = jnp.mean(jnp.square(xf - mu), axis=-1, keepdims=True)
    y = (xf - mu) * lax.rsqrt(var + LN_EPS) * g.astype(jnp.float32) + b.astype(jnp.float32)
    return y.astype(x.dtype)


def rms_norm(x, g):
    xf = x.astype(jnp.float32)
    y = xf * lax.rsqrt(jnp.mean(jnp.square(xf), axis=-1, keepdims=True) + LN_EPS)
    return (y * g.astype(jnp.float32)).astype(x.dtype)


def t5_bucket(rel):
    half = NUM_BUCKETS // 2
    max_exact = half // 2
    ret = jnp.where(rel > 0, half, 0)
    n = jnp.abs(rel)
    nf = jnp.maximum(n, 1).astype(jnp.float32)
    large = max_exact + (jnp.log(nf / max_exact) / math.log(MAX_DISTANCE / max_exact)
                         * (half - max_exact)).astype(jnp.int32)
    large = jnp.minimum(large, half - 1)
    return ret + jnp.where(n < max_exact, n, large)


def multiscale_pool(u, pool_w, pool_scale):
    B, S, _ = u.shape
    ug = u.reshape(B, S, POOL_GROUPS, POOL_GROUP_DIM).astype(jnp.float32)
    cs = jnp.cumsum(ug, axis=1)
    t = jnp.arange(1, S + 1, dtype=jnp.float32)
    outs = []
    for g, w in enumerate(POOL_WINDOWS):
        c = cs[:, :, g]
        lag = jnp.pad(c[:, :S - w], ((0, 0), (w, 0), (0, 0)))
        mean = (c - lag) / jnp.minimum(t, float(w))[None, :, None]
        outs.append(mean - ug[:, :, g])
    pooled = jnp.stack(outs, axis=2).astype(u.dtype)
    y = jnp.einsum('bsgc,gcd->bsgd', pooled, pool_w).reshape(B, S, POOL_DIM)
    return y * pool_scale


def diff_attention(q, k, v, lam, subln_g, rel_table, lambda_init):
    B, S, _ = q.shape
    q = q.reshape(B, S, ATTN_HEADS, 2, ATTN_HEAD_DIM) * (ATTN_HEAD_DIM ** -0.5)
    k = k.reshape(B, S, ATTN_HEADS, 2, ATTN_HEAD_DIM)
    v = v.reshape(B, S, ATTN_HEADS, 2 * ATTN_HEAD_DIM)
    pos = jnp.arange(S, dtype=jnp.int32)
    outs = []
    for qs in range(0, S, Q_BLOCK):
        qe = min(qs + Q_BLOCK, S)
        ke = min(S, -(-qe // CHUNK) * CHUNK)
        qp, kp = pos[qs:qe], pos[:ke]
        logits = jnp.einsum('bqhmd,bkhmd->bhmqk', q[:, qs:qe], k[:, :ke]).astype(jnp.float32)
        bias = rel_table[t5_bucket(kp[None, :] - qp[:, None])].astype(jnp.float32)
        bias = jnp.transpose(bias, (2, 0, 1))[None, :, None]
        allowed = (kp[None, :] // CHUNK) <= (qp[:, None] // CHUNK)
        logits = jnp.where(allowed, logits + bias, -jnp.inf)
        p = jax.nn.softmax(logits, axis=-1)
        pd = p[:, :, 0] - lam * p[:, :, 1]
        outs.append(jnp.einsum('bhqk,bkhe->bqhe', pd.astype(v.dtype), v[:, :ke]))
    o = jnp.concatenate(outs, axis=1)
    o = rms_norm(o, subln_g) * (1.0 - lambda_init)
    return o.reshape(B, S, V_DIM)


def route(h_flat, w_router, router_bias):
    N = h_flat.shape[0]
    scores = jax.nn.sigmoid(jnp.matmul(h_flat.astype(jnp.float32), w_router.astype(jnp.float32)))
    sel = scores + router_bias.astype(jnp.float32)
    grouped = sel.reshape(N, N_GROUPS, N_EXPERTS // N_GROUPS)
    group_score = lax.top_k(grouped, 2)[0].sum(-1)
    _, gidx = lax.top_k(group_score, TOP_K_GROUPS)
    gmask = jax.nn.one_hot(gidx, N_GROUPS, dtype=jnp.float32).sum(1) > 0
    emask = jnp.repeat(gmask, N_EXPERTS // N_GROUPS, axis=1)
    _, idx = lax.top_k(jnp.where(emask, sel, -jnp.inf), TOP_K)
    w = jnp.take_along_axis(scores, idx, axis=1)
    w = w / jnp.sum(w, axis=-1, keepdims=True) * ROUTED_SCALE
    return idx.astype(jnp.int32), w


def routed_experts(h_flat, idx, wts, w_gate, w_up, w_down):
    N, D = h_flat.shape
    A = N * TOP_K
    M = DISPATCH_BLOCK
    flat_e = idx.reshape(-1)
    flat_w = wts.reshape(-1).astype(h_flat.dtype)
    flat_tok = jnp.arange(A, dtype=jnp.int32) // TOP_K
    order = jnp.argsort(flat_e, stable=True)
    se, stok, sw = flat_e[order], flat_tok[order], flat_w[order]
    counts = jnp.bincount(flat_e, length=N_EXPERTS).astype(jnp.int32)
    starts = jnp.cumsum(counts) - counts
    padded = (counts + M - 1) // M * M
    pends = jnp.cumsum(padded)
    pstarts = pends - padded
    dest = pstarts[se] + (jnp.arange(A, dtype=jnp.int32) - starts[se])
    n_blocks = -(-(A + N_EXPERTS * (M - 1)) // M)
    P = n_blocks * M
    slot_tok = jnp.zeros((P,), jnp.int32).at[dest].set(stok)
    slot_w = jnp.zeros((P,), h_flat.dtype).at[dest].set(sw)
    block_e = jnp.minimum(jnp.searchsorted(pends, jnp.arange(n_blocks, dtype=jnp.int32) * M, side='right'),
                          N_EXPERTS - 1).astype(jnp.int32)

    def body(acc, blk):
        tok, wt, e = blk
        xs = h_flat[tok]
        a = jax.nn.silu(xs @ w_gate[e]) * (xs @ w_up[e])
        y = (a @ w_down[e]) * wt[:, None]
        return acc.at[tok].add(y), None

    acc, _ = lax.scan(body, jnp.zeros_like(h_flat),
                      (slot_tok.reshape(n_blocks, M), slot_w.reshape(n_blocks, M), block_e))
    return acc


def setup_inputs(seed: int = 0) -> dict:
    key = jax.random.key(seed)
    ks = jax.random.split(key, 26)
    f32 = jnp.float32
    L, D = DEPTH, D_MODEL
    nrm = lambda k, shape, s: jax.random.normal(k, shape, f32) * s
    w_in = nrm(ks[4], (L, D, IN_DIM), D ** -0.5)
    v_cols = jnp.arange(IN_DIM) >= POOL_DIM + 2 * QK_DIM
    w_in = w_in * jnp.where(v_cols, BETA, 1.0).astype(f32)
    return {
        "x": nrm(ks[0], (BATCH, SEQ, D), 1.0),
        "c": nrm(ks[1], (BATCH, D), 1.0),
        "w_ada": nrm(ks[2], (L, D, 6 * D), D ** -0.5),
        "b_ada": nrm(ks[3], (L, 6 * D), 0.02),
        "w_in": w_in,
        "pool_w": nrm(ks[5], (L, POOL_GROUPS, POOL_GROUP_DIM, POOL_GROUP_DIM), POOL_GROUP_DIM ** -0.5),
        "pool_scale": 1.0 + nrm(ks[6], (L, POOL_DIM), 0.02),
        "lambda_q1": nrm(ks[7], (L, ATTN_HEAD_DIM), 0.1),
        "lambda_k1": nrm(ks[8], (L, ATTN_HEAD_DIM), 0.1),
        "lambda_q2": nrm(ks[9], (L, ATTN_HEAD_DIM), 0.1),
        "lambda_k2": nrm(ks[10], (L, ATTN_HEAD_DIM), 0.1),
        "subln_g": 1.0 + nrm(ks[11], (L, 2 * ATTN_HEAD_DIM), 0.02),
        "w_out": nrm(ks[12], (L, MIX_DIM, D), MIX_DIM ** -0.5 * BETA),
        "ln1_g": 1.0 + nrm(ks[13], (L, D), 0.02),
        "ln1_b": nrm(ks[14], (L, D), 0.02),
        "w_router": nrm(ks[15], (L, D, N_EXPERTS), D ** -0.5),
        "router_bias": nrm(ks[16], (L, N_EXPERTS), 0.01),
        "w_gate": nrm(ks[17], (L, N_EXPERTS, D, EXPERT_DIM), D ** -0.5),
        "w_up": nrm(ks[18], (L, N_EXPERTS, D, EXPERT_DIM), D ** -0.5),
        "w_down": nrm(ks[19], (L, N_EXPERTS, EXPERT_DIM, D), EXPERT_DIM ** -0.5 * BETA),
        "ws_gate": nrm(ks[20], (L, D, SHARED_DIM), D ** -0.5),
        "ws_up": nrm(ks[21], (L, D, SHARED_DIM), D ** -0.5),
        "ws_down": nrm(ks[22], (L, SHARED_DIM, D), SHARED_DIM ** -0.5 * BETA),
        "ln2_g": 1.0 + nrm(ks[23], (L, D), 0.02),
        "ln2_b": nrm(ks[24], (L, D), 0.02),
        "rel_table": nrm(ks[25], (NUM_BUCKETS, ATTN_HEADS), 0.5),
    }


def reference(x, c, w_ada, b_ada, w_in, pool_w, pool_scale, lambda_q1, lambda_k1, lambda_q2, lambda_k2,
              subln_g, w_out, ln1_g, ln1_b, w_router, router_bias, w_gate, w_up, w_down,
              ws_gate, ws_up, ws_down, ln2_g, ln2_b, rel_table):
    B, S, D = x.shape
    c_act = jax.nn.silu(c)
    for l in range(DEPTH):
        mod = c_act @ w_ada[l] + b_ada[l]
        shift1, scale1, gate1, shift2, scale2, gate2 = [m[:, None, :] for m in jnp.split(mod, 6, axis=-1)]
        lambda_init = 0.8 - 0.6 * math.exp(-0.3 * l)

        h = x * (1.0 + scale1) + shift1
        proj = h @ w_in[l]
        u = proj[..., :POOL_DIM]
        q = proj[..., POOL_DIM:POOL_DIM + QK_DIM]
        k = proj[..., POOL_DIM + QK_DIM:POOL_DIM + 2 * QK_DIM]
        v = proj[..., POOL_DIM + 2 * QK_DIM:]
        lam = (jnp.exp(jnp.sum(lambda_q1[l].astype(jnp.float32) * lambda_k1[l].astype(jnp.float32)))
               - jnp.exp(jnp.sum(lambda_q2[l].astype(jnp.float32) * lambda_k2[l].astype(jnp.float32)))
               + lambda_init)
        y_pool = multiscale_pool(u, pool_w[l], pool_scale[l])
        y_attn = diff_attention(q, k, v, lam, subln_g[l], rel_table, lambda_init)
        mix = jnp.concatenate([y_pool, y_attn], axis=-1) @ w_out[l]
        x = layer_norm(ALPHA * x + gate1 * mix, ln1_g[l], ln1_b[l])

        h2 = (x * (1.0 + scale2) + shift2).reshape(B * S, D)
        idx, wts = route(h2, w_router[l], router_bias[l])
        routed = routed_experts(h2, idx, wts, w_gate[l], w_up[l], w_down[l])
        shared = (jax.nn.silu(h2 @ ws_gate[l]) * (h2 @ ws_up[l])) @ ws_down[l]
        ffn = (routed + shared).reshape(B, S, D)
        x = layer_norm(ALPHA * x + gate2 * ffn, ln2_g[l], ln2_b[l])
    return x
```

```python
import functools
import math

import jax
import jax.numpy as jnp
from jax import lax
from jax.experimental import pallas as pl
from jax.experimental.pallas import tpu as pltpu

F32 = jnp.float32
BF16 = jnp.bfloat16

D_MODEL = 1024
CHUNK = 64
Q_BLOCK = 128
POOL_DIM = 512
POOL_WINDOWS = (2, 4, 8, 16)
POOL_GROUP_DIM = 128
MAX_WINDOW = max(POOL_WINDOWS)
ATTN_HEADS = 4
ATTN_HEAD_DIM = 64
QK_DIM = 512
V_DIM = 512
IN_DIM = 2048
NUM_BUCKETS = 32
MAX_DISTANCE = 128
N_EXPERTS = 256
TOP_K = 8
N_GROUPS = 8
GROUP_SIZE = N_EXPERTS // N_GROUPS
TOP_K_GROUPS = 4
EXPERT_DIM = 256
ROUTED_SCALE = 2.5
DISPATCH_BLOCK = 256
DEPTH = 1
ALPHA = (2.0 * DEPTH) ** 0.25
LN_EPS = 1e-5
LAMBDA_INIT = 0.8 - 0.6 * math.exp(-0.3 * 0)

VMEM_LIMIT = 48 * 1024 * 1024


def _sigmoid(x):
    return 1.0 / (1.0 + jnp.exp(-x))


def _silu(x):
    return x * _sigmoid(x)


def _layer_norm(z, g, b):
    mu = jnp.mean(z, axis=-1, keepdims=True)
    zc = z - mu
    var = jnp.mean(zc * zc, axis=-1, keepdims=True)
    return zc * lax.rsqrt(var + LN_EPS) * g + b


def _params(sem=None):
    return pltpu.CompilerParams(dimension_semantics=sem, vmem_limit_bytes=VMEM_LIMIT)


def _mod_kernel(c_ref, w_ref, b_ref, o_ref):
    ca = _silu(c_ref[...])
    o_ref[...] = jnp.dot(ca, w_ref[...], preferred_element_type=F32,
                         precision=lax.Precision.HIGHEST) + b_ref[...]


def _modulation(c, w_ada, b_ada):
    bsz, d = c.shape
    n_out = w_ada.shape[1]
    return pl.pallas_call(
        _mod_kernel,
        grid=(n_out // d,),
        in_specs=[pl.BlockSpec((bsz, d), lambda j: (0, 0)),
                  pl.BlockSpec((d, d), lambda j: (0, j)),
                  pl.BlockSpec((1, d), lambda j: (0, j))],
        out_specs=pl.BlockSpec((bsz, d), lambda j: (0, j)),
        out_shape=jax.ShapeDtypeStruct((bsz, n_out), F32),
        compiler_params=_params(("arbitrary",)),
    )(c, w_ada, b_ada.reshape(1, n_out))


def _inproj_kernel(x_ref, sc_ref, sh_ref, w_ref, pw_ref, ps_ref,
                   yp_ref, q_ref, k_ref, v_ref, ext_ref, *, tm, seq):
    i = pl.program_id(0)
    tiles_per_seq = seq // tm
    it = i % tiles_per_seq
    h = x_ref[...] * (1.0 + sc_ref[0]) + sh_ref[0]
    proj = jnp.dot(h.astype(BF16), w_ref[...], preferred_element_type=F32)
    u = proj[:, :POOL_DIM]
    q_ref[...] = (proj[:, POOL_DIM:POOL_DIM + QK_DIM] * (ATTN_HEAD_DIM ** -0.5)).astype(BF16)
    k_ref[...] = proj[:, POOL_DIM + QK_DIM:POOL_DIM + 2 * QK_DIM].astype(BF16)
    v_ref[...] = proj[:, POOL_DIM + 2 * QK_DIM:].astype(BF16)

    @pl.when(it == 0)
    def _():
        ext_ref[0:MAX_WINDOW, :] = jnp.zeros((MAX_WINDOW, POOL_DIM), F32)

    ext_ref[MAX_WINDOW:MAX_WINDOW + tm, :] = u
    pos = (it * tm + lax.broadcasted_iota(jnp.int32, (tm, 1), 0) + 1).astype(F32)
    for g, w in enumerate(POOL_WINDOWS):
        c0, c1 = g * POOL_GROUP_DIM, (g + 1) * POOL_GROUP_DIM
        s = ext_ref[MAX_WINDOW:MAX_WINDOW + tm, c0:c1]
        for j in range(1, w):
            s = s + ext_ref[MAX_WINDOW - j:MAX_WINDOW - j + tm, c0:c1]
        pooled = s / jnp.minimum(pos, float(w)) - u[:, c0:c1]
        y = jnp.dot(pooled.astype(BF16), pw_ref[g], preferred_element_type=F32)
        yp_ref[:, c0:c1] = (y * ps_ref[:, c0:c1]).astype(BF16)
    ext_ref[0:MAX_WINDOW, :] = ext_ref[tm:tm + MAX_WINDOW, :]


def _inproj(x2, scale1, shift1, w_in, pool_w, pool_scale, *, seq, tm):
    n, d = x2.shape
    assert n % tm == 0 and seq % tm == 0 and tm >= 2 * MAX_WINDOW
    tps = seq // tm
    mod_spec = pl.BlockSpec((1, 1, d), lambda i: (i // tps, 0, 0))
    row = lambda w: pl.BlockSpec((tm, w), lambda i: (i, 0))
    full = lambda a: pl.BlockSpec(a.shape, lambda i: (0,) * a.ndim)
    return pl.pallas_call(
        functools.partial(_inproj_kernel, tm=tm, seq=seq),
        grid=(n // tm,),
        in_specs=[row(d), mod_spec, mod_spec, full(w_in), full(pool_w), full(pool_scale)],
        out_specs=[row(POOL_DIM), row(QK_DIM), row(QK_DIM), row(V_DIM)],
        out_shape=[jax.ShapeDtypeStruct((n, POOL_DIM), BF16),
                   jax.ShapeDtypeStruct((n, QK_DIM), BF16),
                   jax.ShapeDtypeStruct((n, QK_DIM), BF16),
                   jax.ShapeDtypeStruct((n, V_DIM), BF16)],
        scratch_shapes=[pltpu.VMEM((tm + MAX_WINDOW, POOL_DIM), F32)],
        compiler_params=_params(("arbitrary",)),
    )(x2, scale1, shift1, w_in, pool_w, pool_scale)


def _bias_kernel(tab_ref, o_ref):
    delta = pl.program_id(0)
    r = lax.broadcasted_iota(jnp.int32, (Q_BLOCK, Q_BLOCK), 0)
    c = lax.broadcasted_iota(jnp.int32, (Q_BLOCK, Q_BLOCK), 1)
    rel = c - r - delta * Q_BLOCK
    half = NUM_BUCKETS // 2
    max_exact = half // 2
    ret = jnp.where(rel > 0, half, 0)
    n = jnp.abs(rel)
    nf = jnp.maximum(n, 1).astype(F32)
    large = max_exact + (jnp.log(nf / max_exact) / math.log(MAX_DISTANCE / max_exact)
                         * (half - max_exact)).astype(jnp.int32)
    large = jnp.minimum(large, half - 1)
    bucket = ret + jnp.where(n < max_exact, n, large)
    for h in range(ATTN_HEADS):
        acc = jnp.zeros((Q_BLOCK, Q_BLOCK), F32)
        for b in range(NUM_BUCKETS):
            acc = jnp.where(bucket == b, tab_ref[b, h], acc)
        o_ref[h, 0] = acc


def _bias_tiles(rel_table, n_qblocks):
    return pl.pallas_call(
        _bias_kernel,
        grid=(n_qblocks,),
        in_specs=[pl.BlockSpec(memory_space=pltpu.SMEM)],
        out_specs=pl.BlockSpec((ATTN_HEADS, 1, Q_BLOCK, Q_BLOCK), lambda dlt: (0, dlt, 0, 0)),
        out_shape=jax.ShapeDtypeStruct((ATTN_HEADS, n_qblocks, Q_BLOCK, Q_BLOCK), F32),
        compiler_params=_params(("arbitrary",)),
    )(rel_table)


def _attn_kernel(q_ref, k_ref, v_ref, bias_ref, lq1_ref, lk1_ref, lq2_ref, lk2_ref, g_ref, o_ref):
    qb = pl.program_id(1)
    lam = (jnp.exp(jnp.sum(lq1_ref[...] * lk1_ref[...], axis=-1, keepdims=True))
           - jnp.exp(jnp.sum(lq2_ref[...] * lk2_ref[...], axis=-1, keepdims=True))
           + LAMBDA_INIT)
    r = lax.broadcasted_iota(jnp.int32, (Q_BLOCK, Q_BLOCK), 0)
    c = lax.broadcasted_iota(jnp.int32, (Q_BLOCK, Q_BLOCK), 1)
    allowed = (c // CHUNK) <= (r // CHUNK)
    hd2 = 2 * ATTN_HEAD_DIM

    for h in range(ATTN_HEADS):
        def block(kb, carry, diagonal, h=h):
            koff = pl.multiple_of(kb * Q_BLOCK, Q_BLOCK)
            vblk = v_ref[pl.ds(koff, Q_BLOCK), h * hd2:(h + 1) * hd2]
            bias = bias_ref[h, qb - kb]
            out = []
            for m in range(2):
                m_old, l_old, acc = carry[3 * m:3 * m + 3]
                col = (2 * h + m) * ATTN_HEAD_DIM
                qh = q_ref[:, col:col + ATTN_HEAD_DIM]
                kh = k_ref[pl.ds(koff, Q_BLOCK), col:col + ATTN_HEAD_DIM]
                s = lax.dot_general(qh, kh, (((1,), (1,)), ((), ())),
                                    preferred_element_type=F32) + bias
                if diagonal:
                    s = jnp.where(allowed, s, -jnp.inf)
                m_new = jnp.maximum(m_old, jnp.max(s, axis=-1, keepdims=True))
                alpha = jnp.exp(m_old - m_new)
                p = jnp.exp(s - m_new)
                l_new = alpha * l_old + jnp.sum(p, axis=-1, keepdims=True)
                acc = alpha * acc + jnp.dot(p.astype(BF16), vblk, preferred_element_type=F32)
                out += [m_new, l_new, acc]
            return tuple(out)

        one = (jnp.full((Q_BLOCK, 1), -jnp.inf, F32), jnp.zeros((Q_BLOCK, 1), F32),
               jnp.zeros((Q_BLOCK, hd2), F32))
        carry = lax.fori_loop(0, qb, lambda kb, cr: block(kb, cr, False), one + one)
        _, l0, a0, _, l1, a1 = block(qb, carry, True)
        o = a0 / l0 - lam * (a1 / l1)
        y = o * lax.rsqrt(jnp.mean(o * o, axis=-1, keepdims=True) + LN_EPS) * g_ref[...]
        o_ref[:, h * hd2:(h + 1) * hd2] = (y * (1.0 - LAMBDA_INIT)).astype(BF16)


def _attention(q, k, v, bias_tiles, lq1, lk1, lq2, lk2, subln_g, *, bsz, seq):
    nqb = seq // Q_BLOCK
    full = lambda a: pl.BlockSpec(a.shape, lambda b, j: (0,) * a.ndim)
    kv_spec = pl.BlockSpec((seq, QK_DIM), lambda b, j: (b, 0))
    return pl.pallas_call(
        _attn_kernel,
        grid=(bsz, nqb),
        in_specs=[pl.BlockSpec((Q_BLOCK, QK_DIM), lambda b, j: (b * nqb + j, 0)),
                  kv_spec, kv_spec, full(bias_tiles),
                  full(lq1), full(lk1), full(lq2), full(lk2), full(subln_g)],
        out_specs=pl.BlockSpec((Q_BLOCK, V_DIM), lambda b, j: (b * nqb + j, 0)),
        out_shape=jax.ShapeDtypeStruct((bsz * seq, V_DIM), BF16),
        compiler_params=_params(("arbitrary", "arbitrary")),
    )(q, k, v, bias_tiles, lq1, lk1, lq2, lk2, subln_g)


def _outproj_kernel(x_ref, yp_ref, ya_ref, w_ref, gate_ref, g_ref, b_ref, o_ref):
    mix = (jnp.dot(yp_ref[...], w_ref[0:POOL_DIM, :], preferred_element_type=F32)
           + jnp.dot(ya_ref[...], w_ref[POOL_DIM:, :], preferred_element_type=F32))
    z = ALPHA * x_ref[...] + gate_ref[0] * mix
    o_ref[...] = _layer_norm(z, g_ref[...], b_ref[...])


def _outproj(x2, yp, ya, w_out, gate1, ln_g, ln_b, *, seq, tm):
    n, d = x2.shape
    tps = seq // tm
    row = lambda w: pl.BlockSpec((tm, w), lambda i: (i, 0))
    full = lambda a: pl.BlockSpec(a.shape, lambda i: (0,) * a.ndim)
    return pl.pallas_call(
        _outproj_kernel,
        grid=(n // tm,),
        in_specs=[row(d), row(POOL_DIM), row(V_DIM), full(w_out),
                  pl.BlockSpec((1, 1, d), lambda i: (i // tps, 0, 0)), full(ln_g), full(ln_b)],
        out_specs=row(d),
        out_shape=jax.ShapeDtypeStruct((n, d), F32),
        compiler_params=_params(("arbitrary",)),
    )(x2, yp, ya, w_out, gate1, ln_g, ln_b)


def _route_kernel(x_ref, sc_ref, sh_ref, whi_ref, wlo_ref, rb_ref,
                  eidx_ref, rank_ref, wts_ref, cnt_ref, carry_ref, *, tr):
    i = pl.program_id(0)

    @pl.when(i == 0)
    def _():
        carry_ref[...] = jnp.zeros_like(carry_ref)

    h2 = x_ref[...] * (1.0 + sc_ref[0]) + sh_ref[0]
    hi = h2.astype(BF16)
    lo = (h2 - hi.astype(F32)).astype(BF16)
    logits = (jnp.dot(hi, whi_ref[...], preferred_element_type=F32)
              + jnp.dot(hi, wlo_ref[...], preferred_element_type=F32)
              + jnp.dot(lo, whi_ref[...], preferred_element_type=F32))
    scores = _sigmoid(logits)
    sel = scores + rb_ref[...]
    lane = lax.broadcasted_iota(jnp.int32, (tr, N_EXPERTS), 1)
    lane_f = lane.astype(F32)

    a1 = sel
    a2 = jnp.full_like(sel, -jnp.inf)
    s = 1
    while s < GROUP_SIZE:
        upper = (lane & s) != 0
        b1 = jnp.where(upper, pltpu.roll(a1, s, 1), pltpu.roll(a1, N_EXPERTS - s, 1))
        b2 = jnp.where(upper, pltpu.roll(a2, s, 1), pltpu.roll(a2, N_EXPERTS - s, 1))
        a1, a2 = jnp.maximum(a1, b1), jnp.maximum(jnp.minimum(a1, b1), jnp.maximum(a2, b2))
        s *= 2
    gscore = a1 + a2

    grp = lane // GROUP_SIZE
    beaten_by = jnp.zeros((tr, N_EXPERTS), jnp.int32)
    for kk in range(1, N_GROUPS):
        other = pltpu.roll(gscore, kk * GROUP_SIZE, 1)
        ogrp = (grp - kk) & (N_GROUPS - 1)
        wins = (other > gscore) | ((other == gscore) & (ogrp < grp))
        beaten_by = beaten_by + wins.astype(jnp.int32)
    cur = jnp.where(beaten_by < TOP_K_GROUPS, sel, -jnp.inf)

    picks, weights = [], []
    selmask = jnp.zeros((tr, N_EXPERTS), F32)
    for _ in range(TOP_K):
        mx = jnp.max(cur, axis=-1, keepdims=True)
        pick = jnp.min(jnp.where(cur == mx, lane_f, float(N_EXPERTS)), axis=-1, keepdims=True)
        onehot = lane_f == pick
        weights.append(jnp.sum(jnp.where(onehot, scores, 0.0), axis=-1, keepdims=True))
        cur = jnp.where(onehot, -jnp.inf, cur)
        selmask = jnp.where(onehot, 1.0, selmask)
        picks.append(pick)

    rr = lax.broadcasted_iota(jnp.int32, (tr, tr), 0)
    cc = lax.broadcasted_iota(jnp.int32, (tr, tr), 1)
    earlier = jnp.where(cc < rr, 1.0, 0.0).astype(BF16)
    rankmat = jnp.dot(earlier, selmask.astype(BF16), preferred_element_type=F32) + carry_ref[...]
    carry_ref[...] = carry_ref[...] + jnp.sum(selmask, axis=0, keepdims=True)
    cnt_ref[...] = carry_ref[...]

    wsum = weights[0]
    for wj in weights[1:]:
        wsum = wsum + wj
    lane8 = lax.broadcasted_iota(jnp.int32, (tr, TOP_K), 1)
    eidx = jnp.zeros((tr, TOP_K), jnp.int32)
    rank = jnp.zeros((tr, TOP_K), jnp.int32)
    wts = jnp.zeros((tr, TOP_K), F32)
    for j in range(TOP_K):
        rk = jnp.sum(jnp.where(lane_f == picks[j], rankmat, 0.0), axis=-1, keepdims=True)
        eidx = jnp.where(lane8 == j, picks[j].astype(jnp.int32), eidx)
        rank = jnp.where(lane8 == j, rk.astype(jnp.int32), rank)
        wts = jnp.where(lane8 == j, weights[j] / wsum * ROUTED_SCALE, wts)
    eidx_ref[...] = eidx
    rank_ref[...] = rank
    wts_ref[...] = wts


def _route(x1, scale2, shift2, wr_hi, wr_lo, router_bias, *, seq, tr):
    n, d = x1.shape
    tps = seq // tr
    mod_spec = pl.BlockSpec((1, 1, d), lambda i: (i // tps, 0, 0))
    full = lambda a: pl.BlockSpec(a.shape, lambda i: (0,) * a.ndim)
    k8 = pl.BlockSpec((tr, TOP_K), lambda i: (i, 0))
    return pl.pallas_call(
        functools.partial(_route_kernel, tr=tr),
        grid=(n // tr,),
        in_specs=[pl.BlockSpec((tr, d), lambda i: (i, 0)), mod_spec, mod_spec,
                  full(wr_hi), full(wr_lo), full(router_bias)],
        out_specs=[k8, k8, k8, pl.BlockSpec((1, N_EXPERTS), lambda i: (0, 0))],
        out_shape=[jax.ShapeDtypeStruct((n, TOP_K), jnp.int32),
                   jax.ShapeDtypeStruct((n, TOP_K), jnp.int32),
                   jax.ShapeDtypeStruct((n, TOP_K), F32),
                   jax.ShapeDtypeStruct((1, N_EXPERTS), F32)],
        scratch_shapes=[pltpu.VMEM((1, N_EXPERTS), F32)],
        compiler_params=_params(("arbitrary",)),
    )(x1, scale2, shift2, wr_hi, wr_lo, router_bias)


def _dispatch_kernel(x_ref, sc_ref, sh_ref, dest_ref, xs_in_ref, xs_ref, h2_ref, sem, *, td):
    del xs_in_ref
    h2_ref[...] = x_ref[...] * (1.0 + sc_ref[0]) + sh_ref[0]

    def issue(t, carry):
        for j in range(TOP_K):
            d = dest_ref[t * TOP_K + j]
            pltpu.make_async_copy(h2_ref.at[pl.ds(t, 1), :], xs_ref.at[pl.ds(d, 1), :], sem).start()
        return carry

    lax.fori_loop(0, td, issue, 0)
    for _ in range(TOP_K):
        pltpu.make_async_copy(h2_ref, xs_ref.at[pl.ds(0, td), :], sem).wait()


def _dispatch(x1, scale2, shift2, dest_flat, xs_zero, *, seq, td):
    n, d = x1.shape
    tps = seq // td
    mod_spec = pl.BlockSpec((1, 1, d), lambda i: (i // tps, 0, 0))
    return pl.pallas_call(
        functools.partial(_dispatch_kernel, td=td),
        grid=(n // td,),
        in_specs=[pl.BlockSpec((td, d), lambda i: (i, 0)), mod_spec, mod_spec,
                  pl.BlockSpec((td * TOP_K,), lambda i: (i,), memory_space=pltpu.SMEM),
                  pl.BlockSpec(memory_space=pl.ANY)],
        out_specs=pl.BlockSpec(memory_space=pl.ANY),
        out_shape=jax.ShapeDtypeStruct(xs_zero.shape, xs_zero.dtype),
        scratch_shapes=[pltpu.VMEM((td, d), F32), pltpu.SemaphoreType.DMA(())],
        input_output_aliases={4: 0},
        compiler_params=_params(("arbitrary",)),
    )(x1, scale2, shift2, dest_flat, xs_zero)


def _expert_kernel(be_ref, nused_ref, xs_ref, wg_ref, wu_ref, wd_ref, ys_ref, wgb, wub, wdb):
    b = pl.program_id(0)
    prev = be_ref[jnp.maximum(b - 1, 0)]

    @pl.when((b == 0) | (be_ref[b] != prev))
    def _():
        wgb[...] = wg_ref[0].astype(BF16)
        wub[...] = wu_ref[0].astype(BF16)
        wdb[...] = wd_ref[0].astype(BF16)

    @pl.when(b < nused_ref[0])
    def _():
        x = xs_ref[...].astype(BF16)
        g = jnp.dot(x, wgb[...], preferred_element_type=F32)
        u = jnp.dot(x, wub[...], preferred_element_type=F32)
        a = (_silu(g) * u).astype(BF16)
        ys_ref[...] = jnp.dot(a, wdb[...], preferred_element_type=F32)

    @pl.when(b >= nused_ref[0])
    def _():
        ys_ref[...] = jnp.zeros_like(ys_ref)


def _experts(block_e, n_used, xs, w_gate, w_up, w_down):
    p, d = xs.shape
    m = DISPATCH_BLOCK
    n_blocks = p // m
    f = w_gate.shape[-1]
    grid_spec = pltpu.PrefetchScalarGridSpec(
        num_scalar_prefetch=2,
        grid=(n_blocks,),
        in_specs=[pl.BlockSpec((m, d), lambda b, be, nu: (b, 0)),
                  pl.BlockSpec((1, d, f), lambda b, be, nu: (be[b], 0, 0)),
                  pl.BlockSpec((1, d, f), lambda b, be, nu: (be[b], 0, 0)),
                  pl.BlockSpec((1, f, d), lambda b, be, nu: (be[b], 0, 0))],
        out_specs=pl.BlockSpec((m, d), lambda b, be, nu: (b, 0)),
        scratch_shapes=[pltpu.VMEM((d, f), BF16), pltpu.VMEM((d, f), BF16), pltpu.VMEM((f, d), BF16)],
    )
    return pl.pallas_call(
        _expert_kernel,
        grid_spec=grid_spec,
        out_shape=jax.ShapeDtypeStruct((p, d), F32),
        compiler_params=_params(("arbitrary",)),
    )(block_e, n_used, xs, w_gate, w_up, w_down)


def _combine_kernel(x_ref, sc_ref, sh_ref, gate_ref, dest_ref, wts_ref, ys_ref,
                    wsg_ref, wsu_ref, wsd_ref, g_ref, b_ref, o_ref, gbuf, sem, *, tc):
    def issue(t, carry):
        for j in range(TOP_K):
            d = dest_ref[t * TOP_K + j]
            pltpu.make_async_copy(ys_ref.at[pl.ds(d, 1), :], gbuf.at[pl.ds(j * tc + t, 1), :], sem).start()
        return carry

    lax.fori_loop(0, tc, issue, 0)

    x = x_ref[...]
    hb = (x * (1.0 + sc_ref[0]) + sh_ref[0]).astype(BF16)
    sg = jnp.dot(hb, wsg_ref[...], preferred_element_type=F32)
    su = jnp.dot(hb, wsu_ref[...], preferred_element_type=F32)
    ffn = jnp.dot((_silu(sg) * su).astype(BF16), wsd_ref[...], preferred_element_type=F32)

    for j in range(TOP_K):
        pltpu.make_async_copy(ys_ref.at[pl.ds(0, tc), :], gbuf.at[pl.ds(j * tc, tc), :], sem).wait()
    wts = wts_ref[...]
    for j in range(TOP_K):
        ffn = ffn + wts[:, j:j + 1] * gbuf[j * tc:(j + 1) * tc, :]
    z = ALPHA * x + gate_ref[0] * ffn
    o_ref[...] = _layer_norm(z, g_ref[...], b_ref[...])


def _combine(x1, scale2, shift2, gate2, dest_flat, wts, ys, ws_gate, ws_up, ws_down, ln_g, ln_b, *, seq, tc):
    n, d = x1.shape
    tps = seq // tc
    mod_spec = pl.BlockSpec((1, 1, d), lambda i: (i // tps, 0, 0))
    full = lambda a: pl.BlockSpec(a.shape, lambda i: (0,) * a.ndim)
    return pl.pallas_call(
        functools.partial(_combine_kernel, tc=tc),
        grid=(n // tc,),
        in_specs=[pl.BlockSpec((tc, d), lambda i: (i, 0)), mod_spec, mod_spec, mod_spec,
                  pl.BlockSpec((tc * TOP_K,), lambda i: (i,), memory_space=pltpu.SMEM),
                  pl.BlockSpec((tc, TOP_K), lambda i: (i, 0)),
                  pl.BlockSpec(memory_space=pl.ANY),
                  full(ws_gate), full(ws_up), full(ws_down), full(ln_g), full(ln_b)],
        out_specs=pl.BlockSpec((tc, d), lambda i: (i, 0)),
        out_shape=jax.ShapeDtypeStruct((n, d), F32),
        scratch_shapes=[pltpu.VMEM((TOP_K * tc, d), F32), pltpu.SemaphoreType.DMA(())],
        compiler_params=_params(("arbitrary",)),
    )(x1, scale2, shift2, gate2, dest_flat, wts, ys, ws_gate, ws_up, ws_down, ln_g, ln_b)


def _layer(x, c, w_ada, b_ada, w_in, pool_w, pool_scale, lq1, lk1, lq2, lk2, subln_g, w_out,
           ln1_g, ln1_b, w_router, router_bias, w_gate, w_up, w_down, ws_gate, ws_up, ws_down,
           ln2_g, ln2_b, rel_table, *, tm=512, tr=256, td=256, tc=256):
    bsz, seq, d = x.shape
    n = bsz * seq
    x2 = x.reshape(n, d)
    row = lambda a: a.reshape(1, -1)

    mod = _modulation(c, w_ada, b_ada)
    shift1, scale1, gate1, shift2, scale2, gate2 = [
        mod[:, j * d:(j + 1) * d].reshape(bsz, 1, d) for j in range(6)]

    yp, q, k, v = _inproj(x2, scale1, shift1, w_in.astype(BF16), pool_w.astype(BF16), row(pool_scale),
                          seq=seq, tm=tm)
    bias_tiles = _bias_tiles(rel_table, seq // Q_BLOCK)
    ya = _attention(q, k, v, bias_tiles, row(lq1), row(lk1), row(lq2), row(lk2), row(subln_g),
                    bsz=bsz, seq=seq)
    x1 = _outproj(x2, yp, ya, w_out.astype(BF16), gate1, row(ln1_g), row(ln1_b), seq=seq, tm=tm)

    wr_hi = w_router.astype(BF16)
    wr_lo = (w_router - wr_hi.astype(F32)).astype(BF16)
    eidx, rank, wts, cnt = _route(x1, scale2, shift2, wr_hi, wr_lo, row(router_bias), seq=seq, tr=tr)

    m = DISPATCH_BLOCK
    counts = cnt[0].astype(jnp.int32)
    padded = (counts + m - 1) // m * m
    pends = jnp.cumsum(padded)
    pstarts = pends - padded
    n_blocks = -(-(n * TOP_K + N_EXPERTS * (m - 1)) // m)
    block_e = jnp.minimum(jnp.searchsorted(pends, jnp.arange(n_blocks, dtype=jnp.int32) * m, side='right'),
                          N_EXPERTS - 1).astype(jnp.int32)
    n_used = (pends[-1:] // m).astype(jnp.int32)
    dest_flat = (pstarts[eidx] + rank).reshape(n * TOP_K)

    xs = _dispatch(x1, scale2, shift2, dest_flat, jnp.zeros((n_blocks * m, d), F32), seq=seq, td=td)
    ys = _experts(block_e, n_used, xs, w_gate, w_up, w_down)
    out = _combine(x1, scale2, shift2, gate2, dest_flat, wts, ys,
                   ws_gate.astype(BF16), ws_up.astype(BF16), ws_down.astype(BF16),
                   row(ln2_g), row(ln2_b), seq=seq, tc=tc)
    return out.reshape(bsz, seq, d)


def kernel(x, c, w_ada, b_ada, w_in, pool_w, pool_scale, lambda_q1, lambda_k1, lambda_q2, lambda_k2,
           subln_g, w_out, ln1_g, ln1_b, w_router, router_bias, w_gate, w_up, w_down,
           ws_gate, ws_up, ws_down, ln2_g, ln2_b, rel_table):
    per_layer = (w_ada, b_ada, w_in, pool_w, pool_scale, lambda_q1, lambda_k1, lambda_q2, lambda_k2,
                 subln_g, w_out, ln1_g, ln1_b, w_router, router_bias, w_gate, w_up, w_down,
                 ws_gate, ws_up, ws_down, ln2_g, ln2_b)
    assert all(a.shape[0] == DEPTH == 1 for a in per_layer)
    return _layer(x, c, *[a.reshape(a.shape[1:]) for a in per_layer], rel_table)
```

```python
import functools
import math

import jax
import jax.numpy as jnp
from jax import lax
from jax.experimental import pallas as pl
from jax.experimental.pallas import tpu as pltpu

F32 = jnp.float32
BF16 = jnp.bfloat16

D_MODEL = 1024
CHUNK = 64
Q_BLOCK = 128
ATT_TILE = 256
POOL_DIM = 512
POOL_WINDOWS = (2, 4, 8, 16)
POOL_GROUP_DIM = 128
MAX_WINDOW = max(POOL_WINDOWS)
ATTN_HEADS = 4
ATTN_HEAD_DIM = 64
QK_DIM = 512
V_DIM = 512
IN_DIM = 2048
NUM_BUCKETS = 32
MAX_DISTANCE = 128
N_EXPERTS = 256
TOP_K = 8
N_GROUPS = 8
GROUP_SIZE = N_EXPERTS // N_GROUPS
TOP_K_GROUPS = 4
EXPERT_DIM = 256
ROUTED_SCALE = 2.5
DISPATCH_BLOCK = 256
DEPTH = 1
ALPHA = (2.0 * DEPTH) ** 0.25
LN_EPS = 1e-5
LAMBDA_INIT = 0.8 - 0.6 * math.exp(-0.3 * 0)

VMEM_LIMIT = 48 * 1024 * 1024


def _sigmoid(x):
    return 1.0 / (1.0 + jnp.exp(-x))


def _silu(x):
    return x * _sigmoid(x)


def _layer_norm(z, g, b):
    mu = jnp.mean(z, axis=-1, keepdims=True)
    zc = z - mu
    var = jnp.mean(zc * zc, axis=-1, keepdims=True)
    return zc * lax.rsqrt(var + LN_EPS) * g + b


def _params(sem=None):
    return pltpu.CompilerParams(dimension_semantics=sem, vmem_limit_bytes=VMEM_LIMIT)


def _mod_kernel(c_ref, w_ref, b_ref, o_ref):
    ca = _silu(c_ref[...])
    o_ref[...] = jnp.dot(ca, w_ref[...], preferred_element_type=F32,
                         precision=lax.Precision.HIGHEST) + b_ref[...]


def _modulation(c, w_ada, b_ada):
    bsz, d = c.shape
    n_out = w_ada.shape[1]
    return pl.pallas_call(
        _mod_kernel,
        grid=(n_out // d,),
        in_specs=[pl.BlockSpec((bsz, d), lambda j: (0, 0)),
                  pl.BlockSpec((d, d), lambda j: (0, j)),
                  pl.BlockSpec((1, d), lambda j: (0, j))],
        out_specs=pl.BlockSpec((bsz, d), lambda j: (0, j)),
        out_shape=jax.ShapeDtypeStruct((bsz, n_out), F32),
        compiler_params=_params(("arbitrary",)),
    )(c, w_ada, b_ada.reshape(1, n_out))


def _inproj_kernel(x_ref, sc_ref, sh_ref, w_ref, wvt_ref, pw_ref, ps_ref,
                   yp_ref, q_ref, k_ref, vt_ref, ext_ref, *, tm, seq):
    i = pl.program_id(0)
    tiles_per_seq = seq // tm
    it = i % tiles_per_seq
    h = x_ref[...] * (1.0 + sc_ref[0]) + sh_ref[0]
    hb = h.astype(BF16)
    proj = jnp.dot(hb, w_ref[...], preferred_element_type=F32)
    u = proj[:, :POOL_DIM]
    q_ref[...] = (proj[:, POOL_DIM:POOL_DIM + QK_DIM] * (ATTN_HEAD_DIM ** -0.5)).astype(BF16)
    k_ref[...] = proj[:, POOL_DIM + QK_DIM:POOL_DIM + 2 * QK_DIM].astype(BF16)
    vt = lax.dot_general(wvt_ref[...], hb, (((1,), (1,)), ((), ())), preferred_element_type=F32)
    for j in range(tm // ATT_TILE):
        vt_ref[0, j] = vt[:, j * ATT_TILE:(j + 1) * ATT_TILE].astype(BF16)

    @pl.when(it == 0)
    def _():
        ext_ref[0:MAX_WINDOW, :] = jnp.zeros((MAX_WINDOW, POOL_DIM), F32)

    ext_ref[MAX_WINDOW:MAX_WINDOW + tm, :] = u
    pos = (it * tm + lax.broadcasted_iota(jnp.int32, (tm, 1), 0) + 1).astype(F32)
    for g, w in enumerate(POOL_WINDOWS):
        c0, c1 = g * POOL_GROUP_DIM, (g + 1) * POOL_GROUP_DIM
        s = ext_ref[MAX_WINDOW:MAX_WINDOW + tm, c0:c1]
        for j in range(1, w):
            s = s + ext_ref[MAX_WINDOW - j:MAX_WINDOW - j + tm, c0:c1]
        pooled = s / jnp.minimum(pos, float(w)) - u[:, c0:c1]
        y = jnp.dot(pooled.astype(BF16), pw_ref[g], preferred_element_type=F32)
        yp_ref[:, c0:c1] = (y * ps_ref[:, c0:c1]).astype(BF16)
    ext_ref[0:MAX_WINDOW, :] = ext_ref[tm:tm + MAX_WINDOW, :]


def _inproj(x2, scale1, shift1, w_main, w_vt, pool_w, pool_scale, *, seq, tm):
    n, d = x2.shape
    assert n % tm == 0 and seq % tm == 0 and tm >= 2 * MAX_WINDOW and tm % ATT_TILE == 0
    tps = seq // tm
    tpt = tm // ATT_TILE
    mod_spec = pl.BlockSpec((1, 1, d), lambda i: (i // tps, 0, 0))
    row = lambda w: pl.BlockSpec((tm, w), lambda i: (i, 0))
    full = lambda a: pl.BlockSpec(a.shape, lambda i: (0,) * a.ndim)
    return pl.pallas_call(
        functools.partial(_inproj_kernel, tm=tm, seq=seq),
        grid=(n // tm,),
        in_specs=[row(d), mod_spec, mod_spec, full(w_main), full(w_vt), full(pool_w), full(pool_scale)],
        out_specs=[row(POOL_DIM), row(QK_DIM), row(QK_DIM),
                   pl.BlockSpec((1, tpt, V_DIM, ATT_TILE), lambda i: (i // tps, i % tps, 0, 0))],
        out_shape=[jax.ShapeDtypeStruct((n, POOL_DIM), BF16),
                   jax.ShapeDtypeStruct((n, QK_DIM), BF16),
                   jax.ShapeDtypeStruct((n, QK_DIM), BF16),
                   jax.ShapeDtypeStruct((n // seq, seq // ATT_TILE, V_DIM, ATT_TILE), BF16)],
        scratch_shapes=[pltpu.VMEM((tm + MAX_WINDOW, POOL_DIM), F32)],
        compiler_params=_params(("arbitrary",)),
    )(x2, scale1, shift1, w_main, w_vt, pool_w, pool_scale)


def _bias_kernel(tab_ref, o_ref):
    delta = pl.program_id(0)
    r = lax.broadcasted_iota(jnp.int32, (ATT_TILE, ATT_TILE), 0)
    c = lax.broadcasted_iota(jnp.int32, (ATT_TILE, ATT_TILE), 1)
    rel = r - c - delta * ATT_TILE
    half = NUM_BUCKETS // 2
    max_exact = half // 2
    ret = jnp.where(rel > 0, half, 0)
    n = jnp.abs(rel)
    nf = jnp.maximum(n, 1).astype(F32)
    large = max_exact + (jnp.log(nf / max_exact) / math.log(MAX_DISTANCE / max_exact)
                         * (half - max_exact)).astype(jnp.int32)
    large = jnp.minimum(large, half - 1)
    bucket = ret + jnp.where(n < max_exact, n, large)
    for h in range(ATTN_HEADS):
        acc = jnp.zeros((ATT_TILE, ATT_TILE), F32)
        for b in range(NUM_BUCKETS):
            acc = jnp.where(bucket == b, tab_ref[b, h], acc)
        o_ref[h, 0] = acc


def _bias_tiles(rel_table, n_tiles):
    return pl.pallas_call(
        _bias_kernel,
        grid=(n_tiles,),
        in_specs=[pl.BlockSpec(memory_space=pltpu.SMEM)],
        out_specs=pl.BlockSpec((ATTN_HEADS, 1, ATT_TILE, ATT_TILE), lambda dlt: (0, dlt, 0, 0)),
        out_shape=jax.ShapeDtypeStruct((ATTN_HEADS, n_tiles, ATT_TILE, ATT_TILE), F32),
        compiler_params=_params(("arbitrary",)),
    )(rel_table)


def _attn_kernel(q_ref, k_ref, vt_ref, bias_ref, lq1_ref, lk1_ref, lq2_ref, lk2_ref, g_ref, o_ref, *acc_refs):
    qt = pl.program_id(1)
    t = ATT_TILE
    n_maps = 2 * ATTN_HEADS
    lam = (jnp.exp(jnp.sum(lq1_ref[...] * lk1_ref[...], axis=-1, keepdims=True))
           - jnp.exp(jnp.sum(lq2_ref[...] * lk2_ref[...], axis=-1, keepdims=True))
           + LAMBDA_INIT)
    r = lax.broadcasted_iota(jnp.int32, (t, t), 0)
    c = lax.broadcasted_iota(jnp.int32, (t, t), 1)
    allowed = (r // CHUNK) <= (c // CHUNK)
    hd2 = 2 * ATTN_HEAD_DIM

    def block(kt, carry, diagonal):
        koff = pl.multiple_of(kt * t, t)

        def scores(hm):
            col = hm * ATTN_HEAD_DIM
            qh = q_ref[:, col:col + ATTN_HEAD_DIM]
            kh = k_ref[pl.ds(koff, t), col:col + ATTN_HEAD_DIM]
            s = lax.dot_general(kh, qh, (((1,), (1,)), ((), ())),
                                preferred_element_type=F32) + bias_ref[hm // 2, qt - kt]
            return jnp.where(allowed, s, -jnp.inf) if diagonal else s

        def softmax(hm, s):
            m_old, l_old = carry[2 * hm:2 * hm + 2]
            m_new = jnp.maximum(m_old, jnp.max(s, axis=0, keepdims=True))
            alpha = jnp.exp(m_old - m_new)
            p = jnp.exp(s - m_new)
            return m_new, alpha * l_old + jnp.sum(p, axis=0, keepdims=True), alpha, p.astype(BF16)

        def accumulate(hm, alpha, p):
            h = hm // 2
            vth = vt_ref[0, kt, h * hd2:(h + 1) * hd2, :]
            acc_refs[hm][...] = alpha * acc_refs[hm][...] + jnp.dot(vth, p, preferred_element_type=F32)

        s_vals, sm_vals, out = {}, {}, [None] * (2 * n_maps)
        for step in range(n_maps + 2):
            if step < n_maps:
                s_vals[step] = scores(step)
            if 0 <= step - 1 < n_maps:
                hm = step - 1
                m_new, l_new, alpha, p = softmax(hm, s_vals.pop(hm))
                out[2 * hm], out[2 * hm + 1] = m_new, l_new
                sm_vals[hm] = (alpha, p)
            if 0 <= step - 2 < n_maps:
                accumulate(step - 2, *sm_vals.pop(step - 2))
        return tuple(out)

    for acc in acc_refs:
        acc[...] = jnp.zeros_like(acc)
    one = (jnp.full((1, t), -jnp.inf, F32), jnp.zeros((1, t), F32))
    carry = lax.fori_loop(0, qt, lambda kt, cr: block(kt, cr, False), one * n_maps)
    carry = block(qt, carry, True)
    for h in range(ATTN_HEADS):
        l0, l1 = carry[4 * h + 1], carry[4 * h + 3]
        o = acc_refs[2 * h][...] / l0 - lam * (acc_refs[2 * h + 1][...] / l1)
        y = o * lax.rsqrt(jnp.mean(o * o, axis=0, keepdims=True) + LN_EPS) * g_ref[...]
        o_ref[:, h * hd2:(h + 1) * hd2] = (y * (1.0 - LAMBDA_INIT)).T.astype(BF16)


def _attention(q, k, vt, bias_tiles, lq1, lk1, lq2, lk2, subln_g, *, bsz, seq):
    t = ATT_TILE
    nt = seq // t
    full = lambda a: pl.BlockSpec(a.shape, lambda b, j: (0,) * a.ndim)
    return pl.pallas_call(
        _attn_kernel,
        grid=(bsz, nt),
        in_specs=[pl.BlockSpec((t, QK_DIM), lambda b, j: (b * nt + j, 0)),
                  pl.BlockSpec((seq, QK_DIM), lambda b, j: (b, 0)),
                  pl.BlockSpec((1, nt, V_DIM, t), lambda b, j: (b, 0, 0, 0)),
                  full(bias_tiles), full(lq1), full(lk1), full(lq2), full(lk2), full(subln_g)],
        out_specs=pl.BlockSpec((t, V_DIM), lambda b, j: (b * nt + j, 0)),
        out_shape=jax.ShapeDtypeStruct((bsz * seq, V_DIM), BF16),
        scratch_shapes=[pltpu.VMEM((2 * ATTN_HEAD_DIM, t), F32) for _ in range(2 * ATTN_HEADS)],
        compiler_params=_params(("arbitrary", "arbitrary")),
    )(q, k, vt, bias_tiles, lq1, lk1, lq2, lk2, subln_g)


def _outproj_kernel(x_ref, yp_ref, ya_ref, w_ref, gate_ref, g_ref, b_ref, o_ref):
    mix = (jnp.dot(yp_ref[...], w_ref[0:POOL_DIM, :], preferred_element_type=F32)
           + jnp.dot(ya_ref[...], w_ref[POOL_DIM:, :], preferred_element_type=F32))
    z = ALPHA * x_ref[...] + gate_ref[0] * mix
    o_ref[...] = _layer_norm(z, g_ref[...], b_ref[...])


def _outproj(x2, yp, ya, w_out, gate1, ln_g, ln_b, *, seq, tm):
    n, d = x2.shape
    tps = seq // tm
    row = lambda w: pl.BlockSpec((tm, w), lambda i: (i, 0))
    full = lambda a: pl.BlockSpec(a.shape, lambda i: (0,) * a.ndim)
    return pl.pallas_call(
        _outproj_kernel,
        grid=(n // tm,),
        in_specs=[row(d), row(POOL_DIM), row(V_DIM), full(w_out),
                  pl.BlockSpec((1, 1, d), lambda i: (i // tps, 0, 0)), full(ln_g), full(ln_b)],
        out_specs=row(d),
        out_shape=jax.ShapeDtypeStruct((n, d), F32),
        compiler_params=_params(("arbitrary",)),
    )(x2, yp, ya, w_out, gate1, ln_g, ln_b)


def _route_kernel(x_ref, sc_ref, sh_ref, whi_ref, wlo_ref, rb_ref,
                  eidx_ref, rank_ref, wts_ref, cnt_ref, carry_ref, *, tr):
    i = pl.program_id(0)

    @pl.when(i == 0)
    def _():
        carry_ref[...] = jnp.zeros_like(carry_ref)

    h2 = x_ref[...] * (1.0 + sc_ref[0]) + sh_ref[0]
    hi = h2.astype(BF16)
    lo = (h2 - hi.astype(F32)).astype(BF16)
    logits = (jnp.dot(hi, whi_ref[...], preferred_element_type=F32)
              + jnp.dot(hi, wlo_ref[...], preferred_element_type=F32)
              + jnp.dot(lo, whi_ref[...], preferred_element_type=F32))
    scores = _sigmoid(logits)
    sel = scores + rb_ref[...]
    lane = lax.broadcasted_iota(jnp.int32, (tr, N_EXPERTS), 1)
    lane_f = lane.astype(F32)

    a1 = sel
    a2 = jnp.full_like(sel, -jnp.inf)
    s = 1
    while s < GROUP_SIZE:
        upper = (lane & s) != 0
        b1 = jnp.where(upper, pltpu.roll(a1, s, 1), pltpu.roll(a1, N_EXPERTS - s, 1))
        b2 = jnp.where(upper, pltpu.roll(a2, s, 1), pltpu.roll(a2, N_EXPERTS - s, 1))
        a1, a2 = jnp.maximum(a1, b1), jnp.maximum(jnp.minimum(a1, b1), jnp.maximum(a2, b2))
        s *= 2
    gscore = a1 + a2

    grp = lane // GROUP_SIZE
    beaten_by = jnp.zeros((tr, N_EXPERTS), jnp.int32)
    for kk in range(1, N_GROUPS):
        other = pltpu.roll(gscore, kk * GROUP_SIZE, 1)
        ogrp = (grp - kk) & (N_GROUPS - 1)
        wins = (other > gscore) | ((other == gscore) & (ogrp < grp))
        beaten_by = beaten_by + wins.astype(jnp.int32)
    cur = jnp.where(beaten_by < TOP_K_GROUPS, sel, -jnp.inf)

    picks, weights = [], []
    selmask = jnp.zeros((tr, N_EXPERTS), F32)
    for _ in range(TOP_K):
        mx = jnp.max(cur, axis=-1, keepdims=True)
        pick = jnp.min(jnp.where(cur == mx, lane_f, float(N_EXPERTS)), axis=-1, keepdims=True)
        onehot = lane_f == pick
        weights.append(jnp.sum(jnp.where(onehot, scores, 0.0), axis=-1, keepdims=True))
        cur = jnp.where(onehot, -jnp.inf, cur)
        selmask = jnp.where(onehot, 1.0, selmask)
        picks.append(pick)

    rr = lax.broadcasted_iota(jnp.int32, (tr, tr), 0)
    cc = lax.broadcasted_iota(jnp.int32, (tr, tr), 1)
    earlier = jnp.where(cc < rr, 1.0, 0.0).astype(BF16)
    rankmat = jnp.dot(earlier, selmask.astype(BF16), preferred_element_type=F32) + carry_ref[...]
    carry_ref[...] = carry_ref[...] + jnp.sum(selmask, axis=0, keepdims=True)
    cnt_ref[...] = carry_ref[...]

    wsum = weights[0]
    for wj in weights[1:]:
        wsum = wsum + wj
    lane8 = lax.broadcasted_iota(jnp.int32, (tr, TOP_K), 1)
    eidx = jnp.zeros((tr, TOP_K), jnp.int32)
    rank = jnp.zeros((tr, TOP_K), jnp.int32)
    wts = jnp.zeros((tr, TOP_K), F32)
    for j in range(TOP_K):
        rk = jnp.sum(jnp.where(lane_f == picks[j], rankmat, 0.0), axis=-1, keepdims=True)
        eidx = jnp.where(lane8 == j, picks[j].astype(jnp.int32), eidx)
        rank = jnp.where(lane8 == j, rk.astype(jnp.int32), rank)
        wts = jnp.where(lane8 == j, weights[j] / wsum * ROUTED_SCALE, wts)
    eidx_ref[...] = eidx
    rank_ref[...] = rank
    wts_ref[...] = wts


def _route(x1, scale2, shift2, wr_hi, wr_lo, router_bias, *, seq, tr):
    n, d = x1.shape
    tps = seq // tr
    mod_spec = pl.BlockSpec((1, 1, d), lambda i: (i // tps, 0, 0))
    full = lambda a: pl.BlockSpec(a.shape, lambda i: (0,) * a.ndim)
    k8 = pl.BlockSpec((tr, TOP_K), lambda i: (i, 0))
    return pl.pallas_call(
        functools.partial(_route_kernel, tr=tr),
        grid=(n // tr,),
        in_specs=[pl.BlockSpec((tr, d), lambda i: (i, 0)), mod_spec, mod_spec,
                  full(wr_hi), full(wr_lo), full(router_bias)],
        out_specs=[k8, k8, k8, pl.BlockSpec((1, N_EXPERTS), lambda i: (0, 0))],
        out_shape=[jax.ShapeDtypeStruct((n, TOP_K), jnp.int32),
                   jax.ShapeDtypeStruct((n, TOP_K), jnp.int32),
                   jax.ShapeDtypeStruct((n, TOP_K), F32),
                   jax.ShapeDtypeStruct((1, N_EXPERTS), F32)],
        scratch_shapes=[pltpu.VMEM((1, N_EXPERTS), F32)],
        compiler_params=_params(("arbitrary",)),
    )(x1, scale2, shift2, wr_hi, wr_lo, router_bias)


def _dispatch_kernel(x_ref, sc_ref, sh_ref, dest_ref, xs_in_ref, xs_ref, h2_ref, sem, *, td):
    del xs_in_ref
    h2_ref[...] = x_ref[...] * (1.0 + sc_ref[0]) + sh_ref[0]

    def issue(t, carry):
        for j in range(TOP_K):
            d = dest_ref[t * TOP_K + j]
            pltpu.make_async_copy(h2_ref.at[pl.ds(t, 1), :], xs_ref.at[pl.ds(d, 1), :], sem).start()
        return carry

    lax.fori_loop(0, td, issue, 0)
    for _ in range(TOP_K):
        pltpu.make_async_copy(h2_ref, xs_ref.at[pl.ds(0, td), :], sem).wait()


def _dispatch(x1, scale2, shift2, dest_flat, xs_zero, *, seq, td):
    n, d = x1.shape
    tps = seq // td
    mod_spec = pl.BlockSpec((1, 1, d), lambda i: (i // tps, 0, 0))
    return pl.pallas_call(
        functools.partial(_dispatch_kernel, td=td),
        grid=(n // td,),
        in_specs=[pl.BlockSpec((td, d), lambda i: (i, 0)), mod_spec, mod_spec,
                  pl.BlockSpec((td * TOP_K,), lambda i: (i,), memory_space=pltpu.SMEM),
                  pl.BlockSpec(memory_space=pl.ANY)],
        out_specs=pl.BlockSpec(memory_space=pl.ANY),
        out_shape=jax.ShapeDtypeStruct(xs_zero.shape, xs_zero.dtype),
        scratch_shapes=[pltpu.VMEM((td, d), F32), pltpu.SemaphoreType.DMA(())],
        input_output_aliases={4: 0},
        compiler_params=_params(("arbitrary",)),
    )(x1, scale2, shift2, dest_flat, xs_zero)


def _expert_kernel(be_ref, nused_ref, xs_ref, wg_ref, wu_ref, wd_ref, ys_ref, wgb, wub, wdb):
    b = pl.program_id(0)
    prev = be_ref[jnp.maximum(b - 1, 0)]

    @pl.when((b == 0) | (be_ref[b] != prev))
    def _():
        wgb[...] = wg_ref[0].astype(BF16)
        wub[...] = wu_ref[0].astype(BF16)
        wdb[...] = wd_ref[0].astype(BF16)

    @pl.when(b < nused_ref[0])
    def _():
        x = xs_ref[...].astype(BF16)
        g = jnp.dot(x, wgb[...], preferred_element_type=F32)
        u = jnp.dot(x, wub[...], preferred_element_type=F32)
        a = (_silu(g) * u).astype(BF16)
        ys_ref[...] = jnp.dot(a, wdb[...], preferred_element_type=F32)

    @pl.when(b >= nused_ref[0])
    def _():
        ys_ref[...] = jnp.zeros_like(ys_ref)


def _experts(block_e, n_used, xs, w_gate, w_up, w_down):
    p, d = xs.shape
    m = DISPATCH_BLOCK
    n_blocks = p // m
    f = w_gate.shape[-1]
    grid_spec = pltpu.PrefetchScalarGridSpec(
        num_scalar_prefetch=2,
        grid=(n_blocks,),
        in_specs=[pl.BlockSpec((m, d), lambda b, be, nu: (b, 0)),
                  pl.BlockSpec((1, d, f), lambda b, be, nu: (be[b], 0, 0)),
                  pl.BlockSpec((1, d, f), lambda b, be, nu: (be[b], 0, 0)),
                  pl.BlockSpec((1, f, d), lambda b, be, nu: (be[b], 0, 0))],
        out_specs=pl.BlockSpec((m, d), lambda b, be, nu: (b, 0)),
        scratch_shapes=[pltpu.VMEM((d, f), BF16), pltpu.VMEM((d, f), BF16), pltpu.VMEM((f, d), BF16)],
    )
    return pl.pallas_call(
        _expert_kernel,
        grid_spec=grid_spec,
        out_shape=jax.ShapeDtypeStruct((p, d), F32),
        compiler_params=_params(("arbitrary",)),
    )(block_e, n_used, xs, w_gate, w_up, w_down)


def _combine_kernel(x_ref, sc_ref, sh_ref, gate_ref, dest_ref, wts_ref, ys_ref,
                    wsg_ref, wsu_ref, wsd_ref, g_ref, b_ref, o_ref, gbuf, sem, *, tc):
    def issue(t, carry):
        for j in range(TOP_K):
            d = dest_ref[t * TOP_K + j]
            pltpu.make_async_copy(ys_ref.at[pl.ds(d, 1), :], gbuf.at[pl.ds(j * tc + t, 1), :], sem).start()
        return carry

    lax.fori_loop(0, tc, issue, 0)

    x = x_ref[...]
    hb = (x * (1.0 + sc_ref[0]) + sh_ref[0]).astype(BF16)
    sg = jnp.dot(hb, wsg_ref[...], preferred_element_type=F32)
    su = jnp.dot(hb, wsu_ref[...], preferred_element_type=F32)
    ffn = jnp.dot((_silu(sg) * su).astype(BF16), wsd_ref[...], preferred_element_type=F32)

    for j in range(TOP_K):
        pltpu.make_async_copy(ys_ref.at[pl.ds(0, tc), :], gbuf.at[pl.ds(j * tc, tc), :], sem).wait()
    wts = wts_ref[...]
    for j in range(TOP_K):
        ffn = ffn + wts[:, j:j + 1] * gbuf[j * tc:(j + 1) * tc, :]
    z = ALPHA * x + gate_ref[0] * ffn
    o_ref[...] = _layer_norm(z, g_ref[...], b_ref[...])


def _combine(x1, scale2, shift2, gate2, dest_flat, wts, ys, ws_gate, ws_up, ws_down, ln_g, ln_b, *, seq, tc):
    n, d = x1.shape
    tps = seq // tc
    mod_spec = pl.BlockSpec((1, 1, d), lambda i: (i // tps, 0, 0))
    full = lambda a: pl.BlockSpec(a.shape, lambda i: (0,) * a.ndim)
    return pl.pallas_call(
        functools.partial(_combine_kernel, tc=tc),
        grid=(n // tc,),
        in_specs=[pl.BlockSpec((tc, d), lambda i: (i, 0)), mod_spec, mod_spec, mod_spec,
                  pl.BlockSpec((tc * TOP_K,), lambda i: (i,), memory_space=pltpu.SMEM),
                  pl.BlockSpec((tc, TOP_K), lambda i: (i, 0)),
                  pl.BlockSpec(memory_space=pl.ANY),
                  full(ws_gate), full(ws_up), full(ws_down), full(ln_g), full(ln_b)],
        out_specs=pl.BlockSpec((tc, d), lambda i: (i, 0)),
        out_shape=jax.ShapeDtypeStruct((n, d), F32),
        scratch_shapes=[pltpu.VMEM((TOP_K * tc, d), F32), pltpu.SemaphoreType.DMA(())],
        compiler_params=_params(("arbitrary",)),
    )(x1, scale2, shift2, gate2, dest_flat, wts, ys, ws_gate, ws_up, ws_down, ln_g, ln_b)


def _layer(x, c, w_ada, b_ada, w_in, pool_w, pool_scale, lq1, lk1, lq2, lk2, subln_g, w_out,
           ln1_g, ln1_b, w_router, router_bias, w_gate, w_up, w_down, ws_gate, ws_up, ws_down,
           ln2_g, ln2_b, rel_table, *, tm=512, tr=256, td=256, tc=256):
    bsz, seq, d = x.shape
    n = bsz * seq
    x2 = x.reshape(n, d)
    row = lambda a: a.reshape(1, -1)

    mod = _modulation(c, w_ada, b_ada)
    shift1, scale1, gate1, shift2, scale2, gate2 = [
        mod[:, j * d:(j + 1) * d].reshape(bsz, 1, d) for j in range(6)]

    n_main = POOL_DIM + 2 * QK_DIM
    yp, q, k, vt = _inproj(x2, scale1, shift1, w_in[:, :n_main].astype(BF16), w_in[:, n_main:].T.astype(BF16),
                           pool_w.astype(BF16), row(pool_scale), seq=seq, tm=tm)
    bias_tiles = _bias_tiles(rel_table, seq // ATT_TILE)
    ya = _attention(q, k, vt, bias_tiles, row(lq1), row(lk1), row(lq2), row(lk2), subln_g.reshape(-1, 1),
                    bsz=bsz, seq=seq)
    x1 = _outproj(x2, yp, ya, w_out.astype(BF16), gate1, row(ln1_g), row(ln1_b), seq=seq, tm=tm)

    wr_hi = w_router.astype(BF16)
    wr_lo = (w_router - wr_hi.astype(F32)).astype(BF16)
    eidx, rank, wts, cnt = _route(x1, scale2, shift2, wr_hi, wr_lo, row(router_bias), seq=seq, tr=tr)

    m = DISPATCH_BLOCK
    counts = cnt[0].astype(jnp.int32)
    padded = (counts + m - 1) // m * m
    pends = jnp.cumsum(padded)
    pstarts = pends - padded
    n_blocks = -(-(n * TOP_K + N_EXPERTS * (m - 1)) // m)
    block_e = jnp.minimum(jnp.searchsorted(pends, jnp.arange(n_blocks, dtype=jnp.int32) * m, side='right'),
                          N_EXPERTS - 1).astype(jnp.int32)
    n_used = (pends[-1:] // m).astype(jnp.int32)
    dest_flat = (pstarts[eidx] + rank).reshape(n * TOP_K)

    xs = _dispatch(x1, scale2, shift2, dest_flat, jnp.zeros((n_blocks * m, d), F32), seq=seq, td=td)
    ys = _experts(block_e, n_used, xs, w_gate, w_up, w_down)
    out = _combine(x1, scale2, shift2, gate2, dest_flat, wts, ys,
                   ws_gate.astype(BF16), ws_up.astype(BF16), ws_down.astype(BF16),
                   row(ln2_g), row(ln2_b), seq=seq, tc=tc)
    return out.reshape(bsz, seq, d)


def kernel(x, c, w_ada, b_ada, w_in, pool_w, pool_scale, lambda_q1, lambda_k1, lambda_q2, lambda_k2,
           subln_g, w_out, ln1_g, ln1_b, w_router, router_bias, w_gate, w_up, w_down,
           ws_gate, ws_up, ws_down, ln2_g, ln2_b, rel_table):
    per_layer = (w_ada, b_ada, w_in, pool_w, pool_scale, lambda_q1, lambda_k1, lambda_q2, lambda_k2,
                 subln_g, w_out, ln1_g, ln1_b, w_router, router_bias, w_gate, w_up, w_down,
                 ws_gate, ws_up, ws_down, ln2_g, ln2_b)
    assert all(a.shape[0] == DEPTH == 1 for a in per_layer)
    return _layer(x, c, *[a.reshape(a.shape[1:]) for a in per_layer], rel_table)
```

```python
import functools
import math

import jax
import jax.numpy as jnp
from jax import lax
from jax.experimental import pallas as pl
from jax.experimental.pallas import tpu as pltpu

F32 = jnp.float32
BF16 = jnp.bfloat16

D_MODEL = 1024
CHUNK = 64
Q_BLOCK = 128
ATT_TILE = 256
POOL_DIM = 512
POOL_WINDOWS = (2, 4, 8, 16)
POOL_GROUP_DIM = 128
MAX_WINDOW = max(POOL_WINDOWS)
ATTN_HEADS = 4
ATTN_HEAD_DIM = 64
QK_DIM = 512
V_DIM = 512
IN_DIM = 2048
NUM_BUCKETS = 32
MAX_DISTANCE = 128
N_EXPERTS = 256
TOP_K = 8
N_GROUPS = 8
GROUP_SIZE = N_EXPERTS // N_GROUPS
TOP_K_GROUPS = 4
EXPERT_DIM = 256
ROUTED_SCALE = 2.5
DISPATCH_BLOCK = 256
DEPTH = 1
ALPHA = (2.0 * DEPTH) ** 0.25
LN_EPS = 1e-5
LAMBDA_INIT = 0.8 - 0.6 * math.exp(-0.3 * 0)

VMEM_LIMIT = 48 * 1024 * 1024


def _sigmoid(x):
    return 1.0 / (1.0 + jnp.exp(-x))


def _silu(x):
    return x * _sigmoid(x)


def _layer_norm(z, g, b):
    mu = jnp.mean(z, axis=-1, keepdims=True)
    zc = z - mu
    var = jnp.mean(zc * zc, axis=-1, keepdims=True)
    return zc * lax.rsqrt(var + LN_EPS) * g + b


def _params(sem=None):
    return pltpu.CompilerParams(dimension_semantics=sem, vmem_limit_bytes=VMEM_LIMIT)


def _mod_kernel(c_ref, w_ref, b_ref, o_ref):
    ca = _silu(c_ref[...])
    o_ref[...] = jnp.dot(ca, w_ref[...], preferred_element_type=F32,
                         precision=lax.Precision.HIGHEST) + b_ref[...]


def _modulation(c, w_ada, b_ada):
    bsz, d = c.shape
    n_out = w_ada.shape[1]
    return pl.pallas_call(
        _mod_kernel,
        grid=(n_out // d,),
        in_specs=[pl.BlockSpec((bsz, d), lambda j: (0, 0)),
                  pl.BlockSpec((d, d), lambda j: (0, j)),
                  pl.BlockSpec((1, d), lambda j: (0, j))],
        out_specs=pl.BlockSpec((bsz, d), lambda j: (0, j)),
        out_shape=jax.ShapeDtypeStruct((bsz, n_out), F32),
        compiler_params=_params(("arbitrary",)),
    )(c, w_ada, b_ada.reshape(1, n_out))


def _inproj_kernel(x_ref, sc_ref, sh_ref, w_ref, wvt_ref, pw_ref, ps_ref,
                   yp_ref, q_ref, k_ref, vt_ref, ext_ref, *, tm, seq):
    i = pl.program_id(0)
    tiles_per_seq = seq // tm
    it = i % tiles_per_seq
    h = x_ref[...] * (1.0 + sc_ref[0]) + sh_ref[0]
    hb = h.astype(BF16)
    proj = jnp.dot(hb, w_ref[...], preferred_element_type=F32)
    u = proj[:, :POOL_DIM]
    q_ref[...] = (proj[:, POOL_DIM:POOL_DIM + QK_DIM] * (ATTN_HEAD_DIM ** -0.5)).astype(BF16)
    k_ref[...] = proj[:, POOL_DIM + QK_DIM:POOL_DIM + 2 * QK_DIM].astype(BF16)
    vt = lax.dot_general(wvt_ref[...], hb, (((1,), (1,)), ((), ())), preferred_element_type=F32)
    for j in range(tm // ATT_TILE):
        vt_ref[0, j] = vt[:, j * ATT_TILE:(j + 1) * ATT_TILE].astype(BF16)

    @pl.when(it == 0)
    def _():
        ext_ref[0:MAX_WINDOW, :] = jnp.zeros((MAX_WINDOW, POOL_DIM), F32)

    ext_ref[MAX_WINDOW:MAX_WINDOW + tm, :] = u
    pos = (it * tm + lax.broadcasted_iota(jnp.int32, (tm, 1), 0) + 1).astype(F32)
    for g, w in enumerate(POOL_WINDOWS):
        c0, c1 = g * POOL_GROUP_DIM, (g + 1) * POOL_GROUP_DIM
        s = ext_ref[MAX_WINDOW:MAX_WINDOW + tm, c0:c1]
        for j in range(1, w):
            s = s + ext_ref[MAX_WINDOW - j:MAX_WINDOW - j + tm, c0:c1]
        pooled = s / jnp.minimum(pos, float(w)) - u[:, c0:c1]
        y = jnp.dot(pooled.astype(BF16), pw_ref[g], preferred_element_type=F32)
        yp_ref[:, c0:c1] = (y * ps_ref[:, c0:c1]).astype(BF16)
    ext_ref[0:MAX_WINDOW, :] = ext_ref[tm:tm + MAX_WINDOW, :]


def _inproj(x2, scale1, shift1, w_main, w_vt, pool_w, pool_scale, *, seq, tm):
    n, d = x2.shape
    assert n % tm == 0 and seq % tm == 0 and tm >= 2 * MAX_WINDOW and tm % ATT_TILE == 0
    tps = seq // tm
    tpt = tm // ATT_TILE
    mod_spec = pl.BlockSpec((1, 1, d), lambda i: (i // tps, 0, 0))
    row = lambda w: pl.BlockSpec((tm, w), lambda i: (i, 0))
    full = lambda a: pl.BlockSpec(a.shape, lambda i: (0,) * a.ndim)
    return pl.pallas_call(
        functools.partial(_inproj_kernel, tm=tm, seq=seq),
        grid=(n // tm,),
        in_specs=[row(d), mod_spec, mod_spec, full(w_main), full(w_vt), full(pool_w), full(pool_scale)],
        out_specs=[row(POOL_DIM), row(QK_DIM), row(QK_DIM),
                   pl.BlockSpec((1, tpt, V_DIM, ATT_TILE), lambda i: (i // tps, i % tps, 0, 0))],
        out_shape=[jax.ShapeDtypeStruct((n, POOL_DIM), BF16),
                   jax.ShapeDtypeStruct((n, QK_DIM), BF16),
                   jax.ShapeDtypeStruct((n, QK_DIM), BF16),
                   jax.ShapeDtypeStruct((n // seq, seq // ATT_TILE, V_DIM, ATT_TILE), BF16)],
        scratch_shapes=[pltpu.VMEM((tm + MAX_WINDOW, POOL_DIM), F32)],
        compiler_params=_params(("arbitrary",)),
    )(x2, scale1, shift1, w_main, w_vt, pool_w, pool_scale)


def _bias_kernel(tab_ref, o_ref):
    delta = pl.program_id(0)
    r = lax.broadcasted_iota(jnp.int32, (ATT_TILE, ATT_TILE), 0)
    c = lax.broadcasted_iota(jnp.int32, (ATT_TILE, ATT_TILE), 1)
    rel = r - c - delta * ATT_TILE
    half = NUM_BUCKETS // 2
    max_exact = half // 2
    ret = jnp.where(rel > 0, half, 0)
    n = jnp.abs(rel)
    nf = jnp.maximum(n, 1).astype(F32)
    large = max_exact + (jnp.log(nf / max_exact) / math.log(MAX_DISTANCE / max_exact)
                         * (half - max_exact)).astype(jnp.int32)
    large = jnp.minimum(large, half - 1)
    bucket = ret + jnp.where(n < max_exact, n, large)
    for h in range(ATTN_HEADS):
        acc = jnp.zeros((ATT_TILE, ATT_TILE), F32)
        for b in range(NUM_BUCKETS):
            acc = jnp.where(bucket == b, tab_ref[b, h], acc)
        o_ref[h, 0] = acc


def _bias_tiles(rel_table, n_tiles):
    return pl.pallas_call(
        _bias_kernel,
        grid=(n_tiles,),
        in_specs=[pl.BlockSpec(memory_space=pltpu.SMEM)],
        out_specs=pl.BlockSpec((ATTN_HEADS, 1, ATT_TILE, ATT_TILE), lambda dlt: (0, dlt, 0, 0)),
        out_shape=jax.ShapeDtypeStruct((ATTN_HEADS, n_tiles, ATT_TILE, ATT_TILE), F32),
        compiler_params=_params(("arbitrary",)),
    )(rel_table)


def _attn_kernel(q_ref, k_ref, vt_ref, bias_ref, lq1_ref, lk1_ref, lq2_ref, lk2_ref, g_ref, o_ref, *acc_refs):
    qt = pl.program_id(1)
    t = ATT_TILE
    n_maps = 2 * ATTN_HEADS
    lam = (jnp.exp(jnp.sum(lq1_ref[...] * lk1_ref[...], axis=-1, keepdims=True))
           - jnp.exp(jnp.sum(lq2_ref[...] * lk2_ref[...], axis=-1, keepdims=True))
           + LAMBDA_INIT)
    r = lax.broadcasted_iota(jnp.int32, (t, t), 0)
    c = lax.broadcasted_iota(jnp.int32, (t, t), 1)
    allowed = (r // CHUNK) <= (c // CHUNK)
    hd2 = 2 * ATTN_HEAD_DIM

    def block(kt, carry, diagonal):
        koff = pl.multiple_of(kt * t, t)

        def scores(hm):
            col = hm * ATTN_HEAD_DIM
            qh = q_ref[:, col:col + ATTN_HEAD_DIM]
            kh = k_ref[pl.ds(koff, t), col:col + ATTN_HEAD_DIM]
            s = lax.dot_general(kh, qh, (((1,), (1,)), ((), ())),
                                preferred_element_type=F32) + bias_ref[hm // 2, qt - kt]
            return jnp.where(allowed, s, -jnp.inf) if diagonal else s

        def softmax(hm, s):
            m_old, l_old = carry[2 * hm:2 * hm + 2]
            m_new = jnp.maximum(m_old, jnp.max(s, axis=0, keepdims=True))
            alpha = jnp.exp(m_old - m_new)
            p = jnp.exp(s - m_new)
            return m_new, alpha * l_old + jnp.sum(p, axis=0, keepdims=True), alpha, p.astype(BF16)

        def accumulate(hm, alpha, p):
            h = hm // 2
            vth = vt_ref[0, kt, h * hd2:(h + 1) * hd2, :]
            acc_refs[hm][...] = alpha * acc_refs[hm][...] + jnp.dot(vth, p, preferred_element_type=F32)

        s_vals, sm_vals, out = {}, {}, [None] * (2 * n_maps)
        for step in range(n_maps + 2):
            if step < n_maps:
                s_vals[step] = scores(step)
            if 0 <= step - 1 < n_maps:
                hm = step - 1
                m_new, l_new, alpha, p = softmax(hm, s_vals.pop(hm))
                out[2 * hm], out[2 * hm + 1] = m_new, l_new
                sm_vals[hm] = (alpha, p)
            if 0 <= step - 2 < n_maps:
                accumulate(step - 2, *sm_vals.pop(step - 2))
        return tuple(out)

    for acc in acc_refs:
        acc[...] = jnp.zeros_like(acc)
    one = (jnp.full((1, t), -jnp.inf, F32), jnp.zeros((1, t), F32))
    carry = lax.fori_loop(0, qt, lambda kt, cr: block(kt, cr, False), one * n_maps)
    carry = block(qt, carry, True)
    for h in range(ATTN_HEADS):
        l0, l1 = carry[4 * h + 1], carry[4 * h + 3]
        o = acc_refs[2 * h][...] / l0 - lam * (acc_refs[2 * h + 1][...] / l1)
        y = o * lax.rsqrt(jnp.mean(o * o, axis=0, keepdims=True) + LN_EPS) * g_ref[...]
        o_ref[:, h * hd2:(h + 1) * hd2] = (y * (1.0 - LAMBDA_INIT)).T.astype(BF16)


def _attention(q, k, vt, bias_tiles, lq1, lk1, lq2, lk2, subln_g, *, bsz, seq):
    t = ATT_TILE
    nt = seq // t
    full = lambda a: pl.BlockSpec(a.shape, lambda b, j: (0,) * a.ndim)
    return pl.pallas_call(
        _attn_kernel,
        grid=(bsz, nt),
        in_specs=[pl.BlockSpec((t, QK_DIM), lambda b, j: (b * nt + j, 0)),
                  pl.BlockSpec((seq, QK_DIM), lambda b, j: (b, 0)),
                  pl.BlockSpec((1, nt, V_DIM, t), lambda b, j: (b, 0, 0, 0)),
                  full(bias_tiles), full(lq1), full(lk1), full(lq2), full(lk2), full(subln_g)],
        out_specs=pl.BlockSpec((t, V_DIM), lambda b, j: (b * nt + j, 0)),
        out_shape=jax.ShapeDtypeStruct((bsz * seq, V_DIM), BF16),
        scratch_shapes=[pltpu.VMEM((2 * ATTN_HEAD_DIM, t), F32) for _ in range(2 * ATTN_HEADS)],
        compiler_params=_params(("arbitrary", "arbitrary")),
    )(q, k, vt, bias_tiles, lq1, lk1, lq2, lk2, subln_g)


def _outproj_kernel(x_ref, yp_ref, ya_ref, w_ref, gate_ref, g_ref, b_ref, o_ref):
    mix = (jnp.dot(yp_ref[...], w_ref[0:POOL_DIM, :], preferred_element_type=F32)
           + jnp.dot(ya_ref[...], w_ref[POOL_DIM:, :], preferred_element_type=F32))
    z = ALPHA * x_ref[...] + gate_ref[0] * mix
    o_ref[...] = _layer_norm(z, g_ref[...], b_ref[...])


def _outproj(x2, yp, ya, w_out, gate1, ln_g, ln_b, *, seq, tm):
    n, d = x2.shape
    tps = seq // tm
    row = lambda w: pl.BlockSpec((tm, w), lambda i: (i, 0))
    full = lambda a: pl.BlockSpec(a.shape, lambda i: (0,) * a.ndim)
    return pl.pallas_call(
        _outproj_kernel,
        grid=(n // tm,),
        in_specs=[row(d), row(POOL_DIM), row(V_DIM), full(w_out),
                  pl.BlockSpec((1, 1, d), lambda i: (i // tps, 0, 0)), full(ln_g), full(ln_b)],
        out_specs=row(d),
        out_shape=jax.ShapeDtypeStruct((n, d), F32),
        compiler_params=_params(("arbitrary",)),
    )(x2, yp, ya, w_out, gate1, ln_g, ln_b)


def _route_kernel(x_ref, sc_ref, sh_ref, whi_ref, wlo_ref, rb_ref,
                  eidx_ref, rank_ref, wts_ref, cnt_ref, carry_ref, *, tr):
    i = pl.program_id(0)

    @pl.when(i == 0)
    def _():
        carry_ref[...] = jnp.zeros_like(carry_ref)

    h2 = x_ref[...] * (1.0 + sc_ref[0]) + sh_ref[0]
    hi = h2.astype(BF16)
    lo = (h2 - hi.astype(F32)).astype(BF16)
    logits = (jnp.dot(hi, whi_ref[...], preferred_element_type=F32)
              + jnp.dot(hi, wlo_ref[...], preferred_element_type=F32)
              + jnp.dot(lo, whi_ref[...], preferred_element_type=F32))
    scores = _sigmoid(logits)
    sel = scores + rb_ref[...]
    lane = lax.broadcasted_iota(jnp.int32, (tr, N_EXPERTS), 1)
    lane_f = lane.astype(F32)

    a1 = sel
    a2 = jnp.full_like(sel, -jnp.inf)
    s = 1
    while s < GROUP_SIZE:
        upper = (lane & s) != 0
        b1 = jnp.where(upper, pltpu.roll(a1, s, 1), pltpu.roll(a1, N_EXPERTS - s, 1))
        b2 = jnp.where(upper, pltpu.roll(a2, s, 1), pltpu.roll(a2, N_EXPERTS - s, 1))
        a1, a2 = jnp.maximum(a1, b1), jnp.maximum(jnp.minimum(a1, b1), jnp.maximum(a2, b2))
        s *= 2
    gscore = a1 + a2

    grp = lane // GROUP_SIZE
    beaten_by = jnp.zeros((tr, N_EXPERTS), jnp.int32)
    for kk in range(1, N_GROUPS):
        other = pltpu.roll(gscore, kk * GROUP_SIZE, 1)
        ogrp = (grp - kk) & (N_GROUPS - 1)
        wins = (other > gscore) | ((other == gscore) & (ogrp < grp))
        beaten_by = beaten_by + wins.astype(jnp.int32)
    cur = jnp.where(beaten_by < TOP_K_GROUPS, sel, -jnp.inf)

    picks, weights = [], []
    selmask = jnp.zeros((tr, N_EXPERTS), F32)
    for _ in range(TOP_K):
        mx = jnp.max(cur, axis=-1, keepdims=True)
        pick = jnp.min(jnp.where(cur == mx, lane_f, float(N_EXPERTS)), axis=-1, keepdims=True)
        onehot = lane_f == pick
        weights.append(jnp.sum(jnp.where(onehot, scores, 0.0), axis=-1, keepdims=True))
        cur = jnp.where(onehot, -jnp.inf, cur)
        selmask = jnp.where(onehot, 1.0, selmask)
        picks.append(pick)

    rr = lax.broadcasted_iota(jnp.int32, (tr, tr), 0)
    cc = lax.broadcasted_iota(jnp.int32, (tr, tr), 1)
    earlier = jnp.where(cc < rr, 1.0, 0.0).astype(BF16)
    rankmat = jnp.dot(earlier, selmask.astype(BF16), preferred_element_type=F32) + carry_ref[...]
    carry_ref[...] = carry_ref[...] + jnp.sum(selmask, axis=0, keepdims=True)
    cnt_ref[...] = carry_ref[...]

    wsum = weights[0]
    for wj in weights[1:]:
        wsum = wsum + wj
    lane8 = lax.broadcasted_iota(jnp.int32, (tr, TOP_K), 1)
    eidx = jnp.zeros((tr, TOP_K), jnp.int32)
    rank = jnp.zeros((tr, TOP_K), jnp.int32)
    wts = jnp.zeros((tr, TOP_K), F32)
    for j in range(TOP_K):
        rk = jnp.sum(jnp.where(lane_f == picks[j], rankmat, 0.0), axis=-1, keepdims=True)
        eidx = jnp.where(lane8 == j, picks[j].astype(jnp.int32), eidx)
        rank = jnp.where(lane8 == j, rk.astype(jnp.int32), rank)
        wts = jnp.where(lane8 == j, weights[j] / wsum * ROUTED_SCALE, wts)
    eidx_ref[...] = eidx
    rank_ref[...] = rank
    wts_ref[...] = wts


def _route(x1, scale2, shift2, wr_hi, wr_lo, router_bias, *, seq, tr):
    n, d = x1.shape
    tps = seq // tr
    mod_spec = pl.BlockSpec((1, 1, d), lambda i: (i // tps, 0, 0))
    full = lambda a: pl.BlockSpec(a.shape, lambda i: (0,) * a.ndim)
    k8 = pl.BlockSpec((tr, TOP_K), lambda i: (i, 0))
    return pl.pallas_call(
        functools.partial(_route_kernel, tr=tr),
        grid=(n // tr,),
        in_specs=[pl.BlockSpec((tr, d), lambda i: (i, 0)), mod_spec, mod_spec,
                  full(wr_hi), full(wr_lo), full(router_bias)],
        out_specs=[k8, k8, k8, pl.BlockSpec((1, N_EXPERTS), lambda i: (0, 0))],
        out_shape=[jax.ShapeDtypeStruct((n, TOP_K), jnp.int32),
                   jax.ShapeDtypeStruct((n, TOP_K), jnp.int32),
                   jax.ShapeDtypeStruct((n, TOP_K), F32),
                   jax.ShapeDtypeStruct((1, N_EXPERTS), F32)],
        scratch_shapes=[pltpu.VMEM((1, N_EXPERTS), F32)],
        compiler_params=_params(("arbitrary",)),
    )(x1, scale2, shift2, wr_hi, wr_lo, router_bias)


def _zero_pad_rows(cnt_ref, pst_ref, nused_ref, zero_ref, xs_ref, sem):
    m = DISPATCH_BLOCK
    sub = 8

    def copies(e, wait):
        cnt = cnt_ref[e]
        n_pad = (-cnt) & (m - 1)
        start = pst_ref[e] + cnt
        head = n_pad & (sub - 1)

        def go(n_rows, pos):
            cp = pltpu.make_async_copy(zero_ref.at[pl.ds(0, n_rows), :], xs_ref.at[pl.ds(pos, n_rows), :], sem)
            cp.wait() if wait else cp.start()

        for r in range(sub - 1):
            pl.when(r < head)(functools.partial(go, 1, start + r))
        bit = m // 2
        while bit >= sub:
            pos = pl.multiple_of(start + head + (n_pad & ~(2 * bit - 1) & ~(sub - 1)), sub)
            pl.when((n_pad & bit) != 0)(functools.partial(go, bit, pos))
            bit //= 2

    def start_all(e, carry):
        copies(e, False)
        return carry

    def wait_all(e, carry):
        copies(e, True)
        return carry

    lax.fori_loop(0, N_EXPERTS, start_all, 0)
    lax.fori_loop(0, N_EXPERTS, wait_all, 0)

    half = m // 2
    n_halves = xs_ref.shape[0] // half

    def tail(wait):
        def body(i, carry):
            cp = pltpu.make_async_copy(zero_ref, xs_ref.at[pl.ds(pl.multiple_of(i * half, half), half), :], sem)
            cp.wait() if wait else cp.start()
            return carry
        return body

    first = 2 * nused_ref[0]
    lax.fori_loop(first, n_halves, tail(False), 0)
    lax.fori_loop(first, n_halves, tail(True), 0)


def _dispatch_kernel(x_ref, sc_ref, sh_ref, eidx_ref, rank_ref, pst_ref, cnt_ref, nused_ref, xs_ref,
                     h2_ref, zero_ref, sem, pad_sem, *, td):
    @pl.when(pl.program_id(0) == 0)
    def _():
        zero_ref[...] = jnp.zeros_like(zero_ref)
        _zero_pad_rows(cnt_ref, pst_ref, nused_ref, zero_ref, xs_ref, pad_sem)

    h2_ref[...] = x_ref[...] * (1.0 + sc_ref[0]) + sh_ref[0]

    def issue(t, carry):
        for j in range(TOP_K):
            i = t * TOP_K + j
            d = pst_ref[eidx_ref[i]] + rank_ref[i]
            pltpu.make_async_copy(h2_ref.at[pl.ds(t, 1), :], xs_ref.at[pl.ds(d, 1), :], sem).start()
        return carry

    lax.fori_loop(0, td, issue, 0)
    for _ in range(TOP_K):
        pltpu.make_async_copy(h2_ref, xs_ref.at[pl.ds(0, td), :], sem).wait()


def _dispatch(x1, scale2, shift2, eidx_flat, rank_flat, pstarts, counts, n_used, n_slots, *, seq, td):
    n, d = x1.shape
    tps = seq // td
    mod_spec = pl.BlockSpec((1, 1, d), lambda i: (i // tps, 0, 0))
    k_spec = pl.BlockSpec((td * TOP_K,), lambda i: (i,), memory_space=pltpu.SMEM)
    smem = pl.BlockSpec(memory_space=pltpu.SMEM)
    return pl.pallas_call(
        functools.partial(_dispatch_kernel, td=td),
        grid=(n // td,),
        in_specs=[pl.BlockSpec((td, d), lambda i: (i, 0)), mod_spec, mod_spec, k_spec, k_spec, smem, smem, smem],
        out_specs=pl.BlockSpec(memory_space=pl.ANY),
        out_shape=jax.ShapeDtypeStruct((n_slots, d), F32),
        scratch_shapes=[pltpu.VMEM((td, d), F32), pltpu.VMEM((DISPATCH_BLOCK // 2, d), F32),
                        pltpu.SemaphoreType.DMA(()), pltpu.SemaphoreType.DMA(())],
        compiler_params=_params(("arbitrary",)),
    )(x1, scale2, shift2, eidx_flat, rank_flat, pstarts, counts, n_used)


def _expert_kernel(be_ref, nused_ref, xs_ref, wg_ref, wu_ref, wd_ref, ys_ref, wgb, wub, wdb):
    b = pl.program_id(0)
    prev = be_ref[jnp.maximum(b - 1, 0)]

    @pl.when((b == 0) | (be_ref[b] != prev))
    def _():
        wgb[...] = wg_ref[0].astype(BF16)
        wub[...] = wu_ref[0].astype(BF16)
        wdb[...] = wd_ref[0].astype(BF16)

    @pl.when(b < nused_ref[0])
    def _():
        x = xs_ref[...].astype(BF16)
        g = jnp.dot(x, wgb[...], preferred_element_type=F32)
        u = jnp.dot(x, wub[...], preferred_element_type=F32)
        a = (_silu(g) * u).astype(BF16)
        ys_ref[...] = jnp.dot(a, wdb[...], preferred_element_type=F32)

    @pl.when(b >= nused_ref[0])
    def _():
        ys_ref[...] = jnp.zeros_like(ys_ref)


def _experts(block_e, n_used, xs, w_gate, w_up, w_down):
    p, d = xs.shape
    m = DISPATCH_BLOCK
    n_blocks = p // m
    f = w_gate.shape[-1]
    grid_spec = pltpu.PrefetchScalarGridSpec(
        num_scalar_prefetch=2,
        grid=(n_blocks,),
        in_specs=[pl.BlockSpec((m, d), lambda b, be, nu: (jnp.minimum(b, nu[0] - 1), 0)),
                  pl.BlockSpec((1, d, f), lambda b, be, nu: (be[b], 0, 0)),
                  pl.BlockSpec((1, d, f), lambda b, be, nu: (be[b], 0, 0)),
                  pl.BlockSpec((1, f, d), lambda b, be, nu: (be[b], 0, 0))],
        out_specs=pl.BlockSpec((m, d), lambda b, be, nu: (b, 0)),
        scratch_shapes=[pltpu.VMEM((d, f), BF16), pltpu.VMEM((d, f), BF16), pltpu.VMEM((f, d), BF16)],
    )
    return pl.pallas_call(
        _expert_kernel,
        grid_spec=grid_spec,
        out_shape=jax.ShapeDtypeStruct((p, d), F32),
        compiler_params=_params(("arbitrary",)),
    )(block_e, n_used, xs, w_gate, w_up, w_down)


def _combine_kernel(x_ref, sc_ref, sh_ref, gate_ref, eidx_ref, rank_ref, pst_ref, wts_ref, ys_ref,
                    wsg_ref, wsu_ref, wsd_ref, g_ref, b_ref, o_ref, *scratch, tc):
    gbufs, sem = scratch[:TOP_K], scratch[TOP_K]

    def issue(t, carry):
        for j in range(TOP_K):
            i = t * TOP_K + j
            d = pst_ref[eidx_ref[i]] + rank_ref[i]
            pltpu.make_async_copy(ys_ref.at[pl.ds(d, 1), :], gbufs[j].at[pl.ds(t, 1), :], sem).start()
        return carry

    lax.fori_loop(0, tc, issue, 0)

    x = x_ref[...]
    hb = (x * (1.0 + sc_ref[0]) + sh_ref[0]).astype(BF16)
    sg = jnp.dot(hb, wsg_ref[...], preferred_element_type=F32)
    su = jnp.dot(hb, wsu_ref[...], preferred_element_type=F32)
    ffn = jnp.dot((_silu(sg) * su).astype(BF16), wsd_ref[...], preferred_element_type=F32)

    for j in range(TOP_K):
        pltpu.make_async_copy(ys_ref.at[pl.ds(0, tc), :], gbufs[j], sem).wait()
    wts = wts_ref[...]
    for j in range(TOP_K):
        ffn = ffn + wts[:, j:j + 1] * gbufs[j][...]
    z = ALPHA * x + gate_ref[0] * ffn
    o_ref[...] = _layer_norm(z, g_ref[...], b_ref[...])


def _combine(x1, scale2, shift2, gate2, eidx_flat, rank_flat, pstarts, wts, ys,
             ws_gate, ws_up, ws_down, ln_g, ln_b, *, seq, tc):
    n, d = x1.shape
    tps = seq // tc
    mod_spec = pl.BlockSpec((1, 1, d), lambda i: (i // tps, 0, 0))
    full = lambda a: pl.BlockSpec(a.shape, lambda i: (0,) * a.ndim)
    k_spec = pl.BlockSpec((tc * TOP_K,), lambda i: (i,), memory_space=pltpu.SMEM)
    return pl.pallas_call(
        functools.partial(_combine_kernel, tc=tc),
        grid=(n // tc,),
        in_specs=[pl.BlockSpec((tc, d), lambda i: (i, 0)), mod_spec, mod_spec, mod_spec,
                  k_spec, k_spec, pl.BlockSpec(memory_space=pltpu.SMEM),
                  pl.BlockSpec((tc, TOP_K), lambda i: (i, 0)),
                  pl.BlockSpec(memory_space=pl.ANY),
                  full(ws_gate), full(ws_up), full(ws_down), full(ln_g), full(ln_b)],
        out_specs=pl.BlockSpec((tc, d), lambda i: (i, 0)),
        out_shape=jax.ShapeDtypeStruct((n, d), F32),
        scratch_shapes=[pltpu.VMEM((tc, d), F32) for _ in range(TOP_K)] + [pltpu.SemaphoreType.DMA(())],
        compiler_params=_params(("arbitrary",)),
    )(x1, scale2, shift2, gate2, eidx_flat, rank_flat, pstarts, wts, ys,
      ws_gate, ws_up, ws_down, ln_g, ln_b)


def _layer(x, c, w_ada, b_ada, w_in, pool_w, pool_scale, lq1, lk1, lq2, lk2, subln_g, w_out,
           ln1_g, ln1_b, w_router, router_bias, w_gate, w_up, w_down, ws_gate, ws_up, ws_down,
           ln2_g, ln2_b, rel_table, *, tm=512, tr=256, td=256, tc=256):
    bsz, seq, d = x.shape
    n = bsz * seq
    x2 = x.reshape(n, d)
    row = lambda a: a.reshape(1, -1)

    mod = _modulation(c, w_ada, b_ada)
    shift1, scale1, gate1, shift2, scale2, gate2 = [
        mod[:, j * d:(j + 1) * d].reshape(bsz, 1, d) for j in range(6)]

    n_main = POOL_DIM + 2 * QK_DIM
    yp, q, k, vt = _inproj(x2, scale1, shift1, w_in[:, :n_main].astype(BF16), w_in[:, n_main:].T.astype(BF16),
                           pool_w.astype(BF16), row(pool_scale), seq=seq, tm=tm)
    bias_tiles = _bias_tiles(rel_table, seq // ATT_TILE)
    ya = _attention(q, k, vt, bias_tiles, row(lq1), row(lk1), row(lq2), row(lk2), subln_g.reshape(-1, 1),
                    bsz=bsz, seq=seq)
    x1 = _outproj(x2, yp, ya, w_out.astype(BF16), gate1, row(ln1_g), row(ln1_b), seq=seq, tm=tm)

    wr_hi = w_router.astype(BF16)
    wr_lo = (w_router - wr_hi.astype(F32)).astype(BF16)
    eidx, rank, wts, cnt = _route(x1, scale2, shift2, wr_hi, wr_lo, row(router_bias), seq=seq, tr=tr)

    m = DISPATCH_BLOCK
    counts = cnt[0].astype(jnp.int32)
    padded = (counts + m - 1) // m * m
    pends = jnp.cumsum(padded)
    pstarts = pends - padded
    n_blocks = -(-(n * TOP_K + N_EXPERTS * (m - 1)) // m)
    block_starts = jnp.arange(n_blocks, dtype=jnp.int32) * m
    block_e = jnp.minimum(jnp.sum((pends[None, :] <= block_starts[:, None]).astype(jnp.int32), axis=1),
                          N_EXPERTS - 1)
    n_used = (pends[-1:] // m).astype(jnp.int32)
    eidx_flat, rank_flat = eidx.reshape(n * TOP_K), rank.reshape(n * TOP_K)

    xs = _dispatch(x1, scale2, shift2, eidx_flat, rank_flat, pstarts, counts, n_used, n_blocks * m,
                   seq=seq, td=td)
    ys = _experts(block_e, n_used, xs, w_gate, w_up, w_down)
    out = _combine(x1, scale2, shift2, gate2, eidx_flat, rank_flat, pstarts, wts, ys,
                   ws_gate.astype(BF16), ws_up.astype(BF16), ws_down.astype(BF16),
                   row(ln2_g), row(ln2_b), seq=seq, tc=tc)
    return out.reshape(bsz, seq, d)


def kernel(x, c, w_ada, b_ada, w_in, pool_w, pool_scale, lambda_q1, lambda_k1, lambda_q2, lambda_k2,
           subln_g, w_out, ln1_g, ln1_b, w_router, router_bias, w_gate, w_up, w_down,
           ws_gate, ws_up, ws_down, ln2_g, ln2_b, rel_table):
    per_layer = (w_ada, b_ada, w_in, pool_w, pool_scale, lambda_q1, lambda_k1, lambda_q2, lambda_k2,
                 subln_g, w_out, ln1_g, ln1_b, w_router, router_bias, w_gate, w_up, w_down,
                 ws_gate, ws_up, ws_down, ln2_g, ln2_b)
    assert all(a.shape[0] == DEPTH == 1 for a in per_layer)
    return _layer(x, c, *[a.reshape(a.shape[1:]) for a in per_layer], rel_table)
```

```python
import functools
import math

import jax
import jax.numpy as jnp
from jax import lax
from jax.experimental import pallas as pl
from jax.experimental.pallas import tpu as pltpu

F32 = jnp.float32
BF16 = jnp.bfloat16
U32 = jnp.uint32
LANES = 128
SLOT_TABLE_RADIX = 64

D_MODEL = 1024
CHUNK = 64
Q_BLOCK = 128
ATT_TILE = 256
POOL_DIM = 512
POOL_WINDOWS = (2, 4, 8, 16)
POOL_GROUP_DIM = 128
MAX_WINDOW = max(POOL_WINDOWS)
ATTN_HEADS = 4
ATTN_HEAD_DIM = 64
QK_DIM = 512
V_DIM = 512
IN_DIM = 2048
NUM_BUCKETS = 32
MAX_DISTANCE = 128
N_EXPERTS = 256
TOP_K = 8
N_GROUPS = 8
GROUP_SIZE = N_EXPERTS // N_GROUPS
TOP_K_GROUPS = 4
EXPERT_DIM = 256
ROUTED_SCALE = 2.5
DISPATCH_BLOCK = 256
DEPTH = 1
ALPHA = (2.0 * DEPTH) ** 0.25
LN_EPS = 1e-5
LAMBDA_INIT = 0.8 - 0.6 * math.exp(-0.3 * 0)

VMEM_LIMIT = 48 * 1024 * 1024


def _sigmoid(x):
    return 1.0 / (1.0 + jnp.exp(-x))


def _silu(x):
    return x * _sigmoid(x)


def _layer_norm(z, g, b):
    mu = jnp.mean(z, axis=-1, keepdims=True)
    zc = z - mu
    var = jnp.mean(zc * zc, axis=-1, keepdims=True)
    return zc * lax.rsqrt(var + LN_EPS) * g + b


def _params(sem=None):
    return pltpu.CompilerParams(dimension_semantics=sem, vmem_limit_bytes=VMEM_LIMIT)


def _mod_kernel(c_ref, w_ref, b_ref, o_ref):
    ca = _silu(c_ref[...])
    o_ref[...] = jnp.dot(ca, w_ref[...], preferred_element_type=F32,
                         precision=lax.Precision.HIGHEST) + b_ref[...]


def _modulation(c, w_ada, b_ada):
    bsz, d = c.shape
    n_out = w_ada.shape[1]
    return pl.pallas_call(
        _mod_kernel,
        grid=(n_out // d,),
        in_specs=[pl.BlockSpec((bsz, d), lambda j: (0, 0)),
                  pl.BlockSpec((d, d), lambda j: (0, j)),
                  pl.BlockSpec((1, d), lambda j: (0, j))],
        out_specs=pl.BlockSpec((bsz, d), lambda j: (0, j)),
        out_shape=jax.ShapeDtypeStruct((bsz, n_out), F32),
        compiler_params=_params(("arbitrary",)),
    )(c, w_ada, b_ada.reshape(1, n_out))


def _inproj_kernel(x_ref, sc_ref, sh_ref, w_ref, wvt_ref, pw_ref, ps_ref,
                   yp_ref, q_ref, k_ref, vt_ref, ext_ref, *, tm, seq):
    i = pl.program_id(0)
    tiles_per_seq = seq // tm
    it = i % tiles_per_seq
    h = x_ref[...] * (1.0 + sc_ref[0]) + sh_ref[0]
    hb = h.astype(BF16)
    proj = jnp.dot(hb, w_ref[...], preferred_element_type=F32)
    u = proj[:, :POOL_DIM]
    q_ref[...] = (proj[:, POOL_DIM:POOL_DIM + QK_DIM] * (ATTN_HEAD_DIM ** -0.5)).astype(BF16)
    k_ref[...] = proj[:, POOL_DIM + QK_DIM:POOL_DIM + 2 * QK_DIM].astype(BF16)
    vt = lax.dot_general(wvt_ref[...], hb, (((1,), (1,)), ((), ())), preferred_element_type=F32)
    for j in range(tm // ATT_TILE):
        vt_ref[0, j] = vt[:, j * ATT_TILE:(j + 1) * ATT_TILE].astype(BF16)

    @pl.when(it == 0)
    def _():
        ext_ref[0:MAX_WINDOW, :] = jnp.zeros((MAX_WINDOW, POOL_DIM), F32)

    ext_ref[MAX_WINDOW:MAX_WINDOW + tm, :] = u
    pos = (it * tm + lax.broadcasted_iota(jnp.int32, (tm, 1), 0) + 1).astype(F32)
    for g, w in enumerate(POOL_WINDOWS):
        c0, c1 = g * POOL_GROUP_DIM, (g + 1) * POOL_GROUP_DIM
        s = ext_ref[MAX_WINDOW:MAX_WINDOW + tm, c0:c1]
        for j in range(1, w):
            s = s + ext_ref[MAX_WINDOW - j:MAX_WINDOW - j + tm, c0:c1]
        pooled = s / jnp.minimum(pos, float(w)) - u[:, c0:c1]
        y = jnp.dot(pooled.astype(BF16), pw_ref[g], preferred_element_type=F32)
        yp_ref[:, c0:c1] = (y * ps_ref[:, c0:c1]).astype(BF16)
    ext_ref[0:MAX_WINDOW, :] = ext_ref[tm:tm + MAX_WINDOW, :]


def _inproj(x2, scale1, shift1, w_main, w_vt, pool_w, pool_scale, *, seq, tm):
    n, d = x2.shape
    assert n % tm == 0 and seq % tm == 0 and tm >= 2 * MAX_WINDOW and tm % ATT_TILE == 0
    tps = seq // tm
    tpt = tm // ATT_TILE
    mod_spec = pl.BlockSpec((1, 1, d), lambda i: (i // tps, 0, 0))
    row = lambda w: pl.BlockSpec((tm, w), lambda i: (i, 0))
    full = lambda a: pl.BlockSpec(a.shape, lambda i: (0,) * a.ndim)
    return pl.pallas_call(
        functools.partial(_inproj_kernel, tm=tm, seq=seq),
        grid=(n // tm,),
        in_specs=[row(d), mod_spec, mod_spec, full(w_main), full(w_vt), full(pool_w), full(pool_scale)],
        out_specs=[row(POOL_DIM), row(QK_DIM), row(QK_DIM),
                   pl.BlockSpec((1, tpt, V_DIM, ATT_TILE), lambda i: (i // tps, i % tps, 0, 0))],
        out_shape=[jax.ShapeDtypeStruct((n, POOL_DIM), BF16),
                   jax.ShapeDtypeStruct((n, QK_DIM), BF16),
                   jax.ShapeDtypeStruct((n, QK_DIM), BF16),
                   jax.ShapeDtypeStruct((n // seq, seq // ATT_TILE, V_DIM, ATT_TILE), BF16)],
        scratch_shapes=[pltpu.VMEM((tm + MAX_WINDOW, POOL_DIM), F32)],
        compiler_params=_params(("arbitrary",)),
    )(x2, scale1, shift1, w_main, w_vt, pool_w, pool_scale)


def _bias_kernel(tab_ref, o_ref):
    delta = pl.program_id(0)
    r = lax.broadcasted_iota(jnp.int32, (ATT_TILE, ATT_TILE), 0)
    c = lax.broadcasted_iota(jnp.int32, (ATT_TILE, ATT_TILE), 1)
    rel = r - c - delta * ATT_TILE
    half = NUM_BUCKETS // 2
    max_exact = half // 2
    ret = jnp.where(rel > 0, half, 0)
    n = jnp.abs(rel)
    nf = jnp.maximum(n, 1).astype(F32)
    large = max_exact + (jnp.log(nf / max_exact) / math.log(MAX_DISTANCE / max_exact)
                         * (half - max_exact)).astype(jnp.int32)
    large = jnp.minimum(large, half - 1)
    bucket = ret + jnp.where(n < max_exact, n, large)
    for h in range(ATTN_HEADS):
        acc = jnp.zeros((ATT_TILE, ATT_TILE), F32)
        for b in range(NUM_BUCKETS):
            acc = jnp.where(bucket == b, tab_ref[b, h], acc)
        o_ref[h, 0] = acc


def _bias_tiles(rel_table, n_tiles):
    return pl.pallas_call(
        _bias_kernel,
        grid=(n_tiles,),
        in_specs=[pl.BlockSpec(memory_space=pltpu.SMEM)],
        out_specs=pl.BlockSpec((ATTN_HEADS, 1, ATT_TILE, ATT_TILE), lambda dlt: (0, dlt, 0, 0)),
        out_shape=jax.ShapeDtypeStruct((ATTN_HEADS, n_tiles, ATT_TILE, ATT_TILE), F32),
        compiler_params=_params(("arbitrary",)),
    )(rel_table)


def _attn_kernel(q_ref, k_ref, vt_ref, bias_ref, lq1_ref, lk1_ref, lq2_ref, lk2_ref, g_ref, o_ref, *acc_refs):
    qt = pl.program_id(1)
    t = ATT_TILE
    n_maps = 2 * ATTN_HEADS
    lam = (jnp.exp(jnp.sum(lq1_ref[...] * lk1_ref[...], axis=-1, keepdims=True))
           - jnp.exp(jnp.sum(lq2_ref[...] * lk2_ref[...], axis=-1, keepdims=True))
           + LAMBDA_INIT)
    r = lax.broadcasted_iota(jnp.int32, (t, t), 0)
    c = lax.broadcasted_iota(jnp.int32, (t, t), 1)
    allowed = (r // CHUNK) <= (c // CHUNK)
    hd2 = 2 * ATTN_HEAD_DIM

    def block(kt, carry, diagonal):
        koff = pl.multiple_of(kt * t, t)

        def scores(hm):
            col = hm * ATTN_HEAD_DIM
            qh = q_ref[:, col:col + ATTN_HEAD_DIM]
            kh = k_ref[pl.ds(koff, t), col:col + ATTN_HEAD_DIM]
            s = lax.dot_general(kh, qh, (((1,), (1,)), ((), ())),
                                preferred_element_type=F32) + bias_ref[hm // 2, qt - kt]
            return jnp.where(allowed, s, -jnp.inf) if diagonal else s

        def softmax(hm, s):
            m_old, l_old = carry[2 * hm:2 * hm + 2]
            m_new = jnp.maximum(m_old, jnp.max(s, axis=0, keepdims=True))
            alpha = jnp.exp(m_old - m_new)
            p = jnp.exp(s - m_new)
            return m_new, alpha * l_old + jnp.sum(p, axis=0, keepdims=True), alpha, p.astype(BF16)

        def accumulate(hm, alpha, p):
            h = hm // 2
            vth = vt_ref[0, kt, h * hd2:(h + 1) * hd2, :]
            acc_refs[hm][...] = alpha * acc_refs[hm][...] + jnp.dot(vth, p, preferred_element_type=F32)

        s_vals, sm_vals, out = {}, {}, [None] * (2 * n_maps)
        for step in range(n_maps + 2):
            if step < n_maps:
                s_vals[step] = scores(step)
            if 0 <= step - 1 < n_maps:
                hm = step - 1
                m_new, l_new, alpha, p = softmax(hm, s_vals.pop(hm))
                out[2 * hm], out[2 * hm + 1] = m_new, l_new
                sm_vals[hm] = (alpha, p)
            if 0 <= step - 2 < n_maps:
                accumulate(step - 2, *sm_vals.pop(step - 2))
        return tuple(out)

    for acc in acc_refs:
        acc[...] = jnp.zeros_like(acc)
    one = (jnp.full((1, t), -jnp.inf, F32), jnp.zeros((1, t), F32))
    carry = lax.fori_loop(0, qt, lambda kt, cr: block(kt, cr, False), one * n_maps)
    carry = block(qt, carry, True)
    for h in range(ATTN_HEADS):
        l0, l1 = carry[4 * h + 1], carry[4 * h + 3]
        o = acc_refs[2 * h][...] / l0 - lam * (acc_refs[2 * h + 1][...] / l1)
        y = o * lax.rsqrt(jnp.mean(o * o, axis=0, keepdims=True) + LN_EPS) * g_ref[...]
        o_ref[:, h * hd2:(h + 1) * hd2] = (y * (1.0 - LAMBDA_INIT)).T.astype(BF16)


def _attention(q, k, vt, bias_tiles, lq1, lk1, lq2, lk2, subln_g, *, bsz, seq):
    t = ATT_TILE
    nt = seq // t
    full = lambda a: pl.BlockSpec(a.shape, lambda b, j: (0,) * a.ndim)
    return pl.pallas_call(
        _attn_kernel,
        grid=(bsz, nt),
        in_specs=[pl.BlockSpec((t, QK_DIM), lambda b, j: (b * nt + j, 0)),
                  pl.BlockSpec((seq, QK_DIM), lambda b, j: (b, 0)),
                  pl.BlockSpec((1, nt, V_DIM, t), lambda b, j: (b, 0, 0, 0)),
                  full(bias_tiles), full(lq1), full(lk1), full(lq2), full(lk2), full(subln_g)],
        out_specs=pl.BlockSpec((t, V_DIM), lambda b, j: (b * nt + j, 0)),
        out_shape=jax.ShapeDtypeStruct((bsz * seq, V_DIM), BF16),
        scratch_shapes=[pltpu.VMEM((2 * ATTN_HEAD_DIM, t), F32) for _ in range(2 * ATTN_HEADS)],
        compiler_params=_params(("arbitrary", "arbitrary")),
    )(q, k, vt, bias_tiles, lq1, lk1, lq2, lk2, subln_g)


def _outproj_kernel(x_ref, yp_ref, ya_ref, w_ref, gate_ref, g_ref, b_ref, o_ref):
    mix = (jnp.dot(yp_ref[...], w_ref[0:POOL_DIM, :], preferred_element_type=F32)
           + jnp.dot(ya_ref[...], w_ref[POOL_DIM:, :], preferred_element_type=F32))
    z = ALPHA * x_ref[...] + gate_ref[0] * mix
    o_ref[...] = _layer_norm(z, g_ref[...], b_ref[...])


def _outproj(x2, yp, ya, w_out, gate1, ln_g, ln_b, *, seq, tm):
    n, d = x2.shape
    tps = seq // tm
    row = lambda w: pl.BlockSpec((tm, w), lambda i: (i, 0))
    full = lambda a: pl.BlockSpec(a.shape, lambda i: (0,) * a.ndim)
    return pl.pallas_call(
        _outproj_kernel,
        grid=(n // tm,),
        in_specs=[row(d), row(POOL_DIM), row(V_DIM), full(w_out),
                  pl.BlockSpec((1, 1, d), lambda i: (i // tps, 0, 0)), full(ln_g), full(ln_b)],
        out_specs=row(d),
        out_shape=jax.ShapeDtypeStruct((n, d), F32),
        compiler_params=_params(("arbitrary",)),
    )(x2, yp, ya, w_out, gate1, ln_g, ln_b)


def _route_kernel(x_ref, sc_ref, sh_ref, whi_ref, wlo_ref, rb_ref,
                  eidx_ref, rank_ref, wts_ref, cnt_ref, carry_ref, *, tr):
    i = pl.program_id(0)

    @pl.when(i == 0)
    def _():
        carry_ref[...] = jnp.zeros_like(carry_ref)

    h2 = x_ref[...] * (1.0 + sc_ref[0]) + sh_ref[0]
    hi = h2.astype(BF16)
    lo = (h2 - hi.astype(F32)).astype(BF16)
    logits = (jnp.dot(hi, whi_ref[...], preferred_element_type=F32)
              + jnp.dot(hi, wlo_ref[...], preferred_element_type=F32)
              + jnp.dot(lo, whi_ref[...], preferred_element_type=F32))
    scores = _sigmoid(logits)
    sel = scores + rb_ref[...]
    lane = lax.broadcasted_iota(jnp.int32, (tr, N_EXPERTS), 1)
    lane_f = lane.astype(F32)

    a1 = sel
    a2 = jnp.full_like(sel, -jnp.inf)
    s = 1
    while s < GROUP_SIZE:
        upper = (lane & s) != 0
        b1 = jnp.where(upper, pltpu.roll(a1, s, 1), pltpu.roll(a1, N_EXPERTS - s, 1))
        b2 = jnp.where(upper, pltpu.roll(a2, s, 1), pltpu.roll(a2, N_EXPERTS - s, 1))
        a1, a2 = jnp.maximum(a1, b1), jnp.maximum(jnp.minimum(a1, b1), jnp.maximum(a2, b2))
        s *= 2
    gscore = a1 + a2

    grp = lane // GROUP_SIZE
    beaten_by = jnp.zeros((tr, N_EXPERTS), jnp.int32)
    for kk in range(1, N_GROUPS):
        other = pltpu.roll(gscore, kk * GROUP_SIZE, 1)
        ogrp = (grp - kk) & (N_GROUPS - 1)
        wins = (other > gscore) | ((other == gscore) & (ogrp < grp))
        beaten_by = beaten_by + wins.astype(jnp.int32)
    cur = jnp.where(beaten_by < TOP_K_GROUPS, sel, -jnp.inf)

    picks, weights = [], []
    selmask = jnp.zeros((tr, N_EXPERTS), F32)
    for _ in range(TOP_K):
        mx = jnp.max(cur, axis=-1, keepdims=True)
        pick = jnp.min(jnp.where(cur == mx, lane_f, float(N_EXPERTS)), axis=-1, keepdims=True)
        onehot = lane_f == pick
        weights.append(jnp.sum(jnp.where(onehot, scores, 0.0), axis=-1, keepdims=True))
        cur = jnp.where(onehot, -jnp.inf, cur)
        selmask = jnp.where(onehot, 1.0, selmask)
        picks.append(pick)

    rr = lax.broadcasted_iota(jnp.int32, (tr, tr), 0)
    cc = lax.broadcasted_iota(jnp.int32, (tr, tr), 1)
    earlier = jnp.where(cc < rr, 1.0, 0.0).astype(BF16)
    rankmat = jnp.dot(earlier, selmask.astype(BF16), preferred_element_type=F32) + carry_ref[...]
    carry_ref[...] = carry_ref[...] + jnp.sum(selmask, axis=0, keepdims=True)
    cnt_ref[...] = carry_ref[...]

    wsum = weights[0]
    for wj in weights[1:]:
        wsum = wsum + wj
    lane8 = lax.broadcasted_iota(jnp.int32, (tr, TOP_K), 1)
    eidx = jnp.zeros((tr, TOP_K), jnp.int32)
    rank = jnp.zeros((tr, TOP_K), jnp.int32)
    wts = jnp.zeros((tr, TOP_K), F32)
    for j in range(TOP_K):
        rk = jnp.sum(jnp.where(lane_f == picks[j], rankmat, 0.0), axis=-1, keepdims=True)
        eidx = jnp.where(lane8 == j, picks[j].astype(jnp.int32), eidx)
        rank = jnp.where(lane8 == j, rk.astype(jnp.int32), rank)
        wts = jnp.where(lane8 == j, weights[j] / wsum * ROUTED_SCALE, wts)
    eidx_ref[...] = eidx
    rank_ref[...] = rank
    wts_ref[...] = wts


def _route(x1, scale2, shift2, wr_hi, wr_lo, router_bias, *, seq, tr):
    n, d = x1.shape
    tps = seq // tr
    mod_spec = pl.BlockSpec((1, 1, d), lambda i: (i // tps, 0, 0))
    full = lambda a: pl.BlockSpec(a.shape, lambda i: (0,) * a.ndim)
    k8 = pl.BlockSpec((tr, TOP_K), lambda i: (i, 0))
    return pl.pallas_call(
        functools.partial(_route_kernel, tr=tr),
        grid=(n // tr,),
        in_specs=[pl.BlockSpec((tr, d), lambda i: (i, 0)), mod_spec, mod_spec,
                  full(wr_hi), full(wr_lo), full(router_bias)],
        out_specs=[k8, k8, k8, pl.BlockSpec((1, N_EXPERTS), lambda i: (0, 0))],
        out_shape=[jax.ShapeDtypeStruct((n, TOP_K), jnp.int32),
                   jax.ShapeDtypeStruct((n, TOP_K), jnp.int32),
                   jax.ShapeDtypeStruct((n, TOP_K), F32),
                   jax.ShapeDtypeStruct((1, N_EXPERTS), F32)],
        scratch_shapes=[pltpu.VMEM((1, N_EXPERTS), F32)],
        compiler_params=_params(("arbitrary",)),
    )(x1, scale2, shift2, wr_hi, wr_lo, router_bias)


def _zero_pad_rows(cnt_ref, pst_ref, nused_ref, zero_ref, xs_ref, sem):
    m = DISPATCH_BLOCK
    sub = 8

    def copies(e, wait):
        cnt = cnt_ref[e]
        n_pad = (-cnt) & (m - 1)
        start = pst_ref[e] + cnt
        head = n_pad & (sub - 1)

        def go(n_rows, pos):
            cp = pltpu.make_async_copy(zero_ref.at[pl.ds(0, n_rows), :], xs_ref.at[pl.ds(pos, n_rows), :], sem)
            cp.wait() if wait else cp.start()

        for r in range(sub - 1):
            pl.when(r < head)(functools.partial(go, 1, start + r))
        bit = m // 2
        while bit >= sub:
            pos = pl.multiple_of(start + head + (n_pad & ~(2 * bit - 1) & ~(sub - 1)), sub)
            pl.when((n_pad & bit) != 0)(functools.partial(go, bit, pos))
            bit //= 2

    def start_all(e, carry):
        copies(e, False)
        return carry

    def wait_all(e, carry):
        copies(e, True)
        return carry

    lax.fori_loop(0, N_EXPERTS, start_all, 0)
    lax.fori_loop(0, N_EXPERTS, wait_all, 0)

    half = m // 2
    n_halves = xs_ref.shape[0] // half

    def tail(wait):
        def body(i, carry):
            cp = pltpu.make_async_copy(zero_ref, xs_ref.at[pl.ds(pl.multiple_of(i * half, half), half), :], sem)
            cp.wait() if wait else cp.start()
            return carry
        return body

    first = 2 * nused_ref[0]
    lax.fori_loop(first, n_halves, tail(False), 0)
    lax.fori_loop(first, n_halves, tail(True), 0)


def _pack_bf16_halves(x):
    w = x.shape[1] // 2
    lo = pltpu.bitcast(x[:, :w].astype(BF16).astype(F32), U32) >> 16
    hi = pltpu.bitcast(x[:, w:].astype(BF16).astype(F32), U32) & jnp.uint32(0xFFFF0000)
    return lo | hi


def _unpack_bf16_halves(p):
    return pltpu.bitcast(p << 16, F32), pltpu.bitcast(p & jnp.uint32(0xFFFF0000), F32)


def _dispatch_kernel(x_ref, sc_ref, sh_ref, dest_ref, pst_ref, cnt_ref, nused_ref, xs_ref,
                     h2_ref, zero_ref, sem, pad_sem, *, td):
    @pl.when(pl.program_id(0) == 0)
    def _():
        zero_ref[...] = jnp.zeros_like(zero_ref)
        _zero_pad_rows(cnt_ref, pst_ref, nused_ref, zero_ref, xs_ref, pad_sem)

    h2_ref[...] = _pack_bf16_halves(x_ref[...] * (1.0 + sc_ref[0]) + sh_ref[0])

    def issue(t, carry):
        for j in range(TOP_K):
            d = dest_ref[t * TOP_K + j]
            pltpu.make_async_copy(h2_ref.at[pl.ds(t, 1), :], xs_ref.at[pl.ds(d, 1), :], sem).start()
        return carry

    lax.fori_loop(0, td, issue, 0)
    for _ in range(TOP_K):
        pltpu.make_async_copy(h2_ref, xs_ref.at[pl.ds(0, td), :], sem).wait()


def _dispatch(x1, scale2, shift2, dest_flat, pstarts, counts, n_used, n_slots, *, seq, td):
    n, d = x1.shape
    tps = seq // td
    mod_spec = pl.BlockSpec((1, 1, d), lambda i: (i // tps, 0, 0))
    k_spec = pl.BlockSpec((td * TOP_K,), lambda i: (i,), memory_space=pltpu.SMEM)
    smem = pl.BlockSpec(memory_space=pltpu.SMEM)
    return pl.pallas_call(
        functools.partial(_dispatch_kernel, td=td),
        grid=(n // td,),
        in_specs=[pl.BlockSpec((td, d), lambda i: (i, 0)), mod_spec, mod_spec, k_spec, smem, smem, smem],
        out_specs=pl.BlockSpec(memory_space=pl.ANY),
        out_shape=jax.ShapeDtypeStruct((n_slots, d // 2), U32),
        scratch_shapes=[pltpu.VMEM((td, d // 2), U32), pltpu.VMEM((DISPATCH_BLOCK // 2, d // 2), U32),
                        pltpu.SemaphoreType.DMA(()), pltpu.SemaphoreType.DMA(())],
        compiler_params=_params(("arbitrary",)),
    )(x1, scale2, shift2, dest_flat, pstarts, counts, n_used)


def _dest_kernel(eidx_ref, rank_ref, tab_ref, dest_ref):
    tt = eidx_ref.shape[0]
    lane = lax.broadcasted_iota(jnp.int32, (tt, N_EXPERTS), 1)
    lane8 = lax.broadcasted_iota(jnp.int32, (tt, TOP_K), 1)
    eidx = eidx_ref[...]
    rank = rank_ref[...]
    dest = jnp.zeros((tt, TOP_K), jnp.int32)
    for j in range(TOP_K):
        onehot = jnp.where(lane == eidx[:, j:j + 1], 1.0, 0.0).astype(BF16)
        r = jnp.dot(onehot, tab_ref[...], preferred_element_type=F32)
        first_block = (r[:, 0:1] * float(SLOT_TABLE_RADIX) + r[:, 1:2]).astype(jnp.int32)
        dest = jnp.where(lane8 == j, first_block * DISPATCH_BLOCK + rank[:, j:j + 1], dest)
    dest_ref[...] = dest


def _dest(eidx, rank, pstarts, *, tt):
    n = eidx.shape[0]
    first_block = pstarts // DISPATCH_BLOCK
    tab = jnp.stack([first_block // SLOT_TABLE_RADIX, first_block % SLOT_TABLE_RADIX], axis=1)
    tab = jnp.pad(tab, ((0, 0), (0, LANES - 2))).astype(BF16)
    k8 = pl.BlockSpec((tt, TOP_K), lambda i: (i, 0))
    return pl.pallas_call(
        _dest_kernel,
        grid=(n // tt,),
        in_specs=[k8, k8, pl.BlockSpec(tab.shape, lambda i: (0, 0))],
        out_specs=k8,
        out_shape=jax.ShapeDtypeStruct((n, TOP_K), jnp.int32),
        compiler_params=_params(("arbitrary",)),
    )(eidx, rank, tab)


def _expert_kernel(be_ref, nused_ref, xs_ref, wg_ref, wu_ref, wd_ref, ys_ref, wgb, wub, wdb):
    b = pl.program_id(0)
    prev = be_ref[jnp.maximum(b - 1, 0)]

    @pl.when((b == 0) | (be_ref[b] != prev))
    def _():
        wgb[...] = wg_ref[0].astype(BF16)
        wub[...] = wu_ref[0].astype(BF16)
        wdb[...] = wd_ref[0].astype(BF16)

    @pl.when(b < nused_ref[0])
    def _():
        x_lo, x_hi = [h.astype(BF16) for h in _unpack_bf16_halves(xs_ref[...])]
        half = x_lo.shape[1]

        def up_proj(w):
            return (jnp.dot(x_lo, w[0:half, :], preferred_element_type=F32)
                    + jnp.dot(x_hi, w[half:, :], preferred_element_type=F32))

        a = (_silu(up_proj(wgb)) * up_proj(wub)).astype(BF16)
        ys_ref[...] = _pack_bf16_halves(jnp.dot(a, wdb[...], preferred_element_type=F32))

    @pl.when(b >= nused_ref[0])
    def _():
        ys_ref[...] = jnp.zeros_like(ys_ref)


def _experts(block_e, n_used, xs, w_gate, w_up, w_down):
    p, dp = xs.shape
    m = DISPATCH_BLOCK
    n_blocks = p // m
    _, d, f = w_gate.shape
    assert d == 2 * dp
    grid_spec = pltpu.PrefetchScalarGridSpec(
        num_scalar_prefetch=2,
        grid=(n_blocks,),
        in_specs=[pl.BlockSpec((m, dp), lambda b, be, nu: (jnp.minimum(b, nu[0] - 1), 0)),
                  pl.BlockSpec((1, d, f), lambda b, be, nu: (be[b], 0, 0)),
                  pl.BlockSpec((1, d, f), lambda b, be, nu: (be[b], 0, 0)),
                  pl.BlockSpec((1, f, d), lambda b, be, nu: (be[b], 0, 0))],
        out_specs=pl.BlockSpec((m, dp), lambda b, be, nu: (b, 0)),
        scratch_shapes=[pltpu.VMEM((d, f), BF16), pltpu.VMEM((d, f), BF16), pltpu.VMEM((f, d), BF16)],
    )
    return pl.pallas_call(
        _expert_kernel,
        grid_spec=grid_spec,
        out_shape=jax.ShapeDtypeStruct((p, dp), U32),
        compiler_params=_params(("arbitrary",)),
    )(block_e, n_used, xs, w_gate, w_up, w_down)


def _combine_kernel(x_ref, sc_ref, sh_ref, gate_ref, dest_ref, wts_ref, ys_ref,
                    wsg_ref, wsu_ref, wsd_ref, g_ref, b_ref, o_ref, *scratch, tc):
    gbufs, sem = scratch[:TOP_K], scratch[TOP_K]

    def issue(t, carry):
        for j in range(TOP_K):
            d = dest_ref[t * TOP_K + j]
            pltpu.make_async_copy(ys_ref.at[pl.ds(d, 1), :], gbufs[j].at[pl.ds(t, 1), :], sem).start()
        return carry

    lax.fori_loop(0, tc, issue, 0)

    x = x_ref[...]
    hb = (x * (1.0 + sc_ref[0]) + sh_ref[0]).astype(BF16)
    sg = jnp.dot(hb, wsg_ref[...], preferred_element_type=F32)
    su = jnp.dot(hb, wsu_ref[...], preferred_element_type=F32)
    shared = jnp.dot((_silu(sg) * su).astype(BF16), wsd_ref[...], preferred_element_type=F32)

    for j in range(TOP_K):
        pltpu.make_async_copy(ys_ref.at[pl.ds(0, tc), :], gbufs[j], sem).wait()
    wts = wts_ref[...]
    half = shared.shape[1] // 2
    lo, hi = shared[:, :half], shared[:, half:]
    for j in range(TOP_K):
        y_lo, y_hi = _unpack_bf16_halves(gbufs[j][...])
        lo = lo + wts[:, j:j + 1] * y_lo
        hi = hi + wts[:, j:j + 1] * y_hi
    z = ALPHA * x + gate_ref[0] * jnp.concatenate([lo, hi], axis=1)
    o_ref[...] = _layer_norm(z, g_ref[...], b_ref[...])


def _combine(x1, scale2, shift2, gate2, dest_flat, wts, ys, ws_gate, ws_up, ws_down, ln_g, ln_b, *, seq, tc):
    n, d = x1.shape
    tps = seq // tc
    mod_spec = pl.BlockSpec((1, 1, d), lambda i: (i // tps, 0, 0))
    full = lambda a: pl.BlockSpec(a.shape, lambda i: (0,) * a.ndim)
    return pl.pallas_call(
        functools.partial(_combine_kernel, tc=tc),
        grid=(n // tc,),
        in_specs=[pl.BlockSpec((tc, d), lambda i: (i, 0)), mod_spec, mod_spec, mod_spec,
                  pl.BlockSpec((tc * TOP_K,), lambda i: (i,), memory_space=pltpu.SMEM),
                  pl.BlockSpec((tc, TOP_K), lambda i: (i, 0)),
                  pl.BlockSpec(memory_space=pl.ANY),
                  full(ws_gate), full(ws_up), full(ws_down), full(ln_g), full(ln_b)],
        out_specs=pl.BlockSpec((tc, d), lambda i: (i, 0)),
        out_shape=jax.ShapeDtypeStruct((n, d), F32),
        scratch_shapes=[pltpu.VMEM((tc, d // 2), U32) for _ in range(TOP_K)] + [pltpu.SemaphoreType.DMA(())],
        compiler_params=_params(("arbitrary",)),
    )(x1, scale2, shift2, gate2, dest_flat, wts, ys, ws_gate, ws_up, ws_down, ln_g, ln_b)


def _layer(x, c, w_ada, b_ada, w_in, pool_w, pool_scale, lq1, lk1, lq2, lk2, subln_g, w_out,
           ln1_g, ln1_b, w_router, router_bias, w_gate, w_up, w_down, ws_gate, ws_up, ws_down,
           ln2_g, ln2_b, rel_table, *, tm=512, tr=256, td=256, tc=256):
    bsz, seq, d = x.shape
    n = bsz * seq
    x2 = x.reshape(n, d)
    row = lambda a: a.reshape(1, -1)

    mod = _modulation(c, w_ada, b_ada)
    shift1, scale1, gate1, shift2, scale2, gate2 = [
        mod[:, j * d:(j + 1) * d].reshape(bsz, 1, d) for j in range(6)]

    n_main = POOL_DIM + 2 * QK_DIM
    yp, q, k, vt = _inproj(x2, scale1, shift1, w_in[:, :n_main].astype(BF16), w_in[:, n_main:].T.astype(BF16),
                           pool_w.astype(BF16), row(pool_scale), seq=seq, tm=tm)
    bias_tiles = _bias_tiles(rel_table, seq // ATT_TILE)
    ya = _attention(q, k, vt, bias_tiles, row(lq1), row(lk1), row(lq2), row(lk2), subln_g.reshape(-1, 1),
                    bsz=bsz, seq=seq)
    x1 = _outproj(x2, yp, ya, w_out.astype(BF16), gate1, row(ln1_g), row(ln1_b), seq=seq, tm=tm)

    wr_hi = w_router.astype(BF16)
    wr_lo = (w_router - wr_hi.astype(F32)).astype(BF16)
    eidx, rank, wts, cnt = _route(x1, scale2, shift2, wr_hi, wr_lo, row(router_bias), seq=seq, tr=tr)

    m = DISPATCH_BLOCK
    counts = cnt[0].astype(jnp.int32)
    padded = (counts + m - 1) // m * m
    pends = jnp.cumsum(padded)
    pstarts = pends - padded
    n_blocks = -(-(n * TOP_K + N_EXPERTS * (m - 1)) // m)
    block_starts = jnp.arange(n_blocks, dtype=jnp.int32) * m
    block_e = jnp.minimum(jnp.sum((pends[None, :] <= block_starts[:, None]).astype(jnp.int32), axis=1),
                          N_EXPERTS - 1)
    n_used = (pends[-1:] // m).astype(jnp.int32)
    assert n_blocks <= SLOT_TABLE_RADIX * 256
    dest_flat = _dest(eidx, rank, pstarts, tt=tr).reshape(n * TOP_K)

    xs = _dispatch(x1, scale2, shift2, dest_flat, pstarts, counts, n_used, n_blocks * m, seq=seq, td=td)
    ys = _experts(block_e, n_used, xs, w_gate, w_up, w_down)
    out = _combine(x1, scale2, shift2, gate2, dest_flat, wts, ys,
                   ws_gate.astype(BF16), ws_up.astype(BF16), ws_down.astype(BF16),
                   row(ln2_g), row(ln2_b), seq=seq, tc=tc)
    return out.reshape(bsz, seq, d)


def kernel(x, c, w_ada, b_ada, w_in, pool_w, pool_scale, lambda_q1, lambda_k1, lambda_q2, lambda_k2,
           subln_g, w_out, ln1_g, ln1_b, w_router, router_bias, w_gate, w_up, w_down,
           ws_gate, ws_up, ws_down, ln2_g, ln2_b, rel_table):
    per_layer = (w_ada, b_ada, w_in, pool_w, pool_scale, lambda_q1, lambda_k1, lambda_q2, lambda_k2,
                 subln_g, w_out, ln1_g, ln1_b, w_router, router_bias, w_gate, w_up, w_down,
                 ws_gate, ws_up, ws_down, ln2_g, ln2_b)
    assert all(a.shape[0] == DEPTH == 1 for a in per_layer)
    return _layer(x, c, *[a.reshape(a.shape[1:]) for a in per_layer], rel_table)
```

```python
import functools
import math

import jax
import jax.numpy as jnp
from jax import lax
from jax.experimental import pallas as pl
from jax.experimental.pallas import tpu as pltpu

F32 = jnp.float32
BF16 = jnp.bfloat16
U32 = jnp.uint32

D_MODEL = 1024
CHUNK = 64
Q_BLOCK = 128
ATT_TILE = 256
POOL_DIM = 512
POOL_WINDOWS = (2, 4, 8, 16)
POOL_GROUP_DIM = 128
MAX_WINDOW = max(POOL_WINDOWS)
ATTN_HEADS = 4
ATTN_HEAD_DIM = 64
QK_DIM = 512
V_DIM = 512
IN_DIM = 2048
NUM_BUCKETS = 32
MAX_DISTANCE = 128
N_EXPERTS = 256
TOP_K = 8
N_GROUPS = 8
GROUP_SIZE = N_EXPERTS // N_GROUPS
TOP_K_GROUPS = 4
EXPERT_DIM = 256
ROUTED_SCALE = 2.5
DISPATCH_BLOCK = 256
DEPTH = 1
ALPHA = (2.0 * DEPTH) ** 0.25
LN_EPS = 1e-5
LAMBDA_INIT = 0.8 - 0.6 * math.exp(-0.3 * 0)

VMEM_LIMIT = 48 * 1024 * 1024


def _sigmoid(x):
    return 1.0 / (1.0 + jnp.exp(-x))


def _silu(x):
    return x * _sigmoid(x)


def _layer_norm(z, g, b):
    mu = jnp.mean(z, axis=-1, keepdims=True)
    zc = z - mu
    var = jnp.mean(zc * zc, axis=-1, keepdims=True)
    return zc * lax.rsqrt(var + LN_EPS) * g + b


def _params(sem=None):
    return pltpu.CompilerParams(dimension_semantics=sem, vmem_limit_bytes=VMEM_LIMIT)


def _mod_kernel(c_ref, w_ref, b_ref, o_ref):
    ca = _silu(c_ref[...])
    o_ref[...] = jnp.dot(ca, w_ref[...], preferred_element_type=F32,
                         precision=lax.Precision.HIGHEST) + b_ref[...]


def _modulation(c, w_ada, b_ada):
    bsz, d = c.shape
    n_out = w_ada.shape[1]
    return pl.pallas_call(
        _mod_kernel,
        grid=(n_out // d,),
        in_specs=[pl.BlockSpec((bsz, d), lambda j: (0, 0)),
                  pl.BlockSpec((d, d), lambda j: (0, j)),
                  pl.BlockSpec((1, d), lambda j: (0, j))],
        out_specs=pl.BlockSpec((bsz, d), lambda j: (0, j)),
        out_shape=jax.ShapeDtypeStruct((bsz, n_out), F32),
        compiler_params=_params(("arbitrary",)),
    )(c, w_ada, b_ada.reshape(1, n_out))


def _inproj_kernel(x_ref, sc_ref, sh_ref, w_ref, wvt_ref, pw_ref, ps_ref,
                   yp_ref, q_ref, k_ref, vt_ref, ext_ref, *, tm, seq):
    i = pl.program_id(0)
    tiles_per_seq = seq // tm
    it = i % tiles_per_seq
    h = x_ref[...] * (1.0 + sc_ref[0]) + sh_ref[0]
    hb = h.astype(BF16)
    proj = jnp.dot(hb, w_ref[...], preferred_element_type=F32)
    u = proj[:, :POOL_DIM]
    q_ref[...] = (proj[:, POOL_DIM:POOL_DIM + QK_DIM] * (ATTN_HEAD_DIM ** -0.5)).astype(BF16)
    k_ref[...] = proj[:, POOL_DIM + QK_DIM:POOL_DIM + 2 * QK_DIM].astype(BF16)
    vt = lax.dot_general(wvt_ref[...], hb, (((1,), (1,)), ((), ())), preferred_element_type=F32)
    for j in range(tm // ATT_TILE):
        vt_ref[0, j] = vt[:, j * ATT_TILE:(j + 1) * ATT_TILE].astype(BF16)

    @pl.when(it == 0)
    def _():
        ext_ref[0:MAX_WINDOW, :] = jnp.zeros((MAX_WINDOW, POOL_DIM), F32)

    ext_ref[MAX_WINDOW:MAX_WINDOW + tm, :] = u
    pos = (it * tm + lax.broadcasted_iota(jnp.int32, (tm, 1), 0) + 1).astype(F32)
    for g, w in enumerate(POOL_WINDOWS):
        c0, c1 = g * POOL_GROUP_DIM, (g + 1) * POOL_GROUP_DIM
        s = ext_ref[MAX_WINDOW:MAX_WINDOW + tm, c0:c1]
        for j in range(1, w):
            s = s + ext_ref[MAX_WINDOW - j:MAX_WINDOW - j + tm, c0:c1]
        pooled = s / jnp.minimum(pos, float(w)) - u[:, c0:c1]
        y = jnp.dot(pooled.astype(BF16), pw_ref[g], preferred_element_type=F32)
        yp_ref[:, c0:c1] = (y * ps_ref[:, c0:c1]).astype(BF16)
    ext_ref[0:MAX_WINDOW, :] = ext_ref[tm:tm + MAX_WINDOW, :]


def _inproj(x2, scale1, shift1, w_main, w_vt, pool_w, pool_scale, *, seq, tm):
    n, d = x2.shape
    assert n % tm == 0 and seq % tm == 0 and tm >= 2 * MAX_WINDOW and tm % ATT_TILE == 0
    tps = seq // tm
    tpt = tm // ATT_TILE
    mod_spec = pl.BlockSpec((1, 1, d), lambda i: (i // tps, 0, 0))
    row = lambda w: pl.BlockSpec((tm, w), lambda i: (i, 0))
    full = lambda a: pl.BlockSpec(a.shape, lambda i: (0,) * a.ndim)
    return pl.pallas_call(
        functools.partial(_inproj_kernel, tm=tm, seq=seq),
        grid=(n // tm,),
        in_specs=[row(d), mod_spec, mod_spec, full(w_main), full(w_vt), full(pool_w), full(pool_scale)],
        out_specs=[row(POOL_DIM), row(QK_DIM), row(QK_DIM),
                   pl.BlockSpec((1, tpt, V_DIM, ATT_TILE), lambda i: (i // tps, i % tps, 0, 0))],
        out_shape=[jax.ShapeDtypeStruct((n, POOL_DIM), BF16),
                   jax.ShapeDtypeStruct((n, QK_DIM), BF16),
                   jax.ShapeDtypeStruct((n, QK_DIM), BF16),
                   jax.ShapeDtypeStruct((n // seq, seq // ATT_TILE, V_DIM, ATT_TILE), BF16)],
        scratch_shapes=[pltpu.VMEM((tm + MAX_WINDOW, POOL_DIM), F32)],
        compiler_params=_params(("arbitrary",)),
    )(x2, scale1, shift1, w_main, w_vt, pool_w, pool_scale)


def _bias_kernel(tab_ref, o_ref):
    delta = pl.program_id(0)
    r = lax.broadcasted_iota(jnp.int32, (ATT_TILE, ATT_TILE), 0)
    c = lax.broadcasted_iota(jnp.int32, (ATT_TILE, ATT_TILE), 1)
    rel = r - c - delta * ATT_TILE
    half = NUM_BUCKETS // 2
    max_exact = half // 2
    ret = jnp.where(rel > 0, half, 0)
    n = jnp.abs(rel)
    nf = jnp.maximum(n, 1).astype(F32)
    large = max_exact + (jnp.log(nf / max_exact) / math.log(MAX_DISTANCE / max_exact)
                         * (half - max_exact)).astype(jnp.int32)
    large = jnp.minimum(large, half - 1)
    bucket = ret + jnp.where(n < max_exact, n, large)
    for h in range(ATTN_HEADS):
        acc = jnp.zeros((ATT_TILE, ATT_TILE), F32)
        for b in range(NUM_BUCKETS):
            acc = jnp.where(bucket == b, tab_ref[b, h], acc)
        o_ref[h, 0] = acc


def _bias_tiles(rel_table, n_tiles):
    return pl.pallas_call(
        _bias_kernel,
        grid=(n_tiles,),
        in_specs=[pl.BlockSpec(memory_space=pltpu.SMEM)],
        out_specs=pl.BlockSpec((ATTN_HEADS, 1, ATT_TILE, ATT_TILE), lambda dlt: (0, dlt, 0, 0)),
        out_shape=jax.ShapeDtypeStruct((ATTN_HEADS, n_tiles, ATT_TILE, ATT_TILE), F32),
        compiler_params=_params(("arbitrary",)),
    )(rel_table)


def _attn_kernel(q_ref, k_ref, vt_ref, bias_ref, lq1_ref, lk1_ref, lq2_ref, lk2_ref, g_ref, o_ref, *acc_refs):
    qt = pl.program_id(1)
    t = ATT_TILE
    n_maps = 2 * ATTN_HEADS
    lam = (jnp.exp(jnp.sum(lq1_ref[...] * lk1_ref[...], axis=-1, keepdims=True))
           - jnp.exp(jnp.sum(lq2_ref[...] * lk2_ref[...], axis=-1, keepdims=True))
           + LAMBDA_INIT)
    r = lax.broadcasted_iota(jnp.int32, (t, t), 0)
    c = lax.broadcasted_iota(jnp.int32, (t, t), 1)
    allowed = (r // CHUNK) <= (c // CHUNK)
    hd2 = 2 * ATTN_HEAD_DIM

    def block(kt, carry, diagonal):
        koff = pl.multiple_of(kt * t, t)

        def scores(hm):
            col = hm * ATTN_HEAD_DIM
            qh = q_ref[:, col:col + ATTN_HEAD_DIM]
            kh = k_ref[pl.ds(koff, t), col:col + ATTN_HEAD_DIM]
            s = lax.dot_general(kh, qh, (((1,), (1,)), ((), ())),
                                preferred_element_type=F32) + bias_ref[hm // 2, qt - kt]
            return jnp.where(allowed, s, -jnp.inf) if diagonal else s

        def softmax(hm, s):
            m_old, l_old = carry[2 * hm:2 * hm + 2]
            m_new = jnp.maximum(m_old, jnp.max(s, axis=0, keepdims=True))
            alpha = jnp.exp(m_old - m_new)
            p = jnp.exp(s - m_new)
            return m_new, alpha * l_old + jnp.sum(p, axis=0, keepdims=True), alpha, p.astype(BF16)

        def accumulate(hm, alpha, p):
            h = hm // 2
            vth = vt_ref[0, kt, h * hd2:(h + 1) * hd2, :]
            acc_refs[hm][...] = alpha * acc_refs[hm][...] + jnp.dot(vth, p, preferred_element_type=F32)

        s_vals, sm_vals, out = {}, {}, [None] * (2 * n_maps)
        for step in range(n_maps + 2):
            if step < n_maps:
                s_vals[step] = scores(step)
            if 0 <= step - 1 < n_maps:
                hm = step - 1
                m_new, l_new, alpha, p = softmax(hm, s_vals.pop(hm))
                out[2 * hm], out[2 * hm + 1] = m_new, l_new
                sm_vals[hm] = (alpha, p)
            if 0 <= step - 2 < n_maps:
                accumulate(step - 2, *sm_vals.pop(step - 2))
        return tuple(out)

    for acc in acc_refs:
        acc[...] = jnp.zeros_like(acc)
    one = (jnp.full((1, t), -jnp.inf, F32), jnp.zeros((1, t), F32))
    carry = lax.fori_loop(0, qt, lambda kt, cr: block(kt, cr, False), one * n_maps)
    carry = block(qt, carry, True)
    for h in range(ATTN_HEADS):
        l0, l1 = carry[4 * h + 1], carry[4 * h + 3]
        o = acc_refs[2 * h][...] / l0 - lam * (acc_refs[2 * h + 1][...] / l1)
        y = o * lax.rsqrt(jnp.mean(o * o, axis=0, keepdims=True) + LN_EPS) * g_ref[...]
        o_ref[:, h * hd2:(h + 1) * hd2] = (y * (1.0 - LAMBDA_INIT)).T.astype(BF16)


def _attention(q, k, vt, bias_tiles, lq1, lk1, lq2, lk2, subln_g, *, bsz, seq):
    t = ATT_TILE
    nt = seq // t
    full = lambda a: pl.BlockSpec(a.shape, lambda b, j: (0,) * a.ndim)
    return pl.pallas_call(
        _attn_kernel,
        grid=(bsz, nt),
        in_specs=[pl.BlockSpec((t, QK_DIM), lambda b, j: (b * nt + j, 0)),
                  pl.BlockSpec((seq, QK_DIM), lambda b, j: (b, 0)),
                  pl.BlockSpec((1, nt, V_DIM, t), lambda b, j: (b, 0, 0, 0)),
                  full(bias_tiles), full(lq1), full(lk1), full(lq2), full(lk2), full(subln_g)],
        out_specs=pl.BlockSpec((t, V_DIM), lambda b, j: (b * nt + j, 0)),
        out_shape=jax.ShapeDtypeStruct((bsz * seq, V_DIM), BF16),
        scratch_shapes=[pltpu.VMEM((2 * ATTN_HEAD_DIM, t), F32) for _ in range(2 * ATTN_HEADS)],
        compiler_params=_params(("arbitrary", "arbitrary")),
    )(q, k, vt, bias_tiles, lq1, lk1, lq2, lk2, subln_g)


def _outproj_kernel(x_ref, yp_ref, ya_ref, w_ref, gate_ref, g_ref, b_ref, o_ref):
    mix = (jnp.dot(yp_ref[...], w_ref[0:POOL_DIM, :], preferred_element_type=F32)
           + jnp.dot(ya_ref[...], w_ref[POOL_DIM:, :], preferred_element_type=F32))
    z = ALPHA * x_ref[...] + gate_ref[0] * mix
    o_ref[...] = _layer_norm(z, g_ref[...], b_ref[...])


def _outproj(x2, yp, ya, w_out, gate1, ln_g, ln_b, *, seq, tm):
    n, d = x2.shape
    tps = seq // tm
    row = lambda w: pl.BlockSpec((tm, w), lambda i: (i, 0))
    full = lambda a: pl.BlockSpec(a.shape, lambda i: (0,) * a.ndim)
    return pl.pallas_call(
        _outproj_kernel,
        grid=(n // tm,),
        in_specs=[row(d), row(POOL_DIM), row(V_DIM), full(w_out),
                  pl.BlockSpec((1, 1, d), lambda i: (i // tps, 0, 0)), full(ln_g), full(ln_b)],
        out_specs=row(d),
        out_shape=jax.ShapeDtypeStruct((n, d), F32),
        compiler_params=_params(("arbitrary",)),
    )(x2, yp, ya, w_out, gate1, ln_g, ln_b)


def _route_kernel(x_ref, sc_ref, sh_ref, whi_ref, wlo_ref, rb_ref,
                  eidx_ref, rank_ref, wts_ref, cnt_ref, carry_ref, *, tr):
    i = pl.program_id(0)

    @pl.when(i == 0)
    def _():
        carry_ref[...] = jnp.zeros_like(carry_ref)

    h2 = x_ref[...] * (1.0 + sc_ref[0]) + sh_ref[0]
    hi = h2.astype(BF16)
    lo = (h2 - hi.astype(F32)).astype(BF16)
    logits = (jnp.dot(hi, whi_ref[...], preferred_element_type=F32)
              + jnp.dot(hi, wlo_ref[...], preferred_element_type=F32)
              + jnp.dot(lo, whi_ref[...], preferred_element_type=F32))
    scores = _sigmoid(logits)
    sel = scores + rb_ref[...]
    lane = lax.broadcasted_iota(jnp.int32, (tr, N_EXPERTS), 1)
    lane_f = lane.astype(F32)

    a1 = sel
    a2 = jnp.full_like(sel, -jnp.inf)
    s = 1
    while s < GROUP_SIZE:
        upper = (lane & s) != 0
        b1 = jnp.where(upper, pltpu.roll(a1, s, 1), pltpu.roll(a1, N_EXPERTS - s, 1))
        b2 = jnp.where(upper, pltpu.roll(a2, s, 1), pltpu.roll(a2, N_EXPERTS - s, 1))
        a1, a2 = jnp.maximum(a1, b1), jnp.maximum(jnp.minimum(a1, b1), jnp.maximum(a2, b2))
        s *= 2
    gscore = a1 + a2

    grp = lane // GROUP_SIZE
    beaten_by = jnp.zeros((tr, N_EXPERTS), jnp.int32)
    for kk in range(1, N_GROUPS):
        other = pltpu.roll(gscore, kk * GROUP_SIZE, 1)
        ogrp = (grp - kk) & (N_GROUPS - 1)
        wins = (other > gscore) | ((other == gscore) & (ogrp < grp))
        beaten_by = beaten_by + wins.astype(jnp.int32)
    cur = jnp.where(beaten_by < TOP_K_GROUPS, sel, -jnp.inf)

    picks, weights = [], []
    selmask = jnp.zeros((tr, N_EXPERTS), F32)
    for _ in range(TOP_K):
        mx = jnp.max(cur, axis=-1, keepdims=True)
        pick = jnp.min(jnp.where(cur == mx, lane_f, float(N_EXPERTS)), axis=-1, keepdims=True)
        onehot = lane_f == pick
        weights.append(jnp.sum(jnp.where(onehot, scores, 0.0), axis=-1, keepdims=True))
        cur = jnp.where(onehot, -jnp.inf, cur)
        selmask = jnp.where(onehot, 1.0, selmask)
        picks.append(pick)

    rr = lax.broadcasted_iota(jnp.int32, (tr, tr), 0)
    cc = lax.broadcasted_iota(jnp.int32, (tr, tr), 1)
    earlier = jnp.where(cc < rr, 1.0, 0.0).astype(BF16)
    rankmat = jnp.dot(earlier, selmask.astype(BF16), preferred_element_type=F32) + carry_ref[...]
    carry_ref[...] = carry_ref[...] + jnp.sum(selmask, axis=0, keepdims=True)
    cnt_ref[...] = carry_ref[...]

    wsum = weights[0]
    for wj in weights[1:]:
        wsum = wsum + wj
    lane8 = lax.broadcasted_iota(jnp.int32, (tr, TOP_K), 1)
    eidx = jnp.zeros((tr, TOP_K), jnp.int32)
    rank = jnp.zeros((tr, TOP_K), jnp.int32)
    wts = jnp.zeros((tr, TOP_K), F32)
    for j in range(TOP_K):
        rk = jnp.sum(jnp.where(lane_f == picks[j], rankmat, 0.0), axis=-1, keepdims=True)
        eidx = jnp.where(lane8 == j, picks[j].astype(jnp.int32), eidx)
        rank = jnp.where(lane8 == j, rk.astype(jnp.int32), rank)
        wts = jnp.where(lane8 == j, weights[j] / wsum * ROUTED_SCALE, wts)
    eidx_ref[...] = eidx
    rank_ref[...] = rank
    wts_ref[...] = wts


def _route(x1, scale2, shift2, wr_hi, wr_lo, router_bias, *, seq, tr):
    n, d = x1.shape
    tps = seq // tr
    mod_spec = pl.BlockSpec((1, 1, d), lambda i: (i // tps, 0, 0))
    full = lambda a: pl.BlockSpec(a.shape, lambda i: (0,) * a.ndim)
    k8 = pl.BlockSpec((tr, TOP_K), lambda i: (i, 0))
    return pl.pallas_call(
        functools.partial(_route_kernel, tr=tr),
        grid=(n // tr,),
        in_specs=[pl.BlockSpec((tr, d), lambda i: (i, 0)), mod_spec, mod_spec,
                  full(wr_hi), full(wr_lo), full(router_bias)],
        out_specs=[k8, k8, k8, pl.BlockSpec((1, N_EXPERTS), lambda i: (0, 0))],
        out_shape=[jax.ShapeDtypeStruct((n, TOP_K), jnp.int32),
                   jax.ShapeDtypeStruct((n, TOP_K), jnp.int32),
                   jax.ShapeDtypeStruct((n, TOP_K), F32),
                   jax.ShapeDtypeStruct((1, N_EXPERTS), F32)],
        scratch_shapes=[pltpu.VMEM((1, N_EXPERTS), F32)],
        compiler_params=_params(("arbitrary",)),
    )(x1, scale2, shift2, wr_hi, wr_lo, router_bias)


def _zero_pad_rows(cnt_ref, pst_ref, nused_ref, zero_ref, xs_ref, sem):
    m = DISPATCH_BLOCK
    sub = 8

    def copies(e, wait):
        cnt = cnt_ref[e]
        n_pad = (-cnt) & (m - 1)
        start = pst_ref[e] + cnt
        head = n_pad & (sub - 1)

        def go(n_rows, pos):
            cp = pltpu.make_async_copy(zero_ref.at[pl.ds(0, n_rows), :], xs_ref.at[pl.ds(pos, n_rows), :], sem)
            cp.wait() if wait else cp.start()

        for r in range(sub - 1):
            pl.when(r < head)(functools.partial(go, 1, start + r))
        bit = m // 2
        while bit >= sub:
            pos = pl.multiple_of(start + head + (n_pad & ~(2 * bit - 1) & ~(sub - 1)), sub)
            pl.when((n_pad & bit) != 0)(functools.partial(go, bit, pos))
            bit //= 2

    def start_all(e, carry):
        copies(e, False)
        return carry

    def wait_all(e, carry):
        copies(e, True)
        return carry

    lax.fori_loop(0, N_EXPERTS, start_all, 0)
    lax.fori_loop(0, N_EXPERTS, wait_all, 0)

    half = m // 2
    n_halves = xs_ref.shape[0] // half

    def tail(wait):
        def body(i, carry):
            cp = pltpu.make_async_copy(zero_ref, xs_ref.at[pl.ds(pl.multiple_of(i * half, half), half), :], sem)
            cp.wait() if wait else cp.start()
            return carry
        return body

    first = 2 * nused_ref[0]
    lax.fori_loop(first, n_halves, tail(False), 0)
    lax.fori_loop(first, n_halves, tail(True), 0)


def _pack_bf16_halves(x):
    w = x.shape[1] // 2
    lo = pltpu.bitcast(x[:, :w].astype(BF16).astype(F32), U32) >> 16
    hi = pltpu.bitcast(x[:, w:].astype(BF16).astype(F32), U32) & jnp.uint32(0xFFFF0000)
    return lo | hi


def _unpack_bf16_halves(p):
    return pltpu.bitcast(p << 16, F32), pltpu.bitcast(p & jnp.uint32(0xFFFF0000), F32)


def _dispatch_kernel(x_ref, sc_ref, sh_ref, dest_ref, pst_ref, cnt_ref, nused_ref, xs_ref,
                     h2_ref, zero_ref, sem, pad_sem, *, td):
    @pl.when(pl.program_id(0) == 0)
    def _():
        zero_ref[...] = jnp.zeros_like(zero_ref)
        _zero_pad_rows(cnt_ref, pst_ref, nused_ref, zero_ref, xs_ref, pad_sem)

    h2_ref[...] = _pack_bf16_halves(x_ref[...] * (1.0 + sc_ref[0]) + sh_ref[0])

    def issue(t, carry):
        for j in range(TOP_K):
            d = dest_ref[t * TOP_K + j]
            pltpu.make_async_copy(h2_ref.at[pl.ds(t, 1), :], xs_ref.at[pl.ds(d, 1), :], sem).start()
        return carry

    lax.fori_loop(0, td, issue, 0)
    for _ in range(TOP_K):
        pltpu.make_async_copy(h2_ref, xs_ref.at[pl.ds(0, td), :], sem).wait()


def _dispatch(x1, scale2, shift2, dest_flat, pstarts, counts, n_used, n_slots, *, seq, td):
    n, d = x1.shape
    tps = seq // td
    mod_spec = pl.BlockSpec((1, 1, d), lambda i: (i // tps, 0, 0))
    k_spec = pl.BlockSpec((td * TOP_K,), lambda i: (i,), memory_space=pltpu.SMEM)
    smem = pl.BlockSpec(memory_space=pltpu.SMEM)
    return pl.pallas_call(
        functools.partial(_dispatch_kernel, td=td),
        grid=(n // td,),
        in_specs=[pl.BlockSpec((td, d), lambda i: (i, 0)), mod_spec, mod_spec, k_spec, smem, smem, smem],
        out_specs=pl.BlockSpec(memory_space=pl.ANY),
        out_shape=jax.ShapeDtypeStruct((n_slots, d // 2), U32),
        scratch_shapes=[pltpu.VMEM((td, d // 2), U32), pltpu.VMEM((DISPATCH_BLOCK // 2, d // 2), U32),
                        pltpu.SemaphoreType.DMA(()), pltpu.SemaphoreType.DMA(())],
        compiler_params=_params(("arbitrary",)),
    )(x1, scale2, shift2, dest_flat, pstarts, counts, n_used)


def _dest_kernel(eidx_ref, rank_ref, first_ref, dest_ref):
    tt = eidx_ref.shape[0]
    lane = lax.broadcasted_iota(jnp.int32, (tt, N_EXPERTS), 1)
    lane8 = lax.broadcasted_iota(jnp.int32, (tt, TOP_K), 1)
    eidx = eidx_ref[...]
    rank = rank_ref[...]
    first = first_ref[...]
    dest = jnp.zeros((tt, TOP_K), jnp.int32)
    for j in range(TOP_K):
        start = jnp.sum(jnp.where(lane == eidx[:, j:j + 1], first, 0.0), axis=-1, keepdims=True)
        dest = jnp.where(lane8 == j, start.astype(jnp.int32) + rank[:, j:j + 1], dest)
    dest_ref[...] = dest


def _dest(eidx, rank, pstarts, *, tt):
    n = eidx.shape[0]
    first = pstarts.astype(F32).reshape(1, N_EXPERTS)
    k8 = pl.BlockSpec((tt, TOP_K), lambda i: (i, 0))
    return pl.pallas_call(
        _dest_kernel,
        grid=(n // tt,),
        in_specs=[k8, k8, pl.BlockSpec(first.shape, lambda i: (0, 0))],
        out_specs=k8,
        out_shape=jax.ShapeDtypeStruct((n, TOP_K), jnp.int32),
        compiler_params=_params(("arbitrary",)),
    )(eidx, rank, first)


def _expert_kernel(fb_ref, nb_ref, nused_ref, wg_ref, wu_ref, wd_ref, xs_ref, ys_ref,
                   wgb, wub, wdb, xbuf, ybuf, in_sem, out_sem):
    e = pl.program_id(0)
    m = DISPATCH_BLOCK
    n_used = nused_ref[0]

    def rows(g):
        return pl.ds(pl.multiple_of(g * m, m), m)

    def fetch(g, slot):
        return pltpu.make_async_copy(xs_ref.at[rows(g), :], xbuf.at[slot], in_sem.at[slot])

    def put(g, slot):
        return pltpu.make_async_copy(ybuf.at[slot], ys_ref.at[rows(g), :], out_sem.at[slot])

    @pl.when(e == 0)
    def _():
        fetch(0, 0).start()

    wgb[...] = wg_ref[0].astype(BF16)
    wub[...] = wu_ref[0].astype(BF16)
    wdb[...] = wd_ref[0].astype(BF16)

    def block(i, carry):
        g = fb_ref[e] + i
        slot = g & 1
        fetch(g, slot).wait()

        @pl.when(g + 1 < n_used)
        def _():
            fetch(g + 1, 1 - slot).start()

        @pl.when(g >= 2)
        def _():
            put(g - 2, slot).wait()

        x_lo, x_hi = [h.astype(BF16) for h in _unpack_bf16_halves(xbuf[slot])]
        half = x_lo.shape[1]

        def up_proj(w):
            return (jnp.dot(x_lo, w[0:half, :], preferred_element_type=F32)
                    + jnp.dot(x_hi, w[half:, :], preferred_element_type=F32))

        a = (_silu(up_proj(wgb)) * up_proj(wub)).astype(BF16)
        ybuf[slot] = _pack_bf16_halves(jnp.dot(a, wdb[...], preferred_element_type=F32))
        put(g, slot).start()
        return carry

    lax.fori_loop(0, nb_ref[e], block, 0)

    @pl.when(e == pl.num_programs(0) - 1)
    def _():
        for back in (2, 1):
            @pl.when(n_used >= back)
            def _(back=back):
                put(n_used - back, (n_used - back) & 1).wait()

        ybuf[0] = jnp.zeros(ybuf.shape[1:], ybuf.dtype)
        n_blocks = ys_ref.shape[0] // m

        def tail(wait):
            def body(g, carry):
                cp = put(g, 0)
                cp.wait() if wait else cp.start()
                return carry
            return body

        lax.fori_loop(n_used, n_blocks, tail(False), 0)
        lax.fori_loop(n_used, n_blocks, tail(True), 0)


def _experts(first_block, n_blocks_e, n_used, xs, w_gate, w_up, w_down):
    p, dp = xs.shape
    m = DISPATCH_BLOCK
    n_e, d, f = w_gate.shape
    assert d == 2 * dp and p % m == 0
    grid_spec = pltpu.PrefetchScalarGridSpec(
        num_scalar_prefetch=3,
        grid=(n_e,),
        in_specs=[pl.BlockSpec((1, d, f), lambda e, *_: (e, 0, 0)),
                  pl.BlockSpec((1, d, f), lambda e, *_: (e, 0, 0)),
                  pl.BlockSpec((1, f, d), lambda e, *_: (e, 0, 0)),
                  pl.BlockSpec(memory_space=pl.ANY)],
        out_specs=pl.BlockSpec(memory_space=pl.ANY),
        scratch_shapes=[pltpu.VMEM((d, f), BF16), pltpu.VMEM((d, f), BF16), pltpu.VMEM((f, d), BF16),
                        pltpu.VMEM((2, m, dp), U32), pltpu.VMEM((2, m, dp), U32),
                        pltpu.SemaphoreType.DMA((2,)), pltpu.SemaphoreType.DMA((2,))],
    )
    return pl.pallas_call(
        _expert_kernel,
        grid_spec=grid_spec,
        out_shape=jax.ShapeDtypeStruct((p, dp), U32),
        compiler_params=_params(("arbitrary",)),
    )(first_block, n_blocks_e, n_used, w_gate, w_up, w_down, xs)


def _combine_kernel(x_ref, sc_ref, sh_ref, gate_ref, dest_ref, wts_ref, ys_ref,
                    wsg_ref, wsu_ref, wsd_ref, g_ref, b_ref, o_ref, *scratch, tc):
    gbufs, sem = scratch[:TOP_K], scratch[TOP_K]

    def issue(t, carry):
        for j in range(TOP_K):
            d = dest_ref[t * TOP_K + j]
            pltpu.make_async_copy(ys_ref.at[pl.ds(d, 1), :], gbufs[j].at[pl.ds(t, 1), :], sem).start()
        return carry

    lax.fori_loop(0, tc, issue, 0)

    x = x_ref[...]
    hb = (x * (1.0 + sc_ref[0]) + sh_ref[0]).astype(BF16)
    sg = jnp.dot(hb, wsg_ref[...], preferred_element_type=F32)
    su = jnp.dot(hb, wsu_ref[...], preferred_element_type=F32)
    shared = jnp.dot((_silu(sg) * su).astype(BF16), wsd_ref[...], preferred_element_type=F32)

    for j in range(TOP_K):
        pltpu.make_async_copy(ys_ref.at[pl.ds(0, tc), :], gbufs[j], sem).wait()
    wts = wts_ref[...]
    half = shared.shape[1] // 2
    lo, hi = shared[:, :half], shared[:, half:]
    for j in range(TOP_K):
        y_lo, y_hi = _unpack_bf16_halves(gbufs[j][...])
        lo = lo + wts[:, j:j + 1] * y_lo
        hi = hi + wts[:, j:j + 1] * y_hi
    z = ALPHA * x + gate_ref[0] * jnp.concatenate([lo, hi], axis=1)
    o_ref[...] = _layer_norm(z, g_ref[...], b_ref[...])


def _combine(x1, scale2, shift2, gate2, dest_flat, wts, ys, ws_gate, ws_up, ws_down, ln_g, ln_b, *, seq, tc):
    n, d = x1.shape
    tps = seq // tc
    mod_spec = pl.BlockSpec((1, 1, d), lambda i: (i // tps, 0, 0))
    full = lambda a: pl.BlockSpec(a.shape, lambda i: (0,) * a.ndim)
    return pl.pallas_call(
        functools.partial(_combine_kernel, tc=tc),
        grid=(n // tc,),
        in_specs=[pl.BlockSpec((tc, d), lambda i: (i, 0)), mod_spec, mod_spec, mod_spec,
                  pl.BlockSpec((tc * TOP_K,), lambda i: (i,), memory_space=pltpu.SMEM),
                  pl.BlockSpec((tc, TOP_K), lambda i: (i, 0)),
                  pl.BlockSpec(memory_space=pl.ANY),
                  full(ws_gate), full(ws_up), full(ws_down), full(ln_g), full(ln_b)],
        out_specs=pl.BlockSpec((tc, d), lambda i: (i, 0)),
        out_shape=jax.ShapeDtypeStruct((n, d), F32),
        scratch_shapes=[pltpu.VMEM((tc, d // 2), U32) for _ in range(TOP_K)] + [pltpu.SemaphoreType.DMA(())],
        compiler_params=_params(("arbitrary",)),
    )(x1, scale2, shift2, gate2, dest_flat, wts, ys, ws_gate, ws_up, ws_down, ln_g, ln_b)


def _layer(x, c, w_ada, b_ada, w_in, pool_w, pool_scale, lq1, lk1, lq2, lk2, subln_g, w_out,
           ln1_g, ln1_b, w_router, router_bias, w_gate, w_up, w_down, ws_gate, ws_up, ws_down,
           ln2_g, ln2_b, rel_table, *, tm=512, tr=256, td=256, tc=256):
    bsz, seq, d = x.shape
    n = bsz * seq
    x2 = x.reshape(n, d)
    row = lambda a: a.reshape(1, -1)

    mod = _modulation(c, w_ada, b_ada)
    shift1, scale1, gate1, shift2, scale2, gate2 = [
        mod[:, j * d:(j + 1) * d].reshape(bsz, 1, d) for j in range(6)]

    n_main = POOL_DIM + 2 * QK_DIM
    yp, q, k, vt = _inproj(x2, scale1, shift1, w_in[:, :n_main].astype(BF16), w_in[:, n_main:].T.astype(BF16),
                           pool_w.astype(BF16), row(pool_scale), seq=seq, tm=tm)
    bias_tiles = _bias_tiles(rel_table, seq // ATT_TILE)
    ya = _attention(q, k, vt, bias_tiles, row(lq1), row(lk1), row(lq2), row(lk2), subln_g.reshape(-1, 1),
                    bsz=bsz, seq=seq)
    x1 = _outproj(x2, yp, ya, w_out.astype(BF16), gate1, row(ln1_g), row(ln1_b), seq=seq, tm=tm)

    wr_hi = w_router.astype(BF16)
    wr_lo = (w_router - wr_hi.astype(F32)).astype(BF16)
    eidx, rank, wts, cnt = _route(x1, scale2, shift2, wr_hi, wr_lo, row(router_bias), seq=seq, tr=tr)

    m = DISPATCH_BLOCK
    counts = cnt[0].astype(jnp.int32)
    padded = (counts + m - 1) // m * m
    pends = jnp.cumsum(padded)
    pstarts = pends - padded
    n_blocks = -(-(n * TOP_K + N_EXPERTS * (m - 1)) // m)
    n_used = (pends[-1:] // m).astype(jnp.int32)
    dest_flat = _dest(eidx, rank, pstarts, tt=tr).reshape(n * TOP_K)

    xs = _dispatch(x1, scale2, shift2, dest_flat, pstarts, counts, n_used, n_blocks * m, seq=seq, td=td)
    ys = _experts(pstarts // m, padded // m, n_used, xs, w_gate, w_up, w_down)
    out = _combine(x1, scale2, shift2, gate2, dest_flat, wts, ys,
                   ws_gate.astype(BF16), ws_up.astype(BF16), ws_down.astype(BF16),
                   row(ln2_g), row(ln2_b), seq=seq, tc=tc)
    return out.reshape(bsz, seq, d)


def kernel(x, c, w_ada, b_ada, w_in, pool_w, pool_scale, lambda_q1, lambda_k1, lambda_q2, lambda_k2,
           subln_g, w_out, ln1_g, ln1_b, w_router, router_bias, w_gate, w_up, w_down,
           ws_gate, ws_up, ws_down, ln2_g, ln2_b, rel_table):
    per_layer = (w_ada, b_ada, w_in, pool_w, pool_scale, lambda_q1, lambda_k1, lambda_q2, lambda_k2,
                 subln_g, w_out, ln1_g, ln1_b, w_router, router_bias, w_gate, w_up, w_down,
                 ws_gate, ws_up, ws_down, ln2_g, ln2_b)
    assert all(a.shape[0] == DEPTH == 1 for a in per_layer)
    return _layer(x, c, *[a.reshape(a.shape[1:]) for a in per_layer], rel_table)
```

```python
import functools
import math

import jax
import jax.numpy as jnp
from jax import lax
from jax.experimental import pallas as pl
from jax.experimental.pallas import tpu as pltpu

F32 = jnp.float32
BF16 = jnp.bfloat16
U32 = jnp.uint32
LANES = 128

D_MODEL = 1024
CHUNK = 64
Q_BLOCK = 128
ATT_TILE = 256
POOL_DIM = 512
POOL_WINDOWS = (2, 4, 8, 16)
POOL_GROUP_DIM = 128
MAX_WINDOW = max(POOL_WINDOWS)
ATTN_HEADS = 4
ATTN_HEAD_DIM = 64
QK_DIM = 512
V_DIM = 512
IN_DIM = 2048
NUM_BUCKETS = 32
MAX_DISTANCE = 128
N_EXPERTS = 256
TOP_K = 8
N_GROUPS = 8
GROUP_SIZE = N_EXPERTS // N_GROUPS
TOP_K_GROUPS = 4
EXPERT_DIM = 256
ROUTED_SCALE = 2.5
DISPATCH_BLOCK = 256
EXPERT_SLOTS = 4
DEPTH = 1
ALPHA = (2.0 * DEPTH) ** 0.25
LN_EPS = 1e-5
LAMBDA_INIT = 0.8 - 0.6 * math.exp(-0.3 * 0)

VMEM_LIMIT = 48 * 1024 * 1024


def _sigmoid(x):
    return 1.0 / (1.0 + jnp.exp(-x))


def _silu(x):
    return x * _sigmoid(x)


def _layer_norm(z, g, b):
    mu = jnp.mean(z, axis=-1, keepdims=True)
    zc = z - mu
    var = jnp.mean(zc * zc, axis=-1, keepdims=True)
    return zc * lax.rsqrt(var + LN_EPS) * g + b


def _params(sem=None):
    return pltpu.CompilerParams(dimension_semantics=sem, vmem_limit_bytes=VMEM_LIMIT)


def _mod_kernel(c_ref, w_ref, b_ref, o_ref):
    ca = _silu(c_ref[...])
    o_ref[...] = jnp.dot(ca, w_ref[...], preferred_element_type=F32,
                         precision=lax.Precision.HIGHEST) + b_ref[...]


def _modulation(c, w_ada, b_ada):
    bsz, d = c.shape
    n_out = w_ada.shape[1]
    return pl.pallas_call(
        _mod_kernel,
        grid=(n_out // d,),
        in_specs=[pl.BlockSpec((bsz, d), lambda j: (0, 0)),
                  pl.BlockSpec((d, d), lambda j: (0, j)),
                  pl.BlockSpec((1, d), lambda j: (0, j))],
        out_specs=pl.BlockSpec((bsz, d), lambda j: (0, j)),
        out_shape=jax.ShapeDtypeStruct((bsz, n_out), F32),
        compiler_params=_params(("arbitrary",)),
    )(c, w_ada, b_ada.reshape(1, n_out))


def _inproj_kernel(x_ref, sc_ref, sh_ref, w_ref, wvt_ref, pw_ref, ps_ref,
                   yp_ref, q_ref, k_ref, vt_ref, ext_ref, *, tm, seq):
    i = pl.program_id(0)
    tiles_per_seq = seq // tm
    it = i % tiles_per_seq
    h = x_ref[...] * (1.0 + sc_ref[0]) + sh_ref[0]
    hb = h.astype(BF16)
    proj = jnp.dot(hb, w_ref[...], preferred_element_type=F32)
    u = proj[:, :POOL_DIM]
    q_ref[...] = (proj[:, POOL_DIM:POOL_DIM + QK_DIM] * (ATTN_HEAD_DIM ** -0.5)).astype(BF16)
    k_ref[...] = proj[:, POOL_DIM + QK_DIM:POOL_DIM + 2 * QK_DIM].astype(BF16)
    vt = lax.dot_general(wvt_ref[...], hb, (((1,), (1,)), ((), ())), preferred_element_type=F32)
    for j in range(tm // ATT_TILE):
        vt_ref[0, j] = vt[:, j * ATT_TILE:(j + 1) * ATT_TILE].astype(BF16)

    @pl.when(it == 0)
    def _():
        ext_ref[0:MAX_WINDOW, :] = jnp.zeros((MAX_WINDOW, POOL_DIM), F32)

    ext_ref[MAX_WINDOW:MAX_WINDOW + tm, :] = u
    pos = (it * tm + lax.broadcasted_iota(jnp.int32, (tm, 1), 0) + 1).astype(F32)
    for g, w in enumerate(POOL_WINDOWS):
        c0, c1 = g * POOL_GROUP_DIM, (g + 1) * POOL_GROUP_DIM
        s = ext_ref[MAX_WINDOW:MAX_WINDOW + tm, c0:c1]
        for j in range(1, w):
            s = s + ext_ref[MAX_WINDOW - j:MAX_WINDOW - j + tm, c0:c1]
        pooled = s / jnp.minimum(pos, float(w)) - u[:, c0:c1]
        y = jnp.dot(pooled.astype(BF16), pw_ref[g], preferred_element_type=F32)
        yp_ref[:, c0:c1] = (y * ps_ref[:, c0:c1]).astype(BF16)
    ext_ref[0:MAX_WINDOW, :] = ext_ref[tm:tm + MAX_WINDOW, :]


def _inproj(x2, scale1, shift1, w_main, w_vt, pool_w, pool_scale, *, seq, tm):
    n, d = x2.shape
    assert n % tm == 0 and seq % tm == 0 and tm >= 2 * MAX_WINDOW and tm % ATT_TILE == 0
    tps = seq // tm
    tpt = tm // ATT_TILE
    mod_spec = pl.BlockSpec((1, 1, d), lambda i: (i // tps, 0, 0))
    row = lambda w: pl.BlockSpec((tm, w), lambda i: (i, 0))
    full = lambda a: pl.BlockSpec(a.shape, lambda i: (0,) * a.ndim)
    return pl.pallas_call(
        functools.partial(_inproj_kernel, tm=tm, seq=seq),
        grid=(n // tm,),
        in_specs=[row(d), mod_spec, mod_spec, full(w_main), full(w_vt), full(pool_w), full(pool_scale)],
        out_specs=[row(POOL_DIM), row(QK_DIM), row(QK_DIM),
                   pl.BlockSpec((1, tpt, V_DIM, ATT_TILE), lambda i: (i // tps, i % tps, 0, 0))],
        out_shape=[jax.ShapeDtypeStruct((n, POOL_DIM), BF16),
                   jax.ShapeDtypeStruct((n, QK_DIM), BF16),
                   jax.ShapeDtypeStruct((n, QK_DIM), BF16),
                   jax.ShapeDtypeStruct((n // seq, seq // ATT_TILE, V_DIM, ATT_TILE), BF16)],
        scratch_shapes=[pltpu.VMEM((tm + MAX_WINDOW, POOL_DIM), F32)],
        compiler_params=_params(("arbitrary",)),
    )(x2, scale1, shift1, w_main, w_vt, pool_w, pool_scale)


def _bias_kernel(tab_ref, o_ref):
    delta = pl.program_id(0)
    r = lax.broadcasted_iota(jnp.int32, (ATT_TILE, ATT_TILE), 0)
    c = lax.broadcasted_iota(jnp.int32, (ATT_TILE, ATT_TILE), 1)
    rel = r - c - delta * ATT_TILE
    half = NUM_BUCKETS // 2
    max_exact = half // 2
    ret = jnp.where(rel > 0, half, 0)
    n = jnp.abs(rel)
    nf = jnp.maximum(n, 1).astype(F32)
    large = max_exact + (jnp.log(nf / max_exact) / math.log(MAX_DISTANCE / max_exact)
                         * (half - max_exact)).astype(jnp.int32)
    large = jnp.minimum(large, half - 1)
    bucket = ret + jnp.where(n < max_exact, n, large)
    for h in range(ATTN_HEADS):
        acc = jnp.zeros((ATT_TILE, ATT_TILE), F32)
        for b in range(NUM_BUCKETS):
            acc = jnp.where(bucket == b, tab_ref[b, h], acc)
        o_ref[h, 0] = acc


def _bias_tiles(rel_table, n_tiles):
    return pl.pallas_call(
        _bias_kernel,
        grid=(n_tiles,),
        in_specs=[pl.BlockSpec(memory_space=pltpu.SMEM)],
        out_specs=pl.BlockSpec((ATTN_HEADS, 1, ATT_TILE, ATT_TILE), lambda dlt: (0, dlt, 0, 0)),
        out_shape=jax.ShapeDtypeStruct((ATTN_HEADS, n_tiles, ATT_TILE, ATT_TILE), F32),
        compiler_params=_params(("arbitrary",)),
    )(rel_table)


def _attn_kernel(q_ref, k_ref, vt_ref, bias_ref, lq1_ref, lk1_ref, lq2_ref, lk2_ref, g_ref, o_ref, *acc_refs):
    qt = pl.program_id(1)
    t = ATT_TILE
    n_maps = 2 * ATTN_HEADS
    lam = (jnp.exp(jnp.sum(lq1_ref[...] * lk1_ref[...], axis=-1, keepdims=True))
           - jnp.exp(jnp.sum(lq2_ref[...] * lk2_ref[...], axis=-1, keepdims=True))
           + LAMBDA_INIT)
    r = lax.broadcasted_iota(jnp.int32, (t, t), 0)
    c = lax.broadcasted_iota(jnp.int32, (t, t), 1)
    allowed = (r // CHUNK) <= (c // CHUNK)
    hd2 = 2 * ATTN_HEAD_DIM

    def block(kt, carry, diagonal):
        koff = pl.multiple_of(kt * t, t)

        def scores(hm):
            col = hm * ATTN_HEAD_DIM
            qh = q_ref[:, col:col + ATTN_HEAD_DIM]
            kh = k_ref[pl.ds(koff, t), col:col + ATTN_HEAD_DIM]
            s = lax.dot_general(kh, qh, (((1,), (1,)), ((), ())),
                                preferred_element_type=F32) + bias_ref[hm // 2, qt - kt]
            return jnp.where(allowed, s, -jnp.inf) if diagonal else s

        def softmax(hm, s):
            m_old, l_old = carry[2 * hm:2 * hm + 2]
            m_new = jnp.maximum(m_old, jnp.max(s, axis=0, keepdims=True))
            alpha = jnp.exp(m_old - m_new)
            p = jnp.exp(s - m_new)
            return m_new, alpha * l_old + jnp.sum(p, axis=0, keepdims=True), alpha, p.astype(BF16)

        def accumulate(hm, alpha, p):
            h = hm // 2
            vth = vt_ref[0, kt, h * hd2:(h + 1) * hd2, :]
            acc_refs[hm][...] = alpha * acc_refs[hm][...] + jnp.dot(vth, p, preferred_element_type=F32)

        s_vals, sm_vals, out = {}, {}, [None] * (2 * n_maps)
        for step in range(n_maps + 2):
            if step < n_maps:
                s_vals[step] = scores(step)
            if 0 <= step - 1 < n_maps:
                hm = step - 1
                m_new, l_new, alpha, p = softmax(hm, s_vals.pop(hm))
                out[2 * hm], out[2 * hm + 1] = m_new, l_new
                sm_vals[hm] = (alpha, p)
            if 0 <= step - 2 < n_maps:
                accumulate(step - 2, *sm_vals.pop(step - 2))
        return tuple(out)

    for acc in acc_refs:
        acc[...] = jnp.zeros_like(acc)
    one = (jnp.full((1, t), -jnp.inf, F32), jnp.zeros((1, t), F32))
    carry = lax.fori_loop(0, qt, lambda kt, cr: block(kt, cr, False), one * n_maps)
    carry = block(qt, carry, True)
    for h in range(ATTN_HEADS):
        l0, l1 = carry[4 * h + 1], carry[4 * h + 3]
        o = acc_refs[2 * h][...] / l0 - lam * (acc_refs[2 * h + 1][...] / l1)
        y = o * lax.rsqrt(jnp.mean(o * o, axis=0, keepdims=True) + LN_EPS) * g_ref[...]
        o_ref[:, h * hd2:(h + 1) * hd2] = (y * (1.0 - LAMBDA_INIT)).T.astype(BF16)


def _attention(q, k, vt, bias_tiles, lq1, lk1, lq2, lk2, subln_g, *, bsz, seq):
    t = ATT_TILE
    nt = seq // t
    full = lambda a: pl.BlockSpec(a.shape, lambda b, j: (0,) * a.ndim)
    return pl.pallas_call(
        _attn_kernel,
        grid=(bsz, nt),
        in_specs=[pl.BlockSpec((t, QK_DIM), lambda b, j: (b * nt + j, 0)),
                  pl.BlockSpec((seq, QK_DIM), lambda b, j: (b, 0)),
                  pl.BlockSpec((1, nt, V_DIM, t), lambda b, j: (b, 0, 0, 0)),
                  full(bias_tiles), full(lq1), full(lk1), full(lq2), full(lk2), full(subln_g)],
        out_specs=pl.BlockSpec((t, V_DIM), lambda b, j: (b * nt + j, 0)),
        out_shape=jax.ShapeDtypeStruct((bsz * seq, V_DIM), BF16),
        scratch_shapes=[pltpu.VMEM((2 * ATTN_HEAD_DIM, t), F32) for _ in range(2 * ATTN_HEADS)],
        compiler_params=_params(("arbitrary", "arbitrary")),
    )(q, k, vt, bias_tiles, lq1, lk1, lq2, lk2, subln_g)


def _outproj_kernel(x_ref, yp_ref, ya_ref, w_ref, gate_ref, g_ref, b_ref, o_ref):
    mix = (jnp.dot(yp_ref[...], w_ref[0:POOL_DIM, :], preferred_element_type=F32)
           + jnp.dot(ya_ref[...], w_ref[POOL_DIM:, :], preferred_element_type=F32))
    z = ALPHA * x_ref[...] + gate_ref[0] * mix
    o_ref[...] = _layer_norm(z, g_ref[...], b_ref[...])


def _outproj(x2, yp, ya, w_out, gate1, ln_g, ln_b, *, seq, tm):
    n, d = x2.shape
    tps = seq // tm
    row = lambda w: pl.BlockSpec((tm, w), lambda i: (i, 0))
    full = lambda a: pl.BlockSpec(a.shape, lambda i: (0,) * a.ndim)
    return pl.pallas_call(
        _outproj_kernel,
        grid=(n // tm,),
        in_specs=[row(d), row(POOL_DIM), row(V_DIM), full(w_out),
                  pl.BlockSpec((1, 1, d), lambda i: (i // tps, 0, 0)), full(ln_g), full(ln_b)],
        out_specs=row(d),
        out_shape=jax.ShapeDtypeStruct((n, d), F32),
        compiler_params=_params(("arbitrary",)),
    )(x2, yp, ya, w_out, gate1, ln_g, ln_b)


def _route_kernel(x_ref, sc_ref, sh_ref, whi_ref, wlo_ref, rb_ref,
                  eidx_ref, rank_ref, wts_ref, cnt_ref, carry_ref, *, tr):
    i = pl.program_id(0)

    @pl.when(i == 0)
    def _():
        carry_ref[...] = jnp.zeros_like(carry_ref)

    h2 = x_ref[...] * (1.0 + sc_ref[0]) + sh_ref[0]
    hi = h2.astype(BF16)
    lo = (h2 - hi.astype(F32)).astype(BF16)
    logits = (jnp.dot(hi, whi_ref[...], preferred_element_type=F32)
              + jnp.dot(hi, wlo_ref[...], preferred_element_type=F32)
              + jnp.dot(lo, whi_ref[...], preferred_element_type=F32))
    scores = _sigmoid(logits)
    sel = scores + rb_ref[...]
    lane = lax.broadcasted_iota(jnp.int32, (tr, N_EXPERTS), 1)
    lane_f = lane.astype(F32)

    a1 = sel
    a2 = jnp.full_like(sel, -jnp.inf)
    s = 1
    while s < GROUP_SIZE:
        upper = (lane & s) != 0
        b1 = jnp.where(upper, pltpu.roll(a1, s, 1), pltpu.roll(a1, N_EXPERTS - s, 1))
        b2 = jnp.where(upper, pltpu.roll(a2, s, 1), pltpu.roll(a2, N_EXPERTS - s, 1))
        a1, a2 = jnp.maximum(a1, b1), jnp.maximum(jnp.minimum(a1, b1), jnp.maximum(a2, b2))
        s *= 2
    gscore = a1 + a2

    grp = lane // GROUP_SIZE
    beaten_by = jnp.zeros((tr, N_EXPERTS), jnp.int32)
    for kk in range(1, N_GROUPS):
        other = pltpu.roll(gscore, kk * GROUP_SIZE, 1)
        ogrp = (grp - kk) & (N_GROUPS - 1)
        wins = (other > gscore) | ((other == gscore) & (ogrp < grp))
        beaten_by = beaten_by + wins.astype(jnp.int32)
    cur = jnp.where(beaten_by < TOP_K_GROUPS, sel, -jnp.inf)

    picks, weights = [], []
    selmask = jnp.zeros((tr, N_EXPERTS), F32)
    for _ in range(TOP_K):
        mx = jnp.max(cur, axis=-1, keepdims=True)
        pick = jnp.min(jnp.where(cur == mx, lane_f, float(N_EXPERTS)), axis=-1, keepdims=True)
        onehot = lane_f == pick
        weights.append(jnp.sum(jnp.where(onehot, scores, 0.0), axis=-1, keepdims=True))
        cur = jnp.where(onehot, -jnp.inf, cur)
        selmask = jnp.where(onehot, 1.0, selmask)
        picks.append(pick)

    rr = lax.broadcasted_iota(jnp.int32, (tr, tr), 0)
    cc = lax.broadcasted_iota(jnp.int32, (tr, tr), 1)
    earlier = jnp.where(cc < rr, 1.0, 0.0).astype(BF16)
    rankmat = jnp.dot(earlier, selmask.astype(BF16), preferred_element_type=F32) + carry_ref[...]
    carry_ref[...] = carry_ref[...] + jnp.sum(selmask, axis=0, keepdims=True)
    cnt_ref[...] = carry_ref[...]

    wsum = weights[0]
    for wj in weights[1:]:
        wsum = wsum + wj
    lane8 = lax.broadcasted_iota(jnp.int32, (tr, TOP_K), 1)
    eidx = jnp.zeros((tr, TOP_K), jnp.int32)
    rank = jnp.zeros((tr, TOP_K), jnp.int32)
    wts = jnp.zeros((tr, TOP_K), F32)
    for j in range(TOP_K):
        rk = jnp.sum(jnp.where(lane_f == picks[j], rankmat, 0.0), axis=-1, keepdims=True)
        eidx = jnp.where(lane8 == j, picks[j].astype(jnp.int32), eidx)
        rank = jnp.where(lane8 == j, rk.astype(jnp.int32), rank)
        wts = jnp.where(lane8 == j, weights[j] / wsum * ROUTED_SCALE, wts)
    eidx_ref[...] = eidx
    rank_ref[...] = rank
    wts_ref[...] = wts


def _route(x1, scale2, shift2, wr_hi, wr_lo, router_bias, *, seq, tr):
    n, d = x1.shape
    tps = seq // tr
    mod_spec = pl.BlockSpec((1, 1, d), lambda i: (i // tps, 0, 0))
    full = lambda a: pl.BlockSpec(a.shape, lambda i: (0,) * a.ndim)
    k8 = pl.BlockSpec((tr, TOP_K), lambda i: (i, 0))
    return pl.pallas_call(
        functools.partial(_route_kernel, tr=tr),
        grid=(n // tr,),
        in_specs=[pl.BlockSpec((tr, d), lambda i: (i, 0)), mod_spec, mod_spec,
                  full(wr_hi), full(wr_lo), full(router_bias)],
        out_specs=[k8, k8, k8, pl.BlockSpec((1, N_EXPERTS), lambda i: (0, 0))],
        out_shape=[jax.ShapeDtypeStruct((n, TOP_K), jnp.int32),
                   jax.ShapeDtypeStruct((n, TOP_K), jnp.int32),
                   jax.ShapeDtypeStruct((n, TOP_K), F32),
                   jax.ShapeDtypeStruct((1, N_EXPERTS), F32)],
        scratch_shapes=[pltpu.VMEM((1, N_EXPERTS), F32)],
        compiler_params=_params(("arbitrary",)),
    )(x1, scale2, shift2, wr_hi, wr_lo, router_bias)


def _zero_pad_rows(cnt_ref, pst_ref, nused_ref, zero_ref, xs_ref, sem):
    m = DISPATCH_BLOCK
    sub = 8

    def copies(e, wait):
        cnt = cnt_ref[e]
        n_pad = (-cnt) & (m - 1)
        start = pst_ref[e] + cnt
        head = n_pad & (sub - 1)

        def go(n_rows, pos):
            cp = pltpu.make_async_copy(zero_ref.at[pl.ds(0, n_rows), :], xs_ref.at[pl.ds(pos, n_rows), :], sem)
            cp.wait() if wait else cp.start()

        for r in range(sub - 1):
            pl.when(r < head)(functools.partial(go, 1, start + r))
        bit = m // 2
        while bit >= sub:
            pos = pl.multiple_of(start + head + (n_pad & ~(2 * bit - 1) & ~(sub - 1)), sub)
            pl.when((n_pad & bit) != 0)(functools.partial(go, bit, pos))
            bit //= 2

    def start_all(e, carry):
        copies(e, False)
        return carry

    def wait_all(e, carry):
        copies(e, True)
        return carry

    lax.fori_loop(0, N_EXPERTS, start_all, 0)
    lax.fori_loop(0, N_EXPERTS, wait_all, 0)

    half = m // 2
    n_halves = xs_ref.shape[0] // half

    def tail(wait):
        def body(i, carry):
            cp = pltpu.make_async_copy(zero_ref, xs_ref.at[pl.ds(pl.multiple_of(i * half, half), half), :], sem)
            cp.wait() if wait else cp.start()
            return carry
        return body

    first = 2 * nused_ref[0]
    lax.fori_loop(first, n_halves, tail(False), 0)
    lax.fori_loop(first, n_halves, tail(True), 0)


def _pack_bf16_halves(x):
    w = x.shape[1] // 2
    lo = pltpu.bitcast(x[:, :w].astype(BF16).astype(F32), U32) >> 16
    hi = pltpu.bitcast(x[:, w:].astype(BF16).astype(F32), U32) & jnp.uint32(0xFFFF0000)
    return lo | hi


def _unpack_bf16_halves(p):
    return pltpu.bitcast(p << 16, F32), pltpu.bitcast(p & jnp.uint32(0xFFFF0000), F32)


def _dispatch_kernel(x_ref, sc_ref, sh_ref, dest_ref, pst_ref, cnt_ref, nused_ref, xs_ref,
                     h2_ref, zero_ref, sem, pad_sem, *, td):
    @pl.when(pl.program_id(0) == 0)
    def _():
        zero_ref[...] = jnp.zeros_like(zero_ref)
        _zero_pad_rows(cnt_ref, pst_ref, nused_ref, zero_ref, xs_ref, pad_sem)

    h2_ref[...] = _pack_bf16_halves(x_ref[...] * (1.0 + sc_ref[0]) + sh_ref[0])

    def issue(t, carry):
        for j in range(TOP_K):
            d = dest_ref[t * TOP_K + j]
            pltpu.make_async_copy(h2_ref.at[pl.ds(t, 1), :], xs_ref.at[pl.ds(d, 1), :], sem).start()
        return carry

    lax.fori_loop(0, td, issue, 0)
    for _ in range(TOP_K):
        pltpu.make_async_copy(h2_ref, xs_ref.at[pl.ds(0, td), :], sem).wait()


def _dispatch(x1, scale2, shift2, dest_flat, pstarts, counts, n_used, n_slots, *, seq, td):
    n, d = x1.shape
    tps = seq // td
    mod_spec = pl.BlockSpec((1, 1, d), lambda i: (i // tps, 0, 0))
    k_spec = pl.BlockSpec((td * TOP_K,), lambda i: (i,), memory_space=pltpu.SMEM)
    smem = pl.BlockSpec(memory_space=pltpu.SMEM)
    return pl.pallas_call(
        functools.partial(_dispatch_kernel, td=td),
        grid=(n // td,),
        in_specs=[pl.BlockSpec((td, d), lambda i: (i, 0)), mod_spec, mod_spec, k_spec, smem, smem, smem],
        out_specs=pl.BlockSpec(memory_space=pl.ANY),
        out_shape=jax.ShapeDtypeStruct((n_slots, d // 2), U32),
        scratch_shapes=[pltpu.VMEM((td, d // 2), U32), pltpu.VMEM((DISPATCH_BLOCK // 2, d // 2), U32),
                        pltpu.SemaphoreType.DMA(()), pltpu.SemaphoreType.DMA(())],
        compiler_params=_params(("arbitrary",)),
    )(x1, scale2, shift2, dest_flat, pstarts, counts, n_used)


def _dest_kernel(eidx_ref, rank_ref, first_ref, dest_ref):
    tt = eidx_ref.shape[1]
    pieces = [jnp.broadcast_to(first_ref[:, p * LANES:(p + 1) * LANES], (TOP_K, LANES))
              for p in range(N_EXPERTS // LANES)]
    for c in range(tt // LANES):
        cols = slice(c * LANES, (c + 1) * LANES)
        e = eidx_ref[:, cols]
        within = e & (LANES - 1)
        start = jnp.take_along_axis(pieces[0], within, axis=1)
        for p in range(1, len(pieces)):
            start = jnp.where(e // LANES == p, jnp.take_along_axis(pieces[p], within, axis=1), start)
        dest_ref[:, cols] = start + rank_ref[:, cols]


def _dest(eidx_t, rank_t, pstarts, *, tt):
    n = eidx_t.shape[1]
    first = pstarts.reshape(1, N_EXPERTS)
    k8 = pl.BlockSpec((TOP_K, tt), lambda i: (0, i))
    return pl.pallas_call(
        _dest_kernel,
        grid=(n // tt,),
        in_specs=[k8, k8, pl.BlockSpec(first.shape, lambda i: (0, 0))],
        out_specs=k8,
        out_shape=jax.ShapeDtypeStruct((TOP_K, n), jnp.int32),
        compiler_params=_params(("arbitrary",)),
    )(eidx_t, rank_t, first)


def _expert_kernel(fb_ref, nb_ref, nused_ref, wg_ref, wu_ref, wd_ref, xs_ref, ys_ref,
                   wgb, wub, wdb, xbuf, ybuf, in_sem, out_sem):
    e = pl.program_id(0)
    m = DISPATCH_BLOCK
    ns = EXPERT_SLOTS
    n_used = nused_ref[0]

    def rows(g):
        return pl.ds(pl.multiple_of(g * m, m), m)

    def fetch(g):
        slot = g & (ns - 1)
        return pltpu.make_async_copy(xs_ref.at[rows(g), :], xbuf.at[slot], in_sem.at[slot])

    def put(g):
        slot = g & (ns - 1)
        return pltpu.make_async_copy(ybuf.at[slot], ys_ref.at[rows(g), :], out_sem.at[slot])

    @pl.when(e == 0)
    def _():
        for g0 in range(ns - 1):
            @pl.when(g0 < n_used)
            def _(g0=g0):
                fetch(g0).start()

    wgb[...] = wg_ref[0].astype(BF16)
    wub[...] = wu_ref[0].astype(BF16)
    wdb[...] = wd_ref[0].astype(BF16)

    def block(i, carry):
        g = fb_ref[e] + i
        slot = g & (ns - 1)
        fetch(g).wait()

        @pl.when(g + ns - 1 < n_used)
        def _():
            fetch(g + ns - 1).start()

        @pl.when(g >= ns)
        def _():
            put(g - ns).wait()

        x_lo, x_hi = [h.astype(BF16) for h in _unpack_bf16_halves(xbuf[slot])]
        half = x_lo.shape[1]

        def up_proj(w):
            return (jnp.dot(x_lo, w[0:half, :], preferred_element_type=F32)
                    + jnp.dot(x_hi, w[half:, :], preferred_element_type=F32))

        a = (_silu(up_proj(wgb)) * up_proj(wub)).astype(BF16)
        ybuf[slot] = _pack_bf16_halves(jnp.dot(a, wdb[...], preferred_element_type=F32))
        put(g).start()
        return carry

    lax.fori_loop(0, nb_ref[e], block, 0)

    @pl.when(e == pl.num_programs(0) - 1)
    def _():
        for back in range(ns, 0, -1):
            @pl.when(n_used >= back)
            def _(back=back):
                put(n_used - back).wait()

        ybuf[0] = jnp.zeros(ybuf.shape[1:], ybuf.dtype)
        n_blocks = ys_ref.shape[0] // m

        def tail(wait):
            def body(g, carry):
                cp = pltpu.make_async_copy(ybuf.at[0], ys_ref.at[rows(g), :], out_sem.at[0])
                cp.wait() if wait else cp.start()
                return carry
            return body

        lax.fori_loop(n_used, n_blocks, tail(False), 0)
        lax.fori_loop(n_used, n_blocks, tail(True), 0)


def _experts(first_block, n_blocks_e, n_used, xs, w_gate, w_up, w_down):
    p, dp = xs.shape
    m = DISPATCH_BLOCK
    n_e, d, f = w_gate.shape
    assert d == 2 * dp and p % m == 0
    grid_spec = pltpu.PrefetchScalarGridSpec(
        num_scalar_prefetch=3,
        grid=(n_e,),
        in_specs=[pl.BlockSpec((1, d, f), lambda e, *_: (e, 0, 0)),
                  pl.BlockSpec((1, d, f), lambda e, *_: (e, 0, 0)),
                  pl.BlockSpec((1, f, d), lambda e, *_: (e, 0, 0)),
                  pl.BlockSpec(memory_space=pl.ANY)],
        out_specs=pl.BlockSpec(memory_space=pl.ANY),
        scratch_shapes=[pltpu.VMEM((d, f), BF16), pltpu.VMEM((d, f), BF16), pltpu.VMEM((f, d), BF16),
                        pltpu.VMEM((EXPERT_SLOTS, m, dp), U32), pltpu.VMEM((EXPERT_SLOTS, m, dp), U32),
                        pltpu.SemaphoreType.DMA((EXPERT_SLOTS,)), pltpu.SemaphoreType.DMA((EXPERT_SLOTS,))],
    )
    return pl.pallas_call(
        _expert_kernel,
        grid_spec=grid_spec,
        out_shape=jax.ShapeDtypeStruct((p, dp), U32),
        compiler_params=_params(("arbitrary",)),
    )(first_block, n_blocks_e, n_used, w_gate, w_up, w_down, xs)


def _combine_kernel(x_ref, sc_ref, sh_ref, gate_ref, dest_ref, dest_next_ref, wts_ref, ys_ref,
                    wsg_ref, wsu_ref, wsd_ref, g_ref, b_ref, o_ref, *scratch, tc):
    gbufs, sems = scratch[:TOP_K], scratch[TOP_K]
    i = pl.program_id(0)
    slot = i & 1

    def gather(idx_ref, into):
        def issue(t, carry):
            for j in range(TOP_K):
                d = idx_ref[t * TOP_K + j]
                pltpu.make_async_copy(ys_ref.at[pl.ds(d, 1), :], gbufs[j].at[into, pl.ds(t, 1), :],
                                      sems.at[into]).start()
            return carry
        lax.fori_loop(0, tc, issue, 0)

    @pl.when(i == 0)
    def _():
        gather(dest_ref, 0)

    @pl.when(i + 1 < pl.num_programs(0))
    def _():
        gather(dest_next_ref, 1 - slot)

    x = x_ref[...]
    hb = (x * (1.0 + sc_ref[0]) + sh_ref[0]).astype(BF16)
    sg = jnp.dot(hb, wsg_ref[...], preferred_element_type=F32)
    su = jnp.dot(hb, wsu_ref[...], preferred_element_type=F32)
    shared = jnp.dot((_silu(sg) * su).astype(BF16), wsd_ref[...], preferred_element_type=F32)

    for j in range(TOP_K):
        pltpu.make_async_copy(ys_ref.at[pl.ds(0, tc), :], gbufs[j].at[slot], sems.at[slot]).wait()
    wts = wts_ref[...]
    half = shared.shape[1] // 2
    lo, hi = shared[:, :half], shared[:, half:]
    for j in range(TOP_K):
        y_lo, y_hi = _unpack_bf16_halves(gbufs[j][slot])
        lo = lo + wts[:, j:j + 1] * y_lo
        hi = hi + wts[:, j:j + 1] * y_hi
    z = ALPHA * x + gate_ref[0] * jnp.concatenate([lo, hi], axis=1)
    o_ref[...] = _layer_norm(z, g_ref[...], b_ref[...])


def _combine(x1, scale2, shift2, gate2, dest_flat, wts, ys, ws_gate, ws_up, ws_down, ln_g, ln_b, *, seq, tc):
    n, d = x1.shape
    tps = seq // tc
    mod_spec = pl.BlockSpec((1, 1, d), lambda i: (i // tps, 0, 0))
    full = lambda a: pl.BlockSpec(a.shape, lambda i: (0,) * a.ndim)
    last = n // tc - 1
    return pl.pallas_call(
        functools.partial(_combine_kernel, tc=tc),
        grid=(n // tc,),
        in_specs=[pl.BlockSpec((tc, d), lambda i: (i, 0)), mod_spec, mod_spec, mod_spec,
                  pl.BlockSpec((tc * TOP_K,), lambda i: (i,), memory_space=pltpu.SMEM),
                  pl.BlockSpec((tc * TOP_K,), lambda i: (jnp.minimum(i + 1, last),), memory_space=pltpu.SMEM),
                  pl.BlockSpec((tc, TOP_K), lambda i: (i, 0)),
                  pl.BlockSpec(memory_space=pl.ANY),
                  full(ws_gate), full(ws_up), full(ws_down), full(ln_g), full(ln_b)],
        out_specs=pl.BlockSpec((tc, d), lambda i: (i, 0)),
        out_shape=jax.ShapeDtypeStruct((n, d), F32),
        scratch_shapes=([pltpu.VMEM((2, tc, d // 2), U32) for _ in range(TOP_K)]
                        + [pltpu.SemaphoreType.DMA((2,))]),
        compiler_params=_params(("arbitrary",)),
    )(x1, scale2, shift2, gate2, dest_flat, dest_flat, wts, ys, ws_gate, ws_up, ws_down, ln_g, ln_b)


def _layer(x, c, w_ada, b_ada, w_in, pool_w, pool_scale, lq1, lk1, lq2, lk2, subln_g, w_out,
           ln1_g, ln1_b, w_router, router_bias, w_gate, w_up, w_down, ws_gate, ws_up, ws_down,
           ln2_g, ln2_b, rel_table, *, tm=512, tr=256, td=256, tc=256):
    bsz, seq, d = x.shape
    n = bsz * seq
    x2 = x.reshape(n, d)
    row = lambda a: a.reshape(1, -1)

    mod = _modulation(c, w_ada, b_ada)
    shift1, scale1, gate1, shift2, scale2, gate2 = [
        mod[:, j * d:(j + 1) * d].reshape(bsz, 1, d) for j in range(6)]

    n_main = POOL_DIM + 2 * QK_DIM
    yp, q, k, vt = _inproj(x2, scale1, shift1, w_in[:, :n_main].astype(BF16), w_in[:, n_main:].T.astype(BF16),
                           pool_w.astype(BF16), row(pool_scale), seq=seq, tm=tm)
    bias_tiles = _bias_tiles(rel_table, seq // ATT_TILE)
    ya = _attention(q, k, vt, bias_tiles, row(lq1), row(lk1), row(lq2), row(lk2), subln_g.reshape(-1, 1),
                    bsz=bsz, seq=seq)
    x1 = _outproj(x2, yp, ya, w_out.astype(BF16), gate1, row(ln1_g), row(ln1_b), seq=seq, tm=tm)

    wr_hi = w_router.astype(BF16)
    wr_lo = (w_router - wr_hi.astype(F32)).astype(BF16)
    eidx, rank, wts, cnt = _route(x1, scale2, shift2, wr_hi, wr_lo, row(router_bias), seq=seq, tr=tr)

    m = DISPATCH_BLOCK
    counts = cnt[0].astype(jnp.int32)
    padded = (counts + m - 1) // m * m
    pends = jnp.cumsum(padded)
    pstarts = pends - padded
    n_blocks = -(-(n * TOP_K + N_EXPERTS * (m - 1)) // m)
    n_used = (pends[-1:] // m).astype(jnp.int32)
    dest_flat = _dest(eidx.T, rank.T, pstarts, tt=min(n, 2048)).T.reshape(n * TOP_K)

    xs = _dispatch(x1, scale2, shift2, dest_flat, pstarts, counts, n_used, n_blocks * m, seq=seq, td=td)
    ys = _experts(pstarts // m, padded // m, n_used, xs, w_gate, w_up, w_down)
    out = _combine(x1, scale2, shift2, gate2, dest_flat, wts, ys,
                   ws_gate.astype(BF16), ws_up.astype(BF16), ws_down.astype(BF16),
                   row(ln2_g), row(ln2_b), seq=seq, tc=tc)
    return out.reshape(bsz, seq, d)


def kernel(x, c, w_ada, b_ada, w_in, pool_w, pool_scale, lambda_q1, lambda_k1, lambda_q2, lambda_k2,
           subln_g, w_out, ln1_g, ln1_b, w_router, router_bias, w_gate, w_up, w_down,
           ws_gate, ws_up, ws_down, ln2_g, ln2_b, rel_table):
    per_layer = (w_ada, b_ada, w_in, pool_w, pool_scale, lambda_q1, lambda_k1, lambda_q2, lambda_k2,
                 subln_g, w_out, ln1_g, ln1_b, w_router, router_bias, w_gate, w_up, w_down,
                 ws_gate, ws_up, ws_down, ln2_g, ln2_b)
    assert all(a.shape[0] == DEPTH == 1 for a in per_layer)
    return _layer(x, c, *[a.reshape(a.shape[1:]) for a in per_layer], rel_table)
```

```python
import functools
import math

import jax
import jax.numpy as jnp
from jax import lax
from jax.experimental import pallas as pl
from jax.experimental.pallas import tpu as pltpu
from jax.experimental.pallas import tpu_sc as plsc

F32 = jnp.float32
BF16 = jnp.bfloat16
U32 = jnp.uint32
LANES = 128

D_MODEL = 1024
CHUNK = 64
Q_BLOCK = 128
ATT_TILE = 256
POOL_DIM = 512
POOL_WINDOWS = (2, 4, 8, 16)
POOL_GROUP_DIM = 128
MAX_WINDOW = max(POOL_WINDOWS)
ATTN_HEADS = 4
ATTN_HEAD_DIM = 64
QK_DIM = 512
V_DIM = 512
IN_DIM = 2048
NUM_BUCKETS = 32
MAX_DISTANCE = 128
N_EXPERTS = 256
TOP_K = 8
N_GROUPS = 8
GROUP_SIZE = N_EXPERTS // N_GROUPS
TOP_K_GROUPS = 4
EXPERT_DIM = 256
ROUTED_SCALE = 2.5
DISPATCH_BLOCK = 256
EXPERT_SLOTS = 4
SC_CORES, SC_SUBCORES = 2, 16
SC_GATHER_ROWS = 64
DEPTH = 1
ALPHA = (2.0 * DEPTH) ** 0.25
LN_EPS = 1e-5
LAMBDA_INIT = 0.8 - 0.6 * math.exp(-0.3 * 0)

VMEM_LIMIT = 48 * 1024 * 1024


def _sigmoid(x):
    return 1.0 / (1.0 + jnp.exp(-x))


def _silu(x):
    return x * _sigmoid(x)


def _layer_norm(z, g, b):
    mu = jnp.mean(z, axis=-1, keepdims=True)
    zc = z - mu
    var = jnp.mean(zc * zc, axis=-1, keepdims=True)
    return zc * lax.rsqrt(var + LN_EPS) * g + b


def _params(sem=None):
    return pltpu.CompilerParams(dimension_semantics=sem, vmem_limit_bytes=VMEM_LIMIT)


def _mod_kernel(c_ref, w_ref, b_ref, o_ref):
    ca = _silu(c_ref[...])
    o_ref[...] = jnp.dot(ca, w_ref[...], preferred_element_type=F32,
                         precision=lax.Precision.HIGHEST) + b_ref[...]


def _modulation(c, w_ada, b_ada):
    bsz, d = c.shape
    n_out = w_ada.shape[1]
    return pl.pallas_call(
        _mod_kernel,
        grid=(n_out // d,),
        in_specs=[pl.BlockSpec((bsz, d), lambda j: (0, 0)),
                  pl.BlockSpec((d, d), lambda j: (0, j)),
                  pl.BlockSpec((1, d), lambda j: (0, j))],
        out_specs=pl.BlockSpec((bsz, d), lambda j: (0, j)),
        out_shape=jax.ShapeDtypeStruct((bsz, n_out), F32),
        compiler_params=_params(("arbitrary",)),
    )(c, w_ada, b_ada.reshape(1, n_out))


def _inproj_kernel(x_ref, sc_ref, sh_ref, w_ref, wvt_ref, pw_ref, ps_ref,
                   yp_ref, q_ref, k_ref, vt_ref, ext_ref, *, tm, seq):
    i = pl.program_id(0)
    tiles_per_seq = seq // tm
    it = i % tiles_per_seq
    h = x_ref[...] * (1.0 + sc_ref[0]) + sh_ref[0]
    hb = h.astype(BF16)
    proj = jnp.dot(hb, w_ref[...], preferred_element_type=F32)
    u = proj[:, :POOL_DIM]
    q_ref[...] = (proj[:, POOL_DIM:POOL_DIM + QK_DIM] * (ATTN_HEAD_DIM ** -0.5)).astype(BF16)
    k_ref[...] = proj[:, POOL_DIM + QK_DIM:POOL_DIM + 2 * QK_DIM].astype(BF16)
    vt = lax.dot_general(wvt_ref[...], hb, (((1,), (1,)), ((), ())), preferred_element_type=F32)
    for j in range(tm // ATT_TILE):
        vt_ref[0, j] = vt[:, j * ATT_TILE:(j + 1) * ATT_TILE].astype(BF16)

    @pl.when(it == 0)
    def _():
        ext_ref[0:MAX_WINDOW, :] = jnp.zeros((MAX_WINDOW, POOL_DIM), F32)

    ext_ref[MAX_WINDOW:MAX_WINDOW + tm, :] = u
    pos = (it * tm + lax.broadcasted_iota(jnp.int32, (tm, 1), 0) + 1).astype(F32)
    for g, w in enumerate(POOL_WINDOWS):
        c0, c1 = g * POOL_GROUP_DIM, (g + 1) * POOL_GROUP_DIM
        s = ext_ref[MAX_WINDOW:MAX_WINDOW + tm, c0:c1]
        for j in range(1, w):
            s = s + ext_ref[MAX_WINDOW - j:MAX_WINDOW - j + tm, c0:c1]
        pooled = s / jnp.minimum(pos, float(w)) - u[:, c0:c1]
        y = jnp.dot(pooled.astype(BF16), pw_ref[g], preferred_element_type=F32)
        yp_ref[:, c0:c1] = (y * ps_ref[:, c0:c1]).astype(BF16)
    ext_ref[0:MAX_WINDOW, :] = ext_ref[tm:tm + MAX_WINDOW, :]


def _inproj(x2, scale1, shift1, w_main, w_vt, pool_w, pool_scale, *, seq, tm):
    n, d = x2.shape
    assert n % tm == 0 and seq % tm == 0 and tm >= 2 * MAX_WINDOW and tm % ATT_TILE == 0
    tps = seq // tm
    tpt = tm // ATT_TILE
    mod_spec = pl.BlockSpec((1, 1, d), lambda i: (i // tps, 0, 0))
    row = lambda w: pl.BlockSpec((tm, w), lambda i: (i, 0))
    full = lambda a: pl.BlockSpec(a.shape, lambda i: (0,) * a.ndim)
    return pl.pallas_call(
        functools.partial(_inproj_kernel, tm=tm, seq=seq),
        grid=(n // tm,),
        in_specs=[row(d), mod_spec, mod_spec, full(w_main), full(w_vt), full(pool_w), full(pool_scale)],
        out_specs=[row(POOL_DIM), row(QK_DIM), row(QK_DIM),
                   pl.BlockSpec((1, tpt, V_DIM, ATT_TILE), lambda i: (i // tps, i % tps, 0, 0))],
        out_shape=[jax.ShapeDtypeStruct((n, POOL_DIM), BF16),
                   jax.ShapeDtypeStruct((n, QK_DIM), BF16),
                   jax.ShapeDtypeStruct((n, QK_DIM), BF16),
                   jax.ShapeDtypeStruct((n // seq, seq // ATT_TILE, V_DIM, ATT_TILE), BF16)],
        scratch_shapes=[pltpu.VMEM((tm + MAX_WINDOW, POOL_DIM), F32)],
        compiler_params=_params(("arbitrary",)),
    )(x2, scale1, shift1, w_main, w_vt, pool_w, pool_scale)


def _bias_kernel(tab_ref, o_ref):
    delta = pl.program_id(0)
    r = lax.broadcasted_iota(jnp.int32, (ATT_TILE, ATT_TILE), 0)
    c = lax.broadcasted_iota(jnp.int32, (ATT_TILE, ATT_TILE), 1)
    rel = r - c - delta * ATT_TILE
    half = NUM_BUCKETS // 2
    max_exact = half // 2
    ret = jnp.where(rel > 0, half, 0)
    n = jnp.abs(rel)
    nf = jnp.maximum(n, 1).astype(F32)
    large = max_exact + (jnp.log(nf / max_exact) / math.log(MAX_DISTANCE / max_exact)
                         * (half - max_exact)).astype(jnp.int32)
    large = jnp.minimum(large, half - 1)
    bucket = ret + jnp.where(n < max_exact, n, large)
    for h in range(ATTN_HEADS):
        acc = jnp.zeros((ATT_TILE, ATT_TILE), F32)
        for b in range(NUM_BUCKETS):
            acc = jnp.where(bucket == b, tab_ref[b, h], acc)
        o_ref[h, 0] = acc


def _bias_tiles(rel_table, n_tiles):
    return pl.pallas_call(
        _bias_kernel,
        grid=(n_tiles,),
        in_specs=[pl.BlockSpec(memory_space=pltpu.SMEM)],
        out_specs=pl.BlockSpec((ATTN_HEADS, 1, ATT_TILE, ATT_TILE), lambda dlt: (0, dlt, 0, 0)),
        out_shape=jax.ShapeDtypeStruct((ATTN_HEADS, n_tiles, ATT_TILE, ATT_TILE), F32),
        compiler_params=_params(("arbitrary",)),
    )(rel_table)


def _attn_kernel(q_ref, k_ref, vt_ref, bias_ref, lq1_ref, lk1_ref, lq2_ref, lk2_ref, g_ref, o_ref, *acc_refs):
    qt = pl.program_id(1)
    t = ATT_TILE
    n_maps = 2 * ATTN_HEADS
    lam = (jnp.exp(jnp.sum(lq1_ref[...] * lk1_ref[...], axis=-1, keepdims=True))
           - jnp.exp(jnp.sum(lq2_ref[...] * lk2_ref[...], axis=-1, keepdims=True))
           + LAMBDA_INIT)
    r = lax.broadcasted_iota(jnp.int32, (t, t), 0)
    c = lax.broadcasted_iota(jnp.int32, (t, t), 1)
    allowed = (r // CHUNK) <= (c // CHUNK)
    hd2 = 2 * ATTN_HEAD_DIM

    def block(kt, carry, diagonal):
        koff = pl.multiple_of(kt * t, t)

        def scores(hm):
            col = hm * ATTN_HEAD_DIM
            qh = q_ref[:, col:col + ATTN_HEAD_DIM]
            kh = k_ref[pl.ds(koff, t), col:col + ATTN_HEAD_DIM]
            s = lax.dot_general(kh, qh, (((1,), (1,)), ((), ())),
                                preferred_element_type=F32) + bias_ref[hm // 2, qt - kt]
            return jnp.where(allowed, s, -jnp.inf) if diagonal else s

        def softmax(hm, s):
            m_old, l_old = carry[2 * hm:2 * hm + 2]
            m_new = jnp.maximum(m_old, jnp.max(s, axis=0, keepdims=True))
            alpha = jnp.exp(m_old - m_new)
            p = jnp.exp(s - m_new)
            return m_new, alpha * l_old + jnp.sum(p, axis=0, keepdims=True), alpha, p.astype(BF16)

        def accumulate(hm, alpha, p):
            h = hm // 2
            vth = vt_ref[0, kt, h * hd2:(h + 1) * hd2, :]
            acc_refs[hm][...] = alpha * acc_refs[hm][...] + jnp.dot(vth, p, preferred_element_type=F32)

        s_vals, sm_vals, out = {}, {}, [None] * (2 * n_maps)
        for step in range(n_maps + 2):
            if step < n_maps:
                s_vals[step] = scores(step)
            if 0 <= step - 1 < n_maps:
                hm = step - 1
                m_new, l_new, alpha, p = softmax(hm, s_vals.pop(hm))
                out[2 * hm], out[2 * hm + 1] = m_new, l_new
                sm_vals[hm] = (alpha, p)
            if 0 <= step - 2 < n_maps:
                accumulate(step - 2, *sm_vals.pop(step - 2))
        return tuple(out)

    for acc in acc_refs:
        acc[...] = jnp.zeros_like(acc)
    one = (jnp.full((1, t), -jnp.inf, F32), jnp.zeros((1, t), F32))
    carry = lax.fori_loop(0, qt, lambda kt, cr: block(kt, cr, False), one * n_maps)
    carry = block(qt, carry, True)
    for h in range(ATTN_HEADS):
        l0, l1 = carry[4 * h + 1], carry[4 * h + 3]
        o = acc_refs[2 * h][...] / l0 - lam * (acc_refs[2 * h + 1][...] / l1)
        y = o * lax.rsqrt(jnp.mean(o * o, axis=0, keepdims=True) + LN_EPS) * g_ref[...]
        o_ref[:, h * hd2:(h + 1) * hd2] = (y * (1.0 - LAMBDA_INIT)).T.astype(BF16)


def _attention(q, k, vt, bias_tiles, lq1, lk1, lq2, lk2, subln_g, *, bsz, seq):
    t = ATT_TILE
    nt = seq // t
    full = lambda a: pl.BlockSpec(a.shape, lambda b, j: (0,) * a.ndim)
    return pl.pallas_call(
        _attn_kernel,
        grid=(bsz, nt),
        in_specs=[pl.BlockSpec((t, QK_DIM), lambda b, j: (b * nt + j, 0)),
                  pl.BlockSpec((seq, QK_DIM), lambda b, j: (b, 0)),
                  pl.BlockSpec((1, nt, V_DIM, t), lambda b, j: (b, 0, 0, 0)),
                  full(bias_tiles), full(lq1), full(lk1), full(lq2), full(lk2), full(subln_g)],
        out_specs=pl.BlockSpec((t, V_DIM), lambda b, j: (b * nt + j, 0)),
        out_shape=jax.ShapeDtypeStruct((bsz * seq, V_DIM), BF16),
        scratch_shapes=[pltpu.VMEM((2 * ATTN_HEAD_DIM, t), F32) for _ in range(2 * ATTN_HEADS)],
        compiler_params=_params(("arbitrary", "arbitrary")),
    )(q, k, vt, bias_tiles, lq1, lk1, lq2, lk2, subln_g)


def _outproj_kernel(x_ref, yp_ref, ya_ref, w_ref, gate_ref, g_ref, b_ref, o_ref):
    mix = (jnp.dot(yp_ref[...], w_ref[0:POOL_DIM, :], preferred_element_type=F32)
           + jnp.dot(ya_ref[...], w_ref[POOL_DIM:, :], preferred_element_type=F32))
    z = ALPHA * x_ref[...] + gate_ref[0] * mix
    o_ref[...] = _layer_norm(z, g_ref[...], b_ref[...])


def _outproj(x2, yp, ya, w_out, gate1, ln_g, ln_b, *, seq, tm):
    n, d = x2.shape
    tps = seq // tm
    row = lambda w: pl.BlockSpec((tm, w), lambda i: (i, 0))
    full = lambda a: pl.BlockSpec(a.shape, lambda i: (0,) * a.ndim)
    return pl.pallas_call(
        _outproj_kernel,
        grid=(n // tm,),
        in_specs=[row(d), row(POOL_DIM), row(V_DIM), full(w_out),
                  pl.BlockSpec((1, 1, d), lambda i: (i // tps, 0, 0)), full(ln_g), full(ln_b)],
        out_specs=row(d),
        out_shape=jax.ShapeDtypeStruct((n, d), F32),
        compiler_params=_params(("arbitrary",)),
    )(x2, yp, ya, w_out, gate1, ln_g, ln_b)


def _route_kernel(x_ref, sc_ref, sh_ref, whi_ref, wlo_ref, rb_ref,
                  eidx_ref, rank_ref, wts_ref, cnt_ref, carry_ref, *, tr):
    i = pl.program_id(0)

    @pl.when(i == 0)
    def _():
        carry_ref[...] = jnp.zeros_like(carry_ref)

    h2 = x_ref[...] * (1.0 + sc_ref[0]) + sh_ref[0]
    hi = h2.astype(BF16)
    lo = (h2 - hi.astype(F32)).astype(BF16)
    logits = (jnp.dot(hi, whi_ref[...], preferred_element_type=F32)
              + jnp.dot(hi, wlo_ref[...], preferred_element_type=F32)
              + jnp.dot(lo, whi_ref[...], preferred_element_type=F32))
    scores = _sigmoid(logits)
    sel = scores + rb_ref[...]
    lane = lax.broadcasted_iota(jnp.int32, (tr, N_EXPERTS), 1)
    lane_f = lane.astype(F32)

    a1 = sel
    a2 = jnp.full_like(sel, -jnp.inf)
    s = 1
    while s < GROUP_SIZE:
        upper = (lane & s) != 0
        b1 = jnp.where(upper, pltpu.roll(a1, s, 1), pltpu.roll(a1, N_EXPERTS - s, 1))
        b2 = jnp.where(upper, pltpu.roll(a2, s, 1), pltpu.roll(a2, N_EXPERTS - s, 1))
        a1, a2 = jnp.maximum(a1, b1), jnp.maximum(jnp.minimum(a1, b1), jnp.maximum(a2, b2))
        s *= 2
    gscore = a1 + a2

    grp = lane // GROUP_SIZE
    beaten_by = jnp.zeros((tr, N_EXPERTS), jnp.int32)
    for kk in range(1, N_GROUPS):
        other = pltpu.roll(gscore, kk * GROUP_SIZE, 1)
        ogrp = (grp - kk) & (N_GROUPS - 1)
        wins = (other > gscore) | ((other == gscore) & (ogrp < grp))
        beaten_by = beaten_by + wins.astype(jnp.int32)
    cur = jnp.where(beaten_by < TOP_K_GROUPS, sel, -jnp.inf)

    picks, weights = [], []
    selmask = jnp.zeros((tr, N_EXPERTS), F32)
    for _ in range(TOP_K):
        mx = jnp.max(cur, axis=-1, keepdims=True)
        pick = jnp.min(jnp.where(cur == mx, lane_f, float(N_EXPERTS)), axis=-1, keepdims=True)
        onehot = lane_f == pick
        weights.append(jnp.sum(jnp.where(onehot, scores, 0.0), axis=-1, keepdims=True))
        cur = jnp.where(onehot, -jnp.inf, cur)
        selmask = jnp.where(onehot, 1.0, selmask)
        picks.append(pick)

    rr = lax.broadcasted_iota(jnp.int32, (tr, tr), 0)
    cc = lax.broadcasted_iota(jnp.int32, (tr, tr), 1)
    earlier = jnp.where(cc < rr, 1.0, 0.0).astype(BF16)
    rankmat = jnp.dot(earlier, selmask.astype(BF16), preferred_element_type=F32) + carry_ref[...]
    carry_ref[...] = carry_ref[...] + jnp.sum(selmask, axis=0, keepdims=True)
    cnt_ref[...] = carry_ref[...]

    wsum = weights[0]
    for wj in weights[1:]:
        wsum = wsum + wj
    lane8 = lax.broadcasted_iota(jnp.int32, (tr, TOP_K), 1)
    eidx = jnp.zeros((tr, TOP_K), jnp.int32)
    rank = jnp.zeros((tr, TOP_K), jnp.int32)
    wts = jnp.zeros((tr, TOP_K), F32)
    for j in range(TOP_K):
        rk = jnp.sum(jnp.where(lane_f == picks[j], rankmat, 0.0), axis=-1, keepdims=True)
        eidx = jnp.where(lane8 == j, picks[j].astype(jnp.int32), eidx)
        rank = jnp.where(lane8 == j, rk.astype(jnp.int32), rank)
        wts = jnp.where(lane8 == j, weights[j] / wsum * ROUTED_SCALE, wts)
    eidx_ref[...] = eidx
    rank_ref[...] = rank
    wts_ref[...] = wts


def _route(x1, scale2, shift2, wr_hi, wr_lo, router_bias, *, seq, tr):
    n, d = x1.shape
    tps = seq // tr
    mod_spec = pl.BlockSpec((1, 1, d), lambda i: (i // tps, 0, 0))
    full = lambda a: pl.BlockSpec(a.shape, lambda i: (0,) * a.ndim)
    k8 = pl.BlockSpec((tr, TOP_K), lambda i: (i, 0))
    return pl.pallas_call(
        functools.partial(_route_kernel, tr=tr),
        grid=(n // tr,),
        in_specs=[pl.BlockSpec((tr, d), lambda i: (i, 0)), mod_spec, mod_spec,
                  full(wr_hi), full(wr_lo), full(router_bias)],
        out_specs=[k8, k8, k8, pl.BlockSpec((1, N_EXPERTS), lambda i: (0, 0))],
        out_shape=[jax.ShapeDtypeStruct((n, TOP_K), jnp.int32),
                   jax.ShapeDtypeStruct((n, TOP_K), jnp.int32),
                   jax.ShapeDtypeStruct((n, TOP_K), F32),
                   jax.ShapeDtypeStruct((1, N_EXPERTS), F32)],
        scratch_shapes=[pltpu.VMEM((1, N_EXPERTS), F32)],
        compiler_params=_params(("arbitrary",)),
    )(x1, scale2, shift2, wr_hi, wr_lo, router_bias)


def _zero_pad_rows(cnt_ref, pst_ref, nused_ref, zero_ref, xs_ref, sem):
    m = DISPATCH_BLOCK
    sub = 8

    def copies(e, wait):
        cnt = cnt_ref[e]
        n_pad = (-cnt) & (m - 1)
        start = pst_ref[e] + cnt
        head = n_pad & (sub - 1)

        def go(n_rows, pos):
            cp = pltpu.make_async_copy(zero_ref.at[pl.ds(0, n_rows), :], xs_ref.at[pl.ds(pos, n_rows), :], sem)
            cp.wait() if wait else cp.start()

        for r in range(sub - 1):
            pl.when(r < head)(functools.partial(go, 1, start + r))
        bit = m // 2
        while bit >= sub:
            pos = pl.multiple_of(start + head + (n_pad & ~(2 * bit - 1) & ~(sub - 1)), sub)
            pl.when((n_pad & bit) != 0)(functools.partial(go, bit, pos))
            bit //= 2

    def start_all(e, carry):
        copies(e, False)
        return carry

    def wait_all(e, carry):
        copies(e, True)
        return carry

    lax.fori_loop(0, N_EXPERTS, start_all, 0)
    lax.fori_loop(0, N_EXPERTS, wait_all, 0)

    half = m // 2
    n_halves = xs_ref.shape[0] // half

    def tail(wait):
        def body(i, carry):
            cp = pltpu.make_async_copy(zero_ref, xs_ref.at[pl.ds(pl.multiple_of(i * half, half), half), :], sem)
            cp.wait() if wait else cp.start()
            return carry
        return body

    first = 2 * nused_ref[0]
    lax.fori_loop(first, n_halves, tail(False), 0)
    lax.fori_loop(first, n_halves, tail(True), 0)


def _pack_bf16_halves(x):
    w = x.shape[1] // 2
    lo = pltpu.bitcast(x[:, :w].astype(BF16).astype(F32), U32) >> 16
    hi = pltpu.bitcast(x[:, w:].astype(BF16).astype(F32), U32) & jnp.uint32(0xFFFF0000)
    return lo | hi


def _unpack_bf16_halves(p):
    return pltpu.bitcast(p << 16, F32), pltpu.bitcast(p & jnp.uint32(0xFFFF0000), F32)


def _dispatch_kernel(x_ref, sc_ref, sh_ref, dest_ref, pst_ref, cnt_ref, nused_ref, xs_ref,
                     h2_ref, zero_ref, sem, pad_sem, *, td):
    @pl.when(pl.program_id(0) == 0)
    def _():
        zero_ref[...] = jnp.zeros_like(zero_ref)
        _zero_pad_rows(cnt_ref, pst_ref, nused_ref, zero_ref, xs_ref, pad_sem)

    h2_ref[...] = _pack_bf16_halves(x_ref[...] * (1.0 + sc_ref[0]) + sh_ref[0])

    def issue(t, carry):
        for j in range(TOP_K):
            d = dest_ref[t * TOP_K + j]
            pltpu.make_async_copy(h2_ref.at[pl.ds(t, 1), :], xs_ref.at[pl.ds(d, 1), :], sem).start()
        return carry

    lax.fori_loop(0, td, issue, 0)
    for _ in range(TOP_K):
        pltpu.make_async_copy(h2_ref, xs_ref.at[pl.ds(0, td), :], sem).wait()


def _dispatch(x1, scale2, shift2, dest_flat, pstarts, counts, n_used, n_slots, *, seq, td):
    n, d = x1.shape
    tps = seq // td
    mod_spec = pl.BlockSpec((1, 1, d), lambda i: (i // tps, 0, 0))
    k_spec = pl.BlockSpec((td * TOP_K,), lambda i: (i,), memory_space=pltpu.SMEM)
    smem = pl.BlockSpec(memory_space=pltpu.SMEM)
    return pl.pallas_call(
        functools.partial(_dispatch_kernel, td=td),
        grid=(n // td,),
        in_specs=[pl.BlockSpec((td, d), lambda i: (i, 0)), mod_spec, mod_spec, k_spec, smem, smem, smem],
        out_specs=pl.BlockSpec(memory_space=pl.ANY),
        out_shape=jax.ShapeDtypeStruct((n_slots, d // 2), U32),
        scratch_shapes=[pltpu.VMEM((td, d // 2), U32), pltpu.VMEM((DISPATCH_BLOCK // 2, d // 2), U32),
                        pltpu.SemaphoreType.DMA(()), pltpu.SemaphoreType.DMA(())],
        compiler_params=_params(("arbitrary",)),
    )(x1, scale2, shift2, dest_flat, pstarts, counts, n_used)


def _dest_kernel(eidx_ref, rank_ref, first_ref, dest_ref):
    tt = eidx_ref.shape[1]
    pieces = [jnp.broadcast_to(first_ref[:, p * LANES:(p + 1) * LANES], (TOP_K, LANES))
              for p in range(N_EXPERTS // LANES)]
    for c in range(tt // LANES):
        cols = slice(c * LANES, (c + 1) * LANES)
        e = eidx_ref[:, cols]
        within = e & (LANES - 1)
        start = jnp.take_along_axis(pieces[0], within, axis=1)
        for p in range(1, len(pieces)):
            start = jnp.where(e // LANES == p, jnp.take_along_axis(pieces[p], within, axis=1), start)
        dest_ref[:, cols] = start + rank_ref[:, cols]


def _dest(eidx_t, rank_t, pstarts, *, tt):
    n = eidx_t.shape[1]
    first = pstarts.reshape(1, N_EXPERTS)
    k8 = pl.BlockSpec((TOP_K, tt), lambda i: (0, i))
    return pl.pallas_call(
        _dest_kernel,
        grid=(n // tt,),
        in_specs=[k8, k8, pl.BlockSpec(first.shape, lambda i: (0, 0))],
        out_specs=k8,
        out_shape=jax.ShapeDtypeStruct((TOP_K, n), jnp.int32),
        compiler_params=_params(("arbitrary",)),
    )(eidx_t, rank_t, first)


def _expert_kernel(fb_ref, nb_ref, nused_ref, wg_ref, wu_ref, wd_ref, xs_ref, ys_ref,
                   wgb, wub, wdb, xbuf, ybuf, in_sem, out_sem):
    e = pl.program_id(0)
    m = DISPATCH_BLOCK
    ns = EXPERT_SLOTS
    n_used = nused_ref[0]

    def rows(g):
        return pl.ds(pl.multiple_of(g * m, m), m)

    def fetch(g):
        slot = g & (ns - 1)
        return pltpu.make_async_copy(xs_ref.at[rows(g), :], xbuf.at[slot], in_sem.at[slot])

    def put(g):
        slot = g & (ns - 1)
        return pltpu.make_async_copy(ybuf.at[slot], ys_ref.at[rows(g), :], out_sem.at[slot])

    @pl.when(e == 0)
    def _():
        for g0 in range(ns - 1):
            @pl.when(g0 < n_used)
            def _(g0=g0):
                fetch(g0).start()

    wgb[...] = wg_ref[0].astype(BF16)
    wub[...] = wu_ref[0].astype(BF16)
    wdb[...] = wd_ref[0].astype(BF16)

    def block(i, carry):
        g = fb_ref[e] + i
        slot = g & (ns - 1)
        fetch(g).wait()

        @pl.when(g + ns - 1 < n_used)
        def _():
            fetch(g + ns - 1).start()

        @pl.when(g >= ns)
        def _():
            put(g - ns).wait()

        x_lo, x_hi = [h.astype(BF16) for h in _unpack_bf16_halves(xbuf[slot])]
        half = x_lo.shape[1]

        def up_proj(w):
            return (jnp.dot(x_lo, w[0:half, :], preferred_element_type=F32)
                    + jnp.dot(x_hi, w[half:, :], preferred_element_type=F32))

        a = (_silu(up_proj(wgb)) * up_proj(wub)).astype(BF16)
        ybuf[slot] = _pack_bf16_halves(jnp.dot(a, wdb[...], preferred_element_type=F32))
        put(g).start()
        return carry

    lax.fori_loop(0, nb_ref[e], block, 0)

    @pl.when(e == pl.num_programs(0) - 1)
    def _():
        for back in range(ns, 0, -1):
            @pl.when(n_used >= back)
            def _(back=back):
                put(n_used - back).wait()

        ybuf[0] = jnp.zeros(ybuf.shape[1:], ybuf.dtype)
        n_blocks = ys_ref.shape[0] // m

        def tail(wait):
            def body(g, carry):
                cp = pltpu.make_async_copy(ybuf.at[0], ys_ref.at[rows(g), :], out_sem.at[0])
                cp.wait() if wait else cp.start()
                return carry
            return body

        lax.fori_loop(n_used, n_blocks, tail(False), 0)
        lax.fori_loop(n_used, n_blocks, tail(True), 0)


def _experts(first_block, n_blocks_e, n_used, xs, w_gate, w_up, w_down):
    p, dp = xs.shape
    m = DISPATCH_BLOCK
    n_e, d, f = w_gate.shape
    assert d == 2 * dp and p % m == 0
    grid_spec = pltpu.PrefetchScalarGridSpec(
        num_scalar_prefetch=3,
        grid=(n_e,),
        in_specs=[pl.BlockSpec((1, d, f), lambda e, *_: (e, 0, 0)),
                  pl.BlockSpec((1, d, f), lambda e, *_: (e, 0, 0)),
                  pl.BlockSpec((1, f, d), lambda e, *_: (e, 0, 0)),
                  pl.BlockSpec(memory_space=pl.ANY)],
        out_specs=pl.BlockSpec(memory_space=pl.ANY),
        scratch_shapes=[pltpu.VMEM((d, f), BF16), pltpu.VMEM((d, f), BF16), pltpu.VMEM((f, d), BF16),
                        pltpu.VMEM((EXPERT_SLOTS, m, dp), U32), pltpu.VMEM((EXPERT_SLOTS, m, dp), U32),
                        pltpu.SemaphoreType.DMA((EXPERT_SLOTS,)), pltpu.SemaphoreType.DMA((EXPERT_SLOTS,))],
    )
    return pl.pallas_call(
        _expert_kernel,
        grid_spec=grid_spec,
        out_shape=jax.ShapeDtypeStruct((p, dp), U32),
        compiler_params=_params(("arbitrary",)),
    )(first_block, n_blocks_e, n_used, w_gate, w_up, w_down, xs)


def _gather_rows_sc(table, idx):
    n_idx, (_, width) = idx.shape[0], table.shape
    n_workers = SC_CORES * SC_SUBCORES
    w = SC_GATHER_ROWS
    per_worker = n_idx // n_workers
    n_chunks = per_worker // w
    assert n_idx == n_workers * n_chunks * w and n_chunks % 2 == 0
    mesh = plsc.VectorSubcoreMesh(core_axis_name="core", subcore_axis_name="subcore")

    @functools.partial(
        pl.kernel, mesh=mesh,
        out_type=jax.ShapeDtypeStruct((n_idx, width), table.dtype),
        scratch_types=[pltpu.VMEM((per_worker,), jnp.int32), pltpu.VMEM((2, w, width), table.dtype),
                       pltpu.SemaphoreType.DMA((2,)), pltpu.SemaphoreType.DMA((2,))])
    def gather(table_hbm, idx_hbm, out_hbm, idx_v, rows_v, in_sem, out_sem):
        worker = lax.axis_index("subcore") * SC_CORES + lax.axis_index("core")
        base = worker * per_worker
        pltpu.sync_copy(idx_hbm.at[pl.ds(base, per_worker)], idx_v)

        def fetch(i, s):
            return pltpu.make_async_copy(table_hbm.at[idx_v.at[pl.ds(i * w, w)]], rows_v.at[s], in_sem.at[s])

        def put(i, s):
            return pltpu.make_async_copy(rows_v.at[s], out_hbm.at[pl.ds(base + i * w, w)], out_sem.at[s])

        fetch(0, 0).start()

        def pair(ii, carry):
            for s in range(2):
                i = ii * 2 + s
                fetch(i, s).wait()

                @pl.when(i >= 1)
                def _():
                    put(i - 1, 1 - s).wait()

                @pl.when(i + 1 < n_chunks)
                def _():
                    fetch(i + 1, 1 - s).start()

                put(i, s).start()
            return carry

        lax.fori_loop(0, n_chunks // 2, pair, 0)
        put(n_chunks - 1, 1).wait()

    return gather(table, idx)


def _combine_kernel(x_ref, sc_ref, sh_ref, gate_ref, wts_ref, rows_ref,
                    wsg_ref, wsu_ref, wsd_ref, g_ref, b_ref, o_ref):
    x = x_ref[...]
    hb = (x * (1.0 + sc_ref[0]) + sh_ref[0]).astype(BF16)
    sg = jnp.dot(hb, wsg_ref[...], preferred_element_type=F32)
    su = jnp.dot(hb, wsu_ref[...], preferred_element_type=F32)
    shared = jnp.dot((_silu(sg) * su).astype(BF16), wsd_ref[...], preferred_element_type=F32)

    wts = wts_ref[...]
    half = shared.shape[1] // 2
    lo, hi = shared[:, :half], shared[:, half:]
    for j in range(TOP_K):
        y_lo, y_hi = _unpack_bf16_halves(rows_ref[j])
        lo = lo + wts[:, j:j + 1] * y_lo
        hi = hi + wts[:, j:j + 1] * y_hi
    z = ALPHA * x + gate_ref[0] * jnp.concatenate([lo, hi], axis=1)
    o_ref[...] = _layer_norm(z, g_ref[...], b_ref[...])


def _combine(x1, scale2, shift2, gate2, wts, rows, ws_gate, ws_up, ws_down, ln_g, ln_b, *, seq, tc):
    n, d = x1.shape
    tps = seq // tc
    mod_spec = pl.BlockSpec((1, 1, d), lambda i: (i // tps, 0, 0))
    full = lambda a: pl.BlockSpec(a.shape, lambda i: (0,) * a.ndim)
    return pl.pallas_call(
        _combine_kernel,
        grid=(n // tc,),
        in_specs=[pl.BlockSpec((tc, d), lambda i: (i, 0)), mod_spec, mod_spec, mod_spec,
                  pl.BlockSpec((tc, TOP_K), lambda i: (i, 0)),
                  pl.BlockSpec((TOP_K, tc, d // 2), lambda i: (0, i, 0)),
                  full(ws_gate), full(ws_up), full(ws_down), full(ln_g), full(ln_b)],
        out_specs=pl.BlockSpec((tc, d), lambda i: (i, 0)),
        out_shape=jax.ShapeDtypeStruct((n, d), F32),
        compiler_params=_params(("arbitrary",)),
    )(x1, scale2, shift2, gate2, wts, rows, ws_gate, ws_up, ws_down, ln_g, ln_b)


def _layer(x, c, w_ada, b_ada, w_in, pool_w, pool_scale, lq1, lk1, lq2, lk2, subln_g, w_out,
           ln1_g, ln1_b, w_router, router_bias, w_gate, w_up, w_down, ws_gate, ws_up, ws_down,
           ln2_g, ln2_b, rel_table, *, tm=512, tr=256, td=256, tc=256):
    bsz, seq, d = x.shape
    n = bsz * seq
    x2 = x.reshape(n, d)
    row = lambda a: a.reshape(1, -1)

    mod = _modulation(c, w_ada, b_ada)
    shift1, scale1, gate1, shift2, scale2, gate2 = [
        mod[:, j * d:(j + 1) * d].reshape(bsz, 1, d) for j in range(6)]

    n_main = POOL_DIM + 2 * QK_DIM
    yp, q, k, vt = _inproj(x2, scale1, shift1, w_in[:, :n_main].astype(BF16), w_in[:, n_main:].T.astype(BF16),
                           pool_w.astype(BF16), row(pool_scale), seq=seq, tm=tm)
    bias_tiles = _bias_tiles(rel_table, seq // ATT_TILE)
    ya = _attention(q, k, vt, bias_tiles, row(lq1), row(lk1), row(lq2), row(lk2), subln_g.reshape(-1, 1),
                    bsz=bsz, seq=seq)
    x1 = _outproj(x2, yp, ya, w_out.astype(BF16), gate1, row(ln1_g), row(ln1_b), seq=seq, tm=tm)

    wr_hi = w_router.astype(BF16)
    wr_lo = (w_router - wr_hi.astype(F32)).astype(BF16)
    eidx, rank, wts, cnt = _route(x1, scale2, shift2, wr_hi, wr_lo, row(router_bias), seq=seq, tr=tr)

    m = DISPATCH_BLOCK
    counts = cnt[0].astype(jnp.int32)
    padded = (counts + m - 1) // m * m
    pends = jnp.cumsum(padded)
    pstarts = pends - padded
    n_blocks = -(-(n * TOP_K + N_EXPERTS * (m - 1)) // m)
    n_used = (pends[-1:] // m).astype(jnp.int32)
    dest_t = _dest(eidx.T, rank.T, pstarts, tt=min(n, 2048))

    xs = _dispatch(x1, scale2, shift2, dest_t.T.reshape(n * TOP_K), pstarts, counts, n_used, n_blocks * m,
                   seq=seq, td=td)
    ys = _experts(pstarts // m, padded // m, n_used, xs, w_gate, w_up, w_down)
    picked = _gather_rows_sc(ys, dest_t.reshape(TOP_K * n)).reshape(TOP_K, n, d // 2)
    out = _combine(x1, scale2, shift2, gate2, wts, picked,
                   ws_gate.astype(BF16), ws_up.astype(BF16), ws_down.astype(BF16),
                   row(ln2_g), row(ln2_b), seq=seq, tc=tc)
    return out.reshape(bsz, seq, d)


def kernel(x, c, w_ada, b_ada, w_in, pool_w, pool_scale, lambda_q1, lambda_k1, lambda_q2, lambda_k2,
           subln_g, w_out, ln1_g, ln1_b, w_router, router_bias, w_gate, w_up, w_down,
           ws_gate, ws_up, ws_down, ln2_g, ln2_b, rel_table):
    per_layer = (w_ada, b_ada, w_in, pool_w, pool_scale, lambda_q1, lambda_k1, lambda_q2, lambda_k2,
                 subln_g, w_out, ln1_g, ln1_b, w_router, router_bias, w_gate, w_up, w_down,
                 ws_gate, ws_up, ws_down, ln2_g, ln2_b)
    assert all(a.shape[0] == DEPTH == 1 for a in per_layer)
    return _layer(x, c, *[a.reshape(a.shape[1:]) for a in per_layer], rel_table)
```

```python
import functools
import math

import jax
import jax.numpy as jnp
from jax import lax
from jax.experimental import pallas as pl
from jax.experimental.pallas import tpu as pltpu
from jax.experimental.pallas import tpu_sc as plsc

F32 = jnp.float32
BF16 = jnp.bfloat16
U32 = jnp.uint32
LANES = 128

D_MODEL = 1024
CHUNK = 64
Q_BLOCK = 128
ATT_TILE = 256
POOL_DIM = 512
POOL_WINDOWS = (2, 4, 8, 16)
POOL_GROUP_DIM = 128
MAX_WINDOW = max(POOL_WINDOWS)
ATTN_HEADS = 4
ATTN_HEAD_DIM = 64
QK_DIM = 512
V_DIM = 512
IN_DIM = 2048
NUM_BUCKETS = 32
MAX_DISTANCE = 128
N_EXPERTS = 256
TOP_K = 8
N_GROUPS = 8
GROUP_SIZE = N_EXPERTS // N_GROUPS
TOP_K_GROUPS = 4
EXPERT_DIM = 256
ROUTED_SCALE = 2.5
DISPATCH_BLOCK = 256
EXPERT_SLOTS = 4
SC_CORES, SC_SUBCORES = 2, 16
SC_ROWS = 64
DEPTH = 1
ALPHA = (2.0 * DEPTH) ** 0.25
LN_EPS = 1e-5
LAMBDA_INIT = 0.8 - 0.6 * math.exp(-0.3 * 0)

VMEM_LIMIT = 48 * 1024 * 1024


def _sigmoid(x):
    return 1.0 / (1.0 + jnp.exp(-x))


def _silu(x):
    return x * _sigmoid(x)


def _layer_norm(z, g, b):
    mu = jnp.mean(z, axis=-1, keepdims=True)
    zc = z - mu
    var = jnp.mean(zc * zc, axis=-1, keepdims=True)
    return zc * lax.rsqrt(var + LN_EPS) * g + b


def _params(sem=None):
    return pltpu.CompilerParams(dimension_semantics=sem, vmem_limit_bytes=VMEM_LIMIT)


def _mod_kernel(c_ref, w_ref, b_ref, o_ref):
    ca = _silu(c_ref[...])
    o_ref[...] = jnp.dot(ca, w_ref[...], preferred_element_type=F32,
                         precision=lax.Precision.HIGHEST) + b_ref[...]


def _modulation(c, w_ada, b_ada):
    bsz, d = c.shape
    n_out = w_ada.shape[1]
    return pl.pallas_call(
        _mod_kernel,
        grid=(n_out // d,),
        in_specs=[pl.BlockSpec((bsz, d), lambda j: (0, 0)),
                  pl.BlockSpec((d, d), lambda j: (0, j)),
                  pl.BlockSpec((1, d), lambda j: (0, j))],
        out_specs=pl.BlockSpec((bsz, d), lambda j: (0, j)),
        out_shape=jax.ShapeDtypeStruct((bsz, n_out), F32),
        compiler_params=_params(("arbitrary",)),
    )(c, w_ada, b_ada.reshape(1, n_out))


def _inproj_kernel(x_ref, sc_ref, sh_ref, w_ref, wvt_ref, pw_ref, ps_ref,
                   yp_ref, q_ref, k_ref, vt_ref, ext_ref, *, tm, seq):
    i = pl.program_id(0)
    tiles_per_seq = seq // tm
    it = i % tiles_per_seq
    h = x_ref[...] * (1.0 + sc_ref[0]) + sh_ref[0]
    hb = h.astype(BF16)
    proj = jnp.dot(hb, w_ref[...], preferred_element_type=F32)
    u = proj[:, :POOL_DIM]
    q_ref[...] = (proj[:, POOL_DIM:POOL_DIM + QK_DIM] * (ATTN_HEAD_DIM ** -0.5)).astype(BF16)
    k_ref[...] = proj[:, POOL_DIM + QK_DIM:POOL_DIM + 2 * QK_DIM].astype(BF16)
    vt = lax.dot_general(wvt_ref[...], hb, (((1,), (1,)), ((), ())), preferred_element_type=F32)
    for j in range(tm // ATT_TILE):
        vt_ref[0, j] = vt[:, j * ATT_TILE:(j + 1) * ATT_TILE].astype(BF16)

    @pl.when(it == 0)
    def _():
        ext_ref[0:MAX_WINDOW, :] = jnp.zeros((MAX_WINDOW, POOL_DIM), F32)

    ext_ref[MAX_WINDOW:MAX_WINDOW + tm, :] = u
    pos = (it * tm + lax.broadcasted_iota(jnp.int32, (tm, 1), 0) + 1).astype(F32)
    for g, w in enumerate(POOL_WINDOWS):
        c0, c1 = g * POOL_GROUP_DIM, (g + 1) * POOL_GROUP_DIM
        s = ext_ref[MAX_WINDOW:MAX_WINDOW + tm, c0:c1]
        for j in range(1, w):
            s = s + ext_ref[MAX_WINDOW - j:MAX_WINDOW - j + tm, c0:c1]
        pooled = s / jnp.minimum(pos, float(w)) - u[:, c0:c1]
        y = jnp.dot(pooled.astype(BF16), pw_ref[g], preferred_element_type=F32)
        yp_ref[:, c0:c1] = (y * ps_ref[:, c0:c1]).astype(BF16)
    ext_ref[0:MAX_WINDOW, :] = ext_ref[tm:tm + MAX_WINDOW, :]


def _inproj(x2, scale1, shift1, w_main, w_vt, pool_w, pool_scale, *, seq, tm):
    n, d = x2.shape
    assert n % tm == 0 and seq % tm == 0 and tm >= 2 * MAX_WINDOW and tm % ATT_TILE == 0
    tps = seq // tm
    tpt = tm // ATT_TILE
    mod_spec = pl.BlockSpec((1, 1, d), lambda i: (i // tps, 0, 0))
    row = lambda w: pl.BlockSpec((tm, w), lambda i: (i, 0))
    full = lambda a: pl.BlockSpec(a.shape, lambda i: (0,) * a.ndim)
    return pl.pallas_call(
        functools.partial(_inproj_kernel, tm=tm, seq=seq),
        grid=(n // tm,),
        in_specs=[row(d), mod_spec, mod_spec, full(w_main), full(w_vt), full(pool_w), full(pool_scale)],
        out_specs=[row(POOL_DIM), row(QK_DIM), row(QK_DIM),
                   pl.BlockSpec((1, tpt, V_DIM, ATT_TILE), lambda i: (i // tps, i % tps, 0, 0))],
        out_shape=[jax.ShapeDtypeStruct((n, POOL_DIM), BF16),
                   jax.ShapeDtypeStruct((n, QK_DIM), BF16),
                   jax.ShapeDtypeStruct((n, QK_DIM), BF16),
                   jax.ShapeDtypeStruct((n // seq, seq // ATT_TILE, V_DIM, ATT_TILE), BF16)],
        scratch_shapes=[pltpu.VMEM((tm + MAX_WINDOW, POOL_DIM), F32)],
        compiler_params=_params(("arbitrary",)),
    )(x2, scale1, shift1, w_main, w_vt, pool_w, pool_scale)


def _bias_kernel(tab_ref, o_ref):
    delta = pl.program_id(0)
    r = lax.broadcasted_iota(jnp.int32, (ATT_TILE, ATT_TILE), 0)
    c = lax.broadcasted_iota(jnp.int32, (ATT_TILE, ATT_TILE), 1)
    rel = r - c - delta * ATT_TILE
    half = NUM_BUCKETS // 2
    max_exact = half // 2
    ret = jnp.where(rel > 0, half, 0)
    n = jnp.abs(rel)
    nf = jnp.maximum(n, 1).astype(F32)
    large = max_exact + (jnp.log(nf / max_exact) / math.log(MAX_DISTANCE / max_exact)
                         * (half - max_exact)).astype(jnp.int32)
    large = jnp.minimum(large, half - 1)
    bucket = ret + jnp.where(n < max_exact, n, large)
    for h in range(ATTN_HEADS):
        acc = jnp.zeros((ATT_TILE, ATT_TILE), F32)
        for b in range(NUM_BUCKETS):
            acc = jnp.where(bucket == b, tab_ref[b, h], acc)
        o_ref[h, 0] = acc


def _bias_tiles(rel_table, n_tiles):
    return pl.pallas_call(
        _bias_kernel,
        grid=(n_tiles,),
        in_specs=[pl.BlockSpec(memory_space=pltpu.SMEM)],
        out_specs=pl.BlockSpec((ATTN_HEADS, 1, ATT_TILE, ATT_TILE), lambda dlt: (0, dlt, 0, 0)),
        out_shape=jax.ShapeDtypeStruct((ATTN_HEADS, n_tiles, ATT_TILE, ATT_TILE), F32),
        compiler_params=_params(("arbitrary",)),
    )(rel_table)


def _attn_kernel(q_ref, k_ref, vt_ref, bias_ref, lq1_ref, lk1_ref, lq2_ref, lk2_ref, g_ref, o_ref, *acc_refs):
    qt = pl.program_id(1)
    t = ATT_TILE
    n_maps = 2 * ATTN_HEADS
    lam = (jnp.exp(jnp.sum(lq1_ref[...] * lk1_ref[...], axis=-1, keepdims=True))
           - jnp.exp(jnp.sum(lq2_ref[...] * lk2_ref[...], axis=-1, keepdims=True))
           + LAMBDA_INIT)
    r = lax.broadcasted_iota(jnp.int32, (t, t), 0)
    c = lax.broadcasted_iota(jnp.int32, (t, t), 1)
    allowed = (r // CHUNK) <= (c // CHUNK)
    hd2 = 2 * ATTN_HEAD_DIM

    def block(kt, carry, diagonal):
        koff = pl.multiple_of(kt * t, t)

        def scores(hm):
            col = hm * ATTN_HEAD_DIM
            qh = q_ref[:, col:col + ATTN_HEAD_DIM]
            kh = k_ref[pl.ds(koff, t), col:col + ATTN_HEAD_DIM]
            s = lax.dot_general(kh, qh, (((1,), (1,)), ((), ())),
                                preferred_element_type=F32) + bias_ref[hm // 2, qt - kt]
            return jnp.where(allowed, s, -jnp.inf) if diagonal else s

        def softmax(hm, s):
            m_old, l_old = carry[2 * hm:2 * hm + 2]
            m_new = jnp.maximum(m_old, jnp.max(s, axis=0, keepdims=True))
            alpha = jnp.exp(m_old - m_new)
            p = jnp.exp(s - m_new)
            return m_new, alpha * l_old + jnp.sum(p, axis=0, keepdims=True), alpha, p.astype(BF16)

        def accumulate(hm, alpha, p):
            h = hm // 2
            vth = vt_ref[0, kt, h * hd2:(h + 1) * hd2, :]
            acc_refs[hm][...] = alpha * acc_refs[hm][...] + jnp.dot(vth, p, preferred_element_type=F32)

        s_vals, sm_vals, out = {}, {}, [None] * (2 * n_maps)
        for step in range(n_maps + 2):
            if step < n_maps:
                s_vals[step] = scores(step)
            if 0 <= step - 1 < n_maps:
                hm = step - 1
                m_new, l_new, alpha, p = softmax(hm, s_vals.pop(hm))
                out[2 * hm], out[2 * hm + 1] = m_new, l_new
                sm_vals[hm] = (alpha, p)
            if 0 <= step - 2 < n_maps:
                accumulate(step - 2, *sm_vals.pop(step - 2))
        return tuple(out)

    for acc in acc_refs:
        acc[...] = jnp.zeros_like(acc)
    one = (jnp.full((1, t), -jnp.inf, F32), jnp.zeros((1, t), F32))
    carry = lax.fori_loop(0, qt, lambda kt, cr: block(kt, cr, False), one * n_maps)
    carry = block(qt, carry, True)
    for h in range(ATTN_HEADS):
        l0, l1 = carry[4 * h + 1], carry[4 * h + 3]
        o = acc_refs[2 * h][...] / l0 - lam * (acc_refs[2 * h + 1][...] / l1)
        y = o * lax.rsqrt(jnp.mean(o * o, axis=0, keepdims=True) + LN_EPS) * g_ref[...]
        o_ref[:, h * hd2:(h + 1) * hd2] = (y * (1.0 - LAMBDA_INIT)).T.astype(BF16)


def _attention(q, k, vt, bias_tiles, lq1, lk1, lq2, lk2, subln_g, *, bsz, seq):
    t = ATT_TILE
    nt = seq // t
    full = lambda a: pl.BlockSpec(a.shape, lambda b, j: (0,) * a.ndim)
    return pl.pallas_call(
        _attn_kernel,
        grid=(bsz, nt),
        in_specs=[pl.BlockSpec((t, QK_DIM), lambda b, j: (b * nt + j, 0)),
                  pl.BlockSpec((seq, QK_DIM), lambda b, j: (b, 0)),
                  pl.BlockSpec((1, nt, V_DIM, t), lambda b, j: (b, 0, 0, 0)),
                  full(bias_tiles), full(lq1), full(lk1), full(lq2), full(lk2), full(subln_g)],
        out_specs=pl.BlockSpec((t, V_DIM), lambda b, j: (b * nt + j, 0)),
        out_shape=jax.ShapeDtypeStruct((bsz * seq, V_DIM), BF16),
        scratch_shapes=[pltpu.VMEM((2 * ATTN_HEAD_DIM, t), F32) for _ in range(2 * ATTN_HEADS)],
        compiler_params=_params(("arbitrary", "arbitrary")),
    )(q, k, vt, bias_tiles, lq1, lk1, lq2, lk2, subln_g)


def _outproj_kernel(x_ref, yp_ref, ya_ref, w_ref, gate_ref, g_ref, b_ref, o_ref):
    mix = (jnp.dot(yp_ref[...], w_ref[0:POOL_DIM, :], preferred_element_type=F32)
           + jnp.dot(ya_ref[...], w_ref[POOL_DIM:, :], preferred_element_type=F32))
    z = ALPHA * x_ref[...] + gate_ref[0] * mix
    o_ref[...] = _layer_norm(z, g_ref[...], b_ref[...])


def _outproj(x2, yp, ya, w_out, gate1, ln_g, ln_b, *, seq, tm):
    n, d = x2.shape
    tps = seq // tm
    row = lambda w: pl.BlockSpec((tm, w), lambda i: (i, 0))
    full = lambda a: pl.BlockSpec(a.shape, lambda i: (0,) * a.ndim)
    return pl.pallas_call(
        _outproj_kernel,
        grid=(n // tm,),
        in_specs=[row(d), row(POOL_DIM), row(V_DIM), full(w_out),
                  pl.BlockSpec((1, 1, d), lambda i: (i // tps, 0, 0)), full(ln_g), full(ln_b)],
        out_specs=row(d),
        out_shape=jax.ShapeDtypeStruct((n, d), F32),
        compiler_params=_params(("arbitrary",)),
    )(x2, yp, ya, w_out, gate1, ln_g, ln_b)


def _route_kernel(x_ref, sc_ref, sh_ref, whi_ref, wlo_ref, rb_ref,
                  eidx_ref, rank_ref, wts_ref, cnt_ref, h2p_ref, carry_ref, *, tr):
    i = pl.program_id(0)

    @pl.when(i == 0)
    def _():
        carry_ref[...] = jnp.zeros_like(carry_ref)

    h2 = x_ref[...] * (1.0 + sc_ref[0]) + sh_ref[0]
    h2p_ref[...] = _pack_bf16_halves(h2)
    hi = h2.astype(BF16)
    lo = (h2 - hi.astype(F32)).astype(BF16)
    logits = (jnp.dot(hi, whi_ref[...], preferred_element_type=F32)
              + jnp.dot(hi, wlo_ref[...], preferred_element_type=F32)
              + jnp.dot(lo, whi_ref[...], preferred_element_type=F32))
    scores = _sigmoid(logits)
    sel = scores + rb_ref[...]
    lane = lax.broadcasted_iota(jnp.int32, (tr, N_EXPERTS), 1)
    lane_f = lane.astype(F32)

    a1 = sel
    a2 = jnp.full_like(sel, -jnp.inf)
    s = 1
    while s < GROUP_SIZE:
        upper = (lane & s) != 0
        b1 = jnp.where(upper, pltpu.roll(a1, s, 1), pltpu.roll(a1, N_EXPERTS - s, 1))
        b2 = jnp.where(upper, pltpu.roll(a2, s, 1), pltpu.roll(a2, N_EXPERTS - s, 1))
        a1, a2 = jnp.maximum(a1, b1), jnp.maximum(jnp.minimum(a1, b1), jnp.maximum(a2, b2))
        s *= 2
    gscore = a1 + a2

    grp = lane // GROUP_SIZE
    beaten_by = jnp.zeros((tr, N_EXPERTS), jnp.int32)
    for kk in range(1, N_GROUPS):
        other = pltpu.roll(gscore, kk * GROUP_SIZE, 1)
        ogrp = (grp - kk) & (N_GROUPS - 1)
        wins = (other > gscore) | ((other == gscore) & (ogrp < grp))
        beaten_by = beaten_by + wins.astype(jnp.int32)
    cur = jnp.where(beaten_by < TOP_K_GROUPS, sel, -jnp.inf)

    picks, weights = [], []
    selmask = jnp.zeros((tr, N_EXPERTS), F32)
    for _ in range(TOP_K):
        mx = jnp.max(cur, axis=-1, keepdims=True)
        pick = jnp.min(jnp.where(cur == mx, lane_f, float(N_EXPERTS)), axis=-1, keepdims=True)
        onehot = lane_f == pick
        weights.append(jnp.sum(jnp.where(onehot, scores, 0.0), axis=-1, keepdims=True))
        cur = jnp.where(onehot, -jnp.inf, cur)
        selmask = jnp.where(onehot, 1.0, selmask)
        picks.append(pick)

    rr = lax.broadcasted_iota(jnp.int32, (tr, tr), 0)
    cc = lax.broadcasted_iota(jnp.int32, (tr, tr), 1)
    earlier = jnp.where(cc < rr, 1.0, 0.0).astype(BF16)
    rankmat = jnp.dot(earlier, selmask.astype(BF16), preferred_element_type=F32) + carry_ref[...]
    carry_ref[...] = carry_ref[...] + jnp.sum(selmask, axis=0, keepdims=True)
    cnt_ref[...] = carry_ref[...]

    wsum = weights[0]
    for wj in weights[1:]:
        wsum = wsum + wj
    lane8 = lax.broadcasted_iota(jnp.int32, (tr, TOP_K), 1)
    eidx = jnp.zeros((tr, TOP_K), jnp.int32)
    rank = jnp.zeros((tr, TOP_K), jnp.int32)
    wts = jnp.zeros((tr, TOP_K), F32)
    for j in range(TOP_K):
        rk = jnp.sum(jnp.where(lane_f == picks[j], rankmat, 0.0), axis=-1, keepdims=True)
        eidx = jnp.where(lane8 == j, picks[j].astype(jnp.int32), eidx)
        rank = jnp.where(lane8 == j, rk.astype(jnp.int32), rank)
        wts = jnp.where(lane8 == j, weights[j] / wsum * ROUTED_SCALE, wts)
    eidx_ref[...] = eidx
    rank_ref[...] = rank
    wts_ref[...] = wts


def _route(x1, scale2, shift2, wr_hi, wr_lo, router_bias, *, seq, tr):
    n, d = x1.shape
    tps = seq // tr
    mod_spec = pl.BlockSpec((1, 1, d), lambda i: (i // tps, 0, 0))
    full = lambda a: pl.BlockSpec(a.shape, lambda i: (0,) * a.ndim)
    k8 = pl.BlockSpec((tr, TOP_K), lambda i: (i, 0))
    return pl.pallas_call(
        functools.partial(_route_kernel, tr=tr),
        grid=(n // tr,),
        in_specs=[pl.BlockSpec((tr, d), lambda i: (i, 0)), mod_spec, mod_spec,
                  full(wr_hi), full(wr_lo), full(router_bias)],
        out_specs=[k8, k8, k8, pl.BlockSpec((1, N_EXPERTS), lambda i: (0, 0)),
                   pl.BlockSpec((tr, d // 2), lambda i: (i, 0))],
        out_shape=[jax.ShapeDtypeStruct((n, TOP_K), jnp.int32),
                   jax.ShapeDtypeStruct((n, TOP_K), jnp.int32),
                   jax.ShapeDtypeStruct((n, TOP_K), F32),
                   jax.ShapeDtypeStruct((1, N_EXPERTS), F32),
                   jax.ShapeDtypeStruct((n, d // 2), U32)],
        scratch_shapes=[pltpu.VMEM((1, N_EXPERTS), F32)],
        compiler_params=_params(("arbitrary",)),
    )(x1, scale2, shift2, wr_hi, wr_lo, router_bias)


def _pack_bf16_halves(x):
    w = x.shape[1] // 2
    lo = pltpu.bitcast(x[:, :w].astype(BF16).astype(F32), U32) >> 16
    hi = pltpu.bitcast(x[:, w:].astype(BF16).astype(F32), U32) & jnp.uint32(0xFFFF0000)
    return lo | hi


def _unpack_bf16_halves(p):
    return pltpu.bitcast(p << 16, F32), pltpu.bitcast(p & jnp.uint32(0xFFFF0000), F32)


def _sc_worker():
    return lax.axis_index("subcore") * SC_CORES + lax.axis_index("core")


def _scatter_rows_sc(rows, idx, n_slots):
    n, width = rows.shape
    k = idx.shape[0] // n
    n_workers = SC_CORES * SC_SUBCORES
    w = SC_ROWS
    per_worker = n // n_workers
    n_chunks = per_worker // w
    assert n == n_workers * n_chunks * w and n_chunks % 2 == 0 and idx.shape[0] == k * n
    mesh = plsc.VectorSubcoreMesh(core_axis_name="core", subcore_axis_name="subcore")

    @functools.partial(
        pl.kernel, mesh=mesh,
        out_type=jax.ShapeDtypeStruct((n_slots, width), rows.dtype),
        scratch_types=[pltpu.VMEM((k * per_worker,), jnp.int32), pltpu.VMEM((2, w, width), rows.dtype),
                       pltpu.SemaphoreType.DMA((2,)), pltpu.SemaphoreType.DMA((2,))])
    def scatter(rows_hbm, idx_hbm, out_hbm, idx_v, rows_v, in_sem, out_sem):
        t0 = _sc_worker() * per_worker
        for j in range(k):
            pltpu.sync_copy(idx_hbm.at[pl.ds(j * n + t0, per_worker)], idx_v.at[pl.ds(j * per_worker, per_worker)])

        def load(i, s):
            return pltpu.make_async_copy(rows_hbm.at[pl.ds(t0 + i * w, w)], rows_v.at[s], in_sem.at[s])

        def send(i, s, j):
            slots = idx_v.at[pl.ds(j * per_worker + i * w, w)]
            return pltpu.make_async_copy(rows_v.at[s], out_hbm.at[slots], out_sem.at[s])

        load(0, 0).start()

        def pair(ii, carry):
            for s in range(2):
                i = ii * 2 + s
                load(i, s).wait()

                @pl.when(i >= 1)
                def _():
                    for j in range(k):
                        send(i - 1, 1 - s, j).wait()

                @pl.when(i + 1 < n_chunks)
                def _():
                    load(i + 1, 1 - s).start()

                for j in range(k):
                    send(i, s, j).start()
            return carry

        lax.fori_loop(0, n_chunks // 2, pair, 0)
        for j in range(k):
            send(n_chunks - 1, 1, j).wait()

    return scatter(rows, idx)


def _dest_kernel(eidx_ref, rank_ref, first_ref, dest_ref):
    tt = eidx_ref.shape[1]
    pieces = [jnp.broadcast_to(first_ref[:, p * LANES:(p + 1) * LANES], (TOP_K, LANES))
              for p in range(N_EXPERTS // LANES)]
    for c in range(tt // LANES):
        cols = slice(c * LANES, (c + 1) * LANES)
        e = eidx_ref[:, cols]
        within = e & (LANES - 1)
        start = jnp.take_along_axis(pieces[0], within, axis=1)
        for p in range(1, len(pieces)):
            start = jnp.where(e // LANES == p, jnp.take_along_axis(pieces[p], within, axis=1), start)
        dest_ref[:, cols] = start + rank_ref[:, cols]


def _dest(eidx_t, rank_t, pstarts, *, tt):
    n = eidx_t.shape[1]
    first = pstarts.reshape(1, N_EXPERTS)
    k8 = pl.BlockSpec((TOP_K, tt), lambda i: (0, i))
    return pl.pallas_call(
        _dest_kernel,
        grid=(n // tt,),
        in_specs=[k8, k8, pl.BlockSpec(first.shape, lambda i: (0, 0))],
        out_specs=k8,
        out_shape=jax.ShapeDtypeStruct((TOP_K, n), jnp.int32),
        compiler_params=_params(("arbitrary",)),
    )(eidx_t, rank_t, first)


def _expert_kernel(fb_ref, nb_ref, cnt_ref, nused_ref, wg_ref, wu_ref, wd_ref, xs_ref, ys_ref,
                   wgb, wub, wdb, xbuf, ybuf, in_sem, out_sem):
    e = pl.program_id(0)
    m = DISPATCH_BLOCK
    ns = EXPERT_SLOTS
    n_used = nused_ref[0]

    def rows(g):
        return pl.ds(pl.multiple_of(g * m, m), m)

    def fetch(g):
        slot = g & (ns - 1)
        return pltpu.make_async_copy(xs_ref.at[rows(g), :], xbuf.at[slot], in_sem.at[slot])

    def put(g):
        slot = g & (ns - 1)
        return pltpu.make_async_copy(ybuf.at[slot], ys_ref.at[rows(g), :], out_sem.at[slot])

    @pl.when(e == 0)
    def _():
        for g0 in range(ns - 1):
            @pl.when(g0 < n_used)
            def _(g0=g0):
                fetch(g0).start()

    wgb[...] = wg_ref[0].astype(BF16)
    wub[...] = wu_ref[0].astype(BF16)
    wdb[...] = wd_ref[0].astype(BF16)

    def block(i, carry):
        g = fb_ref[e] + i
        slot = g & (ns - 1)
        fetch(g).wait()

        @pl.when(g + ns - 1 < n_used)
        def _():
            fetch(g + ns - 1).start()

        @pl.when(g >= ns)
        def _():
            put(g - ns).wait()

        row = lax.broadcasted_iota(jnp.int32, (m, 1), 0)
        packed = jnp.where(row < cnt_ref[e] - i * m, xbuf[slot], jnp.uint32(0))
        x_lo, x_hi = [h.astype(BF16) for h in _unpack_bf16_halves(packed)]
        half = x_lo.shape[1]

        def up_proj(w):
            return (jnp.dot(x_lo, w[0:half, :], preferred_element_type=F32)
                    + jnp.dot(x_hi, w[half:, :], preferred_element_type=F32))

        a = (_silu(up_proj(wgb)) * up_proj(wub)).astype(BF16)
        ybuf[slot] = _pack_bf16_halves(jnp.dot(a, wdb[...], preferred_element_type=F32))
        put(g).start()
        return carry

    lax.fori_loop(0, nb_ref[e], block, 0)

    @pl.when(e == pl.num_programs(0) - 1)
    def _():
        for back in range(ns, 0, -1):
            @pl.when(n_used >= back)
            def _(back=back):
                put(n_used - back).wait()

        ybuf[0] = jnp.zeros(ybuf.shape[1:], ybuf.dtype)
        n_blocks = ys_ref.shape[0] // m

        def tail(wait):
            def body(g, carry):
                cp = pltpu.make_async_copy(ybuf.at[0], ys_ref.at[rows(g), :], out_sem.at[0])
                cp.wait() if wait else cp.start()
                return carry
            return body

        lax.fori_loop(n_used, n_blocks, tail(False), 0)
        lax.fori_loop(n_used, n_blocks, tail(True), 0)


def _experts(first_block, n_blocks_e, counts, n_used, xs, w_gate, w_up, w_down):
    p, dp = xs.shape
    m = DISPATCH_BLOCK
    n_e, d, f = w_gate.shape
    assert d == 2 * dp and p % m == 0
    grid_spec = pltpu.PrefetchScalarGridSpec(
        num_scalar_prefetch=4,
        grid=(n_e,),
        in_specs=[pl.BlockSpec((1, d, f), lambda e, *_: (e, 0, 0)),
                  pl.BlockSpec((1, d, f), lambda e, *_: (e, 0, 0)),
                  pl.BlockSpec((1, f, d), lambda e, *_: (e, 0, 0)),
                  pl.BlockSpec(memory_space=pl.ANY)],
        out_specs=pl.BlockSpec(memory_space=pl.ANY),
        scratch_shapes=[pltpu.VMEM((d, f), BF16), pltpu.VMEM((d, f), BF16), pltpu.VMEM((f, d), BF16),
                        pltpu.VMEM((EXPERT_SLOTS, m, dp), U32), pltpu.VMEM((EXPERT_SLOTS, m, dp), U32),
                        pltpu.SemaphoreType.DMA((EXPERT_SLOTS,)), pltpu.SemaphoreType.DMA((EXPERT_SLOTS,))],
    )
    return pl.pallas_call(
        _expert_kernel,
        grid_spec=grid_spec,
        out_shape=jax.ShapeDtypeStruct((p, dp), U32),
        compiler_params=_params(("arbitrary",)),
    )(first_block, n_blocks_e, counts, n_used, w_gate, w_up, w_down, xs)


def _gather_rows_sc(table, idx):
    n_idx, (_, width) = idx.shape[0], table.shape
    n_workers = SC_CORES * SC_SUBCORES
    w = SC_ROWS
    per_worker = n_idx // n_workers
    n_chunks = per_worker // w
    assert n_idx == n_workers * n_chunks * w and n_chunks % 2 == 0
    mesh = plsc.VectorSubcoreMesh(core_axis_name="core", subcore_axis_name="subcore")

    @functools.partial(
        pl.kernel, mesh=mesh,
        out_type=jax.ShapeDtypeStruct((n_idx, width), table.dtype),
        scratch_types=[pltpu.VMEM((per_worker,), jnp.int32), pltpu.VMEM((2, w, width), table.dtype),
                       pltpu.SemaphoreType.DMA((2,)), pltpu.SemaphoreType.DMA((2,))])
    def gather(table_hbm, idx_hbm, out_hbm, idx_v, rows_v, in_sem, out_sem):
        base = _sc_worker() * per_worker
        pltpu.sync_copy(idx_hbm.at[pl.ds(base, per_worker)], idx_v)

        def fetch(i, s):
            return pltpu.make_async_copy(table_hbm.at[idx_v.at[pl.ds(i * w, w)]], rows_v.at[s], in_sem.at[s])

        def put(i, s):
            return pltpu.make_async_copy(rows_v.at[s], out_hbm.at[pl.ds(base + i * w, w)], out_sem.at[s])

        fetch(0, 0).start()

        def pair(ii, carry):
            for s in range(2):
                i = ii * 2 + s
                fetch(i, s).wait()

                @pl.when(i >= 1)
                def _():
                    put(i - 1, 1 - s).wait()

                @pl.when(i + 1 < n_chunks)
                def _():
                    fetch(i + 1, 1 - s).start()

                put(i, s).start()
            return carry

        lax.fori_loop(0, n_chunks // 2, pair, 0)
        put(n_chunks - 1, 1).wait()

    return gather(table, idx)


def _combine_kernel(x_ref, sc_ref, sh_ref, gate_ref, wts_ref, rows_ref,
                    wsg_ref, wsu_ref, wsd_ref, g_ref, b_ref, o_ref):
    x = x_ref[...]
    hb = (x * (1.0 + sc_ref[0]) + sh_ref[0]).astype(BF16)
    sg = jnp.dot(hb, wsg_ref[...], preferred_element_type=F32)
    su = jnp.dot(hb, wsu_ref[...], preferred_element_type=F32)
    shared = jnp.dot((_silu(sg) * su).astype(BF16), wsd_ref[...], preferred_element_type=F32)

    wts = wts_ref[...]
    half = shared.shape[1] // 2
    lo, hi = shared[:, :half], shared[:, half:]
    for j in range(TOP_K):
        y_lo, y_hi = _unpack_bf16_halves(rows_ref[j])
        lo = lo + wts[:, j:j + 1] * y_lo
        hi = hi + wts[:, j:j + 1] * y_hi
    z = ALPHA * x + gate_ref[0] * jnp.concatenate([lo, hi], axis=1)
    o_ref[...] = _layer_norm(z, g_ref[...], b_ref[...])


def _combine(x1, scale2, shift2, gate2, wts, rows, ws_gate, ws_up, ws_down, ln_g, ln_b, *, seq, tc):
    n, d = x1.shape
    tps = seq // tc
    mod_spec = pl.BlockSpec((1, 1, d), lambda i: (i // tps, 0, 0))
    full = lambda a: pl.BlockSpec(a.shape, lambda i: (0,) * a.ndim)
    return pl.pallas_call(
        _combine_kernel,
        grid=(n // tc,),
        in_specs=[pl.BlockSpec((tc, d), lambda i: (i, 0)), mod_spec, mod_spec, mod_spec,
                  pl.BlockSpec((tc, TOP_K), lambda i: (i, 0)),
                  pl.BlockSpec((TOP_K, tc, d // 2), lambda i: (0, i, 0)),
                  full(ws_gate), full(ws_up), full(ws_down), full(ln_g), full(ln_b)],
        out_specs=pl.BlockSpec((tc, d), lambda i: (i, 0)),
        out_shape=jax.ShapeDtypeStruct((n, d), F32),
        compiler_params=_params(("arbitrary",)),
    )(x1, scale2, shift2, gate2, wts, rows, ws_gate, ws_up, ws_down, ln_g, ln_b)


def _layer(x, c, w_ada, b_ada, w_in, pool_w, pool_scale, lq1, lk1, lq2, lk2, subln_g, w_out,
           ln1_g, ln1_b, w_router, router_bias, w_gate, w_up, w_down, ws_gate, ws_up, ws_down,
           ln2_g, ln2_b, rel_table, *, tm=512, tr=256, tc=256):
    bsz, seq, d = x.shape
    n = bsz * seq
    x2 = x.reshape(n, d)
    row = lambda a: a.reshape(1, -1)

    mod = _modulation(c, w_ada, b_ada)
    shift1, scale1, gate1, shift2, scale2, gate2 = [
        mod[:, j * d:(j + 1) * d].reshape(bsz, 1, d) for j in range(6)]

    n_main = POOL_DIM + 2 * QK_DIM
    yp, q, k, vt = _inproj(x2, scale1, shift1, w_in[:, :n_main].astype(BF16), w_in[:, n_main:].T.astype(BF16),
                           pool_w.astype(BF16), row(pool_scale), seq=seq, tm=tm)
    bias_tiles = _bias_tiles(rel_table, seq // ATT_TILE)
    ya = _attention(q, k, vt, bias_tiles, row(lq1), row(lk1), row(lq2), row(lk2), subln_g.reshape(-1, 1),
                    bsz=bsz, seq=seq)
    x1 = _outproj(x2, yp, ya, w_out.astype(BF16), gate1, row(ln1_g), row(ln1_b), seq=seq, tm=tm)

    wr_hi = w_router.astype(BF16)
    wr_lo = (w_router - wr_hi.astype(F32)).astype(BF16)
    eidx, rank, wts, cnt, h2p = _route(x1, scale2, shift2, wr_hi, wr_lo, row(router_bias), seq=seq, tr=tr)

    m = DISPATCH_BLOCK
    counts = cnt[0].astype(jnp.int32)
    padded = (counts + m - 1) // m * m
    pends = jnp.cumsum(padded)
    pstarts = pends - padded
    n_blocks = -(-(n * TOP_K + N_EXPERTS * (m - 1)) // m)
    n_used = (pends[-1:] // m).astype(jnp.int32)
    dest = _dest(eidx.T, rank.T, pstarts, tt=min(n, 2048)).reshape(TOP_K * n)

    xs = _scatter_rows_sc(h2p, dest, n_blocks * m)
    ys = _experts(pstarts // m, padded // m, counts, n_used, xs, w_gate, w_up, w_down)
    picked = _gather_rows_sc(ys, dest).reshape(TOP_K, n, d // 2)
    out = _combine(x1, scale2, shift2, gate2, wts, picked,
                   ws_gate.astype(BF16), ws_up.astype(BF16), ws_down.astype(BF16),
                   row(ln2_g), row(ln2_b), seq=seq, tc=tc)
    return out.reshape(bsz, seq, d)


def kernel(x, c, w_ada, b_ada, w_in, pool_w, pool_scale, lambda_q1, lambda_k1, lambda_q2, lambda_k2,
           subln_g, w_out, ln1_g, ln1_b, w_router, router_bias, w_gate, w_up, w_down,
           ws_gate, ws_up, ws_down, ln2_g, ln2_b, rel_table):
    per_layer = (w_ada, b_ada, w_in, pool_w, pool_scale, lambda_q1, lambda_k1, lambda_q2, lambda_k2,
                 subln_g, w_out, ln1_g, ln1_b, w_router, router_bias, w_gate, w_up, w_down,
                 ws_gate, ws_up, ws_down, ln2_g, ln2_b)
    assert all(a.shape[0] == DEPTH == 1 for a in per_layer)
    return _layer(x, c, *[a.reshape(a.shape[1:]) for a in per_layer], rel_table)
```

```python
import functools
import math

import jax
import jax.numpy as jnp
from jax import lax
from jax.experimental import pallas as pl
from jax.experimental.pallas import tpu as pltpu
from jax.experimental.pallas import tpu_sc as plsc

F32 = jnp.float32
BF16 = jnp.bfloat16
U32 = jnp.uint32
LANES = 128

D_MODEL = 1024
CHUNK = 64
Q_BLOCK = 128
ATT_TILE = 256
POOL_DIM = 512
POOL_WINDOWS = (2, 4, 8, 16)
POOL_GROUP_DIM = 128
MAX_WINDOW = max(POOL_WINDOWS)
ATTN_HEADS = 4
ATTN_HEAD_DIM = 64
QK_DIM = 512
V_DIM = 512
IN_DIM = 2048
NUM_BUCKETS = 32
MAX_DISTANCE = 128
N_EXPERTS = 256
TOP_K = 8
N_GROUPS = 8
GROUP_SIZE = N_EXPERTS // N_GROUPS
TOP_K_GROUPS = 4
EXPERT_DIM = 256
ROUTED_SCALE = 2.5
DISPATCH_BLOCK = 256
EXPERT_SLOTS = 4
SC_CORES, SC_SUBCORES = 2, 16
SC_ROWS = 64
DEPTH = 1
ALPHA = (2.0 * DEPTH) ** 0.25
LN_EPS = 1e-5
LAMBDA_INIT = 0.8 - 0.6 * math.exp(-0.3 * 0)

VMEM_LIMIT = 48 * 1024 * 1024


def _sigmoid(x):
    return 1.0 / (1.0 + jnp.exp(-x))


def _silu(x):
    return x * _sigmoid(x)


def _layer_norm(z, g, b):
    mu = jnp.mean(z, axis=-1, keepdims=True)
    zc = z - mu
    var = jnp.mean(zc * zc, axis=-1, keepdims=True)
    return zc * lax.rsqrt(var + LN_EPS) * g + b


def _params(sem=None):
    return pltpu.CompilerParams(dimension_semantics=sem, vmem_limit_bytes=VMEM_LIMIT)


def _mod_kernel(c_ref, w_ref, b_ref, o_ref):
    ca = _silu(c_ref[...])
    o_ref[...] = jnp.dot(ca, w_ref[...], preferred_element_type=F32,
                         precision=lax.Precision.HIGHEST) + b_ref[...]


def _modulation(c, w_ada, b_ada):
    bsz, d = c.shape
    n_out = w_ada.shape[1]
    return pl.pallas_call(
        _mod_kernel,
        grid=(n_out // d,),
        in_specs=[pl.BlockSpec((bsz, d), lambda j: (0, 0)),
                  pl.BlockSpec((d, d), lambda j: (0, j)),
                  pl.BlockSpec((1, d), lambda j: (0, j))],
        out_specs=pl.BlockSpec((bsz, d), lambda j: (0, j)),
        out_shape=jax.ShapeDtypeStruct((bsz, n_out), F32),
        compiler_params=_params(("arbitrary",)),
    )(c, w_ada, b_ada.reshape(1, n_out))


def _inproj_kernel(x_ref, sc_ref, sh_ref, w_ref, wvt_ref, pw_ref, ps_ref,
                   yp_ref, q_ref, k_ref, vt_ref, ext_ref, *, tm, seq):
    i = pl.program_id(0)
    tiles_per_seq = seq // tm
    it = i % tiles_per_seq
    h = x_ref[...] * (1.0 + sc_ref[0]) + sh_ref[0]
    hb = h.astype(BF16)
    proj = jnp.dot(hb, w_ref[...], preferred_element_type=F32)
    u = proj[:, :POOL_DIM]
    q_ref[...] = (proj[:, POOL_DIM:POOL_DIM + QK_DIM] * (ATTN_HEAD_DIM ** -0.5)).astype(BF16)
    k_ref[...] = proj[:, POOL_DIM + QK_DIM:POOL_DIM + 2 * QK_DIM].astype(BF16)
    vt = lax.dot_general(wvt_ref[...], hb, (((1,), (1,)), ((), ())), preferred_element_type=F32)
    for j in range(tm // ATT_TILE):
        vt_ref[0, j] = vt[:, j * ATT_TILE:(j + 1) * ATT_TILE].astype(BF16)

    @pl.when(it == 0)
    def _():
        ext_ref[0:MAX_WINDOW, :] = jnp.zeros((MAX_WINDOW, POOL_DIM), F32)

    ext_ref[MAX_WINDOW:MAX_WINDOW + tm, :] = u
    pos = (it * tm + lax.broadcasted_iota(jnp.int32, (tm, 1), 0) + 1).astype(F32)
    for g, w in enumerate(POOL_WINDOWS):
        c0, c1 = g * POOL_GROUP_DIM, (g + 1) * POOL_GROUP_DIM
        s = ext_ref[MAX_WINDOW:MAX_WINDOW + tm, c0:c1]
        for j in range(1, w):
            s = s + ext_ref[MAX_WINDOW - j:MAX_WINDOW - j + tm, c0:c1]
        pooled = s / jnp.minimum(pos, float(w)) - u[:, c0:c1]
        y = jnp.dot(pooled.astype(BF16), pw_ref[g], preferred_element_type=F32)
        yp_ref[:, c0:c1] = (y * ps_ref[:, c0:c1]).astype(BF16)
    ext_ref[0:MAX_WINDOW, :] = ext_ref[tm:tm + MAX_WINDOW, :]


def _inproj(x2, scale1, shift1, w_main, w_vt, pool_w, pool_scale, *, seq, tm):
    n, d = x2.shape
    assert n % tm == 0 and seq % tm == 0 and tm >= 2 * MAX_WINDOW and tm % ATT_TILE == 0
    tps = seq // tm
    tpt = tm // ATT_TILE
    mod_spec = pl.BlockSpec((1, 1, d), lambda i: (i // tps, 0, 0))
    row = lambda w: pl.BlockSpec((tm, w), lambda i: (i, 0))
    full = lambda a: pl.BlockSpec(a.shape, lambda i: (0,) * a.ndim)
    return pl.pallas_call(
        functools.partial(_inproj_kernel, tm=tm, seq=seq),
        grid=(n // tm,),
        in_specs=[row(d), mod_spec, mod_spec, full(w_main), full(w_vt), full(pool_w), full(pool_scale)],
        out_specs=[row(POOL_DIM), row(QK_DIM), row(QK_DIM),
                   pl.BlockSpec((1, tpt, V_DIM, ATT_TILE), lambda i: (i // tps, i % tps, 0, 0))],
        out_shape=[jax.ShapeDtypeStruct((n, POOL_DIM), BF16),
                   jax.ShapeDtypeStruct((n, QK_DIM), BF16),
                   jax.ShapeDtypeStruct((n, QK_DIM), BF16),
                   jax.ShapeDtypeStruct((n // seq, seq // ATT_TILE, V_DIM, ATT_TILE), BF16)],
        scratch_shapes=[pltpu.VMEM((tm + MAX_WINDOW, POOL_DIM), F32)],
        compiler_params=_params(("arbitrary",)),
    )(x2, scale1, shift1, w_main, w_vt, pool_w, pool_scale)


def _bias_kernel(tab_ref, o_ref):
    delta = pl.program_id(0)
    r = lax.broadcasted_iota(jnp.int32, (ATT_TILE, ATT_TILE), 0)
    c = lax.broadcasted_iota(jnp.int32, (ATT_TILE, ATT_TILE), 1)
    rel = r - c - delta * ATT_TILE
    half = NUM_BUCKETS // 2
    max_exact = half // 2
    ret = jnp.where(rel > 0, half, 0)
    n = jnp.abs(rel)
    nf = jnp.maximum(n, 1).astype(F32)
    large = max_exact + (jnp.log(nf / max_exact) / math.log(MAX_DISTANCE / max_exact)
                         * (half - max_exact)).astype(jnp.int32)
    large = jnp.minimum(large, half - 1)
    bucket = ret + jnp.where(n < max_exact, n, large)
    for h in range(ATTN_HEADS):
        acc = jnp.zeros((ATT_TILE, ATT_TILE), F32)
        for b in range(NUM_BUCKETS):
            acc = jnp.where(bucket == b, tab_ref[b, h], acc)
        o_ref[h, 0] = acc


def _bias_tiles(rel_table, n_tiles):
    return pl.pallas_call(
        _bias_kernel,
        grid=(n_tiles,),
        in_specs=[pl.BlockSpec(memory_space=pltpu.SMEM)],
        out_specs=pl.BlockSpec((ATTN_HEADS, 1, ATT_TILE, ATT_TILE), lambda dlt: (0, dlt, 0, 0)),
        out_shape=jax.ShapeDtypeStruct((ATTN_HEADS, n_tiles, ATT_TILE, ATT_TILE), F32),
        compiler_params=_params(("arbitrary",)),
    )(rel_table)


def _attn_kernel(q_ref, k_ref, vt_ref, bias_ref, lq1_ref, lk1_ref, lq2_ref, lk2_ref, g_ref, o_ref, *acc_refs):
    qt = pl.program_id(1)
    t = ATT_TILE
    n_maps = 2 * ATTN_HEADS
    lam = (jnp.exp(jnp.sum(lq1_ref[...] * lk1_ref[...], axis=-1, keepdims=True))
           - jnp.exp(jnp.sum(lq2_ref[...] * lk2_ref[...], axis=-1, keepdims=True))
           + LAMBDA_INIT)
    r = lax.broadcasted_iota(jnp.int32, (t, t), 0)
    c = lax.broadcasted_iota(jnp.int32, (t, t), 1)
    allowed = (r // CHUNK) <= (c // CHUNK)
    hd2 = 2 * ATTN_HEAD_DIM

    def block(kt, carry, diagonal):
        koff = pl.multiple_of(kt * t, t)

        def scores(hm):
            col = hm * ATTN_HEAD_DIM
            qh = q_ref[:, col:col + ATTN_HEAD_DIM]
            kh = k_ref[pl.ds(koff, t), col:col + ATTN_HEAD_DIM]
            s = lax.dot_general(kh, qh, (((1,), (1,)), ((), ())),
                                preferred_element_type=F32) + bias_ref[hm // 2, qt - kt]
            return jnp.where(allowed, s, -jnp.inf) if diagonal else s

        def softmax(hm, s):
            m_old, l_old = carry[2 * hm:2 * hm + 2]
            m_new = jnp.maximum(m_old, jnp.max(s, axis=0, keepdims=True))
            alpha = jnp.exp(m_old - m_new)
            p = jnp.exp(s - m_new)
            return m_new, alpha * l_old + jnp.sum(p, axis=0, keepdims=True), alpha, p.astype(BF16)

        def accumulate(hm, alpha, p):
            h = hm // 2
            vth = vt_ref[0, kt, h * hd2:(h + 1) * hd2, :]
            acc_refs[hm][...] = alpha * acc_refs[hm][...] + jnp.dot(vth, p, preferred_element_type=F32)

        s_vals, sm_vals, out = {}, {}, [None] * (2 * n_maps)
        for step in range(n_maps + 2):
            if step < n_maps:
                s_vals[step] = scores(step)
            if 0 <= step - 1 < n_maps:
                hm = step - 1
                m_new, l_new, alpha, p = softmax(hm, s_vals.pop(hm))
                out[2 * hm], out[2 * hm + 1] = m_new, l_new
                sm_vals[hm] = (alpha, p)
            if 0 <= step - 2 < n_maps:
                accumulate(step - 2, *sm_vals.pop(step - 2))
        return tuple(out)

    for acc in acc_refs:
        acc[...] = jnp.zeros_like(acc)
    one = (jnp.full((1, t), -jnp.inf, F32), jnp.zeros((1, t), F32))
    carry = lax.fori_loop(0, qt, lambda kt, cr: block(kt, cr, False), one * n_maps)
    carry = block(qt, carry, True)
    for h in range(ATTN_HEADS):
        l0, l1 = carry[4 * h + 1], carry[4 * h + 3]
        o = acc_refs[2 * h][...] / l0 - lam * (acc_refs[2 * h + 1][...] / l1)
        y = o * lax.rsqrt(jnp.mean(o * o, axis=0, keepdims=True) + LN_EPS) * g_ref[...]
        o_ref[:, h * hd2:(h + 1) * hd2] = (y * (1.0 - LAMBDA_INIT)).T.astype(BF16)


def _attention(q, k, vt, bias_tiles, lq1, lk1, lq2, lk2, subln_g, *, bsz, seq):
    t = ATT_TILE
    nt = seq // t
    full = lambda a: pl.BlockSpec(a.shape, lambda b, j: (0,) * a.ndim)
    return pl.pallas_call(
        _attn_kernel,
        grid=(bsz, nt),
        in_specs=[pl.BlockSpec((t, QK_DIM), lambda b, j: (b * nt + j, 0)),
                  pl.BlockSpec((seq, QK_DIM), lambda b, j: (b, 0)),
                  pl.BlockSpec((1, nt, V_DIM, t), lambda b, j: (b, 0, 0, 0)),
                  full(bias_tiles), full(lq1), full(lk1), full(lq2), full(lk2), full(subln_g)],
        out_specs=pl.BlockSpec((t, V_DIM), lambda b, j: (b * nt + j, 0)),
        out_shape=jax.ShapeDtypeStruct((bsz * seq, V_DIM), BF16),
        scratch_shapes=[pltpu.VMEM((2 * ATTN_HEAD_DIM, t), F32) for _ in range(2 * ATTN_HEADS)],
        compiler_params=_params(("arbitrary", "arbitrary")),
    )(q, k, vt, bias_tiles, lq1, lk1, lq2, lk2, subln_g)


def _outproj_kernel(x_ref, yp_ref, ya_ref, w_ref, gate_ref, g_ref, b_ref, o_ref):
    mix = (jnp.dot(yp_ref[...], w_ref[0:POOL_DIM, :], preferred_element_type=F32)
           + jnp.dot(ya_ref[...], w_ref[POOL_DIM:, :], preferred_element_type=F32))
    z = ALPHA * x_ref[...] + gate_ref[0] * mix
    o_ref[...] = _layer_norm(z, g_ref[...], b_ref[...])


def _outproj(x2, yp, ya, w_out, gate1, ln_g, ln_b, *, seq, tm):
    n, d = x2.shape
    tps = seq // tm
    row = lambda w: pl.BlockSpec((tm, w), lambda i: (i, 0))
    full = lambda a: pl.BlockSpec(a.shape, lambda i: (0,) * a.ndim)
    return pl.pallas_call(
        _outproj_kernel,
        grid=(n // tm,),
        in_specs=[row(d), row(POOL_DIM), row(V_DIM), full(w_out),
                  pl.BlockSpec((1, 1, d), lambda i: (i // tps, 0, 0)), full(ln_g), full(ln_b)],
        out_specs=row(d),
        out_shape=jax.ShapeDtypeStruct((n, d), F32),
        compiler_params=_params(("arbitrary",)),
    )(x2, yp, ya, w_out, gate1, ln_g, ln_b)


def _route_kernel(x_ref, sc_ref, sh_ref, whi_ref, wlo_ref, rb_ref,
                  eidx_ref, rank_ref, wts_ref, cnt_ref, h2p_ref, carry_ref, *, tr):
    i = pl.program_id(0)

    @pl.when(i == 0)
    def _():
        carry_ref[...] = jnp.zeros_like(carry_ref)

    h2 = x_ref[...] * (1.0 + sc_ref[0]) + sh_ref[0]
    h2p_ref[...] = _pack_bf16_halves(h2)
    hi = h2.astype(BF16)
    lo = (h2 - hi.astype(F32)).astype(BF16)
    nt = (((1,), (1,)), ((), ()))
    logits = (lax.dot_general(whi_ref[...], hi, nt, preferred_element_type=F32)
              + lax.dot_general(wlo_ref[...], hi, nt, preferred_element_type=F32)
              + lax.dot_general(whi_ref[...], lo, nt, preferred_element_type=F32))
    scores = _sigmoid(logits)
    sel = scores + rb_ref[...]
    erow = lax.broadcasted_iota(jnp.int32, (N_EXPERTS, tr), 0).astype(F32)

    g3 = sel.reshape(N_GROUPS, GROUP_SIZE, tr)
    r3 = lax.broadcasted_iota(jnp.int32, (N_GROUPS, GROUP_SIZE, tr), 1).astype(F32)
    m1 = jnp.max(g3, axis=1, keepdims=True)
    first = jnp.min(jnp.where(g3 == m1, r3, float(GROUP_SIZE)), axis=1, keepdims=True)
    m2 = jnp.max(jnp.where(r3 == first, -jnp.inf, g3), axis=1, keepdims=True)
    gscore = (m1 + m2).reshape(N_GROUPS, tr)

    gidx = lax.broadcasted_iota(jnp.int32, (N_GROUPS, tr), 0)
    beaten_by = jnp.zeros((N_GROUPS, tr), jnp.int32)
    for g in range(N_GROUPS):
        other = gscore[g:g + 1, :]
        wins = (other > gscore) | ((other == gscore) & (g < gidx))
        beaten_by = beaten_by + wins.astype(jnp.int32)
    dropped = jnp.where(beaten_by < TOP_K_GROUPS, 0.0, -jnp.inf)
    cur = (g3 + dropped.reshape(N_GROUPS, 1, tr)).reshape(N_EXPERTS, tr)

    picks, weights = [], []
    selmask = jnp.zeros((N_EXPERTS, tr), F32)
    for _ in range(TOP_K):
        mx = jnp.max(cur, axis=0, keepdims=True)
        pick = jnp.min(jnp.where(cur == mx, erow, float(N_EXPERTS)), axis=0, keepdims=True)
        onehot = erow == pick
        weights.append(jnp.sum(jnp.where(onehot, scores, 0.0), axis=0, keepdims=True))
        cur = jnp.where(onehot, -jnp.inf, cur)
        selmask = jnp.where(onehot, 1.0, selmask)
        picks.append(pick)

    t_from = lax.broadcasted_iota(jnp.int32, (tr, tr), 0)
    t_to = lax.broadcasted_iota(jnp.int32, (tr, tr), 1)
    earlier = jnp.where(t_from < t_to, 1.0, 0.0).astype(BF16)
    chosen = selmask.astype(BF16)
    carry = carry_ref[...]
    rankmat = (jnp.dot(chosen, earlier, preferred_element_type=F32)
               + jnp.concatenate([carry] * (tr // LANES), axis=1))
    carry_ref[...] = carry + jnp.dot(chosen, jnp.ones((tr, LANES), BF16), preferred_element_type=F32)
    cnt_ref[...] = carry_ref[...]

    wsum = weights[0]
    for wj in weights[1:]:
        wsum = wsum + wj
    row8 = lax.broadcasted_iota(jnp.int32, (TOP_K, tr), 0)
    eidx = jnp.zeros((TOP_K, tr), jnp.int32)
    rank = jnp.zeros((TOP_K, tr), jnp.int32)
    wts = jnp.zeros((TOP_K, tr), F32)
    for j in range(TOP_K):
        rk = jnp.sum(jnp.where(erow == picks[j], rankmat, 0.0), axis=0, keepdims=True)
        eidx = jnp.where(row8 == j, picks[j].astype(jnp.int32), eidx)
        rank = jnp.where(row8 == j, rk.astype(jnp.int32), rank)
        wts = jnp.where(row8 == j, weights[j] / wsum * ROUTED_SCALE, wts)
    eidx_ref[...] = eidx
    rank_ref[...] = rank
    wts_ref[...] = wts


def _route(x1, scale2, shift2, wr_hi, wr_lo, router_bias, *, seq, tr):
    n, d = x1.shape
    tps = seq // tr
    mod_spec = pl.BlockSpec((1, 1, d), lambda i: (i // tps, 0, 0))
    full = lambda a: pl.BlockSpec(a.shape, lambda i: (0,) * a.ndim)
    assert tr % LANES == 0
    k8 = pl.BlockSpec((TOP_K, tr), lambda i: (0, i))
    return pl.pallas_call(
        functools.partial(_route_kernel, tr=tr),
        grid=(n // tr,),
        in_specs=[pl.BlockSpec((tr, d), lambda i: (i, 0)), mod_spec, mod_spec,
                  full(wr_hi), full(wr_lo), full(router_bias)],
        out_specs=[k8, k8, k8, pl.BlockSpec((N_EXPERTS, LANES), lambda i: (0, 0)),
                   pl.BlockSpec((tr, d // 2), lambda i: (i, 0))],
        out_shape=[jax.ShapeDtypeStruct((TOP_K, n), jnp.int32),
                   jax.ShapeDtypeStruct((TOP_K, n), jnp.int32),
                   jax.ShapeDtypeStruct((TOP_K, n), F32),
                   jax.ShapeDtypeStruct((N_EXPERTS, LANES), F32),
                   jax.ShapeDtypeStruct((n, d // 2), U32)],
        scratch_shapes=[pltpu.VMEM((N_EXPERTS, LANES), F32)],
        compiler_params=_params(("arbitrary",)),
    )(x1, scale2, shift2, wr_hi, wr_lo, router_bias)


def _pack_bf16_halves(x):
    w = x.shape[1] // 2
    lo = pltpu.bitcast(x[:, :w].astype(BF16).astype(F32), U32) >> 16
    hi = pltpu.bitcast(x[:, w:].astype(BF16).astype(F32), U32) & jnp.uint32(0xFFFF0000)
    return lo | hi


def _unpack_bf16_halves(p):
    return pltpu.bitcast(p << 16, F32), pltpu.bitcast(p & jnp.uint32(0xFFFF0000), F32)


def _sc_worker():
    return lax.axis_index("subcore") * SC_CORES + lax.axis_index("core")


def _scatter_rows_sc(rows, idx, n_slots):
    n, width = rows.shape
    k = idx.shape[0] // n
    n_workers = SC_CORES * SC_SUBCORES
    w = SC_ROWS
    per_worker = n // n_workers
    n_chunks = per_worker // w
    assert n == n_workers * n_chunks * w and n_chunks % 2 == 0 and idx.shape[0] == k * n
    mesh = plsc.VectorSubcoreMesh(core_axis_name="core", subcore_axis_name="subcore")

    @functools.partial(
        pl.kernel, mesh=mesh,
        out_type=jax.ShapeDtypeStruct((n_slots, width), rows.dtype),
        scratch_types=[pltpu.VMEM((k * per_worker,), jnp.int32), pltpu.VMEM((2, w, width), rows.dtype),
                       pltpu.SemaphoreType.DMA((2,)), pltpu.SemaphoreType.DMA((2,))])
    def scatter(rows_hbm, idx_hbm, out_hbm, idx_v, rows_v, in_sem, out_sem):
        t0 = _sc_worker() * per_worker
        for j in range(k):
            pltpu.sync_copy(idx_hbm.at[pl.ds(j * n + t0, per_worker)], idx_v.at[pl.ds(j * per_worker, per_worker)])

        def load(i, s):
            return pltpu.make_async_copy(rows_hbm.at[pl.ds(t0 + i * w, w)], rows_v.at[s], in_sem.at[s])

        def send(i, s, j):
            slots = idx_v.at[pl.ds(j * per_worker + i * w, w)]
            return pltpu.make_async_copy(rows_v.at[s], out_hbm.at[slots], out_sem.at[s])

        load(0, 0).start()

        def pair(ii, carry):
            for s in range(2):
                i = ii * 2 + s
                load(i, s).wait()

                @pl.when(i >= 1)
                def _():
                    for j in range(k):
                        send(i - 1, 1 - s, j).wait()

                @pl.when(i + 1 < n_chunks)
                def _():
                    load(i + 1, 1 - s).start()

                for j in range(k):
                    send(i, s, j).start()
            return carry

        lax.fori_loop(0, n_chunks // 2, pair, 0)
        for j in range(k):
            send(n_chunks - 1, 1, j).wait()

    return scatter(rows, idx)


def _dest_kernel(eidx_ref, rank_ref, first_ref, dest_ref):
    tt = eidx_ref.shape[1]
    pieces = [jnp.broadcast_to(first_ref[:, p * LANES:(p + 1) * LANES], (TOP_K, LANES))
              for p in range(N_EXPERTS // LANES)]
    for c in range(tt // LANES):
        cols = slice(c * LANES, (c + 1) * LANES)
        e = eidx_ref[:, cols]
        within = e & (LANES - 1)
        start = jnp.take_along_axis(pieces[0], within, axis=1)
        for p in range(1, len(pieces)):
            start = jnp.where(e // LANES == p, jnp.take_along_axis(pieces[p], within, axis=1), start)
        dest_ref[:, cols] = start + rank_ref[:, cols]


def _dest(eidx_t, rank_t, pstarts, *, tt):
    n = eidx_t.shape[1]
    first = pstarts.reshape(1, N_EXPERTS)
    k8 = pl.BlockSpec((TOP_K, tt), lambda i: (0, i))
    return pl.pallas_call(
        _dest_kernel,
        grid=(n // tt,),
        in_specs=[k8, k8, pl.BlockSpec(first.shape, lambda i: (0, 0))],
        out_specs=k8,
        out_shape=jax.ShapeDtypeStruct((TOP_K, n), jnp.int32),
        compiler_params=_params(("arbitrary",)),
    )(eidx_t, rank_t, first)


def _expert_kernel(fb_ref, nb_ref, cnt_ref, nused_ref, wg_ref, wu_ref, wd_ref, xs_ref, ys_ref,
                   wgb, wub, wdb, xbuf, ybuf, in_sem, out_sem):
    e = pl.program_id(0)
    m = DISPATCH_BLOCK
    ns = EXPERT_SLOTS
    n_used = nused_ref[0]

    def rows(g):
        return pl.ds(pl.multiple_of(g * m, m), m)

    def fetch(g):
        slot = g & (ns - 1)
        return pltpu.make_async_copy(xs_ref.at[rows(g), :], xbuf.at[slot], in_sem.at[slot])

    def put(g):
        slot = g & (ns - 1)
        return pltpu.make_async_copy(ybuf.at[slot], ys_ref.at[rows(g), :], out_sem.at[slot])

    @pl.when(e == 0)
    def _():
        for g0 in range(ns - 1):
            @pl.when(g0 < n_used)
            def _(g0=g0):
                fetch(g0).start()

    wgb[...] = wg_ref[0].astype(BF16)
    wub[...] = wu_ref[0].astype(BF16)
    wdb[...] = wd_ref[0].astype(BF16)

    def block(i, carry):
        g = fb_ref[e] + i
        slot = g & (ns - 1)
        fetch(g).wait()

        @pl.when(g + ns - 1 < n_used)
        def _():
            fetch(g + ns - 1).start()

        @pl.when(g >= ns)
        def _():
            put(g - ns).wait()

        row = lax.broadcasted_iota(jnp.int32, (m, 1), 0)
        packed = jnp.where(row < cnt_ref[e] - i * m, xbuf[slot], jnp.uint32(0))
        x_lo, x_hi = [h.astype(BF16) for h in _unpack_bf16_halves(packed)]
        half = x_lo.shape[1]

        def up_proj(w):
            return (jnp.dot(x_lo, w[0:half, :], preferred_element_type=F32)
                    + jnp.dot(x_hi, w[half:, :], preferred_element_type=F32))

        a = (_silu(up_proj(wgb)) * up_proj(wub)).astype(BF16)
        ybuf[slot] = _pack_bf16_halves(jnp.dot(a, wdb[...], preferred_element_type=F32))
        put(g).start()
        return carry

    lax.fori_loop(0, nb_ref[e], block, 0)

    @pl.when(e == pl.num_programs(0) - 1)
    def _():
        for back in range(ns, 0, -1):
            @pl.when(n_used >= back)
            def _(back=back):
                put(n_used - back).wait()

        ybuf[0] = jnp.zeros(ybuf.shape[1:], ybuf.dtype)
        n_blocks = ys_ref.shape[0] // m

        def tail(wait):
            def body(g, carry):
                cp = pltpu.make_async_copy(ybuf.at[0], ys_ref.at[rows(g), :], out_sem.at[0])
                cp.wait() if wait else cp.start()
                return carry
            return body

        lax.fori_loop(n_used, n_blocks, tail(False), 0)
        lax.fori_loop(n_used, n_blocks, tail(True), 0)


def _experts(first_block, n_blocks_e, counts, n_used, xs, w_gate, w_up, w_down):
    p, dp = xs.shape
    m = DISPATCH_BLOCK
    n_e, d, f = w_gate.shape
    assert d == 2 * dp and p % m == 0
    grid_spec = pltpu.PrefetchScalarGridSpec(
        num_scalar_prefetch=4,
        grid=(n_e,),
        in_specs=[pl.BlockSpec((1, d, f), lambda e, *_: (e, 0, 0)),
                  pl.BlockSpec((1, d, f), lambda e, *_: (e, 0, 0)),
                  pl.BlockSpec((1, f, d), lambda e, *_: (e, 0, 0)),
                  pl.BlockSpec(memory_space=pl.ANY)],
        out_specs=pl.BlockSpec(memory_space=pl.ANY),
        scratch_shapes=[pltpu.VMEM((d, f), BF16), pltpu.VMEM((d, f), BF16), pltpu.VMEM((f, d), BF16),
                        pltpu.VMEM((EXPERT_SLOTS, m, dp), U32), pltpu.VMEM((EXPERT_SLOTS, m, dp), U32),
                        pltpu.SemaphoreType.DMA((EXPERT_SLOTS,)), pltpu.SemaphoreType.DMA((EXPERT_SLOTS,))],
    )
    return pl.pallas_call(
        _expert_kernel,
        grid_spec=grid_spec,
        out_shape=jax.ShapeDtypeStruct((p, dp), U32),
        compiler_params=_params(("arbitrary",)),
    )(first_block, n_blocks_e, counts, n_used, w_gate, w_up, w_down, xs)


def _gather_rows_sc(table, idx):
    n_idx, (_, width) = idx.shape[0], table.shape
    n_workers = SC_CORES * SC_SUBCORES
    w = SC_ROWS
    per_worker = n_idx // n_workers
    n_chunks = per_worker // w
    assert n_idx == n_workers * n_chunks * w and n_chunks % 2 == 0
    mesh = plsc.VectorSubcoreMesh(core_axis_name="core", subcore_axis_name="subcore")

    @functools.partial(
        pl.kernel, mesh=mesh,
        out_type=jax.ShapeDtypeStruct((n_idx, width), table.dtype),
        scratch_types=[pltpu.VMEM((per_worker,), jnp.int32), pltpu.VMEM((2, w, width), table.dtype),
                       pltpu.SemaphoreType.DMA((2,)), pltpu.SemaphoreType.DMA((2,))])
    def gather(table_hbm, idx_hbm, out_hbm, idx_v, rows_v, in_sem, out_sem):
        base = _sc_worker() * per_worker
        pltpu.sync_copy(idx_hbm.at[pl.ds(base, per_worker)], idx_v)

        def fetch(i, s):
            return pltpu.make_async_copy(table_hbm.at[idx_v.at[pl.ds(i * w, w)]], rows_v.at[s], in_sem.at[s])

        def put(i, s):
            return pltpu.make_async_copy(rows_v.at[s], out_hbm.at[pl.ds(base + i * w, w)], out_sem.at[s])

        fetch(0, 0).start()

        def pair(ii, carry):
            for s in range(2):
                i = ii * 2 + s
                fetch(i, s).wait()

                @pl.when(i >= 1)
                def _():
                    put(i - 1, 1 - s).wait()

                @pl.when(i + 1 < n_chunks)
                def _():
                    fetch(i + 1, 1 - s).start()

                put(i, s).start()
            return carry

        lax.fori_loop(0, n_chunks // 2, pair, 0)
        put(n_chunks - 1, 1).wait()

    return gather(table, idx)


def _combine_kernel(x_ref, sc_ref, sh_ref, gate_ref, wts_ref, rows_ref,
                    wsg_ref, wsu_ref, wsd_ref, g_ref, b_ref, o_ref):
    x = x_ref[...]
    hb = (x * (1.0 + sc_ref[0]) + sh_ref[0]).astype(BF16)
    sg = jnp.dot(hb, wsg_ref[...], preferred_element_type=F32)
    su = jnp.dot(hb, wsu_ref[...], preferred_element_type=F32)
    shared = jnp.dot((_silu(sg) * su).astype(BF16), wsd_ref[...], preferred_element_type=F32)

    wts = wts_ref[...]
    half = shared.shape[1] // 2
    lo, hi = shared[:, :half], shared[:, half:]
    for j in range(TOP_K):
        y_lo, y_hi = _unpack_bf16_halves(rows_ref[j])
        lo = lo + wts[:, j:j + 1] * y_lo
        hi = hi + wts[:, j:j + 1] * y_hi
    z = ALPHA * x + gate_ref[0] * jnp.concatenate([lo, hi], axis=1)
    o_ref[...] = _layer_norm(z, g_ref[...], b_ref[...])


def _combine(x1, scale2, shift2, gate2, wts, rows, ws_gate, ws_up, ws_down, ln_g, ln_b, *, seq, tc):
    n, d = x1.shape
    tps = seq // tc
    mod_spec = pl.BlockSpec((1, 1, d), lambda i: (i // tps, 0, 0))
    full = lambda a: pl.BlockSpec(a.shape, lambda i: (0,) * a.ndim)
    return pl.pallas_call(
        _combine_kernel,
        grid=(n // tc,),
        in_specs=[pl.BlockSpec((tc, d), lambda i: (i, 0)), mod_spec, mod_spec, mod_spec,
                  pl.BlockSpec((tc, TOP_K), lambda i: (i, 0)),
                  pl.BlockSpec((TOP_K, tc, d // 2), lambda i: (0, i, 0)),
                  full(ws_gate), full(ws_up), full(ws_down), full(ln_g), full(ln_b)],
        out_specs=pl.BlockSpec((tc, d), lambda i: (i, 0)),
        out_shape=jax.ShapeDtypeStruct((n, d), F32),
        compiler_params=_params(("arbitrary",)),
    )(x1, scale2, shift2, gate2, wts, rows, ws_gate, ws_up, ws_down, ln_g, ln_b)


def _layer(x, c, w_ada, b_ada, w_in, pool_w, pool_scale, lq1, lk1, lq2, lk2, subln_g, w_out,
           ln1_g, ln1_b, w_router, router_bias, w_gate, w_up, w_down, ws_gate, ws_up, ws_down,
           ln2_g, ln2_b, rel_table, *, tm=512, tr=256, tc=256):
    bsz, seq, d = x.shape
    n = bsz * seq
    x2 = x.reshape(n, d)
    row = lambda a: a.reshape(1, -1)

    mod = _modulation(c, w_ada, b_ada)
    shift1, scale1, gate1, shift2, scale2, gate2 = [
        mod[:, j * d:(j + 1) * d].reshape(bsz, 1, d) for j in range(6)]

    n_main = POOL_DIM + 2 * QK_DIM
    yp, q, k, vt = _inproj(x2, scale1, shift1, w_in[:, :n_main].astype(BF16), w_in[:, n_main:].T.astype(BF16),
                           pool_w.astype(BF16), row(pool_scale), seq=seq, tm=tm)
    bias_tiles = _bias_tiles(rel_table, seq // ATT_TILE)
    ya = _attention(q, k, vt, bias_tiles, row(lq1), row(lk1), row(lq2), row(lk2), subln_g.reshape(-1, 1),
                    bsz=bsz, seq=seq)
    x1 = _outproj(x2, yp, ya, w_out.astype(BF16), gate1, row(ln1_g), row(ln1_b), seq=seq, tm=tm)

    wr_t = w_router.T
    wr_hi = wr_t.astype(BF16)
    wr_lo = (wr_t - wr_hi.astype(F32)).astype(BF16)
    eidx_t, rank_t, wts_t, cnt, h2p = _route(x1, scale2, shift2, wr_hi, wr_lo, router_bias.reshape(-1, 1),
                                             seq=seq, tr=tr)

    m = DISPATCH_BLOCK
    counts = cnt[:, 0].astype(jnp.int32)
    padded = (counts + m - 1) // m * m
    pends = jnp.cumsum(padded)
    pstarts = pends - padded
    n_blocks = -(-(n * TOP_K + N_EXPERTS * (m - 1)) // m)
    n_used = (pends[-1:] // m).astype(jnp.int32)
    dest = _dest(eidx_t, rank_t, pstarts, tt=min(n, 2048)).reshape(TOP_K * n)

    xs = _scatter_rows_sc(h2p, dest, n_blocks * m)
    ys = _experts(pstarts // m, padded // m, counts, n_used, xs, w_gate, w_up, w_down)
    picked = _gather_rows_sc(ys, dest).reshape(TOP_K, n, d // 2)
    out = _combine(x1, scale2, shift2, gate2, wts_t.T, picked,
                   ws_gate.astype(BF16), ws_up.astype(BF16), ws_down.astype(BF16),
                   row(ln2_g), row(ln2_b), seq=seq, tc=tc)
    return out.reshape(bsz, seq, d)


def kernel(x, c, w_ada, b_ada, w_in, pool_w, pool_scale, lambda_q1, lambda_k1, lambda_q2, lambda_k2,
           subln_g, w_out, ln1_g, ln1_b, w_router, router_bias, w_gate, w_up, w_down,
           ws_gate, ws_up, ws_down, ln2_g, ln2_b, rel_table):
    per_layer = (w_ada, b_ada, w_in, pool_w, pool_scale, lambda_q1, lambda_k1, lambda_q2, lambda_k2,
                 subln_g, w_out, ln1_g, ln1_b, w_router, router_bias, w_gate, w_up, w_down,
                 ws_gate, ws_up, ws_down, ln2_g, ln2_b)
    assert all(a.shape[0] == DEPTH == 1 for a in per_layer)
    return _layer(x, c, *[a.reshape(a.shape[1:]) for a in per_layer], rel_table)
```

```python
import functools
import math

import jax
import jax.numpy as jnp
from jax import lax
from jax.experimental import pallas as pl
from jax.experimental.pallas import tpu as pltpu
from jax.experimental.pallas import tpu_sc as plsc

F32 = jnp.float32
BF16 = jnp.bfloat16
U32 = jnp.uint32
LANES = 128

D_MODEL = 1024
CHUNK = 64
Q_BLOCK = 128
ATT_TILE = 256
POOL_DIM = 512
POOL_WINDOWS = (2, 4, 8, 16)
POOL_GROUP_DIM = 128
MAX_WINDOW = max(POOL_WINDOWS)
ATTN_HEADS = 4
ATTN_HEAD_DIM = 64
QK_DIM = 512
V_DIM = 512
IN_DIM = 2048
NUM_BUCKETS = 32
MAX_DISTANCE = 128
N_EXPERTS = 256
TOP_K = 8
N_GROUPS = 8
GROUP_SIZE = N_EXPERTS // N_GROUPS
TOP_K_GROUPS = 4
EXPERT_DIM = 256
ROUTED_SCALE = 2.5
DISPATCH_BLOCK = 512
EXPERT_SLOTS = 4
SC_CORES, SC_SUBCORES = 2, 16
SC_ROWS = 64
DEPTH = 1
ALPHA = (2.0 * DEPTH) ** 0.25
LN_EPS = 1e-5
LAMBDA_INIT = 0.8 - 0.6 * math.exp(-0.3 * 0)

VMEM_LIMIT = 48 * 1024 * 1024


def _sigmoid(x):
    return 1.0 / (1.0 + jnp.exp(-x))


def _silu(x):
    return x * _sigmoid(x)


def _layer_norm(z, g, b):
    mu = jnp.mean(z, axis=-1, keepdims=True)
    zc = z - mu
    var = jnp.mean(zc * zc, axis=-1, keepdims=True)
    return zc * lax.rsqrt(var + LN_EPS) * g + b


def _params(sem=None):
    return pltpu.CompilerParams(dimension_semantics=sem, vmem_limit_bytes=VMEM_LIMIT)


def _mod_kernel(c_ref, w_ref, b_ref, o_ref):
    ca = _silu(c_ref[...])
    o_ref[...] = jnp.dot(ca, w_ref[...], preferred_element_type=F32,
                         precision=lax.Precision.HIGHEST) + b_ref[...]


def _modulation(c, w_ada, b_ada):
    bsz, d = c.shape
    n_out = w_ada.shape[1]
    return pl.pallas_call(
        _mod_kernel,
        grid=(n_out // d,),
        in_specs=[pl.BlockSpec((bsz, d), lambda j: (0, 0)),
                  pl.BlockSpec((d, d), lambda j: (0, j)),
                  pl.BlockSpec((1, d), lambda j: (0, j))],
        out_specs=pl.BlockSpec((bsz, d), lambda j: (0, j)),
        out_shape=jax.ShapeDtypeStruct((bsz, n_out), F32),
        compiler_params=_params(("arbitrary",)),
    )(c, w_ada, b_ada.reshape(1, n_out))


def _inproj_kernel(x_ref, sc_ref, sh_ref, w_ref, wvt_ref, pw_ref, ps_ref,
                   yp_ref, q_ref, k_ref, vt_ref, ext_ref, *, tm, seq):
    i = pl.program_id(0)
    tiles_per_seq = seq // tm
    it = i % tiles_per_seq
    h = x_ref[...] * (1.0 + sc_ref[0]) + sh_ref[0]
    hb = h.astype(BF16)
    proj = jnp.dot(hb, w_ref[...], preferred_element_type=F32)
    u = proj[:, :POOL_DIM]
    q_ref[...] = (proj[:, POOL_DIM:POOL_DIM + QK_DIM] * (ATTN_HEAD_DIM ** -0.5)).astype(BF16)
    k_ref[...] = proj[:, POOL_DIM + QK_DIM:POOL_DIM + 2 * QK_DIM].astype(BF16)
    vt = lax.dot_general(wvt_ref[...], hb, (((1,), (1,)), ((), ())), preferred_element_type=F32)
    for j in range(tm // ATT_TILE):
        vt_ref[0, j] = vt[:, j * ATT_TILE:(j + 1) * ATT_TILE].astype(BF16)

    @pl.when(it == 0)
    def _():
        ext_ref[0:MAX_WINDOW, :] = jnp.zeros((MAX_WINDOW, POOL_DIM), F32)

    ext_ref[MAX_WINDOW:MAX_WINDOW + tm, :] = u
    pos = (it * tm + lax.broadcasted_iota(jnp.int32, (tm, 1), 0) + 1).astype(F32)
    for g, w in enumerate(POOL_WINDOWS):
        c0, c1 = g * POOL_GROUP_DIM, (g + 1) * POOL_GROUP_DIM
        s = ext_ref[MAX_WINDOW:MAX_WINDOW + tm, c0:c1]
        for j in range(1, w):
            s = s + ext_ref[MAX_WINDOW - j:MAX_WINDOW - j + tm, c0:c1]
        pooled = s / jnp.minimum(pos, float(w)) - u[:, c0:c1]
        y = jnp.dot(pooled.astype(BF16), pw_ref[g], preferred_element_type=F32)
        yp_ref[:, c0:c1] = (y * ps_ref[:, c0:c1]).astype(BF16)
    ext_ref[0:MAX_WINDOW, :] = ext_ref[tm:tm + MAX_WINDOW, :]


def _inproj(x2, scale1, shift1, w_main, w_vt, pool_w, pool_scale, *, seq, tm):
    n, d = x2.shape
    assert n % tm == 0 and seq % tm == 0 and tm >= 2 * MAX_WINDOW and tm % ATT_TILE == 0
    tps = seq // tm
    tpt = tm // ATT_TILE
    mod_spec = pl.BlockSpec((1, 1, d), lambda i: (i // tps, 0, 0))
    row = lambda w: pl.BlockSpec((tm, w), lambda i: (i, 0))
    full = lambda a: pl.BlockSpec(a.shape, lambda i: (0,) * a.ndim)
    return pl.pallas_call(
        functools.partial(_inproj_kernel, tm=tm, seq=seq),
        grid=(n // tm,),
        in_specs=[row(d), mod_spec, mod_spec, full(w_main), full(w_vt), full(pool_w), full(pool_scale)],
        out_specs=[row(POOL_DIM), row(QK_DIM), row(QK_DIM),
                   pl.BlockSpec((1, tpt, V_DIM, ATT_TILE), lambda i: (i // tps, i % tps, 0, 0))],
        out_shape=[jax.ShapeDtypeStruct((n, POOL_DIM), BF16),
                   jax.ShapeDtypeStruct((n, QK_DIM), BF16),
                   jax.ShapeDtypeStruct((n, QK_DIM), BF16),
                   jax.ShapeDtypeStruct((n // seq, seq // ATT_TILE, V_DIM, ATT_TILE), BF16)],
        scratch_shapes=[pltpu.VMEM((tm + MAX_WINDOW, POOL_DIM), F32)],
        compiler_params=_params(("arbitrary",)),
    )(x2, scale1, shift1, w_main, w_vt, pool_w, pool_scale)


def _bias_kernel(tab_ref, o_ref):
    delta = pl.program_id(0)
    r = lax.broadcasted_iota(jnp.int32, (ATT_TILE, ATT_TILE), 0)
    c = lax.broadcasted_iota(jnp.int32, (ATT_TILE, ATT_TILE), 1)
    rel = r - c - delta * ATT_TILE
    half = NUM_BUCKETS // 2
    max_exact = half // 2
    ret = jnp.where(rel > 0, half, 0)
    n = jnp.abs(rel)
    nf = jnp.maximum(n, 1).astype(F32)
    large = max_exact + (jnp.log(nf / max_exact) / math.log(MAX_DISTANCE / max_exact)
                         * (half - max_exact)).astype(jnp.int32)
    large = jnp.minimum(large, half - 1)
    bucket = ret + jnp.where(n < max_exact, n, large)
    for h in range(ATTN_HEADS):
        acc = jnp.zeros((ATT_TILE, ATT_TILE), F32)
        for b in range(NUM_BUCKETS):
            acc = jnp.where(bucket == b, tab_ref[b, h], acc)
        o_ref[h, 0] = acc


def _bias_tiles(rel_table, n_tiles):
    return pl.pallas_call(
        _bias_kernel,
        grid=(n_tiles,),
        in_specs=[pl.BlockSpec(memory_space=pltpu.SMEM)],
        out_specs=pl.BlockSpec((ATTN_HEADS, 1, ATT_TILE, ATT_TILE), lambda dlt: (0, dlt, 0, 0)),
        out_shape=jax.ShapeDtypeStruct((ATTN_HEADS, n_tiles, ATT_TILE, ATT_TILE), F32),
        compiler_params=_params(("arbitrary",)),
    )(rel_table)


def _attn_kernel(q_ref, k_ref, vt_ref, bias_ref, lq1_ref, lk1_ref, lq2_ref, lk2_ref, g_ref, o_ref, *acc_refs):
    qt = pl.program_id(1)
    t = ATT_TILE
    n_maps = 2 * ATTN_HEADS
    lam = (jnp.exp(jnp.sum(lq1_ref[...] * lk1_ref[...], axis=-1, keepdims=True))
           - jnp.exp(jnp.sum(lq2_ref[...] * lk2_ref[...], axis=-1, keepdims=True))
           + LAMBDA_INIT)
    r = lax.broadcasted_iota(jnp.int32, (t, t), 0)
    c = lax.broadcasted_iota(jnp.int32, (t, t), 1)
    allowed = (r // CHUNK) <= (c // CHUNK)
    hd2 = 2 * ATTN_HEAD_DIM

    def block(kt, carry, diagonal):
        koff = pl.multiple_of(kt * t, t)

        def scores(hm):
            col = hm * ATTN_HEAD_DIM
            qh = q_ref[:, col:col + ATTN_HEAD_DIM]
            kh = k_ref[pl.ds(koff, t), col:col + ATTN_HEAD_DIM]
            s = lax.dot_general(kh, qh, (((1,), (1,)), ((), ())),
                                preferred_element_type=F32) + bias_ref[hm // 2, qt - kt]
            return jnp.where(allowed, s, -jnp.inf) if diagonal else s

        def softmax(hm, s):
            m_old, l_old = carry[2 * hm:2 * hm + 2]
            m_new = jnp.maximum(m_old, jnp.max(s, axis=0, keepdims=True))
            alpha = jnp.exp(m_old - m_new)
            p = jnp.exp(s - m_new)
            return m_new, alpha * l_old + jnp.sum(p, axis=0, keepdims=True), alpha, p.astype(BF16)

        def accumulate(hm, alpha, p):
            h = hm // 2
            vth = vt_ref[0, kt, h * hd2:(h + 1) * hd2, :]
            acc_refs[hm][...] = alpha * acc_refs[hm][...] + jnp.dot(vth, p, preferred_element_type=F32)

        s_vals, sm_vals, out = {}, {}, [None] * (2 * n_maps)
        for step in range(n_maps + 2):
            if step < n_maps:
                s_vals[step] = scores(step)
            if 0 <= step - 1 < n_maps:
                hm = step - 1
                m_new, l_new, alpha, p = softmax(hm, s_vals.pop(hm))
                out[2 * hm], out[2 * hm + 1] = m_new, l_new
                sm_vals[hm] = (alpha, p)
            if 0 <= step - 2 < n_maps:
                accumulate(step - 2, *sm_vals.pop(step - 2))
        return tuple(out)

    for acc in acc_refs:
        acc[...] = jnp.zeros_like(acc)
    one = (jnp.full((1, t), -jnp.inf, F32), jnp.zeros((1, t), F32))
    carry = lax.fori_loop(0, qt, lambda kt, cr: block(kt, cr, False), one * n_maps)
    carry = block(qt, carry, True)
    for h in range(ATTN_HEADS):
        l0, l1 = carry[4 * h + 1], carry[4 * h + 3]
        o = acc_refs[2 * h][...] / l0 - lam * (acc_refs[2 * h + 1][...] / l1)
        y = o * lax.rsqrt(jnp.mean(o * o, axis=0, keepdims=True) + LN_EPS) * g_ref[...]
        o_ref[:, h * hd2:(h + 1) * hd2] = (y * (1.0 - LAMBDA_INIT)).T.astype(BF16)


def _attention(q, k, vt, bias_tiles, lq1, lk1, lq2, lk2, subln_g, *, bsz, seq):
    t = ATT_TILE
    nt = seq // t
    full = lambda a: pl.BlockSpec(a.shape, lambda b, j: (0,) * a.ndim)
    return pl.pallas_call(
        _attn_kernel,
        grid=(bsz, nt),
        in_specs=[pl.BlockSpec((t, QK_DIM), lambda b, j: (b * nt + j, 0)),
                  pl.BlockSpec((seq, QK_DIM), lambda b, j: (b, 0)),
                  pl.BlockSpec((1, nt, V_DIM, t), lambda b, j: (b, 0, 0, 0)),
                  full(bias_tiles), full(lq1), full(lk1), full(lq2), full(lk2), full(subln_g)],
        out_specs=pl.BlockSpec((t, V_DIM), lambda b, j: (b * nt + j, 0)),
        out_shape=jax.ShapeDtypeStruct((bsz * seq, V_DIM), BF16),
        scratch_shapes=[pltpu.VMEM((2 * ATTN_HEAD_DIM, t), F32) for _ in range(2 * ATTN_HEADS)],
        compiler_params=_params(("arbitrary", "arbitrary")),
    )(q, k, vt, bias_tiles, lq1, lk1, lq2, lk2, subln_g)


def _outproj_kernel(x_ref, yp_ref, ya_ref, w_ref, gate_ref, g_ref, b_ref, o_ref):
    mix = (jnp.dot(yp_ref[...], w_ref[0:POOL_DIM, :], preferred_element_type=F32)
           + jnp.dot(ya_ref[...], w_ref[POOL_DIM:, :], preferred_element_type=F32))
    z = ALPHA * x_ref[...] + gate_ref[0] * mix
    o_ref[...] = _layer_norm(z, g_ref[...], b_ref[...])


def _outproj(x2, yp, ya, w_out, gate1, ln_g, ln_b, *, seq, tm):
    n, d = x2.shape
    tps = seq // tm
    row = lambda w: pl.BlockSpec((tm, w), lambda i: (i, 0))
    full = lambda a: pl.BlockSpec(a.shape, lambda i: (0,) * a.ndim)
    return pl.pallas_call(
        _outproj_kernel,
        grid=(n // tm,),
        in_specs=[row(d), row(POOL_DIM), row(V_DIM), full(w_out),
                  pl.BlockSpec((1, 1, d), lambda i: (i // tps, 0, 0)), full(ln_g), full(ln_b)],
        out_specs=row(d),
        out_shape=jax.ShapeDtypeStruct((n, d), F32),
        compiler_params=_params(("arbitrary",)),
    )(x2, yp, ya, w_out, gate1, ln_g, ln_b)


def _route_kernel(x_ref, sc_ref, sh_ref, whi_ref, wlo_ref, rb_ref,
                  eidx_ref, rank_ref, wts_ref, cnt_ref, h2p_ref, carry_ref, *, tr):
    i = pl.program_id(0)

    @pl.when(i == 0)
    def _():
        carry_ref[...] = jnp.zeros_like(carry_ref)

    h2 = x_ref[...] * (1.0 + sc_ref[0]) + sh_ref[0]
    h2p_ref[...] = _pack_bf16_halves(h2)
    hi = h2.astype(BF16)
    lo = (h2 - hi.astype(F32)).astype(BF16)
    nt = (((1,), (1,)), ((), ()))
    logits = (lax.dot_general(whi_ref[...], hi, nt, preferred_element_type=F32)
              + lax.dot_general(wlo_ref[...], hi, nt, preferred_element_type=F32)
              + lax.dot_general(whi_ref[...], lo, nt, preferred_element_type=F32))
    scores = _sigmoid(logits)
    sel = scores + rb_ref[...]
    erow = lax.broadcasted_iota(jnp.int32, (N_EXPERTS, tr), 0).astype(F32)

    g3 = sel.reshape(N_GROUPS, GROUP_SIZE, tr)
    r3 = lax.broadcasted_iota(jnp.int32, (N_GROUPS, GROUP_SIZE, tr), 1).astype(F32)
    m1 = jnp.max(g3, axis=1, keepdims=True)
    first = jnp.min(jnp.where(g3 == m1, r3, float(GROUP_SIZE)), axis=1, keepdims=True)
    m2 = jnp.max(jnp.where(r3 == first, -jnp.inf, g3), axis=1, keepdims=True)
    gscore = (m1 + m2).reshape(N_GROUPS, tr)

    gidx = lax.broadcasted_iota(jnp.int32, (N_GROUPS, tr), 0)
    beaten_by = jnp.zeros((N_GROUPS, tr), jnp.int32)
    for g in range(N_GROUPS):
        other = gscore[g:g + 1, :]
        wins = (other > gscore) | ((other == gscore) & (g < gidx))
        beaten_by = beaten_by + wins.astype(jnp.int32)
    dropped = jnp.where(beaten_by < TOP_K_GROUPS, 0.0, -jnp.inf)
    cur = (g3 + dropped.reshape(N_GROUPS, 1, tr)).reshape(N_EXPERTS, tr)

    picks, weights = [], []
    selmask = jnp.zeros((N_EXPERTS, tr), F32)
    for _ in range(TOP_K):
        mx = jnp.max(cur, axis=0, keepdims=True)
        pick = jnp.min(jnp.where(cur == mx, erow, float(N_EXPERTS)), axis=0, keepdims=True)
        onehot = erow == pick
        weights.append(jnp.sum(jnp.where(onehot, scores, 0.0), axis=0, keepdims=True))
        cur = jnp.where(onehot, -jnp.inf, cur)
        selmask = jnp.where(onehot, 1.0, selmask)
        picks.append(pick)

    t_from = lax.broadcasted_iota(jnp.int32, (tr, tr), 0)
    t_to = lax.broadcasted_iota(jnp.int32, (tr, tr), 1)
    earlier = jnp.where(t_from < t_to, 1.0, 0.0).astype(BF16)
    chosen = selmask.astype(BF16)
    carry = carry_ref[...]
    rankmat = (jnp.dot(chosen, earlier, preferred_element_type=F32)
               + jnp.concatenate([carry] * (tr // LANES), axis=1))
    carry_ref[...] = carry + jnp.dot(chosen, jnp.ones((tr, LANES), BF16), preferred_element_type=F32)
    cnt_ref[...] = carry_ref[...]

    wsum = weights[0]
    for wj in weights[1:]:
        wsum = wsum + wj
    row8 = lax.broadcasted_iota(jnp.int32, (TOP_K, tr), 0)
    eidx = jnp.zeros((TOP_K, tr), jnp.int32)
    rank = jnp.zeros((TOP_K, tr), jnp.int32)
    wts = jnp.zeros((TOP_K, tr), F32)
    for j in range(TOP_K):
        rk = jnp.sum(jnp.where(erow == picks[j], rankmat, 0.0), axis=0, keepdims=True)
        eidx = jnp.where(row8 == j, picks[j].astype(jnp.int32), eidx)
        rank = jnp.where(row8 == j, rk.astype(jnp.int32), rank)
        wts = jnp.where(row8 == j, weights[j] / wsum * ROUTED_SCALE, wts)
    eidx_ref[...] = eidx
    rank_ref[...] = rank
    wts_ref[...] = wts


def _route(x1, scale2, shift2, wr_hi, wr_lo, router_bias, *, seq, tr):
    n, d = x1.shape
    tps = seq // tr
    mod_spec = pl.BlockSpec((1, 1, d), lambda i: (i // tps, 0, 0))
    full = lambda a: pl.BlockSpec(a.shape, lambda i: (0,) * a.ndim)
    assert tr % LANES == 0
    k8 = pl.BlockSpec((TOP_K, tr), lambda i: (0, i))
    return pl.pallas_call(
        functools.partial(_route_kernel, tr=tr),
        grid=(n // tr,),
        in_specs=[pl.BlockSpec((tr, d), lambda i: (i, 0)), mod_spec, mod_spec,
                  full(wr_hi), full(wr_lo), full(router_bias)],
        out_specs=[k8, k8, k8, pl.BlockSpec((N_EXPERTS, LANES), lambda i: (0, 0)),
                   pl.BlockSpec((tr, d // 2), lambda i: (i, 0))],
        out_shape=[jax.ShapeDtypeStruct((TOP_K, n), jnp.int32),
                   jax.ShapeDtypeStruct((TOP_K, n), jnp.int32),
                   jax.ShapeDtypeStruct((TOP_K, n), F32),
                   jax.ShapeDtypeStruct((N_EXPERTS, LANES), F32),
                   jax.ShapeDtypeStruct((n, d // 2), U32)],
        scratch_shapes=[pltpu.VMEM((N_EXPERTS, LANES), F32)],
        compiler_params=_params(("arbitrary",)),
    )(x1, scale2, shift2, wr_hi, wr_lo, router_bias)


def _pack_bf16_halves(x):
    w = x.shape[1] // 2
    lo = pltpu.bitcast(x[:, :w].astype(BF16).astype(F32), U32) >> 16
    hi = pltpu.bitcast(x[:, w:].astype(BF16).astype(F32), U32) & jnp.uint32(0xFFFF0000)
    return lo | hi


def _unpack_bf16_halves(p):
    return pltpu.bitcast(p << 16, F32), pltpu.bitcast(p & jnp.uint32(0xFFFF0000), F32)


def _sc_worker():
    return lax.axis_index("subcore") * SC_CORES + lax.axis_index("core")


def _scatter_rows_sc(rows, idx, n_slots):
    n, width = rows.shape
    k = idx.shape[0] // n
    n_workers = SC_CORES * SC_SUBCORES
    w = SC_ROWS
    per_worker = n // n_workers
    n_chunks = per_worker // w
    assert n == n_workers * n_chunks * w and n_chunks % 2 == 0 and idx.shape[0] == k * n
    mesh = plsc.VectorSubcoreMesh(core_axis_name="core", subcore_axis_name="subcore")

    @functools.partial(
        pl.kernel, mesh=mesh,
        out_type=jax.ShapeDtypeStruct((n_slots, width), rows.dtype),
        scratch_types=[pltpu.VMEM((k * per_worker,), jnp.int32), pltpu.VMEM((2, w, width), rows.dtype),
                       pltpu.SemaphoreType.DMA((2,)), pltpu.SemaphoreType.DMA((2,))])
    def scatter(rows_hbm, idx_hbm, out_hbm, idx_v, rows_v, in_sem, out_sem):
        t0 = _sc_worker() * per_worker
        for j in range(k):
            pltpu.sync_copy(idx_hbm.at[pl.ds(j * n + t0, per_worker)], idx_v.at[pl.ds(j * per_worker, per_worker)])

        def load(i, s):
            return pltpu.make_async_copy(rows_hbm.at[pl.ds(t0 + i * w, w)], rows_v.at[s], in_sem.at[s])

        def send(i, s, j):
            slots = idx_v.at[pl.ds(j * per_worker + i * w, w)]
            return pltpu.make_async_copy(rows_v.at[s], out_hbm.at[slots], out_sem.at[s])

        load(0, 0).start()

        def pair(ii, carry):
            for s in range(2):
                i = ii * 2 + s
                load(i, s).wait()

                @pl.when(i >= 1)
                def _():
                    for j in range(k):
                        send(i - 1, 1 - s, j).wait()

                @pl.when(i + 1 < n_chunks)
                def _():
                    load(i + 1, 1 - s).start()

                for j in range(k):
                    send(i, s, j).start()
            return carry

        lax.fori_loop(0, n_chunks // 2, pair, 0)
        for j in range(k):
            send(n_chunks - 1, 1, j).wait()

    return scatter(rows, idx)


def _dest_kernel(eidx_ref, rank_ref, first_ref, dest_ref):
    tt = eidx_ref.shape[1]
    pieces = [jnp.broadcast_to(first_ref[:, p * LANES:(p + 1) * LANES], (TOP_K, LANES))
              for p in range(N_EXPERTS // LANES)]
    for c in range(tt // LANES):
        cols = slice(c * LANES, (c + 1) * LANES)
        e = eidx_ref[:, cols]
        within = e & (LANES - 1)
        start = jnp.take_along_axis(pieces[0], within, axis=1)
        for p in range(1, len(pieces)):
            start = jnp.where(e // LANES == p, jnp.take_along_axis(pieces[p], within, axis=1), start)
        dest_ref[:, cols] = start + rank_ref[:, cols]


def _dest(eidx_t, rank_t, pstarts, *, tt):
    n = eidx_t.shape[1]
    first = pstarts.reshape(1, N_EXPERTS)
    k8 = pl.BlockSpec((TOP_K, tt), lambda i: (0, i))
    return pl.pallas_call(
        _dest_kernel,
        grid=(n // tt,),
        in_specs=[k8, k8, pl.BlockSpec(first.shape, lambda i: (0, 0))],
        out_specs=k8,
        out_shape=jax.ShapeDtypeStruct((TOP_K, n), jnp.int32),
        compiler_params=_params(("arbitrary",)),
    )(eidx_t, rank_t, first)


def _expert_kernel(fb_ref, nb_ref, cnt_ref, nused_ref, wg_ref, wu_ref, wd_ref, xs_ref, ys_ref,
                   wgb, wub, wdb, xbuf, ybuf, in_sem, out_sem):
    e = pl.program_id(0)
    m = DISPATCH_BLOCK
    ns = EXPERT_SLOTS
    n_used = nused_ref[0]

    def rows(g):
        return pl.ds(pl.multiple_of(g * m, m), m)

    def fetch(g):
        slot = g & (ns - 1)
        return pltpu.make_async_copy(xs_ref.at[rows(g), :], xbuf.at[slot], in_sem.at[slot])

    def put(g):
        slot = g & (ns - 1)
        return pltpu.make_async_copy(ybuf.at[slot], ys_ref.at[rows(g), :], out_sem.at[slot])

    @pl.when(e == 0)
    def _():
        for g0 in range(ns - 1):
            @pl.when(g0 < n_used)
            def _(g0=g0):
                fetch(g0).start()

    wgb[...] = wg_ref[0].astype(BF16)
    wub[...] = wu_ref[0].astype(BF16)
    wdb[...] = wd_ref[0].astype(BF16)

    def block(i, carry):
        g = fb_ref[e] + i
        slot = g & (ns - 1)
        fetch(g).wait()

        @pl.when(g + ns - 1 < n_used)
        def _():
            fetch(g + ns - 1).start()

        @pl.when(g >= ns)
        def _():
            put(g - ns).wait()

        row = lax.broadcasted_iota(jnp.int32, (m, 1), 0)
        packed = jnp.where(row < cnt_ref[e] - i * m, xbuf[slot], jnp.uint32(0))
        x_lo, x_hi = [h.astype(BF16) for h in _unpack_bf16_halves(packed)]
        half = x_lo.shape[1]

        def up_proj(w):
            return (jnp.dot(x_lo, w[0:half, :], preferred_element_type=F32)
                    + jnp.dot(x_hi, w[half:, :], preferred_element_type=F32))

        a = (_silu(up_proj(wgb)) * up_proj(wub)).astype(BF16)
        ybuf[slot] = _pack_bf16_halves(jnp.dot(a, wdb[...], preferred_element_type=F32))
        put(g).start()
        return carry

    lax.fori_loop(0, nb_ref[e], block, 0)

    @pl.when(e == pl.num_programs(0) - 1)
    def _():
        for back in range(ns, 0, -1):
            @pl.when(n_used >= back)
            def _(back=back):
                put(n_used - back).wait()

        ybuf[0] = jnp.zeros(ybuf.shape[1:], ybuf.dtype)
        n_blocks = ys_ref.shape[0] // m

        def tail(wait):
            def body(g, carry):
                cp = pltpu.make_async_copy(ybuf.at[0], ys_ref.at[rows(g), :], out_sem.at[0])
                cp.wait() if wait else cp.start()
                return carry
            return body

        lax.fori_loop(n_used, n_blocks, tail(False), 0)
        lax.fori_loop(n_used, n_blocks, tail(True), 0)


def _experts(first_block, n_blocks_e, counts, n_used, xs, w_gate, w_up, w_down):
    p, dp = xs.shape
    m = DISPATCH_BLOCK
    n_e, d, f = w_gate.shape
    assert d == 2 * dp and p % m == 0
    grid_spec = pltpu.PrefetchScalarGridSpec(
        num_scalar_prefetch=4,
        grid=(n_e,),
        in_specs=[pl.BlockSpec((1, d, f), lambda e, *_: (e, 0, 0)),
                  pl.BlockSpec((1, d, f), lambda e, *_: (e, 0, 0)),
                  pl.BlockSpec((1, f, d), lambda e, *_: (e, 0, 0)),
                  pl.BlockSpec(memory_space=pl.ANY)],
        out_specs=pl.BlockSpec(memory_space=pl.ANY),
        scratch_shapes=[pltpu.VMEM((d, f), BF16), pltpu.VMEM((d, f), BF16), pltpu.VMEM((f, d), BF16),
                        pltpu.VMEM((EXPERT_SLOTS, m, dp), U32), pltpu.VMEM((EXPERT_SLOTS, m, dp), U32),
                        pltpu.SemaphoreType.DMA((EXPERT_SLOTS,)), pltpu.SemaphoreType.DMA((EXPERT_SLOTS,))],
    )
    return pl.pallas_call(
        _expert_kernel,
        grid_spec=grid_spec,
        out_shape=jax.ShapeDtypeStruct((p, dp), U32),
        compiler_params=_params(("arbitrary",)),
    )(first_block, n_blocks_e, counts, n_used, w_gate, w_up, w_down, xs)


def _gather_rows_sc(table, idx):
    n_idx, (_, width) = idx.shape[0], table.shape
    n_workers = SC_CORES * SC_SUBCORES
    w = SC_ROWS
    per_worker = n_idx // n_workers
    n_chunks = per_worker // w
    assert n_idx == n_workers * n_chunks * w and n_chunks % 2 == 0
    mesh = plsc.VectorSubcoreMesh(core_axis_name="core", subcore_axis_name="subcore")

    @functools.partial(
        pl.kernel, mesh=mesh,
        out_type=jax.ShapeDtypeStruct((n_idx, width), table.dtype),
        scratch_types=[pltpu.VMEM((per_worker,), jnp.int32), pltpu.VMEM((2, w, width), table.dtype),
                       pltpu.SemaphoreType.DMA((2,)), pltpu.SemaphoreType.DMA((2,))])
    def gather(table_hbm, idx_hbm, out_hbm, idx_v, rows_v, in_sem, out_sem):
        base = _sc_worker() * per_worker
        pltpu.sync_copy(idx_hbm.at[pl.ds(base, per_worker)], idx_v)

        def fetch(i, s):
            return pltpu.make_async_copy(table_hbm.at[idx_v.at[pl.ds(i * w, w)]], rows_v.at[s], in_sem.at[s])

        def put(i, s):
            return pltpu.make_async_copy(rows_v.at[s], out_hbm.at[pl.ds(base + i * w, w)], out_sem.at[s])

        fetch(0, 0).start()

        def pair(ii, carry):
            for s in range(2):
                i = ii * 2 + s
                fetch(i, s).wait()

                @pl.when(i >= 1)
                def _():
                    put(i - 1, 1 - s).wait()

                @pl.when(i + 1 < n_chunks)
                def _():
                    fetch(i + 1, 1 - s).start()

                put(i, s).start()
            return carry

        lax.fori_loop(0, n_chunks // 2, pair, 0)
        put(n_chunks - 1, 1).wait()

    return gather(table, idx)


def _combine_kernel(x_ref, sc_ref, sh_ref, gate_ref, wts_ref, rows_ref,
                    wsg_ref, wsu_ref, wsd_ref, g_ref, b_ref, o_ref):
    x = x_ref[...]
    hb = (x * (1.0 + sc_ref[0]) + sh_ref[0]).astype(BF16)
    sg = jnp.dot(hb, wsg_ref[...], preferred_element_type=F32)
    su = jnp.dot(hb, wsu_ref[...], preferred_element_type=F32)
    shared = jnp.dot((_silu(sg) * su).astype(BF16), wsd_ref[...], preferred_element_type=F32)

    wts = wts_ref[...]
    half = shared.shape[1] // 2
    lo, hi = shared[:, :half], shared[:, half:]
    for j in range(TOP_K):
        y_lo, y_hi = _unpack_bf16_halves(rows_ref[j])
        lo = lo + wts[:, j:j + 1] * y_lo
        hi = hi + wts[:, j:j + 1] * y_hi
    z = ALPHA * x + gate_ref[0] * jnp.concatenate([lo, hi], axis=1)
    o_ref[...] = _layer_norm(z, g_ref[...], b_ref[...])


def _combine(x1, scale2, shift2, gate2, wts, rows, ws_gate, ws_up, ws_down, ln_g, ln_b, *, seq, tc):
    n, d = x1.shape
    tps = seq // tc
    mod_spec = pl.BlockSpec((1, 1, d), lambda i: (i // tps, 0, 0))
    full = lambda a: pl.BlockSpec(a.shape, lambda i: (0,) * a.ndim)
    return pl.pallas_call(
        _combine_kernel,
        grid=(n // tc,),
        in_specs=[pl.BlockSpec((tc, d), lambda i: (i, 0)), mod_spec, mod_spec, mod_spec,
                  pl.BlockSpec((tc, TOP_K), lambda i: (i, 0)),
                  pl.BlockSpec((TOP_K, tc, d // 2), lambda i: (0, i, 0)),
                  full(ws_gate), full(ws_up), full(ws_down), full(ln_g), full(ln_b)],
        out_specs=pl.BlockSpec((tc, d), lambda i: (i, 0)),
        out_shape=jax.ShapeDtypeStruct((n, d), F32),
        compiler_params=_params(("arbitrary",)),
    )(x1, scale2, shift2, gate2, wts, rows, ws_gate, ws_up, ws_down, ln_g, ln_b)


def _layer(x, c, w_ada, b_ada, w_in, pool_w, pool_scale, lq1, lk1, lq2, lk2, subln_g, w_out,
           ln1_g, ln1_b, w_router, router_bias, w_gate, w_up, w_down, ws_gate, ws_up, ws_down,
           ln2_g, ln2_b, rel_table, *, tm=512, tr=256, tc=256):
    bsz, seq, d = x.shape
    n = bsz * seq
    x2 = x.reshape(n, d)
    row = lambda a: a.reshape(1, -1)

    mod = _modulation(c, w_ada, b_ada)
    shift1, scale1, gate1, shift2, scale2, gate2 = [
        mod[:, j * d:(j + 1) * d].reshape(bsz, 1, d) for j in range(6)]

    n_main = POOL_DIM + 2 * QK_DIM
    yp, q, k, vt = _inproj(x2, scale1, shift1, w_in[:, :n_main].astype(BF16), w_in[:, n_main:].T.astype(BF16),
                           pool_w.astype(BF16), row(pool_scale), seq=seq, tm=tm)
    bias_tiles = _bias_tiles(rel_table, seq // ATT_TILE)
    ya = _attention(q, k, vt, bias_tiles, row(lq1), row(lk1), row(lq2), row(lk2), subln_g.reshape(-1, 1),
                    bsz=bsz, seq=seq)
    x1 = _outproj(x2, yp, ya, w_out.astype(BF16), gate1, row(ln1_g), row(ln1_b), seq=seq, tm=tm)

    wr_t = w_router.T
    wr_hi = wr_t.astype(BF16)
    wr_lo = (wr_t - wr_hi.astype(F32)).astype(BF16)
    eidx_t, rank_t, wts_t, cnt, h2p = _route(x1, scale2, shift2, wr_hi, wr_lo, router_bias.reshape(-1, 1),
                                             seq=seq, tr=tr)

    m = DISPATCH_BLOCK
    counts = cnt[:, 0].astype(jnp.int32)
    padded = (counts + m - 1) // m * m
    pends = jnp.cumsum(padded)
    pstarts = pends - padded
    n_blocks = -(-(n * TOP_K + N_EXPERTS * (m - 1)) // m)
    n_used = (pends[-1:] // m).astype(jnp.int32)
    dest = _dest(eidx_t, rank_t, pstarts, tt=min(n, 2048)).reshape(TOP_K * n)

    xs = _scatter_rows_sc(h2p, dest, n_blocks * m)
    ys = _experts(pstarts // m, padded // m, counts, n_used, xs, w_gate, w_up, w_down)
    picked = _gather_rows_sc(ys, dest).reshape(TOP_K, n, d // 2)
    out = _combine(x1, scale2, shift2, gate2, wts_t.T, picked,
                   ws_gate.astype(BF16), ws_up.astype(BF16), ws_down.astype(BF16),
                   row(ln2_g), row(ln2_b), seq=seq, tc=tc)
    return out.reshape(bsz, seq, d)


def kernel(x, c, w_ada, b_ada, w_in, pool_w, pool_scale, lambda_q1, lambda_k1, lambda_q2, lambda_k2,
           subln_g, w_out, ln1_g, ln1_b, w_router, router_bias, w_gate, w_up, w_down,
           ws_gate, ws_up, ws_down, ln2_g, ln2_b, rel_table):
    per_layer = (w_ada, b_ada, w_in, pool_w, pool_scale, lambda_q1, lambda_k1, lambda_q2, lambda_k2,
                 subln_g, w_out, ln1_g, ln1_b, w_router, router_bias, w_gate, w_up, w_down,
                 ws_gate, ws_up, ws_down, ln2_g, ln2_b)
    assert all(a.shape[0] == DEPTH == 1 for a in per_layer)
    return _layer(x, c, *[a.reshape(a.shape[1:]) for a in per_layer], rel_table)
```

```python
import functools
import math

import jax
import jax.numpy as jnp
from jax import lax
from jax.experimental import pallas as pl
from jax.experimental.pallas import tpu as pltpu
from jax.experimental.pallas import tpu_sc as plsc

F32 = jnp.float32
BF16 = jnp.bfloat16
U32 = jnp.uint32
LANES = 128

D_MODEL = 1024
CHUNK = 64
Q_BLOCK = 128
ATT_TILE = 256
POOL_DIM = 512
POOL_WINDOWS = (2, 4, 8, 16)
POOL_GROUP_DIM = 128
MAX_WINDOW = max(POOL_WINDOWS)
ATTN_HEADS = 4
ATTN_HEAD_DIM = 64
QK_DIM = 512
V_DIM = 512
IN_DIM = 2048
NUM_BUCKETS = 32
MAX_DISTANCE = 128
N_EXPERTS = 256
TOP_K = 8
N_GROUPS = 8
GROUP_SIZE = N_EXPERTS // N_GROUPS
TOP_K_GROUPS = 4
EXPERT_DIM = 256
ROUTED_SCALE = 2.5
DISPATCH_BLOCK = 512
EXPERT_SLOTS = 4
COMBINE_PARTS = 4
SC_CORES, SC_SUBCORES = 2, 16
SC_ROWS = 64
DEPTH = 1
ALPHA = (2.0 * DEPTH) ** 0.25
LN_EPS = 1e-5
LAMBDA_INIT = 0.8 - 0.6 * math.exp(-0.3 * 0)

VMEM_LIMIT = 48 * 1024 * 1024


def _sigmoid(x):
    return 1.0 / (1.0 + jnp.exp(-x))


def _silu(x):
    return x * _sigmoid(x)


def _layer_norm(z, g, b):
    mu = jnp.mean(z, axis=-1, keepdims=True)
    zc = z - mu
    var = jnp.mean(zc * zc, axis=-1, keepdims=True)
    return zc * lax.rsqrt(var + LN_EPS) * g + b


def _params(sem=None):
    return pltpu.CompilerParams(dimension_semantics=sem, vmem_limit_bytes=VMEM_LIMIT)


def _mod_kernel(c_ref, w_ref, b_ref, o_ref):
    ca = _silu(c_ref[...])
    o_ref[...] = jnp.dot(ca, w_ref[...], preferred_element_type=F32,
                         precision=lax.Precision.HIGHEST) + b_ref[...]


def _modulation(c, w_ada, b_ada):
    bsz, d = c.shape
    n_out = w_ada.shape[1]
    return pl.pallas_call(
        _mod_kernel,
        grid=(n_out // d,),
        in_specs=[pl.BlockSpec((bsz, d), lambda j: (0, 0)),
                  pl.BlockSpec((d, d), lambda j: (0, j)),
                  pl.BlockSpec((1, d), lambda j: (0, j))],
        out_specs=pl.BlockSpec((bsz, d), lambda j: (0, j)),
        out_shape=jax.ShapeDtypeStruct((bsz, n_out), F32),
        compiler_params=_params(("arbitrary",)),
    )(c, w_ada, b_ada.reshape(1, n_out))


def _inproj_kernel(x_ref, sc_ref, sh_ref, w_ref, wvt_ref, pw_ref, ps_ref,
                   yp_ref, q_ref, k_ref, vt_ref, ext_ref, *, tm, seq):
    i = pl.program_id(0)
    tiles_per_seq = seq // tm
    it = i % tiles_per_seq
    h = x_ref[...] * (1.0 + sc_ref[0]) + sh_ref[0]
    hb = h.astype(BF16)
    proj = jnp.dot(hb, w_ref[...], preferred_element_type=F32)
    u = proj[:, :POOL_DIM]
    q_ref[...] = (proj[:, POOL_DIM:POOL_DIM + QK_DIM] * (ATTN_HEAD_DIM ** -0.5)).astype(BF16)
    k_ref[...] = proj[:, POOL_DIM + QK_DIM:POOL_DIM + 2 * QK_DIM].astype(BF16)
    vt = lax.dot_general(wvt_ref[...], hb, (((1,), (1,)), ((), ())), preferred_element_type=F32)
    for j in range(tm // ATT_TILE):
        vt_ref[0, j] = vt[:, j * ATT_TILE:(j + 1) * ATT_TILE].astype(BF16)

    @pl.when(it == 0)
    def _():
        ext_ref[0:MAX_WINDOW, :] = jnp.zeros((MAX_WINDOW, POOL_DIM), F32)

    ext_ref[MAX_WINDOW:MAX_WINDOW + tm, :] = u
    pos = (it * tm + lax.broadcasted_iota(jnp.int32, (tm, 1), 0) + 1).astype(F32)
    for g, w in enumerate(POOL_WINDOWS):
        c0, c1 = g * POOL_GROUP_DIM, (g + 1) * POOL_GROUP_DIM
        s = ext_ref[MAX_WINDOW:MAX_WINDOW + tm, c0:c1]
        for j in range(1, w):
            s = s + ext_ref[MAX_WINDOW - j:MAX_WINDOW - j + tm, c0:c1]
        pooled = s / jnp.minimum(pos, float(w)) - u[:, c0:c1]
        y = jnp.dot(pooled.astype(BF16), pw_ref[g], preferred_element_type=F32)
        yp_ref[:, c0:c1] = (y * ps_ref[:, c0:c1]).astype(BF16)
    ext_ref[0:MAX_WINDOW, :] = ext_ref[tm:tm + MAX_WINDOW, :]


def _inproj(x2, scale1, shift1, w_main, w_vt, pool_w, pool_scale, *, seq, tm):
    n, d = x2.shape
    assert n % tm == 0 and seq % tm == 0 and tm >= 2 * MAX_WINDOW and tm % ATT_TILE == 0
    tps = seq // tm
    tpt = tm // ATT_TILE
    mod_spec = pl.BlockSpec((1, 1, d), lambda i: (i // tps, 0, 0))
    row = lambda w: pl.BlockSpec((tm, w), lambda i: (i, 0))
    full = lambda a: pl.BlockSpec(a.shape, lambda i: (0,) * a.ndim)
    return pl.pallas_call(
        functools.partial(_inproj_kernel, tm=tm, seq=seq),
        grid=(n // tm,),
        in_specs=[row(d), mod_spec, mod_spec, full(w_main), full(w_vt), full(pool_w), full(pool_scale)],
        out_specs=[row(POOL_DIM), row(QK_DIM), row(QK_DIM),
                   pl.BlockSpec((1, tpt, V_DIM, ATT_TILE), lambda i: (i // tps, i % tps, 0, 0))],
        out_shape=[jax.ShapeDtypeStruct((n, POOL_DIM), BF16),
                   jax.ShapeDtypeStruct((n, QK_DIM), BF16),
                   jax.ShapeDtypeStruct((n, QK_DIM), BF16),
                   jax.ShapeDtypeStruct((n // seq, seq // ATT_TILE, V_DIM, ATT_TILE), BF16)],
        scratch_shapes=[pltpu.VMEM((tm + MAX_WINDOW, POOL_DIM), F32)],
        compiler_params=_params(("arbitrary",)),
    )(x2, scale1, shift1, w_main, w_vt, pool_w, pool_scale)


def _bias_kernel(tab_ref, o_ref):
    delta = pl.program_id(0)
    r = lax.broadcasted_iota(jnp.int32, (ATT_TILE, ATT_TILE), 0)
    c = lax.broadcasted_iota(jnp.int32, (ATT_TILE, ATT_TILE), 1)
    rel = r - c - delta * ATT_TILE
    half = NUM_BUCKETS // 2
    max_exact = half // 2
    ret = jnp.where(rel > 0, half, 0)
    n = jnp.abs(rel)
    nf = jnp.maximum(n, 1).astype(F32)
    large = max_exact + (jnp.log(nf / max_exact) / math.log(MAX_DISTANCE / max_exact)
                         * (half - max_exact)).astype(jnp.int32)
    large = jnp.minimum(large, half - 1)
    bucket = ret + jnp.where(n < max_exact, n, large)
    for h in range(ATTN_HEADS):
        acc = jnp.zeros((ATT_TILE, ATT_TILE), F32)
        for b in range(NUM_BUCKETS):
            acc = jnp.where(bucket == b, tab_ref[b, h], acc)
        o_ref[h, 0] = acc


def _bias_tiles(rel_table, n_tiles):
    return pl.pallas_call(
        _bias_kernel,
        grid=(n_tiles,),
        in_specs=[pl.BlockSpec(memory_space=pltpu.SMEM)],
        out_specs=pl.BlockSpec((ATTN_HEADS, 1, ATT_TILE, ATT_TILE), lambda dlt: (0, dlt, 0, 0)),
        out_shape=jax.ShapeDtypeStruct((ATTN_HEADS, n_tiles, ATT_TILE, ATT_TILE), F32),
        compiler_params=_params(("arbitrary",)),
    )(rel_table)


def _attn_kernel(q_ref, k_ref, vt_ref, bias_ref, lq1_ref, lk1_ref, lq2_ref, lk2_ref, g_ref, o_ref, *acc_refs):
    qt = pl.program_id(1)
    t = ATT_TILE
    n_maps = 2 * ATTN_HEADS
    lam = (jnp.exp(jnp.sum(lq1_ref[...] * lk1_ref[...], axis=-1, keepdims=True))
           - jnp.exp(jnp.sum(lq2_ref[...] * lk2_ref[...], axis=-1, keepdims=True))
           + LAMBDA_INIT)
    r = lax.broadcasted_iota(jnp.int32, (t, t), 0)
    c = lax.broadcasted_iota(jnp.int32, (t, t), 1)
    allowed = (r // CHUNK) <= (c // CHUNK)
    hd2 = 2 * ATTN_HEAD_DIM

    def block(kt, carry, diagonal):
        koff = pl.multiple_of(kt * t, t)

        def scores(hm):
            col = hm * ATTN_HEAD_DIM
            qh = q_ref[:, col:col + ATTN_HEAD_DIM]
            kh = k_ref[pl.ds(koff, t), col:col + ATTN_HEAD_DIM]
            s = lax.dot_general(kh, qh, (((1,), (1,)), ((), ())),
                                preferred_element_type=F32) + bias_ref[hm // 2, qt - kt]
            return jnp.where(allowed, s, -jnp.inf) if diagonal else s

        def softmax(hm, s):
            m_old, l_old = carry[2 * hm:2 * hm + 2]
            m_new = jnp.maximum(m_old, jnp.max(s, axis=0, keepdims=True))
            alpha = jnp.exp(m_old - m_new)
            p = jnp.exp(s - m_new)
            return m_new, alpha * l_old + jnp.sum(p, axis=0, keepdims=True), alpha, p.astype(BF16)

        def accumulate(hm, alpha, p):
            h = hm // 2
            vth = vt_ref[0, kt, h * hd2:(h + 1) * hd2, :]
            acc_refs[hm][...] = alpha * acc_refs[hm][...] + jnp.dot(vth, p, preferred_element_type=F32)

        s_vals, sm_vals, out = {}, {}, [None] * (2 * n_maps)
        for step in range(n_maps + 2):
            if step < n_maps:
                s_vals[step] = scores(step)
            if 0 <= step - 1 < n_maps:
                hm = step - 1
                m_new, l_new, alpha, p = softmax(hm, s_vals.pop(hm))
                out[2 * hm], out[2 * hm + 1] = m_new, l_new
                sm_vals[hm] = (alpha, p)
            if 0 <= step - 2 < n_maps:
                accumulate(step - 2, *sm_vals.pop(step - 2))
        return tuple(out)

    for acc in acc_refs:
        acc[...] = jnp.zeros_like(acc)
    one = (jnp.full((1, t), -jnp.inf, F32), jnp.zeros((1, t), F32))
    carry = lax.fori_loop(0, qt, lambda kt, cr: block(kt, cr, False), one * n_maps)
    carry = block(qt, carry, True)
    for h in range(ATTN_HEADS):
        l0, l1 = carry[4 * h + 1], carry[4 * h + 3]
        o = acc_refs[2 * h][...] / l0 - lam * (acc_refs[2 * h + 1][...] / l1)
        y = o * lax.rsqrt(jnp.mean(o * o, axis=0, keepdims=True) + LN_EPS) * g_ref[...]
        o_ref[:, h * hd2:(h + 1) * hd2] = (y * (1.0 - LAMBDA_INIT)).T.astype(BF16)


def _attention(q, k, vt, bias_tiles, lq1, lk1, lq2, lk2, subln_g, *, bsz, seq):
    t = ATT_TILE
    nt = seq // t
    full = lambda a: pl.BlockSpec(a.shape, lambda b, j: (0,) * a.ndim)
    return pl.pallas_call(
        _attn_kernel,
        grid=(bsz, nt),
        in_specs=[pl.BlockSpec((t, QK_DIM), lambda b, j: (b * nt + j, 0)),
                  pl.BlockSpec((seq, QK_DIM), lambda b, j: (b, 0)),
                  pl.BlockSpec((1, nt, V_DIM, t), lambda b, j: (b, 0, 0, 0)),
                  full(bias_tiles), full(lq1), full(lk1), full(lq2), full(lk2), full(subln_g)],
        out_specs=pl.BlockSpec((t, V_DIM), lambda b, j: (b * nt + j, 0)),
        out_shape=jax.ShapeDtypeStruct((bsz * seq, V_DIM), BF16),
        scratch_shapes=[pltpu.VMEM((2 * ATTN_HEAD_DIM, t), F32) for _ in range(2 * ATTN_HEADS)],
        compiler_params=_params(("arbitrary", "arbitrary")),
    )(q, k, vt, bias_tiles, lq1, lk1, lq2, lk2, subln_g)


def _outproj_kernel(x_ref, yp_ref, ya_ref, w_ref, gate_ref, g_ref, b_ref, o_ref):
    mix = (jnp.dot(yp_ref[...], w_ref[0:POOL_DIM, :], preferred_element_type=F32)
           + jnp.dot(ya_ref[...], w_ref[POOL_DIM:, :], preferred_element_type=F32))
    z = ALPHA * x_ref[...] + gate_ref[0] * mix
    o_ref[...] = _layer_norm(z, g_ref[...], b_ref[...])


def _outproj(x2, yp, ya, w_out, gate1, ln_g, ln_b, *, seq, tm):
    n, d = x2.shape
    tps = seq // tm
    row = lambda w: pl.BlockSpec((tm, w), lambda i: (i, 0))
    full = lambda a: pl.BlockSpec(a.shape, lambda i: (0,) * a.ndim)
    return pl.pallas_call(
        _outproj_kernel,
        grid=(n // tm,),
        in_specs=[row(d), row(POOL_DIM), row(V_DIM), full(w_out),
                  pl.BlockSpec((1, 1, d), lambda i: (i // tps, 0, 0)), full(ln_g), full(ln_b)],
        out_specs=row(d),
        out_shape=jax.ShapeDtypeStruct((n, d), F32),
        compiler_params=_params(("arbitrary",)),
    )(x2, yp, ya, w_out, gate1, ln_g, ln_b)


def _route_kernel(x_ref, sc_ref, sh_ref, whi_ref, wlo_ref, rb_ref,
                  eidx_ref, rank_ref, wts_ref, cnt_ref, h2p_ref, carry_ref, *, tr):
    i = pl.program_id(0)

    @pl.when(i == 0)
    def _():
        carry_ref[...] = jnp.zeros_like(carry_ref)

    h2 = x_ref[...] * (1.0 + sc_ref[0]) + sh_ref[0]
    h2p_ref[...] = _pack_bf16_halves(h2)
    hi = h2.astype(BF16)
    lo = (h2 - hi.astype(F32)).astype(BF16)
    nt = (((1,), (1,)), ((), ()))
    logits = (lax.dot_general(whi_ref[...], hi, nt, preferred_element_type=F32)
              + lax.dot_general(wlo_ref[...], hi, nt, preferred_element_type=F32)
              + lax.dot_general(whi_ref[...], lo, nt, preferred_element_type=F32))
    scores = _sigmoid(logits)
    sel = scores + rb_ref[...]
    erow = lax.broadcasted_iota(jnp.int32, (N_EXPERTS, tr), 0).astype(F32)

    g3 = sel.reshape(N_GROUPS, GROUP_SIZE, tr)
    r3 = lax.broadcasted_iota(jnp.int32, (N_GROUPS, GROUP_SIZE, tr), 1).astype(F32)
    m1 = jnp.max(g3, axis=1, keepdims=True)
    first = jnp.min(jnp.where(g3 == m1, r3, float(GROUP_SIZE)), axis=1, keepdims=True)
    m2 = jnp.max(jnp.where(r3 == first, -jnp.inf, g3), axis=1, keepdims=True)
    gscore = (m1 + m2).reshape(N_GROUPS, tr)

    gidx = lax.broadcasted_iota(jnp.int32, (N_GROUPS, tr), 0)
    beaten_by = jnp.zeros((N_GROUPS, tr), jnp.int32)
    for g in range(N_GROUPS):
        other = gscore[g:g + 1, :]
        wins = (other > gscore) | ((other == gscore) & (g < gidx))
        beaten_by = beaten_by + wins.astype(jnp.int32)
    dropped = jnp.where(beaten_by < TOP_K_GROUPS, 0.0, -jnp.inf)
    cur = (g3 + dropped.reshape(N_GROUPS, 1, tr)).reshape(N_EXPERTS, tr)

    picks, weights = [], []
    selmask = jnp.zeros((N_EXPERTS, tr), F32)
    for _ in range(TOP_K):
        mx = jnp.max(cur, axis=0, keepdims=True)
        pick = jnp.min(jnp.where(cur == mx, erow, float(N_EXPERTS)), axis=0, keepdims=True)
        onehot = erow == pick
        weights.append(jnp.sum(jnp.where(onehot, scores, 0.0), axis=0, keepdims=True))
        cur = jnp.where(onehot, -jnp.inf, cur)
        selmask = jnp.where(onehot, 1.0, selmask)
        picks.append(pick)

    t_from = lax.broadcasted_iota(jnp.int32, (tr, tr), 0)
    t_to = lax.broadcasted_iota(jnp.int32, (tr, tr), 1)
    earlier = jnp.where(t_from < t_to, 1.0, 0.0).astype(BF16)
    chosen = selmask.astype(BF16)
    carry = carry_ref[...]
    rankmat = (jnp.dot(chosen, earlier, preferred_element_type=F32)
               + jnp.concatenate([carry] * (tr // LANES), axis=1))
    carry_ref[...] = carry + jnp.dot(chosen, jnp.ones((tr, LANES), BF16), preferred_element_type=F32)
    cnt_ref[...] = carry_ref[...]

    wsum = weights[0]
    for wj in weights[1:]:
        wsum = wsum + wj
    row8 = lax.broadcasted_iota(jnp.int32, (TOP_K, tr), 0)
    eidx = jnp.zeros((TOP_K, tr), jnp.int32)
    rank = jnp.zeros((TOP_K, tr), jnp.int32)
    wts = jnp.zeros((TOP_K, tr), F32)
    for j in range(TOP_K):
        rk = jnp.sum(jnp.where(erow == picks[j], rankmat, 0.0), axis=0, keepdims=True)
        eidx = jnp.where(row8 == j, picks[j].astype(jnp.int32), eidx)
        rank = jnp.where(row8 == j, rk.astype(jnp.int32), rank)
        wts = jnp.where(row8 == j, weights[j] / wsum * ROUTED_SCALE, wts)
    eidx_ref[...] = eidx
    rank_ref[...] = rank
    wts_ref[...] = wts


def _route(x1, scale2, shift2, wr_hi, wr_lo, router_bias, *, seq, tr):
    n, d = x1.shape
    tps = seq // tr
    mod_spec = pl.BlockSpec((1, 1, d), lambda i: (i // tps, 0, 0))
    full = lambda a: pl.BlockSpec(a.shape, lambda i: (0,) * a.ndim)
    assert tr % LANES == 0
    k8 = pl.BlockSpec((TOP_K, tr), lambda i: (0, i))
    return pl.pallas_call(
        functools.partial(_route_kernel, tr=tr),
        grid=(n // tr,),
        in_specs=[pl.BlockSpec((tr, d), lambda i: (i, 0)), mod_spec, mod_spec,
                  full(wr_hi), full(wr_lo), full(router_bias)],
        out_specs=[k8, k8, k8, pl.BlockSpec((N_EXPERTS, LANES), lambda i: (0, 0)),
                   pl.BlockSpec((tr, d // 2), lambda i: (i, 0))],
        out_shape=[jax.ShapeDtypeStruct((TOP_K, n), jnp.int32),
                   jax.ShapeDtypeStruct((TOP_K, n), jnp.int32),
                   jax.ShapeDtypeStruct((TOP_K, n), F32),
                   jax.ShapeDtypeStruct((N_EXPERTS, LANES), F32),
                   jax.ShapeDtypeStruct((n, d // 2), U32)],
        scratch_shapes=[pltpu.VMEM((N_EXPERTS, LANES), F32)],
        compiler_params=_params(("arbitrary",)),
    )(x1, scale2, shift2, wr_hi, wr_lo, router_bias)


def _pack_bf16_halves(x):
    w = x.shape[1] // 2
    lo = pltpu.bitcast(x[:, :w].astype(BF16).astype(F32), U32) >> 16
    hi = pltpu.bitcast(x[:, w:].astype(BF16).astype(F32), U32) & jnp.uint32(0xFFFF0000)
    return lo | hi


def _unpack_bf16_halves(p):
    return pltpu.bitcast(p << 16, F32), pltpu.bitcast(p & jnp.uint32(0xFFFF0000), F32)


def _sc_worker():
    return lax.axis_index("subcore") * SC_CORES + lax.axis_index("core")


def _scatter_rows_sc(rows, idx, n_slots):
    n, width = rows.shape
    k = idx.shape[0] // n
    n_workers = SC_CORES * SC_SUBCORES
    w = SC_ROWS
    per_worker = n // n_workers
    n_chunks = per_worker // w
    assert n == n_workers * n_chunks * w and n_chunks % 2 == 0 and idx.shape[0] == k * n
    mesh = plsc.VectorSubcoreMesh(core_axis_name="core", subcore_axis_name="subcore")

    @functools.partial(
        pl.kernel, mesh=mesh,
        out_type=jax.ShapeDtypeStruct((n_slots, width), rows.dtype),
        scratch_types=[pltpu.VMEM((k * per_worker,), jnp.int32), pltpu.VMEM((2, w, width), rows.dtype),
                       pltpu.SemaphoreType.DMA((2,)), pltpu.SemaphoreType.DMA((2,))])
    def scatter(rows_hbm, idx_hbm, out_hbm, idx_v, rows_v, in_sem, out_sem):
        t0 = _sc_worker() * per_worker
        for j in range(k):
            pltpu.sync_copy(idx_hbm.at[pl.ds(j * n + t0, per_worker)], idx_v.at[pl.ds(j * per_worker, per_worker)])

        def load(i, s):
            return pltpu.make_async_copy(rows_hbm.at[pl.ds(t0 + i * w, w)], rows_v.at[s], in_sem.at[s])

        def send(i, s, j):
            slots = idx_v.at[pl.ds(j * per_worker + i * w, w)]
            return pltpu.make_async_copy(rows_v.at[s], out_hbm.at[slots], out_sem.at[s])

        load(0, 0).start()

        def pair(ii, carry):
            for s in range(2):
                i = ii * 2 + s
                load(i, s).wait()

                @pl.when(i >= 1)
                def _():
                    for j in range(k):
                        send(i - 1, 1 - s, j).wait()

                @pl.when(i + 1 < n_chunks)
                def _():
                    load(i + 1, 1 - s).start()

                for j in range(k):
                    send(i, s, j).start()
            return carry

        lax.fori_loop(0, n_chunks // 2, pair, 0)
        for j in range(k):
            send(n_chunks - 1, 1, j).wait()

    return scatter(rows, idx)


def _dest_kernel(eidx_ref, rank_ref, first_ref, dest_ref):
    tt = eidx_ref.shape[1]
    pieces = [jnp.broadcast_to(first_ref[:, p * LANES:(p + 1) * LANES], (TOP_K, LANES))
              for p in range(N_EXPERTS // LANES)]
    for c in range(tt // LANES):
        cols = slice(c * LANES, (c + 1) * LANES)
        e = eidx_ref[:, cols]
        within = e & (LANES - 1)
        start = jnp.take_along_axis(pieces[0], within, axis=1)
        for p in range(1, len(pieces)):
            start = jnp.where(e // LANES == p, jnp.take_along_axis(pieces[p], within, axis=1), start)
        dest_ref[:, cols] = start + rank_ref[:, cols]


def _dest(eidx_t, rank_t, pstarts, *, tt):
    n = eidx_t.shape[1]
    first = pstarts.reshape(1, N_EXPERTS)
    k8 = pl.BlockSpec((TOP_K, tt), lambda i: (0, i))
    return pl.pallas_call(
        _dest_kernel,
        grid=(n // tt,),
        in_specs=[k8, k8, pl.BlockSpec(first.shape, lambda i: (0, 0))],
        out_specs=k8,
        out_shape=jax.ShapeDtypeStruct((TOP_K, n), jnp.int32),
        compiler_params=_params(("arbitrary",)),
    )(eidx_t, rank_t, first)


def _expert_kernel(fb_ref, nb_ref, cnt_ref, nused_ref, wg_ref, wu_ref, wd_ref, xs_ref, ys_ref,
                   wgb, wub, wdb, xbuf, ybuf, in_sem, out_sem):
    e = pl.program_id(0)
    m = DISPATCH_BLOCK
    ns = EXPERT_SLOTS
    n_used = nused_ref[0]

    def rows(g):
        return pl.ds(pl.multiple_of(g * m, m), m)

    def fetch(g):
        slot = g & (ns - 1)
        return pltpu.make_async_copy(xs_ref.at[rows(g), :], xbuf.at[slot], in_sem.at[slot])

    def put(g):
        slot = g & (ns - 1)
        return pltpu.make_async_copy(ybuf.at[slot], ys_ref.at[rows(g), :], out_sem.at[slot])

    @pl.when(e == 0)
    def _():
        for g0 in range(ns - 1):
            @pl.when(g0 < n_used)
            def _(g0=g0):
                fetch(g0).start()

    wgb[...] = wg_ref[0].astype(BF16)
    wub[...] = wu_ref[0].astype(BF16)
    wdb[...] = wd_ref[0].astype(BF16)

    def block(i, carry):
        g = fb_ref[e] + i
        slot = g & (ns - 1)
        fetch(g).wait()

        @pl.when(g + ns - 1 < n_used)
        def _():
            fetch(g + ns - 1).start()

        @pl.when(g >= ns)
        def _():
            put(g - ns).wait()

        row = lax.broadcasted_iota(jnp.int32, (m, 1), 0)
        packed = jnp.where(row < cnt_ref[e] - i * m, xbuf[slot], jnp.uint32(0))
        x_lo, x_hi = [h.astype(BF16) for h in _unpack_bf16_halves(packed)]
        half = x_lo.shape[1]

        def up_proj(w):
            return (jnp.dot(x_lo, w[0:half, :], preferred_element_type=F32)
                    + jnp.dot(x_hi, w[half:, :], preferred_element_type=F32))

        a = (_silu(up_proj(wgb)) * up_proj(wub)).astype(BF16)
        ybuf[slot] = _pack_bf16_halves(jnp.dot(a, wdb[...], preferred_element_type=F32))
        put(g).start()
        return carry

    lax.fori_loop(0, nb_ref[e], block, 0)

    @pl.when(e == pl.num_programs(0) - 1)
    def _():
        for back in range(ns, 0, -1):
            @pl.when(n_used >= back)
            def _(back=back):
                put(n_used - back).wait()

        ybuf[0] = jnp.zeros(ybuf.shape[1:], ybuf.dtype)
        n_blocks = ys_ref.shape[0] // m

        def tail(wait):
            def body(g, carry):
                cp = pltpu.make_async_copy(ybuf.at[0], ys_ref.at[rows(g), :], out_sem.at[0])
                cp.wait() if wait else cp.start()
                return carry
            return body

        lax.fori_loop(n_used, n_blocks, tail(False), 0)
        lax.fori_loop(n_used, n_blocks, tail(True), 0)


def _experts(first_block, n_blocks_e, counts, n_used, xs, w_gate, w_up, w_down):
    p, dp = xs.shape
    m = DISPATCH_BLOCK
    n_e, d, f = w_gate.shape
    assert d == 2 * dp and p % m == 0
    grid_spec = pltpu.PrefetchScalarGridSpec(
        num_scalar_prefetch=4,
        grid=(n_e,),
        in_specs=[pl.BlockSpec((1, d, f), lambda e, *_: (e, 0, 0)),
                  pl.BlockSpec((1, d, f), lambda e, *_: (e, 0, 0)),
                  pl.BlockSpec((1, f, d), lambda e, *_: (e, 0, 0)),
                  pl.BlockSpec(memory_space=pl.ANY)],
        out_specs=pl.BlockSpec(memory_space=pl.ANY),
        scratch_shapes=[pltpu.VMEM((d, f), BF16), pltpu.VMEM((d, f), BF16), pltpu.VMEM((f, d), BF16),
                        pltpu.VMEM((EXPERT_SLOTS, m, dp), U32), pltpu.VMEM((EXPERT_SLOTS, m, dp), U32),
                        pltpu.SemaphoreType.DMA((EXPERT_SLOTS,)), pltpu.SemaphoreType.DMA((EXPERT_SLOTS,))],
    )
    return pl.pallas_call(
        _expert_kernel,
        grid_spec=grid_spec,
        out_shape=jax.ShapeDtypeStruct((p, dp), U32),
        compiler_params=_params(("arbitrary",)),
    )(first_block, n_blocks_e, counts, n_used, w_gate, w_up, w_down, xs)


def _gather_rows_sc(table, idx):
    n_idx, (_, width) = idx.shape[0], table.shape
    n_workers = SC_CORES * SC_SUBCORES
    w = SC_ROWS
    per_worker = n_idx // n_workers
    n_chunks = per_worker // w
    assert n_idx == n_workers * n_chunks * w and n_chunks % 2 == 0
    mesh = plsc.VectorSubcoreMesh(core_axis_name="core", subcore_axis_name="subcore")

    @functools.partial(
        pl.kernel, mesh=mesh,
        out_type=jax.ShapeDtypeStruct((n_idx, width), table.dtype),
        scratch_types=[pltpu.VMEM((per_worker,), jnp.int32), pltpu.VMEM((2, w, width), table.dtype),
                       pltpu.SemaphoreType.DMA((2,)), pltpu.SemaphoreType.DMA((2,))])
    def gather(table_hbm, idx_hbm, out_hbm, idx_v, rows_v, in_sem, out_sem):
        base = _sc_worker() * per_worker
        pltpu.sync_copy(idx_hbm.at[pl.ds(base, per_worker)], idx_v)

        def fetch(i, s):
            return pltpu.make_async_copy(table_hbm.at[idx_v.at[pl.ds(i * w, w)]], rows_v.at[s], in_sem.at[s])

        def put(i, s):
            return pltpu.make_async_copy(rows_v.at[s], out_hbm.at[pl.ds(base + i * w, w)], out_sem.at[s])

        fetch(0, 0).start()

        def pair(ii, carry):
            for s in range(2):
                i = ii * 2 + s
                fetch(i, s).wait()

                @pl.when(i >= 1)
                def _():
                    put(i - 1, 1 - s).wait()

                @pl.when(i + 1 < n_chunks)
                def _():
                    fetch(i + 1, 1 - s).start()

                put(i, s).start()
            return carry

        lax.fori_loop(0, n_chunks // 2, pair, 0)
        put(n_chunks - 1, 1).wait()

    return gather(table, idx)


def _combine_kernel(x_ref, sc_ref, sh_ref, gate_ref, wts_ref, rows_ref,
                    wsg_ref, wsu_ref, wsd_ref, g_ref, b_ref, *out_refs):
    o_ref = out_refs[-1]
    x = x_ref[...]
    hb = (x * (1.0 + sc_ref[0]) + sh_ref[0]).astype(BF16)
    sg = jnp.dot(hb, wsg_ref[...], preferred_element_type=F32)
    su = jnp.dot(hb, wsu_ref[...], preferred_element_type=F32)
    shared = jnp.dot((_silu(sg) * su).astype(BF16), wsd_ref[...], preferred_element_type=F32)

    wts = wts_ref[...]
    half = shared.shape[1] // 2
    lo, hi = shared[:, :half], shared[:, half:]
    for j in range(TOP_K):
        y_lo, y_hi = _unpack_bf16_halves(rows_ref[j])
        lo = lo + wts[:, j:j + 1] * y_lo
        hi = hi + wts[:, j:j + 1] * y_hi
    z = ALPHA * x + gate_ref[0] * jnp.concatenate([lo, hi], axis=1)
    o_ref[...] = _layer_norm(z, g_ref[...], b_ref[...])


def _combine(x1, scale2, shift2, gate2, wts, rows, ws_gate, ws_up, ws_down, ln_g, ln_b, prev_out,
             *, seq, tc, first_token):
    n, d = x1.shape
    tps = seq // tc
    off = first_token // tc
    n_tiles = rows.shape[1] // tc
    mod_spec = pl.BlockSpec((1, 1, d), lambda i: ((i + off) // tps, 0, 0))
    full = lambda a: pl.BlockSpec(a.shape, lambda i: (0,) * a.ndim)
    args = [x1, scale2, shift2, gate2, wts, rows, ws_gate, ws_up, ws_down, ln_g, ln_b]
    in_specs = [pl.BlockSpec((tc, d), lambda i: (i + off, 0)), mod_spec, mod_spec, mod_spec,
                pl.BlockSpec((tc, TOP_K), lambda i: (i + off, 0)),
                pl.BlockSpec((TOP_K, tc, d // 2), lambda i: (0, i, 0)),
                full(ws_gate), full(ws_up), full(ws_down), full(ln_g), full(ln_b)]
    aliases = {}
    if prev_out is not None:
        aliases = {len(args): 0}
        args.append(prev_out)
        in_specs.append(pl.BlockSpec(memory_space=pl.ANY))
    return pl.pallas_call(
        _combine_kernel,
        grid=(n_tiles,),
        in_specs=in_specs,
        out_specs=pl.BlockSpec((tc, d), lambda i: (i + off, 0)),
        out_shape=jax.ShapeDtypeStruct((n, d), F32),
        input_output_aliases=aliases,
        compiler_params=_params(("arbitrary",)),
    )(*args)


def _layer(x, c, w_ada, b_ada, w_in, pool_w, pool_scale, lq1, lk1, lq2, lk2, subln_g, w_out,
           ln1_g, ln1_b, w_router, router_bias, w_gate, w_up, w_down, ws_gate, ws_up, ws_down,
           ln2_g, ln2_b, rel_table, *, tm=512, tr=256, tc=256):
    bsz, seq, d = x.shape
    n = bsz * seq
    x2 = x.reshape(n, d)
    row = lambda a: a.reshape(1, -1)

    mod = _modulation(c, w_ada, b_ada)
    shift1, scale1, gate1, shift2, scale2, gate2 = [
        mod[:, j * d:(j + 1) * d].reshape(bsz, 1, d) for j in range(6)]

    n_main = POOL_DIM + 2 * QK_DIM
    yp, q, k, vt = _inproj(x2, scale1, shift1, w_in[:, :n_main].astype(BF16), w_in[:, n_main:].T.astype(BF16),
                           pool_w.astype(BF16), row(pool_scale), seq=seq, tm=tm)
    bias_tiles = _bias_tiles(rel_table, seq // ATT_TILE)
    ya = _attention(q, k, vt, bias_tiles, row(lq1), row(lk1), row(lq2), row(lk2), subln_g.reshape(-1, 1),
                    bsz=bsz, seq=seq)
    x1 = _outproj(x2, yp, ya, w_out.astype(BF16), gate1, row(ln1_g), row(ln1_b), seq=seq, tm=tm)

    wr_t = w_router.T
    wr_hi = wr_t.astype(BF16)
    wr_lo = (wr_t - wr_hi.astype(F32)).astype(BF16)
    eidx_t, rank_t, wts_t, cnt, h2p = _route(x1, scale2, shift2, wr_hi, wr_lo, router_bias.reshape(-1, 1),
                                             seq=seq, tr=tr)

    m = DISPATCH_BLOCK
    counts = cnt[:, 0].astype(jnp.int32)
    padded = (counts + m - 1) // m * m
    pends = jnp.cumsum(padded)
    pstarts = pends - padded
    n_blocks = -(-(n * TOP_K + N_EXPERTS * (m - 1)) // m)
    n_used = (pends[-1:] // m).astype(jnp.int32)
    dest_t = _dest(eidx_t, rank_t, pstarts, tt=min(n, 2048))

    xs = _scatter_rows_sc(h2p, dest_t.reshape(TOP_K * n), n_blocks * m)
    ys = _experts(pstarts // m, padded // m, counts, n_used, xs, w_gate, w_up, w_down)

    shared_w = (ws_gate.astype(BF16), ws_up.astype(BF16), ws_down.astype(BF16))
    wts = wts_t.T
    part = n // COMBINE_PARTS
    out = None
    for p in range(COMBINE_PARTS):
        idx = dest_t[:, p * part:(p + 1) * part].reshape(TOP_K * part)
        picked = _gather_rows_sc(ys, idx).reshape(TOP_K, part, d // 2)
        out = _combine(x1, scale2, shift2, gate2, wts, picked, *shared_w, row(ln2_g), row(ln2_b), out,
                       seq=seq, tc=tc, first_token=p * part)
    return out.reshape(bsz, seq, d)


def kernel(x, c, w_ada, b_ada, w_in, pool_w, pool_scale, lambda_q1, lambda_k1, lambda_q2, lambda_k2,
           subln_g, w_out, ln1_g, ln1_b, w_router, router_bias, w_gate, w_up, w_down,
           ws_gate, ws_up, ws_down, ln2_g, ln2_b, rel_table):
    per_layer = (w_ada, b_ada, w_in, pool_w, pool_scale, lambda_q1, lambda_k1, lambda_q2, lambda_k2,
                 subln_g, w_out, ln1_g, ln1_b, w_router, router_bias, w_gate, w_up, w_down,
                 ws_gate, ws_up, ws_down, ln2_g, ln2_b)
    assert all(a.shape[0] == DEPTH == 1 for a in per_layer)
    return _layer(x, c, *[a.reshape(a.shape[1:]) for a in per_layer], rel_table)
```

```python
import functools
import math

import jax
import jax.numpy as jnp
from jax import lax
from jax.experimental import pallas as pl
from jax.experimental.pallas import tpu as pltpu
from jax.experimental.pallas import tpu_sc as plsc

F32 = jnp.float32
BF16 = jnp.bfloat16
U32 = jnp.uint32
LANES = 128

D_MODEL = 1024
CHUNK = 64
Q_BLOCK = 128
ATT_TILE = 256
POOL_DIM = 512
POOL_WINDOWS = (2, 4, 8, 16)
POOL_GROUP_DIM = 128
MAX_WINDOW = max(POOL_WINDOWS)
ATTN_HEADS = 4
ATTN_HEAD_DIM = 64
QK_DIM = 512
V_DIM = 512
IN_DIM = 2048
NUM_BUCKETS = 32
MAX_DISTANCE = 128
N_EXPERTS = 256
TOP_K = 8
N_GROUPS = 8
GROUP_SIZE = N_EXPERTS // N_GROUPS
TOP_K_GROUPS = 4
EXPERT_DIM = 256
ROUTED_SCALE = 2.5
DISPATCH_BLOCK = 512
EXPERT_SLOTS = 4
COMBINE_PARTS = 4
SC_CORES, SC_SUBCORES = 2, 16
SC_ROWS = 64
DEPTH = 1
ALPHA = (2.0 * DEPTH) ** 0.25
LN_EPS = 1e-5
LAMBDA_INIT = 0.8 - 0.6 * math.exp(-0.3 * 0)

VMEM_LIMIT = 48 * 1024 * 1024


def _sigmoid(x):
    return 1.0 / (1.0 + jnp.exp(-x))


def _silu(x):
    return x * _sigmoid(x)


def _layer_norm(z, g, b):
    mu = jnp.mean(z, axis=-1, keepdims=True)
    zc = z - mu
    var = jnp.mean(zc * zc, axis=-1, keepdims=True)
    return zc * lax.rsqrt(var + LN_EPS) * g + b


def _params(sem=None):
    return pltpu.CompilerParams(dimension_semantics=sem, vmem_limit_bytes=VMEM_LIMIT)


def _mod_kernel(c_ref, w_ref, b_ref, o_ref):
    ca = _silu(c_ref[...])
    o_ref[...] = jnp.dot(ca, w_ref[...], preferred_element_type=F32,
                         precision=lax.Precision.HIGHEST) + b_ref[...]


def _modulation(c, w_ada, b_ada):
    bsz, d = c.shape
    n_out = w_ada.shape[1]
    return pl.pallas_call(
        _mod_kernel,
        grid=(n_out // d,),
        in_specs=[pl.BlockSpec((bsz, d), lambda j: (0, 0)),
                  pl.BlockSpec((d, d), lambda j: (0, j)),
                  pl.BlockSpec((1, d), lambda j: (0, j))],
        out_specs=pl.BlockSpec((bsz, d), lambda j: (0, j)),
        out_shape=jax.ShapeDtypeStruct((bsz, n_out), F32),
        compiler_params=_params(("arbitrary",)),
    )(c, w_ada, b_ada.reshape(1, n_out))


def _inproj_kernel(x_ref, sc_ref, sh_ref, w_ref, wvt_ref, pw_ref, ps_ref,
                   yp_ref, q_ref, k_ref, vt_ref, ext_ref, *, tm, seq):
    i = pl.program_id(0)
    tiles_per_seq = seq // tm
    it = i % tiles_per_seq
    h = x_ref[...] * (1.0 + sc_ref[0]) + sh_ref[0]
    hb = h.astype(BF16)
    proj = jnp.dot(hb, w_ref[...], preferred_element_type=F32)
    u = proj[:, :POOL_DIM]
    q_ref[...] = (proj[:, POOL_DIM:POOL_DIM + QK_DIM] * (ATTN_HEAD_DIM ** -0.5)).astype(BF16)
    k_ref[...] = proj[:, POOL_DIM + QK_DIM:POOL_DIM + 2 * QK_DIM].astype(BF16)
    vt = lax.dot_general(wvt_ref[...], hb, (((1,), (1,)), ((), ())), preferred_element_type=F32)
    for j in range(tm // ATT_TILE):
        vt_ref[0, j] = vt[:, j * ATT_TILE:(j + 1) * ATT_TILE].astype(BF16)

    @pl.when(it == 0)
    def _():
        ext_ref[0:MAX_WINDOW, :] = jnp.zeros((MAX_WINDOW, POOL_DIM), F32)

    ext_ref[MAX_WINDOW:MAX_WINDOW + tm, :] = u
    pos = (it * tm + lax.broadcasted_iota(jnp.int32, (tm, 1), 0) + 1).astype(F32)
    for g, w in enumerate(POOL_WINDOWS):
        c0, c1 = g * POOL_GROUP_DIM, (g + 1) * POOL_GROUP_DIM
        s = ext_ref[MAX_WINDOW:MAX_WINDOW + tm, c0:c1]
        for j in range(1, w):
            s = s + ext_ref[MAX_WINDOW - j:MAX_WINDOW - j + tm, c0:c1]
        pooled = s / jnp.minimum(pos, float(w)) - u[:, c0:c1]
        y = jnp.dot(pooled.astype(BF16), pw_ref[g], preferred_element_type=F32)
        yp_ref[:, c0:c1] = (y * ps_ref[:, c0:c1]).astype(BF16)
    ext_ref[0:MAX_WINDOW, :] = ext_ref[tm:tm + MAX_WINDOW, :]


def _inproj(x2, scale1, shift1, w_main, w_vt, pool_w, pool_scale, *, seq, tm):
    n, d = x2.shape
    assert n % tm == 0 and seq % tm == 0 and tm >= 2 * MAX_WINDOW and tm % ATT_TILE == 0
    tps = seq // tm
    tpt = tm // ATT_TILE
    mod_spec = pl.BlockSpec((1, 1, d), lambda i: (i // tps, 0, 0))
    row = lambda w: pl.BlockSpec((tm, w), lambda i: (i, 0))
    full = lambda a: pl.BlockSpec(a.shape, lambda i: (0,) * a.ndim)
    return pl.pallas_call(
        functools.partial(_inproj_kernel, tm=tm, seq=seq),
        grid=(n // tm,),
        in_specs=[row(d), mod_spec, mod_spec, full(w_main), full(w_vt), full(pool_w), full(pool_scale)],
        out_specs=[row(POOL_DIM), row(QK_DIM), row(QK_DIM),
                   pl.BlockSpec((1, tpt, V_DIM, ATT_TILE), lambda i: (i // tps, i % tps, 0, 0))],
        out_shape=[jax.ShapeDtypeStruct((n, POOL_DIM), BF16),
                   jax.ShapeDtypeStruct((n, QK_DIM), BF16),
                   jax.ShapeDtypeStruct((n, QK_DIM), BF16),
                   jax.ShapeDtypeStruct((n // seq, seq // ATT_TILE, V_DIM, ATT_TILE), BF16)],
        scratch_shapes=[pltpu.VMEM((tm + MAX_WINDOW, POOL_DIM), F32)],
        compiler_params=_params(("arbitrary",)),
    )(x2, scale1, shift1, w_main, w_vt, pool_w, pool_scale)


def _bias_kernel(tab_ref, o_ref):
    delta = pl.program_id(0)
    r = lax.broadcasted_iota(jnp.int32, (ATT_TILE, ATT_TILE), 0)
    c = lax.broadcasted_iota(jnp.int32, (ATT_TILE, ATT_TILE), 1)
    rel = r - c - delta * ATT_TILE
    half = NUM_BUCKETS // 2
    max_exact = half // 2
    ret = jnp.where(rel > 0, half, 0)
    n = jnp.abs(rel)
    nf = jnp.maximum(n, 1).astype(F32)
    large = max_exact + (jnp.log(nf / max_exact) / math.log(MAX_DISTANCE / max_exact)
                         * (half - max_exact)).astype(jnp.int32)
    large = jnp.minimum(large, half - 1)
    bucket = ret + jnp.where(n < max_exact, n, large)
    for h in range(ATTN_HEADS):
        acc = jnp.zeros((ATT_TILE, ATT_TILE), F32)
        for b in range(NUM_BUCKETS):
            acc = jnp.where(bucket == b, tab_ref[b, h], acc)
        o_ref[h, 0] = acc


def _bias_tiles(rel_table, n_tiles):
    return pl.pallas_call(
        _bias_kernel,
        grid=(n_tiles,),
        in_specs=[pl.BlockSpec(memory_space=pltpu.SMEM)],
        out_specs=pl.BlockSpec((ATTN_HEADS, 1, ATT_TILE, ATT_TILE), lambda dlt: (0, dlt, 0, 0)),
        out_shape=jax.ShapeDtypeStruct((ATTN_HEADS, n_tiles, ATT_TILE, ATT_TILE), F32),
        compiler_params=_params(("arbitrary",)),
    )(rel_table)


def _attn_kernel(q_ref, k_ref, vt_ref, bias_ref, lq1_ref, lk1_ref, lq2_ref, lk2_ref, g_ref, o_ref, *acc_refs):
    qt = pl.program_id(1)
    t = ATT_TILE
    n_maps = 2 * ATTN_HEADS
    lam = (jnp.exp(jnp.sum(lq1_ref[...] * lk1_ref[...], axis=-1, keepdims=True))
           - jnp.exp(jnp.sum(lq2_ref[...] * lk2_ref[...], axis=-1, keepdims=True))
           + LAMBDA_INIT)
    r = lax.broadcasted_iota(jnp.int32, (t, t), 0)
    c = lax.broadcasted_iota(jnp.int32, (t, t), 1)
    allowed = (r // CHUNK) <= (c // CHUNK)
    hd2 = 2 * ATTN_HEAD_DIM

    def block(kt, carry, diagonal):
        koff = pl.multiple_of(kt * t, t)

        def scores(hm):
            col = hm * ATTN_HEAD_DIM
            qh = q_ref[:, col:col + ATTN_HEAD_DIM]
            kh = k_ref[pl.ds(koff, t), col:col + ATTN_HEAD_DIM]
            s = lax.dot_general(kh, qh, (((1,), (1,)), ((), ())),
                                preferred_element_type=F32) + bias_ref[hm // 2, qt - kt]
            return jnp.where(allowed, s, -jnp.inf) if diagonal else s

        def softmax(hm, s):
            m_old, l_old = carry[2 * hm:2 * hm + 2]
            m_new = jnp.maximum(m_old, jnp.max(s, axis=0, keepdims=True))
            alpha = jnp.exp(m_old - m_new)
            p = jnp.exp(s - m_new)
            return m_new, alpha * l_old + jnp.sum(p, axis=0, keepdims=True), alpha, p.astype(BF16)

        def accumulate(hm, alpha, p):
            h = hm // 2
            vth = vt_ref[0, kt, h * hd2:(h + 1) * hd2, :]
            acc_refs[hm][...] = alpha * acc_refs[hm][...] + jnp.dot(vth, p, preferred_element_type=F32)

        s_vals, sm_vals, out = {}, {}, [None] * (2 * n_maps)
        for step in range(n_maps + 2):
            if step < n_maps:
                s_vals[step] = scores(step)
            if 0 <= step - 1 < n_maps:
                hm = step - 1
                m_new, l_new, alpha, p = softmax(hm, s_vals.pop(hm))
                out[2 * hm], out[2 * hm + 1] = m_new, l_new
                sm_vals[hm] = (alpha, p)
            if 0 <= step - 2 < n_maps:
                accumulate(step - 2, *sm_vals.pop(step - 2))
        return tuple(out)

    for acc in acc_refs:
        acc[...] = jnp.zeros_like(acc)
    one = (jnp.full((1, t), -jnp.inf, F32), jnp.zeros((1, t), F32))
    carry = lax.fori_loop(0, qt, lambda kt, cr: block(kt, cr, False), one * n_maps)
    carry = block(qt, carry, True)
    for h in range(ATTN_HEADS):
        l0, l1 = carry[4 * h + 1], carry[4 * h + 3]
        o = acc_refs[2 * h][...] / l0 - lam * (acc_refs[2 * h + 1][...] / l1)
        y = o * lax.rsqrt(jnp.mean(o * o, axis=0, keepdims=True) + LN_EPS) * g_ref[...]
        o_ref[:, h * hd2:(h + 1) * hd2] = (y * (1.0 - LAMBDA_INIT)).T.astype(BF16)


def _attention(q, k, vt, bias_tiles, lq1, lk1, lq2, lk2, subln_g, *, bsz, seq):
    t = ATT_TILE
    nt = seq // t
    full = lambda a: pl.BlockSpec(a.shape, lambda b, j: (0,) * a.ndim)
    return pl.pallas_call(
        _attn_kernel,
        grid=(bsz, nt),
        in_specs=[pl.BlockSpec((t, QK_DIM), lambda b, j: (b * nt + j, 0)),
                  pl.BlockSpec((seq, QK_DIM), lambda b, j: (b, 0)),
                  pl.BlockSpec((1, nt, V_DIM, t), lambda b, j: (b, 0, 0, 0)),
                  full(bias_tiles), full(lq1), full(lk1), full(lq2), full(lk2), full(subln_g)],
        out_specs=pl.BlockSpec((t, V_DIM), lambda b, j: (b * nt + j, 0)),
        out_shape=jax.ShapeDtypeStruct((bsz * seq, V_DIM), BF16),
        scratch_shapes=[pltpu.VMEM((2 * ATTN_HEAD_DIM, t), F32) for _ in range(2 * ATTN_HEADS)],
        compiler_params=_params(("arbitrary", "arbitrary")),
    )(q, k, vt, bias_tiles, lq1, lk1, lq2, lk2, subln_g)


def _outproj_route_kernel(x_ref, yp_ref, ya_ref, w_ref, gate_ref, g_ref, b_ref,
                          sc_ref, sh_ref, whi_ref, wlo_ref, rb_ref,
                          x1_ref, eidx_ref, rank_ref, wts_ref, cnt_ref, h2p_ref, carry_ref, *, tr):
    mix = (jnp.dot(yp_ref[...], w_ref[0:POOL_DIM, :], preferred_element_type=F32)
           + jnp.dot(ya_ref[...], w_ref[POOL_DIM:, :], preferred_element_type=F32))
    x1 = _layer_norm(ALPHA * x_ref[...] + gate_ref[0] * mix, g_ref[...], b_ref[...])
    x1_ref[...] = x1
    _route_tile(x1, sc_ref, sh_ref, whi_ref, wlo_ref, rb_ref,
                eidx_ref, rank_ref, wts_ref, cnt_ref, h2p_ref, carry_ref, tr=tr)


def _route_tile(x1, sc_ref, sh_ref, whi_ref, wlo_ref, rb_ref,
                eidx_ref, rank_ref, wts_ref, cnt_ref, h2p_ref, carry_ref, *, tr):
    i = pl.program_id(0)

    @pl.when(i == 0)
    def _():
        carry_ref[...] = jnp.zeros_like(carry_ref)

    h2 = x1 * (1.0 + sc_ref[0]) + sh_ref[0]
    h2p_ref[...] = _pack_bf16_halves(h2)
    hi = h2.astype(BF16)
    lo = (h2 - hi.astype(F32)).astype(BF16)
    nt = (((1,), (1,)), ((), ()))
    logits = (lax.dot_general(whi_ref[...], hi, nt, preferred_element_type=F32)
              + lax.dot_general(wlo_ref[...], hi, nt, preferred_element_type=F32)
              + lax.dot_general(whi_ref[...], lo, nt, preferred_element_type=F32))
    scores = _sigmoid(logits)
    sel = scores + rb_ref[...]
    erow = lax.broadcasted_iota(jnp.int32, (N_EXPERTS, tr), 0).astype(F32)

    g3 = sel.reshape(N_GROUPS, GROUP_SIZE, tr)
    r3 = lax.broadcasted_iota(jnp.int32, (N_GROUPS, GROUP_SIZE, tr), 1).astype(F32)
    m1 = jnp.max(g3, axis=1, keepdims=True)
    first = jnp.min(jnp.where(g3 == m1, r3, float(GROUP_SIZE)), axis=1, keepdims=True)
    m2 = jnp.max(jnp.where(r3 == first, -jnp.inf, g3), axis=1, keepdims=True)
    gscore = (m1 + m2).reshape(N_GROUPS, tr)

    gidx = lax.broadcasted_iota(jnp.int32, (N_GROUPS, tr), 0)
    beaten_by = jnp.zeros((N_GROUPS, tr), jnp.int32)
    for g in range(N_GROUPS):
        other = gscore[g:g + 1, :]
        wins = (other > gscore) | ((other == gscore) & (g < gidx))
        beaten_by = beaten_by + wins.astype(jnp.int32)
    dropped = jnp.where(beaten_by < TOP_K_GROUPS, 0.0, -jnp.inf)
    cur = (g3 + dropped.reshape(N_GROUPS, 1, tr)).reshape(N_EXPERTS, tr)

    picks, weights = [], []
    selmask = jnp.zeros((N_EXPERTS, tr), F32)
    for _ in range(TOP_K):
        mx = jnp.max(cur, axis=0, keepdims=True)
        pick = jnp.min(jnp.where(cur == mx, erow, float(N_EXPERTS)), axis=0, keepdims=True)
        onehot = erow == pick
        weights.append(jnp.sum(jnp.where(onehot, scores, 0.0), axis=0, keepdims=True))
        cur = jnp.where(onehot, -jnp.inf, cur)
        selmask = jnp.where(onehot, 1.0, selmask)
        picks.append(pick)

    t_from = lax.broadcasted_iota(jnp.int32, (tr, tr), 0)
    t_to = lax.broadcasted_iota(jnp.int32, (tr, tr), 1)
    earlier = jnp.where(t_from < t_to, 1.0, 0.0).astype(BF16)
    chosen = selmask.astype(BF16)
    carry = carry_ref[...]
    rankmat = (jnp.dot(chosen, earlier, preferred_element_type=F32)
               + jnp.concatenate([carry] * (tr // LANES), axis=1))
    carry_ref[...] = carry + jnp.dot(chosen, jnp.ones((tr, LANES), BF16), preferred_element_type=F32)
    cnt_ref[...] = carry_ref[...]

    wsum = weights[0]
    for wj in weights[1:]:
        wsum = wsum + wj
    row8 = lax.broadcasted_iota(jnp.int32, (TOP_K, tr), 0)
    eidx = jnp.zeros((TOP_K, tr), jnp.int32)
    rank = jnp.zeros((TOP_K, tr), jnp.int32)
    wts = jnp.zeros((TOP_K, tr), F32)
    for j in range(TOP_K):
        rk = jnp.sum(jnp.where(erow == picks[j], rankmat, 0.0), axis=0, keepdims=True)
        eidx = jnp.where(row8 == j, picks[j].astype(jnp.int32), eidx)
        rank = jnp.where(row8 == j, rk.astype(jnp.int32), rank)
        wts = jnp.where(row8 == j, weights[j] / wsum * ROUTED_SCALE, wts)
    eidx_ref[...] = eidx
    rank_ref[...] = rank
    wts_ref[...] = wts


def _outproj_route(x2, yp, ya, w_out, gate1, ln_g, ln_b, scale2, shift2, wr_hi, wr_lo, router_bias,
                   *, seq, tr):
    n, d = x2.shape
    tps = seq // tr
    mod_spec = pl.BlockSpec((1, 1, d), lambda i: (i // tps, 0, 0))
    full = lambda a: pl.BlockSpec(a.shape, lambda i: (0,) * a.ndim)
    rows = lambda w: pl.BlockSpec((tr, w), lambda i: (i, 0))
    assert tr % LANES == 0
    k8 = pl.BlockSpec((TOP_K, tr), lambda i: (0, i))
    return pl.pallas_call(
        functools.partial(_outproj_route_kernel, tr=tr),
        grid=(n // tr,),
        in_specs=[rows(d), rows(POOL_DIM), rows(V_DIM), full(w_out), mod_spec, full(ln_g), full(ln_b),
                  mod_spec, mod_spec, full(wr_hi), full(wr_lo), full(router_bias)],
        out_specs=[rows(d), k8, k8, k8, pl.BlockSpec((N_EXPERTS, LANES), lambda i: (0, 0)), rows(d // 2)],
        out_shape=[jax.ShapeDtypeStruct((n, d), F32),
                   jax.ShapeDtypeStruct((TOP_K, n), jnp.int32),
                   jax.ShapeDtypeStruct((TOP_K, n), jnp.int32),
                   jax.ShapeDtypeStruct((TOP_K, n), F32),
                   jax.ShapeDtypeStruct((N_EXPERTS, LANES), F32),
                   jax.ShapeDtypeStruct((n, d // 2), U32)],
        scratch_shapes=[pltpu.VMEM((N_EXPERTS, LANES), F32)],
        compiler_params=_params(("arbitrary",)),
    )(x2, yp, ya, w_out, gate1, ln_g, ln_b, scale2, shift2, wr_hi, wr_lo, router_bias)


def _pack_bf16_halves(x):
    w = x.shape[1] // 2
    lo = pltpu.bitcast(x[:, :w].astype(BF16).astype(F32), U32) >> 16
    hi = pltpu.bitcast(x[:, w:].astype(BF16).astype(F32), U32) & jnp.uint32(0xFFFF0000)
    return lo | hi


def _unpack_bf16_halves(p):
    return pltpu.bitcast(p << 16, F32), pltpu.bitcast(p & jnp.uint32(0xFFFF0000), F32)


def _sc_worker():
    return lax.axis_index("subcore") * SC_CORES + lax.axis_index("core")


def _scatter_rows_sc(rows, idx, n_slots):
    n, width = rows.shape
    k = idx.shape[0] // n
    n_workers = SC_CORES * SC_SUBCORES
    w = SC_ROWS
    per_worker = n // n_workers
    n_chunks = per_worker // w
    assert n == n_workers * n_chunks * w and n_chunks % 2 == 0 and idx.shape[0] == k * n
    mesh = plsc.VectorSubcoreMesh(core_axis_name="core", subcore_axis_name="subcore")

    @functools.partial(
        pl.kernel, mesh=mesh,
        out_type=jax.ShapeDtypeStruct((n_slots, width), rows.dtype),
        scratch_types=[pltpu.VMEM((k * per_worker,), jnp.int32), pltpu.VMEM((2, w, width), rows.dtype),
                       pltpu.SemaphoreType.DMA((2,)), pltpu.SemaphoreType.DMA((2,))])
    def scatter(rows_hbm, idx_hbm, out_hbm, idx_v, rows_v, in_sem, out_sem):
        t0 = _sc_worker() * per_worker
        for j in range(k):
            pltpu.sync_copy(idx_hbm.at[pl.ds(j * n + t0, per_worker)], idx_v.at[pl.ds(j * per_worker, per_worker)])

        def load(i, s):
            return pltpu.make_async_copy(rows_hbm.at[pl.ds(t0 + i * w, w)], rows_v.at[s], in_sem.at[s])

        def send(i, s, j):
            slots = idx_v.at[pl.ds(j * per_worker + i * w, w)]
            return pltpu.make_async_copy(rows_v.at[s], out_hbm.at[slots], out_sem.at[s])

        load(0, 0).start()

        def pair(ii, carry):
            for s in range(2):
                i = ii * 2 + s
                load(i, s).wait()

                @pl.when(i >= 1)
                def _():
                    for j in range(k):
                        send(i - 1, 1 - s, j).wait()

                @pl.when(i + 1 < n_chunks)
                def _():
                    load(i + 1, 1 - s).start()

                for j in range(k):
                    send(i, s, j).start()
            return carry

        lax.fori_loop(0, n_chunks // 2, pair, 0)
        for j in range(k):
            send(n_chunks - 1, 1, j).wait()

    return scatter(rows, idx)


def _dest_kernel(eidx_ref, rank_ref, first_ref, dest_ref):
    tt = eidx_ref.shape[1]
    pieces = [jnp.broadcast_to(first_ref[:, p * LANES:(p + 1) * LANES], (TOP_K, LANES))
              for p in range(N_EXPERTS // LANES)]
    for c in range(tt // LANES):
        cols = slice(c * LANES, (c + 1) * LANES)
        e = eidx_ref[:, cols]
        within = e & (LANES - 1)
        start = jnp.take_along_axis(pieces[0], within, axis=1)
        for p in range(1, len(pieces)):
            start = jnp.where(e // LANES == p, jnp.take_along_axis(pieces[p], within, axis=1), start)
        dest_ref[:, cols] = start + rank_ref[:, cols]


def _dest(eidx_t, rank_t, pstarts, *, tt):
    n = eidx_t.shape[1]
    first = pstarts.reshape(1, N_EXPERTS)
    k8 = pl.BlockSpec((TOP_K, tt), lambda i: (0, i))
    return pl.pallas_call(
        _dest_kernel,
        grid=(n // tt,),
        in_specs=[k8, k8, pl.BlockSpec(first.shape, lambda i: (0, 0))],
        out_specs=k8,
        out_shape=jax.ShapeDtypeStruct((TOP_K, n), jnp.int32),
        compiler_params=_params(("arbitrary",)),
    )(eidx_t, rank_t, first)


def _expert_kernel(fb_ref, nb_ref, cnt_ref, nused_ref, wg_ref, wu_ref, wd_ref, xs_ref, ys_ref,
                   wgb, wub, wdb, xbuf, ybuf, in_sem, out_sem):
    e = pl.program_id(0)
    m = DISPATCH_BLOCK
    ns = EXPERT_SLOTS
    n_used = nused_ref[0]

    def rows(g):
        return pl.ds(pl.multiple_of(g * m, m), m)

    def fetch(g):
        slot = g & (ns - 1)
        return pltpu.make_async_copy(xs_ref.at[rows(g), :], xbuf.at[slot], in_sem.at[slot])

    def put(g):
        slot = g & (ns - 1)
        return pltpu.make_async_copy(ybuf.at[slot], ys_ref.at[rows(g), :], out_sem.at[slot])

    @pl.when(e == 0)
    def _():
        for g0 in range(ns - 1):
            @pl.when(g0 < n_used)
            def _(g0=g0):
                fetch(g0).start()

    wgb[...] = wg_ref[0].astype(BF16)
    wub[...] = wu_ref[0].astype(BF16)
    wdb[...] = wd_ref[0].astype(BF16)

    def block(i, carry):
        g = fb_ref[e] + i
        slot = g & (ns - 1)
        fetch(g).wait()

        @pl.when(g + ns - 1 < n_used)
        def _():
            fetch(g + ns - 1).start()

        @pl.when(g >= ns)
        def _():
            put(g - ns).wait()

        row = lax.broadcasted_iota(jnp.int32, (m, 1), 0)
        packed = jnp.where(row < cnt_ref[e] - i * m, xbuf[slot], jnp.uint32(0))
        x_lo, x_hi = [h.astype(BF16) for h in _unpack_bf16_halves(packed)]
        half = x_lo.shape[1]

        def up_proj(w):
            return (jnp.dot(x_lo, w[0:half, :], preferred_element_type=F32)
                    + jnp.dot(x_hi, w[half:, :], preferred_element_type=F32))

        a = (_silu(up_proj(wgb)) * up_proj(wub)).astype(BF16)
        ybuf[slot] = _pack_bf16_halves(jnp.dot(a, wdb[...], preferred_element_type=F32))
        put(g).start()
        return carry

    lax.fori_loop(0, nb_ref[e], block, 0)

    @pl.when(e == pl.num_programs(0) - 1)
    def _():
        for back in range(ns, 0, -1):
            @pl.when(n_used >= back)
            def _(back=back):
                put(n_used - back).wait()

        ybuf[0] = jnp.zeros(ybuf.shape[1:], ybuf.dtype)
        n_blocks = ys_ref.shape[0] // m

        def tail(wait):
            def body(g, carry):
                cp = pltpu.make_async_copy(ybuf.at[0], ys_ref.at[rows(g), :], out_sem.at[0])
                cp.wait() if wait else cp.start()
                return carry
            return body

        lax.fori_loop(n_used, n_blocks, tail(False), 0)
        lax.fori_loop(n_used, n_blocks, tail(True), 0)


def _experts(first_block, n_blocks_e, counts, n_used, xs, w_gate, w_up, w_down):
    p, dp = xs.shape
    m = DISPATCH_BLOCK
    n_e, d, f = w_gate.shape
    assert d == 2 * dp and p % m == 0
    grid_spec = pltpu.PrefetchScalarGridSpec(
        num_scalar_prefetch=4,
        grid=(n_e,),
        in_specs=[pl.BlockSpec((1, d, f), lambda e, *_: (e, 0, 0)),
                  pl.BlockSpec((1, d, f), lambda e, *_: (e, 0, 0)),
                  pl.BlockSpec((1, f, d), lambda e, *_: (e, 0, 0)),
                  pl.BlockSpec(memory_space=pl.ANY)],
        out_specs=pl.BlockSpec(memory_space=pl.ANY),
        scratch_shapes=[pltpu.VMEM((d, f), BF16), pltpu.VMEM((d, f), BF16), pltpu.VMEM((f, d), BF16),
                        pltpu.VMEM((EXPERT_SLOTS, m, dp), U32), pltpu.VMEM((EXPERT_SLOTS, m, dp), U32),
                        pltpu.SemaphoreType.DMA((EXPERT_SLOTS,)), pltpu.SemaphoreType.DMA((EXPERT_SLOTS,))],
    )
    return pl.pallas_call(
        _expert_kernel,
        grid_spec=grid_spec,
        out_shape=jax.ShapeDtypeStruct((p, dp), U32),
        compiler_params=_params(("arbitrary",)),
    )(first_block, n_blocks_e, counts, n_used, w_gate, w_up, w_down, xs)


def _gather_rows_sc(table, idx):
    n_idx, (_, width) = idx.shape[0], table.shape
    n_workers = SC_CORES * SC_SUBCORES
    w = SC_ROWS
    per_worker = n_idx // n_workers
    n_chunks = per_worker // w
    assert n_idx == n_workers * n_chunks * w and n_chunks % 2 == 0
    mesh = plsc.VectorSubcoreMesh(core_axis_name="core", subcore_axis_name="subcore")

    @functools.partial(
        pl.kernel, mesh=mesh,
        out_type=jax.ShapeDtypeStruct((n_idx, width), table.dtype),
        scratch_types=[pltpu.VMEM((per_worker,), jnp.int32), pltpu.VMEM((2, w, width), table.dtype),
                       pltpu.SemaphoreType.DMA((2,)), pltpu.SemaphoreType.DMA((2,))])
    def gather(table_hbm, idx_hbm, out_hbm, idx_v, rows_v, in_sem, out_sem):
        base = _sc_worker() * per_worker
        pltpu.sync_copy(idx_hbm.at[pl.ds(base, per_worker)], idx_v)

        def fetch(i, s):
            return pltpu.make_async_copy(table_hbm.at[idx_v.at[pl.ds(i * w, w)]], rows_v.at[s], in_sem.at[s])

        def put(i, s):
            return pltpu.make_async_copy(rows_v.at[s], out_hbm.at[pl.ds(base + i * w, w)], out_sem.at[s])

        fetch(0, 0).start()

        def pair(ii, carry):
            for s in range(2):
                i = ii * 2 + s
                fetch(i, s).wait()

                @pl.when(i >= 1)
                def _():
                    put(i - 1, 1 - s).wait()

                @pl.when(i + 1 < n_chunks)
                def _():
                    fetch(i + 1, 1 - s).start()

                put(i, s).start()
            return carry

        lax.fori_loop(0, n_chunks // 2, pair, 0)
        put(n_chunks - 1, 1).wait()

    return gather(table, idx)


def _combine_kernel(x_ref, sc_ref, sh_ref, gate_ref, wts_ref, rows_ref,
                    wsg_ref, wsu_ref, wsd_ref, g_ref, b_ref, *out_refs):
    o_ref = out_refs[-1]
    x = x_ref[...]
    hb = (x * (1.0 + sc_ref[0]) + sh_ref[0]).astype(BF16)
    sg = jnp.dot(hb, wsg_ref[...], preferred_element_type=F32)
    su = jnp.dot(hb, wsu_ref[...], preferred_element_type=F32)
    shared = jnp.dot((_silu(sg) * su).astype(BF16), wsd_ref[...], preferred_element_type=F32)

    wts = wts_ref[...]
    half = shared.shape[1] // 2
    lo, hi = shared[:, :half], shared[:, half:]
    for j in range(TOP_K):
        y_lo, y_hi = _unpack_bf16_halves(rows_ref[j])
        lo = lo + wts[:, j:j + 1] * y_lo
        hi = hi + wts[:, j:j + 1] * y_hi
    z = ALPHA * x + gate_ref[0] * jnp.concatenate([lo, hi], axis=1)
    o_ref[...] = _layer_norm(z, g_ref[...], b_ref[...])


def _combine(x1, scale2, shift2, gate2, wts, rows, ws_gate, ws_up, ws_down, ln_g, ln_b, prev_out,
             *, seq, tc, first_token):
    n, d = x1.shape
    tps = seq // tc
    off = first_token // tc
    n_tiles = rows.shape[1] // tc
    mod_spec = pl.BlockSpec((1, 1, d), lambda i: ((i + off) // tps, 0, 0))
    full = lambda a: pl.BlockSpec(a.shape, lambda i: (0,) * a.ndim)
    args = [x1, scale2, shift2, gate2, wts, rows, ws_gate, ws_up, ws_down, ln_g, ln_b]
    in_specs = [pl.BlockSpec((tc, d), lambda i: (i + off, 0)), mod_spec, mod_spec, mod_spec,
                pl.BlockSpec((tc, TOP_K), lambda i: (i + off, 0)),
                pl.BlockSpec((TOP_K, tc, d // 2), lambda i: (0, i, 0)),
                full(ws_gate), full(ws_up), full(ws_down), full(ln_g), full(ln_b)]
    aliases = {}
    if prev_out is not None:
        aliases = {len(args): 0}
        args.append(prev_out)
        in_specs.append(pl.BlockSpec(memory_space=pl.ANY))
    return pl.pallas_call(
        _combine_kernel,
        grid=(n_tiles,),
        in_specs=in_specs,
        out_specs=pl.BlockSpec((tc, d), lambda i: (i + off, 0)),
        out_shape=jax.ShapeDtypeStruct((n, d), F32),
        input_output_aliases=aliases,
        compiler_params=_params(("arbitrary",)),
    )(*args)


def _layer(x, c, w_ada, b_ada, w_in, pool_w, pool_scale, lq1, lk1, lq2, lk2, subln_g, w_out,
           ln1_g, ln1_b, w_router, router_bias, w_gate, w_up, w_down, ws_gate, ws_up, ws_down,
           ln2_g, ln2_b, rel_table, *, tm=512, tr=256, tc=256):
    bsz, seq, d = x.shape
    n = bsz * seq
    x2 = x.reshape(n, d)
    row = lambda a: a.reshape(1, -1)

    mod = _modulation(c, w_ada, b_ada)
    shift1, scale1, gate1, shift2, scale2, gate2 = [
        mod[:, j * d:(j + 1) * d].reshape(bsz, 1, d) for j in range(6)]

    n_main = POOL_DIM + 2 * QK_DIM
    yp, q, k, vt = _inproj(x2, scale1, shift1, w_in[:, :n_main].astype(BF16), w_in[:, n_main:].T.astype(BF16),
                           pool_w.astype(BF16), row(pool_scale), seq=seq, tm=tm)
    bias_tiles = _bias_tiles(rel_table, seq // ATT_TILE)
    ya = _attention(q, k, vt, bias_tiles, row(lq1), row(lk1), row(lq2), row(lk2), subln_g.reshape(-1, 1),
                    bsz=bsz, seq=seq)
    wr_t = w_router.T
    wr_hi = wr_t.astype(BF16)
    wr_lo = (wr_t - wr_hi.astype(F32)).astype(BF16)
    x1, eidx_t, rank_t, wts_t, cnt, h2p = _outproj_route(
        x2, yp, ya, w_out.astype(BF16), gate1, row(ln1_g), row(ln1_b),
        scale2, shift2, wr_hi, wr_lo, router_bias.reshape(-1, 1), seq=seq, tr=tr)

    m = DISPATCH_BLOCK
    counts = cnt[:, 0].astype(jnp.int32)
    padded = (counts + m - 1) // m * m
    pends = jnp.cumsum(padded)
    pstarts = pends - padded
    n_blocks = -(-(n * TOP_K + N_EXPERTS * (m - 1)) // m)
    n_used = (pends[-1:] // m).astype(jnp.int32)
    dest_t = _dest(eidx_t, rank_t, pstarts, tt=min(n, 2048))

    xs = _scatter_rows_sc(h2p, dest_t.reshape(TOP_K * n), n_blocks * m)
    ys = _experts(pstarts // m, padded // m, counts, n_used, xs, w_gate, w_up, w_down)

    shared_w = (ws_gate.astype(BF16), ws_up.astype(BF16), ws_down.astype(BF16))
    wts = wts_t.T
    part = n // COMBINE_PARTS
    out = None
    for p in range(COMBINE_PARTS):
        idx = dest_t[:, p * part:(p + 1) * part].reshape(TOP_K * part)
        picked = _gather_rows_sc(ys, idx).reshape(TOP_K, part, d // 2)
        out = _combine(x1, scale2, shift2, gate2, wts, picked, *shared_w, row(ln2_g), row(ln2_b), out,
                       seq=seq, tc=tc, first_token=p * part)
    return out.reshape(bsz, seq, d)


def kernel(x, c, w_ada, b_ada, w_in, pool_w, pool_scale, lambda_q1, lambda_k1, lambda_q2, lambda_k2,
           subln_g, w_out, ln1_g, ln1_b, w_router, router_bias, w_gate, w_up, w_down,
           ws_gate, ws_up, ws_down, ln2_g, ln2_b, rel_table):
    per_layer = (w_ada, b_ada, w_in, pool_w, pool_scale, lambda_q1, lambda_k1, lambda_q2, lambda_k2,
                 subln_g, w_out, ln1_g, ln1_b, w_router, router_bias, w_gate, w_up, w_down,
                 ws_gate, ws_up, ws_down, ln2_g, ln2_b)
    assert all(a.shape[0] == DEPTH == 1 for a in per_layer)
    return _layer(x, c, *[a.reshape(a.shape[1:]) for a in per_layer], rel_table)
```

```python
import functools
import math

import jax
import jax.numpy as jnp
from jax import lax
from jax.experimental import pallas as pl
from jax.experimental.pallas import tpu as pltpu
from jax.experimental.pallas import tpu_sc as plsc

F32 = jnp.float32
BF16 = jnp.bfloat16
U32 = jnp.uint32
LANES = 128

D_MODEL = 1024
CHUNK = 64
Q_BLOCK = 128
ATT_TILE = 256
POOL_DIM = 512
POOL_WINDOWS = (2, 4, 8, 16)
POOL_GROUP_DIM = 128
MAX_WINDOW = max(POOL_WINDOWS)
ATTN_HEADS = 4
ATTN_HEAD_DIM = 64
QK_DIM = 512
V_DIM = 512
IN_DIM = 2048
NUM_BUCKETS = 32
MAX_DISTANCE = 128
N_EXPERTS = 256
TOP_K = 8
N_GROUPS = 8
GROUP_SIZE = N_EXPERTS // N_GROUPS
TOP_K_GROUPS = 4
EXPERT_DIM = 256
ROUTED_SCALE = 2.5
DISPATCH_BLOCK = 512
EXPERT_SLOTS = 4
COMBINE_PARTS = 4
SC_CORES, SC_SUBCORES = 2, 16
SC_ROWS = 64
DEPTH = 1
ALPHA = (2.0 * DEPTH) ** 0.25
LN_EPS = 1e-5
LAMBDA_INIT = 0.8 - 0.6 * math.exp(-0.3 * 0)

VMEM_LIMIT = 48 * 1024 * 1024


def _sigmoid(x):
    return 1.0 / (1.0 + jnp.exp(-x))


def _silu(x):
    return x * _sigmoid(x)


def _layer_norm(z, g, b):
    mu = jnp.mean(z, axis=-1, keepdims=True)
    zc = z - mu
    var = jnp.mean(zc * zc, axis=-1, keepdims=True)
    return zc * lax.rsqrt(var + LN_EPS) * g + b


def _params(sem=None, flags=None):
    return pltpu.CompilerParams(dimension_semantics=sem, vmem_limit_bytes=VMEM_LIMIT, flags=flags)


def _mod_kernel(c_ref, w_ref, b_ref, o_ref):
    ca = _silu(c_ref[...])
    o_ref[...] = jnp.dot(ca, w_ref[...], preferred_element_type=F32,
                         precision=lax.Precision.HIGHEST) + b_ref[...]


def _modulation(c, w_ada, b_ada):
    bsz, d = c.shape
    n_out = w_ada.shape[1]
    return pl.pallas_call(
        _mod_kernel,
        grid=(n_out // d,),
        in_specs=[pl.BlockSpec((bsz, d), lambda j: (0, 0)),
                  pl.BlockSpec((d, d), lambda j: (0, j)),
                  pl.BlockSpec((1, d), lambda j: (0, j))],
        out_specs=pl.BlockSpec((bsz, d), lambda j: (0, j)),
        out_shape=jax.ShapeDtypeStruct((bsz, n_out), F32),
        compiler_params=_params(("arbitrary",)),
    )(c, w_ada, b_ada.reshape(1, n_out))


def _inproj_kernel(x_ref, sc_ref, sh_ref, w_ref, wvt_ref, pw_ref, ps_ref,
                   yp_ref, q_ref, k_ref, vt_ref, ext_ref, *, tm, seq):
    i = pl.program_id(0)
    tiles_per_seq = seq // tm
    it = i % tiles_per_seq
    h = x_ref[...] * (1.0 + sc_ref[0]) + sh_ref[0]
    hb = h.astype(BF16)
    proj = jnp.dot(hb, w_ref[...], preferred_element_type=F32)
    u = proj[:, :POOL_DIM]
    q_ref[...] = (proj[:, POOL_DIM:POOL_DIM + QK_DIM] * (ATTN_HEAD_DIM ** -0.5)).astype(BF16)
    k_ref[...] = proj[:, POOL_DIM + QK_DIM:POOL_DIM + 2 * QK_DIM].astype(BF16)
    vt = lax.dot_general(wvt_ref[...], hb, (((1,), (1,)), ((), ())), preferred_element_type=F32)
    for j in range(tm // ATT_TILE):
        vt_ref[0, j] = vt[:, j * ATT_TILE:(j + 1) * ATT_TILE].astype(BF16)

    @pl.when(it == 0)
    def _():
        ext_ref[0:MAX_WINDOW, :] = jnp.zeros((MAX_WINDOW, POOL_DIM), F32)

    ext_ref[MAX_WINDOW:MAX_WINDOW + tm, :] = u
    pos = (it * tm + lax.broadcasted_iota(jnp.int32, (tm, 1), 0) + 1).astype(F32)
    for g, w in enumerate(POOL_WINDOWS):
        c0, c1 = g * POOL_GROUP_DIM, (g + 1) * POOL_GROUP_DIM
        s = ext_ref[MAX_WINDOW:MAX_WINDOW + tm, c0:c1]
        for j in range(1, w):
            s = s + ext_ref[MAX_WINDOW - j:MAX_WINDOW - j + tm, c0:c1]
        pooled = s / jnp.minimum(pos, float(w)) - u[:, c0:c1]
        y = jnp.dot(pooled.astype(BF16), pw_ref[g], preferred_element_type=F32)
        yp_ref[:, c0:c1] = (y * ps_ref[:, c0:c1]).astype(BF16)
    ext_ref[0:MAX_WINDOW, :] = ext_ref[tm:tm + MAX_WINDOW, :]


def _inproj(x2, scale1, shift1, w_main, w_vt, pool_w, pool_scale, *, seq, tm):
    n, d = x2.shape
    assert n % tm == 0 and seq % tm == 0 and tm >= 2 * MAX_WINDOW and tm % ATT_TILE == 0
    tps = seq // tm
    tpt = tm // ATT_TILE
    mod_spec = pl.BlockSpec((1, 1, d), lambda i: (i // tps, 0, 0))
    row = lambda w: pl.BlockSpec((tm, w), lambda i: (i, 0))
    full = lambda a: pl.BlockSpec(a.shape, lambda i: (0,) * a.ndim)
    return pl.pallas_call(
        functools.partial(_inproj_kernel, tm=tm, seq=seq),
        grid=(n // tm,),
        in_specs=[row(d), mod_spec, mod_spec, full(w_main), full(w_vt), full(pool_w), full(pool_scale)],
        out_specs=[row(POOL_DIM), row(QK_DIM), row(QK_DIM),
                   pl.BlockSpec((1, tpt, V_DIM, ATT_TILE), lambda i: (i // tps, i % tps, 0, 0))],
        out_shape=[jax.ShapeDtypeStruct((n, POOL_DIM), BF16),
                   jax.ShapeDtypeStruct((n, QK_DIM), BF16),
                   jax.ShapeDtypeStruct((n, QK_DIM), BF16),
                   jax.ShapeDtypeStruct((n // seq, seq // ATT_TILE, V_DIM, ATT_TILE), BF16)],
        scratch_shapes=[pltpu.VMEM((tm + MAX_WINDOW, POOL_DIM), F32)],
        compiler_params=_params(("arbitrary",)),
    )(x2, scale1, shift1, w_main, w_vt, pool_w, pool_scale)


def _bias_kernel(tab_ref, o_ref):
    delta = pl.program_id(0)
    r = lax.broadcasted_iota(jnp.int32, (ATT_TILE, ATT_TILE), 0)
    c = lax.broadcasted_iota(jnp.int32, (ATT_TILE, ATT_TILE), 1)
    rel = r - c - delta * ATT_TILE
    half = NUM_BUCKETS // 2
    max_exact = half // 2
    ret = jnp.where(rel > 0, half, 0)
    n = jnp.abs(rel)
    nf = jnp.maximum(n, 1).astype(F32)
    large = max_exact + (jnp.log(nf / max_exact) / math.log(MAX_DISTANCE / max_exact)
                         * (half - max_exact)).astype(jnp.int32)
    large = jnp.minimum(large, half - 1)
    bucket = ret + jnp.where(n < max_exact, n, large)
    for h in range(ATTN_HEADS):
        acc = jnp.zeros((ATT_TILE, ATT_TILE), F32)
        for b in range(NUM_BUCKETS):
            acc = jnp.where(bucket == b, tab_ref[b, h], acc)
        o_ref[h, 0] = acc


def _bias_tiles(rel_table, n_tiles):
    return pl.pallas_call(
        _bias_kernel,
        grid=(n_tiles,),
        in_specs=[pl.BlockSpec(memory_space=pltpu.SMEM)],
        out_specs=pl.BlockSpec((ATTN_HEADS, 1, ATT_TILE, ATT_TILE), lambda dlt: (0, dlt, 0, 0)),
        out_shape=jax.ShapeDtypeStruct((ATTN_HEADS, n_tiles, ATT_TILE, ATT_TILE), F32),
        compiler_params=_params(("arbitrary",)),
    )(rel_table)


def _attn_kernel(q_ref, k_ref, vt_ref, bias_ref, lq1_ref, lk1_ref, lq2_ref, lk2_ref, g_ref, o_ref, *acc_refs):
    qt = pl.program_id(1)
    t = ATT_TILE
    n_maps = 2 * ATTN_HEADS
    lam = (jnp.exp(jnp.sum(lq1_ref[...] * lk1_ref[...], axis=-1, keepdims=True))
           - jnp.exp(jnp.sum(lq2_ref[...] * lk2_ref[...], axis=-1, keepdims=True))
           + LAMBDA_INIT)
    r = lax.broadcasted_iota(jnp.int32, (t, t), 0)
    c = lax.broadcasted_iota(jnp.int32, (t, t), 1)
    allowed = (r // CHUNK) <= (c // CHUNK)
    hd2 = 2 * ATTN_HEAD_DIM

    ahead = 2

    def scores(kt, hm):
        col = hm * ATTN_HEAD_DIM
        qh = q_ref[:, col:col + ATTN_HEAD_DIM]
        kh = k_ref[pl.ds(pl.multiple_of(kt * t, t), t), col:col + ATTN_HEAD_DIM]
        return lax.dot_general(kh, qh, (((1,), (1,)), ((), ())),
                               preferred_element_type=F32) + bias_ref[hm // 2, qt - kt]

    def block(kt, carry, diagonal):
        stats, early = carry[:2 * n_maps], carry[2 * n_maps:]

        def softmax(hm, s):
            if diagonal:
                s = jnp.where(allowed, s, -jnp.inf)
            m_old, l_old = stats[2 * hm:2 * hm + 2]
            m_new = jnp.maximum(m_old, jnp.max(s, axis=0, keepdims=True))
            alpha = jnp.exp(m_old - m_new)
            p = jnp.exp(s - m_new)
            return m_new, alpha * l_old + jnp.sum(p, axis=0, keepdims=True), alpha, p.astype(BF16)

        def accumulate(hm, alpha, p):
            h = hm // 2
            vth = vt_ref[0, kt, h * hd2:(h + 1) * hd2, :]
            acc_refs[hm][...] = alpha * acc_refs[hm][...] + jnp.dot(vth, p, preferred_element_type=F32)

        s_vals = dict(enumerate(early))
        sm_vals, out, nxt = {}, [None] * (2 * n_maps), []
        for step in range(1, n_maps + ahead):
            if ahead <= step < n_maps:
                s_vals[step] = scores(kt, step)
            elif step >= n_maps and not diagonal:
                nxt.append(scores(kt + 1, step - n_maps))
            hm = step - 1
            if hm < n_maps:
                m_new, l_new, alpha, p = softmax(hm, s_vals.pop(hm))
                out[2 * hm], out[2 * hm + 1] = m_new, l_new
                sm_vals[hm] = (alpha, p)
            if step - 2 >= 0:
                accumulate(step - 2, *sm_vals.pop(step - 2))
        return tuple(out) + tuple(nxt)

    for acc in acc_refs:
        acc[...] = jnp.zeros_like(acc)
    one = (jnp.full((1, t), -jnp.inf, F32), jnp.zeros((1, t), F32))
    first = tuple(scores(0, hm) for hm in range(ahead))
    carry = lax.fori_loop(0, qt, lambda kt, cr: block(kt, cr, False), one * n_maps + first)
    carry = block(qt, carry, True)
    for h in range(ATTN_HEADS):
        l0, l1 = carry[4 * h + 1], carry[4 * h + 3]
        o = acc_refs[2 * h][...] / l0 - lam * (acc_refs[2 * h + 1][...] / l1)
        y = o * lax.rsqrt(jnp.mean(o * o, axis=0, keepdims=True) + LN_EPS) * g_ref[...]
        o_ref[:, h * hd2:(h + 1) * hd2] = (y * (1.0 - LAMBDA_INIT)).T.astype(BF16)


def _attention(q, k, vt, bias_tiles, lq1, lk1, lq2, lk2, subln_g, *, bsz, seq):
    t = ATT_TILE
    nt = seq // t
    full = lambda a: pl.BlockSpec(a.shape, lambda b, j: (0,) * a.ndim)
    return pl.pallas_call(
        _attn_kernel,
        grid=(bsz, nt),
        in_specs=[pl.BlockSpec((t, QK_DIM), lambda b, j: (b * nt + j, 0)),
                  pl.BlockSpec((seq, QK_DIM), lambda b, j: (b, 0)),
                  pl.BlockSpec((1, nt, V_DIM, t), lambda b, j: (b, 0, 0, 0)),
                  full(bias_tiles), full(lq1), full(lk1), full(lq2), full(lk2), full(subln_g)],
        out_specs=pl.BlockSpec((t, V_DIM), lambda b, j: (b * nt + j, 0)),
        out_shape=jax.ShapeDtypeStruct((bsz * seq, V_DIM), BF16),
        scratch_shapes=[pltpu.VMEM((2 * ATTN_HEAD_DIM, t), F32) for _ in range(2 * ATTN_HEADS)],
        compiler_params=_params(("arbitrary", "arbitrary")),
    )(q, k, vt, bias_tiles, lq1, lk1, lq2, lk2, subln_g)


def _outproj_kernel(x_ref, yp_ref, ya_ref, w_ref, gate_ref, g_ref, b_ref, o_ref):
    mix = (jnp.dot(yp_ref[...], w_ref[0:POOL_DIM, :], preferred_element_type=F32)
           + jnp.dot(ya_ref[...], w_ref[POOL_DIM:, :], preferred_element_type=F32))
    z = ALPHA * x_ref[...] + gate_ref[0] * mix
    o_ref[...] = _layer_norm(z, g_ref[...], b_ref[...])


def _outproj(x2, yp, ya, w_out, gate1, ln_g, ln_b, *, seq, tm):
    n, d = x2.shape
    tps = seq // tm
    row = lambda w: pl.BlockSpec((tm, w), lambda i: (i, 0))
    full = lambda a: pl.BlockSpec(a.shape, lambda i: (0,) * a.ndim)
    return pl.pallas_call(
        _outproj_kernel,
        grid=(n // tm,),
        in_specs=[row(d), row(POOL_DIM), row(V_DIM), full(w_out),
                  pl.BlockSpec((1, 1, d), lambda i: (i // tps, 0, 0)), full(ln_g), full(ln_b)],
        out_specs=row(d),
        out_shape=jax.ShapeDtypeStruct((n, d), F32),
        compiler_params=_params(("arbitrary",)),
    )(x2, yp, ya, w_out, gate1, ln_g, ln_b)


def _route_kernel(x_ref, sc_ref, sh_ref, whi_ref, wlo_ref, rb_ref,
                  eidx_ref, rank_ref, wts_ref, cnt_ref, h2p_ref, carry_ref, *, tr):
    i = pl.program_id(0)

    @pl.when(i == 0)
    def _():
        carry_ref[...] = jnp.zeros_like(carry_ref)

    h2 = x_ref[...] * (1.0 + sc_ref[0]) + sh_ref[0]
    h2p_ref[...] = _pack_bf16_halves(h2)
    hi = h2.astype(BF16)
    lo = (h2 - hi.astype(F32)).astype(BF16)
    nt = (((1,), (1,)), ((), ()))
    logits = (lax.dot_general(whi_ref[...], hi, nt, preferred_element_type=F32)
              + lax.dot_general(wlo_ref[...], hi, nt, preferred_element_type=F32)
              + lax.dot_general(whi_ref[...], lo, nt, preferred_element_type=F32))
    scores = _sigmoid(logits)
    sel = scores + rb_ref[...]
    erow = lax.broadcasted_iota(jnp.int32, (N_EXPERTS, tr), 0).astype(F32)

    g3 = sel.reshape(N_GROUPS, GROUP_SIZE, tr)
    r3 = lax.broadcasted_iota(jnp.int32, (N_GROUPS, GROUP_SIZE, tr), 1).astype(F32)
    m1 = jnp.max(g3, axis=1, keepdims=True)
    first = jnp.min(jnp.where(g3 == m1, r3, float(GROUP_SIZE)), axis=1, keepdims=True)
    m2 = jnp.max(jnp.where(r3 == first, -jnp.inf, g3), axis=1, keepdims=True)
    gscore = (m1 + m2).reshape(N_GROUPS, tr)

    gidx = lax.broadcasted_iota(jnp.int32, (N_GROUPS, tr), 0)
    beaten_by = jnp.zeros((N_GROUPS, tr), jnp.int32)
    for g in range(N_GROUPS):
        other = gscore[g:g + 1, :]
        wins = (other > gscore) | ((other == gscore) & (g < gidx))
        beaten_by = beaten_by + wins.astype(jnp.int32)
    dropped = jnp.where(beaten_by < TOP_K_GROUPS, 0.0, -jnp.inf)
    cur = (g3 + dropped.reshape(N_GROUPS, 1, tr)).reshape(N_EXPERTS, tr)

    picks, weights = [], []
    selmask = jnp.zeros((N_EXPERTS, tr), F32)
    for _ in range(TOP_K):
        mx = jnp.max(cur, axis=0, keepdims=True)
        pick = jnp.min(jnp.where(cur == mx, erow, float(N_EXPERTS)), axis=0, keepdims=True)
        onehot = erow == pick
        weights.append(jnp.sum(jnp.where(onehot, scores, 0.0), axis=0, keepdims=True))
        cur = jnp.where(onehot, -jnp.inf, cur)
        selmask = jnp.where(onehot, 1.0, selmask)
        picks.append(pick)

    t_from = lax.broadcasted_iota(jnp.int32, (tr, tr), 0)
    t_to = lax.broadcasted_iota(jnp.int32, (tr, tr), 1)
    earlier = jnp.where(t_from < t_to, 1.0, 0.0).astype(BF16)
    chosen = selmask.astype(BF16)
    carry = carry_ref[...]
    rankmat = (jnp.dot(chosen, earlier, preferred_element_type=F32)
               + jnp.concatenate([carry] * (tr // LANES), axis=1))
    carry_ref[...] = carry + jnp.dot(chosen, jnp.ones((tr, LANES), BF16), preferred_element_type=F32)
    cnt_ref[...] = carry_ref[...]

    wsum = weights[0]
    for wj in weights[1:]:
        wsum = wsum + wj
    row8 = lax.broadcasted_iota(jnp.int32, (TOP_K, tr), 0)
    eidx = jnp.zeros((TOP_K, tr), jnp.int32)
    rank = jnp.zeros((TOP_K, tr), jnp.int32)
    wts = jnp.zeros((TOP_K, tr), F32)
    for j in range(TOP_K):
        rk = jnp.sum(jnp.where(erow == picks[j], rankmat, 0.0), axis=0, keepdims=True)
        eidx = jnp.where(row8 == j, picks[j].astype(jnp.int32), eidx)
        rank = jnp.where(row8 == j, rk.astype(jnp.int32), rank)
        wts = jnp.where(row8 == j, weights[j] / wsum * ROUTED_SCALE, wts)
    eidx_ref[...] = eidx
    rank_ref[...] = rank
    wts_ref[...] = wts


def _route(x1, scale2, shift2, wr_hi, wr_lo, router_bias, *, seq, tr):
    n, d = x1.shape
    tps = seq // tr
    mod_spec = pl.BlockSpec((1, 1, d), lambda i: (i // tps, 0, 0))
    full = lambda a: pl.BlockSpec(a.shape, lambda i: (0,) * a.ndim)
    assert tr % LANES == 0
    k8 = pl.BlockSpec((TOP_K, tr), lambda i: (0, i))
    return pl.pallas_call(
        functools.partial(_route_kernel, tr=tr),
        grid=(n // tr,),
        in_specs=[pl.BlockSpec((tr, d), lambda i: (i, 0)), mod_spec, mod_spec,
                  full(wr_hi), full(wr_lo), full(router_bias)],
        out_specs=[k8, k8, k8, pl.BlockSpec((N_EXPERTS, LANES), lambda i: (0, 0)),
                   pl.BlockSpec((tr, d // 2), lambda i: (i, 0))],
        out_shape=[jax.ShapeDtypeStruct((TOP_K, n), jnp.int32),
                   jax.ShapeDtypeStruct((TOP_K, n), jnp.int32),
                   jax.ShapeDtypeStruct((TOP_K, n), F32),
                   jax.ShapeDtypeStruct((N_EXPERTS, LANES), F32),
                   jax.ShapeDtypeStruct((n, d // 2), U32)],
        scratch_shapes=[pltpu.VMEM((N_EXPERTS, LANES), F32)],
        compiler_params=_params(("arbitrary",)),
    )(x1, scale2, shift2, wr_hi, wr_lo, router_bias)


def _pack_bf16_halves(x):
    w = x.shape[1] // 2
    lo = pltpu.bitcast(x[:, :w].astype(BF16).astype(F32), U32) >> 16
    hi = pltpu.bitcast(x[:, w:].astype(BF16).astype(F32), U32) & jnp.uint32(0xFFFF0000)
    return lo | hi


def _unpack_bf16_halves(p):
    return pltpu.bitcast(p << 16, F32), pltpu.bitcast(p & jnp.uint32(0xFFFF0000), F32)


def _sc_worker():
    return lax.axis_index("subcore") * SC_CORES + lax.axis_index("core")


def _scatter_rows_sc(rows, idx, n_slots):
    n, width = rows.shape
    k = idx.shape[0] // n
    n_workers = SC_CORES * SC_SUBCORES
    w = SC_ROWS
    per_worker = n // n_workers
    n_chunks = per_worker // w
    assert n == n_workers * n_chunks * w and n_chunks % 2 == 0 and idx.shape[0] == k * n
    mesh = plsc.VectorSubcoreMesh(core_axis_name="core", subcore_axis_name="subcore")

    @functools.partial(
        pl.kernel, mesh=mesh,
        out_type=jax.ShapeDtypeStruct((n_slots, width), rows.dtype),
        scratch_types=[pltpu.VMEM((k * per_worker,), jnp.int32), pltpu.VMEM((2, w, width), rows.dtype),
                       pltpu.SemaphoreType.DMA((2,)), pltpu.SemaphoreType.DMA((2,))])
    def scatter(rows_hbm, idx_hbm, out_hbm, idx_v, rows_v, in_sem, out_sem):
        t0 = _sc_worker() * per_worker
        for j in range(k):
            pltpu.sync_copy(idx_hbm.at[pl.ds(j * n + t0, per_worker)], idx_v.at[pl.ds(j * per_worker, per_worker)])

        def load(i, s):
            return pltpu.make_async_copy(rows_hbm.at[pl.ds(t0 + i * w, w)], rows_v.at[s], in_sem.at[s])

        def send(i, s, j):
            slots = idx_v.at[pl.ds(j * per_worker + i * w, w)]
            return pltpu.make_async_copy(rows_v.at[s], out_hbm.at[slots], out_sem.at[s])

        load(0, 0).start()

        def pair(ii, carry):
            for s in range(2):
                i = ii * 2 + s
                load(i, s).wait()

                @pl.when(i >= 1)
                def _():
                    for j in range(k):
                        send(i - 1, 1 - s, j).wait()

                @pl.when(i + 1 < n_chunks)
                def _():
                    load(i + 1, 1 - s).start()

                for j in range(k):
                    send(i, s, j).start()
            return carry

        lax.fori_loop(0, n_chunks // 2, pair, 0)
        for j in range(k):
            send(n_chunks - 1, 1, j).wait()

    return scatter(rows, idx)


def _dest_kernel(eidx_ref, rank_ref, first_ref, dest_ref):
    tt = eidx_ref.shape[1]
    pieces = [jnp.broadcast_to(first_ref[:, p * LANES:(p + 1) * LANES], (TOP_K, LANES))
              for p in range(N_EXPERTS // LANES)]
    for c in range(tt // LANES):
        cols = slice(c * LANES, (c + 1) * LANES)
        e = eidx_ref[:, cols]
        within = e & (LANES - 1)
        start = jnp.take_along_axis(pieces[0], within, axis=1)
        for p in range(1, len(pieces)):
            start = jnp.where(e // LANES == p, jnp.take_along_axis(pieces[p], within, axis=1), start)
        dest_ref[:, cols] = start + rank_ref[:, cols]


def _dest(eidx_t, rank_t, pstarts, *, tt):
    n = eidx_t.shape[1]
    first = pstarts.reshape(1, N_EXPERTS)
    k8 = pl.BlockSpec((TOP_K, tt), lambda i: (0, i))
    return pl.pallas_call(
        _dest_kernel,
        grid=(n // tt,),
        in_specs=[k8, k8, pl.BlockSpec(first.shape, lambda i: (0, 0))],
        out_specs=k8,
        out_shape=jax.ShapeDtypeStruct((TOP_K, n), jnp.int32),
        compiler_params=_params(("arbitrary",)),
    )(eidx_t, rank_t, first)


def _expert_kernel(fb_ref, nb_ref, cnt_ref, nused_ref, wg_ref, wu_ref, wd_ref, xs_ref, ys_ref,
                   wgb, wub, wdb, xbuf, ybuf, in_sem, out_sem):
    e = pl.program_id(0)
    m = DISPATCH_BLOCK
    ns = EXPERT_SLOTS
    n_used = nused_ref[0]

    def rows(g):
        return pl.ds(pl.multiple_of(g * m, m), m)

    def fetch(g):
        slot = g & (ns - 1)
        return pltpu.make_async_copy(xs_ref.at[rows(g), :], xbuf.at[slot], in_sem.at[slot])

    def put(g):
        slot = g & (ns - 1)
        return pltpu.make_async_copy(ybuf.at[slot], ys_ref.at[rows(g), :], out_sem.at[slot])

    @pl.when(e == 0)
    def _():
        for g0 in range(ns - 1):
            @pl.when(g0 < n_used)
            def _(g0=g0):
                fetch(g0).start()

    wgb[...] = wg_ref[0].astype(BF16)
    wub[...] = wu_ref[0].astype(BF16)
    wdb[...] = wd_ref[0].astype(BF16)

    def block(i, carry):
        g = fb_ref[e] + i
        slot = g & (ns - 1)
        fetch(g).wait()

        @pl.when(g + ns - 1 < n_used)
        def _():
            fetch(g + ns - 1).start()

        @pl.when(g >= ns)
        def _():
            put(g - ns).wait()

        row = lax.broadcasted_iota(jnp.int32, (m, 1), 0)
        packed = jnp.where(row < cnt_ref[e] - i * m, xbuf[slot], jnp.uint32(0))
        x_lo, x_hi = [h.astype(BF16) for h in _unpack_bf16_halves(packed)]
        half = x_lo.shape[1]

        def up_proj(w):
            return (jnp.dot(x_lo, w[0:half, :], preferred_element_type=F32)
                    + jnp.dot(x_hi, w[half:, :], preferred_element_type=F32))

        a = (_silu(up_proj(wgb)) * up_proj(wub)).astype(BF16)
        ybuf[slot] = _pack_bf16_halves(jnp.dot(a, wdb[...], preferred_element_type=F32))
        put(g).start()
        return carry

    lax.fori_loop(0, nb_ref[e], block, 0)

    @pl.when(e == pl.num_programs(0) - 1)
    def _():
        for back in range(ns, 0, -1):
            @pl.when(n_used >= back)
            def _(back=back):
                put(n_used - back).wait()

        ybuf[0] = jnp.zeros(ybuf.shape[1:], ybuf.dtype)
        n_blocks = ys_ref.shape[0] // m

        def tail(wait):
            def body(g, carry):
                cp = pltpu.make_async_copy(ybuf.at[0], ys_ref.at[rows(g), :], out_sem.at[0])
                cp.wait() if wait else cp.start()
                return carry
            return body

        lax.fori_loop(n_used, n_blocks, tail(False), 0)
        lax.fori_loop(n_used, n_blocks, tail(True), 0)


def _experts(first_block, n_blocks_e, counts, n_used, xs, w_gate, w_up, w_down):
    p, dp = xs.shape
    m = DISPATCH_BLOCK
    n_e, d, f = w_gate.shape
    assert d == 2 * dp and p % m == 0
    grid_spec = pltpu.PrefetchScalarGridSpec(
        num_scalar_prefetch=4,
        grid=(n_e,),
        in_specs=[pl.BlockSpec((1, d, f), lambda e, *_: (e, 0, 0)),
                  pl.BlockSpec((1, d, f), lambda e, *_: (e, 0, 0)),
                  pl.BlockSpec((1, f, d), lambda e, *_: (e, 0, 0)),
                  pl.BlockSpec(memory_space=pl.ANY)],
        out_specs=pl.BlockSpec(memory_space=pl.ANY),
        scratch_shapes=[pltpu.VMEM((d, f), BF16), pltpu.VMEM((d, f), BF16), pltpu.VMEM((f, d), BF16),
                        pltpu.VMEM((EXPERT_SLOTS, m, dp), U32), pltpu.VMEM((EXPERT_SLOTS, m, dp), U32),
                        pltpu.SemaphoreType.DMA((EXPERT_SLOTS,)), pltpu.SemaphoreType.DMA((EXPERT_SLOTS,))],
    )
    return pl.pallas_call(
        _expert_kernel,
        grid_spec=grid_spec,
        out_shape=jax.ShapeDtypeStruct((p, dp), U32),
        compiler_params=_params(("arbitrary",)),
    )(first_block, n_blocks_e, counts, n_used, w_gate, w_up, w_down, xs)


def _gather_rows_sc(table, idx):
    n_idx, (_, width) = idx.shape[0], table.shape
    n_workers = SC_CORES * SC_SUBCORES
    w = SC_ROWS
    per_worker = n_idx // n_workers
    n_chunks = per_worker // w
    assert n_idx == n_workers * n_chunks * w and n_chunks % 2 == 0
    mesh = plsc.VectorSubcoreMesh(core_axis_name="core", subcore_axis_name="subcore")

    @functools.partial(
        pl.kernel, mesh=mesh,
        out_type=jax.ShapeDtypeStruct((n_idx, width), table.dtype),
        scratch_types=[pltpu.VMEM((per_worker,), jnp.int32), pltpu.VMEM((2, w, width), table.dtype),
                       pltpu.SemaphoreType.DMA((2,)), pltpu.SemaphoreType.DMA((2,))])
    def gather(table_hbm, idx_hbm, out_hbm, idx_v, rows_v, in_sem, out_sem):
        base = _sc_worker() * per_worker
        pltpu.sync_copy(idx_hbm.at[pl.ds(base, per_worker)], idx_v)

        def fetch(i, s):
            return pltpu.make_async_copy(table_hbm.at[idx_v.at[pl.ds(i * w, w)]], rows_v.at[s], in_sem.at[s])

        def put(i, s):
            return pltpu.make_async_copy(rows_v.at[s], out_hbm.at[pl.ds(base + i * w, w)], out_sem.at[s])

        fetch(0, 0).start()

        def pair(ii, carry):
            for s in range(2):
                i = ii * 2 + s
                fetch(i, s).wait()

                @pl.when(i >= 1)
                def _():
                    put(i - 1, 1 - s).wait()

                @pl.when(i + 1 < n_chunks)
                def _():
                    fetch(i + 1, 1 - s).start()

                put(i, s).start()
            return carry

        lax.fori_loop(0, n_chunks // 2, pair, 0)
        put(n_chunks - 1, 1).wait()

    return gather(table, idx)


def _combine_kernel(x_ref, sc_ref, sh_ref, gate_ref, wts_ref, rows_ref,
                    wsg_ref, wsu_ref, wsd_ref, g_ref, b_ref, *out_refs):
    o_ref = out_refs[-1]
    x = x_ref[...]
    hb = (x * (1.0 + sc_ref[0]) + sh_ref[0]).astype(BF16)
    sg = jnp.dot(hb, wsg_ref[...], preferred_element_type=F32)
    su = jnp.dot(hb, wsu_ref[...], preferred_element_type=F32)
    shared = jnp.dot((_silu(sg) * su).astype(BF16), wsd_ref[...], preferred_element_type=F32)

    wts = wts_ref[...]
    half = shared.shape[1] // 2
    lo, hi = shared[:, :half], shared[:, half:]
    for j in range(TOP_K):
        y_lo, y_hi = _unpack_bf16_halves(rows_ref[j])
        lo = lo + wts[:, j:j + 1] * y_lo
        hi = hi + wts[:, j:j + 1] * y_hi
    z = ALPHA * x + gate_ref[0] * jnp.concatenate([lo, hi], axis=1)
    o_ref[...] = _layer_norm(z, g_ref[...], b_ref[...])


def _combine(x1, scale2, shift2, gate2, wts, rows, ws_gate, ws_up, ws_down, ln_g, ln_b, prev_out,
             *, seq, tc, first_token):
    n, d = x1.shape
    tps = seq // tc
    off = first_token // tc
    n_tiles = rows.shape[1] // tc
    mod_spec = pl.BlockSpec((1, 1, d), lambda i: ((i + off) // tps, 0, 0))
    full = lambda a: pl.BlockSpec(a.shape, lambda i: (0,) * a.ndim)
    args = [x1, scale2, shift2, gate2, wts, rows, ws_gate, ws_up, ws_down, ln_g, ln_b]
    in_specs = [pl.BlockSpec((tc, d), lambda i: (i + off, 0)), mod_spec, mod_spec, mod_spec,
                pl.BlockSpec((tc, TOP_K), lambda i: (i + off, 0)),
                pl.BlockSpec((TOP_K, tc, d // 2), lambda i: (0, i, 0)),
                full(ws_gate), full(ws_up), full(ws_down), full(ln_g), full(ln_b)]
    aliases = {}
    if prev_out is not None:
        aliases = {len(args): 0}
        args.append(prev_out)
        in_specs.append(pl.BlockSpec(memory_space=pl.ANY))
    return pl.pallas_call(
        _combine_kernel,
        grid=(n_tiles,),
        in_specs=in_specs,
        out_specs=pl.BlockSpec((tc, d), lambda i: (i + off, 0)),
        out_shape=jax.ShapeDtypeStruct((n, d), F32),
        input_output_aliases=aliases,
        compiler_params=_params(("arbitrary",)),
    )(*args)


def _layer(x, c, w_ada, b_ada, w_in, pool_w, pool_scale, lq1, lk1, lq2, lk2, subln_g, w_out,
           ln1_g, ln1_b, w_router, router_bias, w_gate, w_up, w_down, ws_gate, ws_up, ws_down,
           ln2_g, ln2_b, rel_table, *, tm=512, tr=256, tc=256):
    bsz, seq, d = x.shape
    n = bsz * seq
    x2 = x.reshape(n, d)
    row = lambda a: a.reshape(1, -1)

    mod = _modulation(c, w_ada, b_ada)
    shift1, scale1, gate1, shift2, scale2, gate2 = [
        mod[:, j * d:(j + 1) * d].reshape(bsz, 1, d) for j in range(6)]

    n_main = POOL_DIM + 2 * QK_DIM
    yp, q, k, vt = _inproj(x2, scale1, shift1, w_in[:, :n_main].astype(BF16), w_in[:, n_main:].T.astype(BF16),
                           pool_w.astype(BF16), row(pool_scale), seq=seq, tm=tm)
    bias_tiles = _bias_tiles(rel_table, seq // ATT_TILE)
    ya = _attention(q, k, vt, bias_tiles, row(lq1), row(lk1), row(lq2), row(lk2), subln_g.reshape(-1, 1),
                    bsz=bsz, seq=seq)
    x1 = _outproj(x2, yp, ya, w_out.astype(BF16), gate1, row(ln1_g), row(ln1_b), seq=seq, tm=tm)

    wr_t = w_router.T
    wr_hi = wr_t.astype(BF16)
    wr_lo = (wr_t - wr_hi.astype(F32)).astype(BF16)
    eidx_t, rank_t, wts_t, cnt, h2p = _route(x1, scale2, shift2, wr_hi, wr_lo, router_bias.reshape(-1, 1),
                                             seq=seq, tr=tr)

    m = DISPATCH_BLOCK
    counts = cnt[:, 0].astype(jnp.int32)
    padded = (counts + m - 1) // m * m
    pends = jnp.cumsum(padded)
    pstarts = pends - padded
    n_blocks = -(-(n * TOP_K + N_EXPERTS * (m - 1)) // m)
    n_used = (pends[-1:] // m).astype(jnp.int32)
    dest_t = _dest(eidx_t, rank_t, pstarts, tt=min(n, 2048))

    xs = _scatter_rows_sc(h2p, dest_t.reshape(TOP_K * n), n_blocks * m)
    ys = _experts(pstarts // m, padded // m, counts, n_used, xs, w_gate, w_up, w_down)

    shared_w = (ws_gate.astype(BF16), ws_up.astype(BF16), ws_down.astype(BF16))
    wts = wts_t.T
    part = n // COMBINE_PARTS
    out = None
    for p in range(COMBINE_PARTS):
        idx = dest_t[:, p * part:(p + 1) * part].reshape(TOP_K * part)
        picked = _gather_rows_sc(ys, idx).reshape(TOP_K, part, d // 2)
        out = _combine(x1, scale2, shift2, gate2, wts, picked, *shared_w, row(ln2_g), row(ln2_b), out,
                       seq=seq, tc=tc, first_token=p * part)
    return out.reshape(bsz, seq, d)


def kernel(x, c, w_ada, b_ada, w_in, pool_w, pool_scale, lambda_q1, lambda_k1, lambda_q2, lambda_k2,
           subln_g, w_out, ln1_g, ln1_b, w_router, router_bias, w_gate, w_up, w_down,
           ws_gate, ws_up, ws_down, ln2_g, ln2_b, rel_table):
    per_layer = (w_ada, b_ada, w_in, pool_w, pool_scale, lambda_q1, lambda_k1, lambda_q2, lambda_k2,
                 subln_g, w_out, ln1_g, ln1_b, w_router, router_bias, w_gate, w_up, w_down,
                 ws_gate, ws_up, ws_down, ln2_g, ln2_b)
    assert all(a.shape[0] == DEPTH == 1 for a in per_layer)
    return _layer(x, c, *[a.reshape(a.shape[1:]) for a in per_layer], rel_table)
```

```python
import functools
import math

import jax
import jax.numpy as jnp
from jax import lax
from jax.experimental import pallas as pl
from jax.experimental.pallas import tpu as pltpu
from jax.experimental.pallas import tpu_sc as plsc

F32 = jnp.float32
BF16 = jnp.bfloat16
U32 = jnp.uint32
LANES = 128

D_MODEL = 1024
CHUNK = 64
Q_BLOCK = 128
ATT_TILE = 256
POOL_DIM = 512
POOL_WINDOWS = (2, 4, 8, 16)
POOL_GROUP_DIM = 128
MAX_WINDOW = max(POOL_WINDOWS)
ATTN_HEADS = 4
ATTN_HEAD_DIM = 64
QK_DIM = 512
V_DIM = 512
IN_DIM = 2048
NUM_BUCKETS = 32
MAX_DISTANCE = 128
N_EXPERTS = 256
TOP_K = 8
N_GROUPS = 8
GROUP_SIZE = N_EXPERTS // N_GROUPS
TOP_K_GROUPS = 4
EXPERT_DIM = 256
ROUTED_SCALE = 2.5
DISPATCH_BLOCK = 512
EXPERT_SLOTS = 8
COMBINE_PARTS = 4
SC_CORES, SC_SUBCORES = 2, 16
SC_ROWS = 64
DEPTH = 1
ALPHA = (2.0 * DEPTH) ** 0.25
LN_EPS = 1e-5
LAMBDA_INIT = 0.8 - 0.6 * math.exp(-0.3 * 0)

VMEM_LIMIT = 48 * 1024 * 1024


def _sigmoid(x):
    return 1.0 / (1.0 + jnp.exp(-x))


def _silu(x):
    return x * _sigmoid(x)


def _layer_norm(z, g, b):
    mu = jnp.mean(z, axis=-1, keepdims=True)
    zc = z - mu
    var = jnp.mean(zc * zc, axis=-1, keepdims=True)
    return zc * lax.rsqrt(var + LN_EPS) * g + b


def _params(sem=None, flags=None):
    return pltpu.CompilerParams(dimension_semantics=sem, vmem_limit_bytes=VMEM_LIMIT, flags=flags)


def _mod_kernel(c_ref, w_ref, b_ref, o_ref):
    ca = _silu(c_ref[...])
    o_ref[...] = jnp.dot(ca, w_ref[...], preferred_element_type=F32,
                         precision=lax.Precision.HIGHEST) + b_ref[...]


def _modulation(c, w_ada, b_ada):
    bsz, d = c.shape
    n_out = w_ada.shape[1]
    return pl.pallas_call(
        _mod_kernel,
        grid=(n_out // d,),
        in_specs=[pl.BlockSpec((bsz, d), lambda j: (0, 0)),
                  pl.BlockSpec((d, d), lambda j: (0, j)),
                  pl.BlockSpec((1, d), lambda j: (0, j))],
        out_specs=pl.BlockSpec((bsz, d), lambda j: (0, j)),
        out_shape=jax.ShapeDtypeStruct((bsz, n_out), F32),
        compiler_params=_params(("arbitrary",)),
    )(c, w_ada, b_ada.reshape(1, n_out))


def _inproj_kernel(x_ref, sc_ref, sh_ref, w_ref, wvt_ref, pw_ref, ps_ref,
                   yp_ref, q_ref, k_ref, vt_ref, ext_ref, *, tm, seq):
    i = pl.program_id(0)
    tiles_per_seq = seq // tm
    it = i % tiles_per_seq
    h = x_ref[...] * (1.0 + sc_ref[0]) + sh_ref[0]
    hb = h.astype(BF16)
    proj = jnp.dot(hb, w_ref[...], preferred_element_type=F32)
    u = proj[:, :POOL_DIM]
    q_ref[...] = (proj[:, POOL_DIM:POOL_DIM + QK_DIM] * (ATTN_HEAD_DIM ** -0.5)).astype(BF16)
    k_ref[...] = proj[:, POOL_DIM + QK_DIM:POOL_DIM + 2 * QK_DIM].astype(BF16)
    vt = lax.dot_general(wvt_ref[...], hb, (((1,), (1,)), ((), ())), preferred_element_type=F32)
    for j in range(tm // ATT_TILE):
        vt_ref[0, j] = vt[:, j * ATT_TILE:(j + 1) * ATT_TILE].astype(BF16)

    @pl.when(it == 0)
    def _():
        ext_ref[0:MAX_WINDOW, :] = jnp.zeros((MAX_WINDOW, POOL_DIM), F32)

    ext_ref[MAX_WINDOW:MAX_WINDOW + tm, :] = u
    pos = (it * tm + lax.broadcasted_iota(jnp.int32, (tm, 1), 0) + 1).astype(F32)
    for g, w in enumerate(POOL_WINDOWS):
        c0, c1 = g * POOL_GROUP_DIM, (g + 1) * POOL_GROUP_DIM
        s = ext_ref[MAX_WINDOW:MAX_WINDOW + tm, c0:c1]
        for j in range(1, w):
            s = s + ext_ref[MAX_WINDOW - j:MAX_WINDOW - j + tm, c0:c1]
        pooled = s / jnp.minimum(pos, float(w)) - u[:, c0:c1]
        y = jnp.dot(pooled.astype(BF16), pw_ref[g], preferred_element_type=F32)
        yp_ref[:, c0:c1] = (y * ps_ref[:, c0:c1]).astype(BF16)
    ext_ref[0:MAX_WINDOW, :] = ext_ref[tm:tm + MAX_WINDOW, :]


def _inproj(x2, scale1, shift1, w_main, w_vt, pool_w, pool_scale, *, seq, tm):
    n, d = x2.shape
    assert n % tm == 0 and seq % tm == 0 and tm >= 2 * MAX_WINDOW and tm % ATT_TILE == 0
    tps = seq // tm
    tpt = tm // ATT_TILE
    mod_spec = pl.BlockSpec((1, 1, d), lambda i: (i // tps, 0, 0))
    row = lambda w: pl.BlockSpec((tm, w), lambda i: (i, 0))
    full = lambda a: pl.BlockSpec(a.shape, lambda i: (0,) * a.ndim)
    return pl.pallas_call(
        functools.partial(_inproj_kernel, tm=tm, seq=seq),
        grid=(n // tm,),
        in_specs=[row(d), mod_spec, mod_spec, full(w_main), full(w_vt), full(pool_w), full(pool_scale)],
        out_specs=[row(POOL_DIM), row(QK_DIM), row(QK_DIM),
                   pl.BlockSpec((1, tpt, V_DIM, ATT_TILE), lambda i: (i // tps, i % tps, 0, 0))],
        out_shape=[jax.ShapeDtypeStruct((n, POOL_DIM), BF16),
                   jax.ShapeDtypeStruct((n, QK_DIM), BF16),
                   jax.ShapeDtypeStruct((n, QK_DIM), BF16),
                   jax.ShapeDtypeStruct((n // seq, seq // ATT_TILE, V_DIM, ATT_TILE), BF16)],
        scratch_shapes=[pltpu.VMEM((tm + MAX_WINDOW, POOL_DIM), F32)],
        compiler_params=_params(("arbitrary",)),
    )(x2, scale1, shift1, w_main, w_vt, pool_w, pool_scale)


def _bias_kernel(tab_ref, o_ref):
    delta = pl.program_id(0)
    r = lax.broadcasted_iota(jnp.int32, (ATT_TILE, ATT_TILE), 0)
    c = lax.broadcasted_iota(jnp.int32, (ATT_TILE, ATT_TILE), 1)
    rel = r - c - delta * ATT_TILE
    half = NUM_BUCKETS // 2
    max_exact = half // 2
    ret = jnp.where(rel > 0, half, 0)
    n = jnp.abs(rel)
    nf = jnp.maximum(n, 1).astype(F32)
    large = max_exact + (jnp.log(nf / max_exact) / math.log(MAX_DISTANCE / max_exact)
                         * (half - max_exact)).astype(jnp.int32)
    large = jnp.minimum(large, half - 1)
    bucket = ret + jnp.where(n < max_exact, n, large)
    for h in range(ATTN_HEADS):
        acc = jnp.zeros((ATT_TILE, ATT_TILE), F32)
        for b in range(NUM_BUCKETS):
            acc = jnp.where(bucket == b, tab_ref[b, h], acc)
        o_ref[h, 0] = acc


def _bias_tiles(rel_table, n_tiles):
    return pl.pallas_call(
        _bias_kernel,
        grid=(n_tiles,),
        in_specs=[pl.BlockSpec(memory_space=pltpu.SMEM)],
        out_specs=pl.BlockSpec((ATTN_HEADS, 1, ATT_TILE, ATT_TILE), lambda dlt: (0, dlt, 0, 0)),
        out_shape=jax.ShapeDtypeStruct((ATTN_HEADS, n_tiles, ATT_TILE, ATT_TILE), F32),
        compiler_params=_params(("arbitrary",)),
    )(rel_table)


def _attn_kernel(q_ref, k_ref, vt_ref, bias_ref, lq1_ref, lk1_ref, lq2_ref, lk2_ref, g_ref, o_ref, *acc_refs):
    qt = pl.program_id(1)
    t = ATT_TILE
    n_maps = 2 * ATTN_HEADS
    lam = (jnp.exp(jnp.sum(lq1_ref[...] * lk1_ref[...], axis=-1, keepdims=True))
           - jnp.exp(jnp.sum(lq2_ref[...] * lk2_ref[...], axis=-1, keepdims=True))
           + LAMBDA_INIT)
    r = lax.broadcasted_iota(jnp.int32, (t, t), 0)
    c = lax.broadcasted_iota(jnp.int32, (t, t), 1)
    allowed = (r // CHUNK) <= (c // CHUNK)
    hd2 = 2 * ATTN_HEAD_DIM

    ahead = 2

    def scores(kt, hm):
        col = hm * ATTN_HEAD_DIM
        qh = q_ref[:, col:col + ATTN_HEAD_DIM]
        kh = k_ref[pl.ds(pl.multiple_of(kt * t, t), t), col:col + ATTN_HEAD_DIM]
        return lax.dot_general(kh, qh, (((1,), (1,)), ((), ())),
                               preferred_element_type=F32) + bias_ref[hm // 2, qt - kt]

    def block(kt, carry, diagonal):
        stats, early = carry[:2 * n_maps], carry[2 * n_maps:]

        def softmax(hm, s):
            if diagonal:
                s = jnp.where(allowed, s, -jnp.inf)
            m_old, l_old = stats[2 * hm:2 * hm + 2]
            m_new = jnp.maximum(m_old, jnp.max(s, axis=0, keepdims=True))
            alpha = jnp.exp(m_old - m_new)
            p = jnp.exp(s - m_new)
            return m_new, alpha * l_old + jnp.sum(p, axis=0, keepdims=True), alpha, p.astype(BF16)

        def accumulate(hm, alpha, p):
            h = hm // 2
            vth = vt_ref[0, kt, h * hd2:(h + 1) * hd2, :]
            acc_refs[hm][...] = alpha * acc_refs[hm][...] + jnp.dot(vth, p, preferred_element_type=F32)

        s_vals = dict(enumerate(early))
        sm_vals, out, nxt = {}, [None] * (2 * n_maps), []
        for step in range(1, n_maps + ahead):
            if ahead <= step < n_maps:
                s_vals[step] = scores(kt, step)
            elif step >= n_maps and not diagonal:
                nxt.append(scores(kt + 1, step - n_maps))
            hm = step - 1
            if hm < n_maps:
                m_new, l_new, alpha, p = softmax(hm, s_vals.pop(hm))
                out[2 * hm], out[2 * hm + 1] = m_new, l_new
                sm_vals[hm] = (alpha, p)
            if step - 2 >= 0:
                accumulate(step - 2, *sm_vals.pop(step - 2))
        return tuple(out) + tuple(nxt)

    for acc in acc_refs:
        acc[...] = jnp.zeros_like(acc)
    one = (jnp.full((1, t), -jnp.inf, F32), jnp.zeros((1, t), F32))
    first = tuple(scores(0, hm) for hm in range(ahead))
    carry = lax.fori_loop(0, qt, lambda kt, cr: block(kt, cr, False), one * n_maps + first)
    carry = block(qt, carry, True)
    for h in range(ATTN_HEADS):
        l0, l1 = carry[4 * h + 1], carry[4 * h + 3]
        o = acc_refs[2 * h][...] / l0 - lam * (acc_refs[2 * h + 1][...] / l1)
        y = o * lax.rsqrt(jnp.mean(o * o, axis=0, keepdims=True) + LN_EPS) * g_ref[...]
        o_ref[:, h * hd2:(h + 1) * hd2] = (y * (1.0 - LAMBDA_INIT)).T.astype(BF16)


def _attention(q, k, vt, bias_tiles, lq1, lk1, lq2, lk2, subln_g, *, bsz, seq):
    t = ATT_TILE
    nt = seq // t
    full = lambda a: pl.BlockSpec(a.shape, lambda b, j: (0,) * a.ndim)
    return pl.pallas_call(
        _attn_kernel,
        grid=(bsz, nt),
        in_specs=[pl.BlockSpec((t, QK_DIM), lambda b, j: (b * nt + j, 0)),
                  pl.BlockSpec((seq, QK_DIM), lambda b, j: (b, 0)),
                  pl.BlockSpec((1, nt, V_DIM, t), lambda b, j: (b, 0, 0, 0)),
                  full(bias_tiles), full(lq1), full(lk1), full(lq2), full(lk2), full(subln_g)],
        out_specs=pl.BlockSpec((t, V_DIM), lambda b, j: (b * nt + j, 0)),
        out_shape=jax.ShapeDtypeStruct((bsz * seq, V_DIM), BF16),
        scratch_shapes=[pltpu.VMEM((2 * ATTN_HEAD_DIM, t), F32) for _ in range(2 * ATTN_HEADS)],
        compiler_params=_params(("arbitrary", "arbitrary")),
    )(q, k, vt, bias_tiles, lq1, lk1, lq2, lk2, subln_g)


def _outproj_kernel(x_ref, yp_ref, ya_ref, w_ref, gate_ref, g_ref, b_ref, o_ref):
    mix = (jnp.dot(yp_ref[...], w_ref[0:POOL_DIM, :], preferred_element_type=F32)
           + jnp.dot(ya_ref[...], w_ref[POOL_DIM:, :], preferred_element_type=F32))
    z = ALPHA * x_ref[...] + gate_ref[0] * mix
    o_ref[...] = _layer_norm(z, g_ref[...], b_ref[...])


def _outproj(x2, yp, ya, w_out, gate1, ln_g, ln_b, *, seq, tm):
    n, d = x2.shape
    tps = seq // tm
    row = lambda w: pl.BlockSpec((tm, w), lambda i: (i, 0))
    full = lambda a: pl.BlockSpec(a.shape, lambda i: (0,) * a.ndim)
    return pl.pallas_call(
        _outproj_kernel,
        grid=(n // tm,),
        in_specs=[row(d), row(POOL_DIM), row(V_DIM), full(w_out),
                  pl.BlockSpec((1, 1, d), lambda i: (i // tps, 0, 0)), full(ln_g), full(ln_b)],
        out_specs=row(d),
        out_shape=jax.ShapeDtypeStruct((n, d), F32),
        compiler_params=_params(("arbitrary",)),
    )(x2, yp, ya, w_out, gate1, ln_g, ln_b)


def _route_kernel(x_ref, sc_ref, sh_ref, whi_ref, wlo_ref, rb_ref,
                  eidx_ref, rank_ref, wts_ref, cnt_ref, h2p_ref, carry_ref, *, tr):
    i = pl.program_id(0)

    @pl.when(i == 0)
    def _():
        carry_ref[...] = jnp.zeros_like(carry_ref)

    h2 = x_ref[...] * (1.0 + sc_ref[0]) + sh_ref[0]
    h2p_ref[...] = _pack_bf16_halves(h2)
    hi = h2.astype(BF16)
    lo = (h2 - hi.astype(F32)).astype(BF16)
    nt = (((1,), (1,)), ((), ()))
    logits = (lax.dot_general(whi_ref[...], hi, nt, preferred_element_type=F32)
              + lax.dot_general(wlo_ref[...], hi, nt, preferred_element_type=F32)
              + lax.dot_general(whi_ref[...], lo, nt, preferred_element_type=F32))
    scores = _sigmoid(logits)
    sel = scores + rb_ref[...]
    erow = lax.broadcasted_iota(jnp.int32, (N_EXPERTS, tr), 0).astype(F32)

    g3 = sel.reshape(N_GROUPS, GROUP_SIZE, tr)
    r3 = lax.broadcasted_iota(jnp.int32, (N_GROUPS, GROUP_SIZE, tr), 1).astype(F32)
    m1 = jnp.max(g3, axis=1, keepdims=True)
    first = jnp.min(jnp.where(g3 == m1, r3, float(GROUP_SIZE)), axis=1, keepdims=True)
    m2 = jnp.max(jnp.where(r3 == first, -jnp.inf, g3), axis=1, keepdims=True)
    gscore = (m1 + m2).reshape(N_GROUPS, tr)

    gidx = lax.broadcasted_iota(jnp.int32, (N_GROUPS, tr), 0)
    beaten_by = jnp.zeros((N_GROUPS, tr), jnp.int32)
    for g in range(N_GROUPS):
        other = gscore[g:g + 1, :]
        wins = (other > gscore) | ((other == gscore) & (g < gidx))
        beaten_by = beaten_by + wins.astype(jnp.int32)
    dropped = jnp.where(beaten_by < TOP_K_GROUPS, 0.0, -jnp.inf)
    cur = (g3 + dropped.reshape(N_GROUPS, 1, tr)).reshape(N_EXPERTS, tr)

    picks, weights = [], []
    selmask = jnp.zeros((N_EXPERTS, tr), F32)
    for _ in range(TOP_K):
        mx = jnp.max(cur, axis=0, keepdims=True)
        pick = jnp.min(jnp.where(cur == mx, erow, float(N_EXPERTS)), axis=0, keepdims=True)
        onehot = erow == pick
        weights.append(jnp.sum(jnp.where(onehot, scores, 0.0), axis=0, keepdims=True))
        cur = jnp.where(onehot, -jnp.inf, cur)
        selmask = jnp.where(onehot, 1.0, selmask)
        picks.append(pick)

    t_from = lax.broadcasted_iota(jnp.int32, (tr, tr), 0)
    t_to = lax.broadcasted_iota(jnp.int32, (tr, tr), 1)
    earlier = jnp.where(t_from < t_to, 1.0, 0.0).astype(BF16)
    chosen = selmask.astype(BF16)
    carry = carry_ref[...]
    rankmat = (jnp.dot(chosen, earlier, preferred_element_type=F32)
               + jnp.concatenate([carry] * (tr // LANES), axis=1))
    carry_ref[...] = carry + jnp.dot(chosen, jnp.ones((tr, LANES), BF16), preferred_element_type=F32)
    cnt_ref[...] = carry_ref[...]

    wsum = weights[0]
    for wj in weights[1:]:
        wsum = wsum + wj
    row8 = lax.broadcasted_iota(jnp.int32, (TOP_K, tr), 0)
    eidx = jnp.zeros((TOP_K, tr), jnp.int32)
    rank = jnp.zeros((TOP_K, tr), jnp.int32)
    wts = jnp.zeros((TOP_K, tr), F32)
    for j in range(TOP_K):
        rk = jnp.sum(jnp.where(erow == picks[j], rankmat, 0.0), axis=0, keepdims=True)
        eidx = jnp.where(row8 == j, picks[j].astype(jnp.int32), eidx)
        rank = jnp.where(row8 == j, rk.astype(jnp.int32), rank)
        wts = jnp.where(row8 == j, weights[j] / wsum * ROUTED_SCALE, wts)
    eidx_ref[...] = eidx
    rank_ref[...] = rank
    wts_ref[...] = wts


def _route(x1, scale2, shift2, wr_hi, wr_lo, router_bias, *, seq, tr):
    n, d = x1.shape
    tps = seq // tr
    mod_spec = pl.BlockSpec((1, 1, d), lambda i: (i // tps, 0, 0))
    full = lambda a: pl.BlockSpec(a.shape, lambda i: (0,) * a.ndim)
    assert tr % LANES == 0
    k8 = pl.BlockSpec((TOP_K, tr), lambda i: (0, i))
    return pl.pallas_call(
        functools.partial(_route_kernel, tr=tr),
        grid=(n // tr,),
        in_specs=[pl.BlockSpec((tr, d), lambda i: (i, 0)), mod_spec, mod_spec,
                  full(wr_hi), full(wr_lo), full(router_bias)],
        out_specs=[k8, k8, k8, pl.BlockSpec((N_EXPERTS, LANES), lambda i: (0, 0)),
                   pl.BlockSpec((tr, d // 2), lambda i: (i, 0))],
        out_shape=[jax.ShapeDtypeStruct((TOP_K, n), jnp.int32),
                   jax.ShapeDtypeStruct((TOP_K, n), jnp.int32),
                   jax.ShapeDtypeStruct((TOP_K, n), F32),
                   jax.ShapeDtypeStruct((N_EXPERTS, LANES), F32),
                   jax.ShapeDtypeStruct((n, d // 2), U32)],
        scratch_shapes=[pltpu.VMEM((N_EXPERTS, LANES), F32)],
        compiler_params=_params(("arbitrary",)),
    )(x1, scale2, shift2, wr_hi, wr_lo, router_bias)


def _pack_bf16_halves(x):
    w = x.shape[1] // 2
    lo = pltpu.bitcast(x[:, :w].astype(BF16).astype(F32), U32) >> 16
    hi = pltpu.bitcast(x[:, w:].astype(BF16).astype(F32), U32) & jnp.uint32(0xFFFF0000)
    return lo | hi


def _unpack_bf16_halves(p):
    return pltpu.bitcast(p << 16, F32), pltpu.bitcast(p & jnp.uint32(0xFFFF0000), F32)


def _sc_worker():
    return lax.axis_index("subcore") * SC_CORES + lax.axis_index("core")


def _scatter_rows_sc(rows, idx, n_slots):
    n, width = rows.shape
    k = idx.shape[0] // n
    n_workers = SC_CORES * SC_SUBCORES
    w = SC_ROWS
    per_worker = n // n_workers
    n_chunks = per_worker // w
    assert n == n_workers * n_chunks * w and n_chunks % 2 == 0 and idx.shape[0] == k * n
    mesh = plsc.VectorSubcoreMesh(core_axis_name="core", subcore_axis_name="subcore")

    @functools.partial(
        pl.kernel, mesh=mesh,
        out_type=jax.ShapeDtypeStruct((n_slots, width), rows.dtype),
        scratch_types=[pltpu.VMEM((k * per_worker,), jnp.int32), pltpu.VMEM((2, w, width), rows.dtype),
                       pltpu.SemaphoreType.DMA((2,)), pltpu.SemaphoreType.DMA((2,))])
    def scatter(rows_hbm, idx_hbm, out_hbm, idx_v, rows_v, in_sem, out_sem):
        t0 = _sc_worker() * per_worker
        for j in range(k):
            pltpu.sync_copy(idx_hbm.at[pl.ds(j * n + t0, per_worker)], idx_v.at[pl.ds(j * per_worker, per_worker)])

        def load(i, s):
            return pltpu.make_async_copy(rows_hbm.at[pl.ds(t0 + i * w, w)], rows_v.at[s], in_sem.at[s])

        def send(i, s, j):
            slots = idx_v.at[pl.ds(j * per_worker + i * w, w)]
            return pltpu.make_async_copy(rows_v.at[s], out_hbm.at[slots], out_sem.at[s])

        load(0, 0).start()

        def pair(ii, carry):
            for s in range(2):
                i = ii * 2 + s
                load(i, s).wait()

                @pl.when(i >= 1)
                def _():
                    for j in range(k):
                        send(i - 1, 1 - s, j).wait()

                @pl.when(i + 1 < n_chunks)
                def _():
                    load(i + 1, 1 - s).start()

                for j in range(k):
                    send(i, s, j).start()
            return carry

        lax.fori_loop(0, n_chunks // 2, pair, 0)
        for j in range(k):
            send(n_chunks - 1, 1, j).wait()

    return scatter(rows, idx)


def _dest_kernel(eidx_ref, rank_ref, first_ref, dest_ref):
    tt = eidx_ref.shape[1]
    pieces = [jnp.broadcast_to(first_ref[:, p * LANES:(p + 1) * LANES], (TOP_K, LANES))
              for p in range(N_EXPERTS // LANES)]
    for c in range(tt // LANES):
        cols = slice(c * LANES, (c + 1) * LANES)
        e = eidx_ref[:, cols]
        within = e & (LANES - 1)
        start = jnp.take_along_axis(pieces[0], within, axis=1)
        for p in range(1, len(pieces)):
            start = jnp.where(e // LANES == p, jnp.take_along_axis(pieces[p], within, axis=1), start)
        dest_ref[:, cols] = start + rank_ref[:, cols]


def _dest(eidx_t, rank_t, pstarts, *, tt):
    n = eidx_t.shape[1]
    first = pstarts.reshape(1, N_EXPERTS)
    k8 = pl.BlockSpec((TOP_K, tt), lambda i: (0, i))
    return pl.pallas_call(
        _dest_kernel,
        grid=(n // tt,),
        in_specs=[k8, k8, pl.BlockSpec(first.shape, lambda i: (0, 0))],
        out_specs=k8,
        out_shape=jax.ShapeDtypeStruct((TOP_K, n), jnp.int32),
        compiler_params=_params(("arbitrary",)),
    )(eidx_t, rank_t, first)


def _expert_kernel(fb_ref, nb_ref, cnt_ref, nused_ref, wg_ref, wu_ref, wd_ref, xs_ref, ys_ref,
                   wgb, wub, wdb, xbuf, ybuf, in_sem, out_sem):
    e = pl.program_id(0)
    m = DISPATCH_BLOCK
    ns = EXPERT_SLOTS
    n_used = nused_ref[0]

    def rows(g):
        return pl.ds(pl.multiple_of(g * m, m), m)

    def fetch(g):
        slot = g & (ns - 1)
        return pltpu.make_async_copy(xs_ref.at[rows(g), :], xbuf.at[slot], in_sem.at[slot])

    def put(g):
        slot = g & (ns - 1)
        return pltpu.make_async_copy(ybuf.at[slot], ys_ref.at[rows(g), :], out_sem.at[slot])

    @pl.when(e == 0)
    def _():
        for g0 in range(ns - 1):
            @pl.when(g0 < n_used)
            def _(g0=g0):
                fetch(g0).start()

    wgb[...] = wg_ref[0].astype(BF16)
    wub[...] = wu_ref[0].astype(BF16)
    wdb[...] = wd_ref[0].astype(BF16)

    def block(i, carry):
        g = fb_ref[e] + i
        slot = g & (ns - 1)
        fetch(g).wait()

        @pl.when(g + ns - 1 < n_used)
        def _():
            fetch(g + ns - 1).start()

        @pl.when(g >= ns)
        def _():
            put(g - ns).wait()

        row = lax.broadcasted_iota(jnp.int32, (m, 1), 0)
        packed = jnp.where(row < cnt_ref[e] - i * m, xbuf[slot], jnp.uint32(0))
        x_lo, x_hi = [h.astype(BF16) for h in _unpack_bf16_halves(packed)]
        half = x_lo.shape[1]

        def up_proj(w):
            return (jnp.dot(x_lo, w[0:half, :], preferred_element_type=F32)
                    + jnp.dot(x_hi, w[half:, :], preferred_element_type=F32))

        a = (_silu(up_proj(wgb)) * up_proj(wub)).astype(BF16)
        ybuf[slot] = _pack_bf16_halves(jnp.dot(a, wdb[...], preferred_element_type=F32))
        put(g).start()
        return carry

    lax.fori_loop(0, nb_ref[e], block, 0)

    @pl.when(e == pl.num_programs(0) - 1)
    def _():
        for back in range(ns, 0, -1):
            @pl.when(n_used >= back)
            def _(back=back):
                put(n_used - back).wait()

        ybuf[0] = jnp.zeros(ybuf.shape[1:], ybuf.dtype)
        n_blocks = ys_ref.shape[0] // m

        def tail(wait):
            def body(g, carry):
                cp = pltpu.make_async_copy(ybuf.at[0], ys_ref.at[rows(g), :], out_sem.at[0])
                cp.wait() if wait else cp.start()
                return carry
            return body

        lax.fori_loop(n_used, n_blocks, tail(False), 0)
        lax.fori_loop(n_used, n_blocks, tail(True), 0)


def _experts(first_block, n_blocks_e, counts, n_used, xs, w_gate, w_up, w_down):
    p, dp = xs.shape
    m = DISPATCH_BLOCK
    n_e, d, f = w_gate.shape
    assert d == 2 * dp and p % m == 0
    grid_spec = pltpu.PrefetchScalarGridSpec(
        num_scalar_prefetch=4,
        grid=(n_e,),
        in_specs=[pl.BlockSpec((1, d, f), lambda e, *_: (e, 0, 0)),
                  pl.BlockSpec((1, d, f), lambda e, *_: (e, 0, 0)),
                  pl.BlockSpec((1, f, d), lambda e, *_: (e, 0, 0)),
                  pl.BlockSpec(memory_space=pl.ANY)],
        out_specs=pl.BlockSpec(memory_space=pl.ANY),
        scratch_shapes=[pltpu.VMEM((d, f), BF16), pltpu.VMEM((d, f), BF16), pltpu.VMEM((f, d), BF16),
                        pltpu.VMEM((EXPERT_SLOTS, m, dp), U32), pltpu.VMEM((EXPERT_SLOTS, m, dp), U32),
                        pltpu.SemaphoreType.DMA((EXPERT_SLOTS,)), pltpu.SemaphoreType.DMA((EXPERT_SLOTS,))],
    )
    return pl.pallas_call(
        _expert_kernel,
        grid_spec=grid_spec,
        out_shape=jax.ShapeDtypeStruct((p, dp), U32),
        compiler_params=_params(("arbitrary",)),
    )(first_block, n_blocks_e, counts, n_used, w_gate, w_up, w_down, xs)


def _gather_rows_sc(table, idx):
    n_idx, (_, width) = idx.shape[0], table.shape
    n_workers = SC_CORES * SC_SUBCORES
    w = SC_ROWS
    per_worker = n_idx // n_workers
    n_chunks = per_worker // w
    assert n_idx == n_workers * n_chunks * w and n_chunks % 2 == 0
    mesh = plsc.VectorSubcoreMesh(core_axis_name="core", subcore_axis_name="subcore")

    @functools.partial(
        pl.kernel, mesh=mesh,
        out_type=jax.ShapeDtypeStruct((n_idx, width), table.dtype),
        scratch_types=[pltpu.VMEM((per_worker,), jnp.int32), pltpu.VMEM((2, w, width), table.dtype),
                       pltpu.SemaphoreType.DMA((2,)), pltpu.SemaphoreType.DMA((2,))])
    def gather(table_hbm, idx_hbm, out_hbm, idx_v, rows_v, in_sem, out_sem):
        base = _sc_worker() * per_worker
        pltpu.sync_copy(idx_hbm.at[pl.ds(base, per_worker)], idx_v)

        def fetch(i, s):
            return pltpu.make_async_copy(table_hbm.at[idx_v.at[pl.ds(i * w, w)]], rows_v.at[s], in_sem.at[s])

        def put(i, s):
            return pltpu.make_async_copy(rows_v.at[s], out_hbm.at[pl.ds(base + i * w, w)], out_sem.at[s])

        fetch(0, 0).start()

        def pair(ii, carry):
            for s in range(2):
                i = ii * 2 + s
                fetch(i, s).wait()

                @pl.when(i >= 1)
                def _():
                    put(i - 1, 1 - s).wait()

                @pl.when(i + 1 < n_chunks)
                def _():
                    fetch(i + 1, 1 - s).start()

                put(i, s).start()
            return carry

        lax.fori_loop(0, n_chunks // 2, pair, 0)
        put(n_chunks - 1, 1).wait()

    return gather(table, idx)


def _combine_kernel(x_ref, sc_ref, sh_ref, gate_ref, wts_ref, rows_ref,
                    wsg_ref, wsu_ref, wsd_ref, g_ref, b_ref, *out_refs):
    o_ref = out_refs[-1]
    x = x_ref[...]
    hb = (x * (1.0 + sc_ref[0]) + sh_ref[0]).astype(BF16)
    sg = jnp.dot(hb, wsg_ref[...], preferred_element_type=F32)
    su = jnp.dot(hb, wsu_ref[...], preferred_element_type=F32)
    shared = jnp.dot((_silu(sg) * su).astype(BF16), wsd_ref[...], preferred_element_type=F32)

    wts = wts_ref[...]
    half = shared.shape[1] // 2
    lo, hi = shared[:, :half], shared[:, half:]
    for j in range(TOP_K):
        y_lo, y_hi = _unpack_bf16_halves(rows_ref[j])
        lo = lo + wts[:, j:j + 1] * y_lo
        hi = hi + wts[:, j:j + 1] * y_hi
    z = ALPHA * x + gate_ref[0] * jnp.concatenate([lo, hi], axis=1)
    o_ref[...] = _layer_norm(z, g_ref[...], b_ref[...])


def _combine(x1, scale2, shift2, gate2, wts, rows, ws_gate, ws_up, ws_down, ln_g, ln_b, prev_out,
             *, seq, tc, first_token):
    n, d = x1.shape
    tps = seq // tc
    off = first_token // tc
    n_tiles = rows.shape[1] // tc
    mod_spec = pl.BlockSpec((1, 1, d), lambda i: ((i + off) // tps, 0, 0))
    full = lambda a: pl.BlockSpec(a.shape, lambda i: (0,) * a.ndim)
    args = [x1, scale2, shift2, gate2, wts, rows, ws_gate, ws_up, ws_down, ln_g, ln_b]
    in_specs = [pl.BlockSpec((tc, d), lambda i: (i + off, 0)), mod_spec, mod_spec, mod_spec,
                pl.BlockSpec((tc, TOP_K), lambda i: (i + off, 0)),
                pl.BlockSpec((TOP_K, tc, d // 2), lambda i: (0, i, 0)),
                full(ws_gate), full(ws_up), full(ws_down), full(ln_g), full(ln_b)]
    aliases = {}
    if prev_out is not None:
        aliases = {len(args): 0}
        args.append(prev_out)
        in_specs.append(pl.BlockSpec(memory_space=pl.ANY))
    return pl.pallas_call(
        _combine_kernel,
        grid=(n_tiles,),
        in_specs=in_specs,
        out_specs=pl.BlockSpec((tc, d), lambda i: (i + off, 0)),
        out_shape=jax.ShapeDtypeStruct((n, d), F32),
        input_output_aliases=aliases,
        compiler_params=_params(("arbitrary",)),
    )(*args)


def _layer(x, c, w_ada, b_ada, w_in, pool_w, pool_scale, lq1, lk1, lq2, lk2, subln_g, w_out,
           ln1_g, ln1_b, w_router, router_bias, w_gate, w_up, w_down, ws_gate, ws_up, ws_down,
           ln2_g, ln2_b, rel_table, *, tm=1024, tr=256, tc=512):
    bsz, seq, d = x.shape
    n = bsz * seq
    x2 = x.reshape(n, d)
    row = lambda a: a.reshape(1, -1)

    mod = _modulation(c, w_ada, b_ada)
    shift1, scale1, gate1, shift2, scale2, gate2 = [
        mod[:, j * d:(j + 1) * d].reshape(bsz, 1, d) for j in range(6)]

    n_main = POOL_DIM + 2 * QK_DIM
    yp, q, k, vt = _inproj(x2, scale1, shift1, w_in[:, :n_main].astype(BF16), w_in[:, n_main:].T.astype(BF16),
                           pool_w.astype(BF16), row(pool_scale), seq=seq, tm=tm)
    bias_tiles = _bias_tiles(rel_table, seq // ATT_TILE)
    ya = _attention(q, k, vt, bias_tiles, row(lq1), row(lk1), row(lq2), row(lk2), subln_g.reshape(-1, 1),
                    bsz=bsz, seq=seq)
    x1 = _outproj(x2, yp, ya, w_out.astype(BF16), gate1, row(ln1_g), row(ln1_b), seq=seq, tm=tm)

    wr_t = w_router.T
    wr_hi = wr_t.astype(BF16)
    wr_lo = (wr_t - wr_hi.astype(F32)).astype(BF16)
    eidx_t, rank_t, wts_t, cnt, h2p = _route(x1, scale2, shift2, wr_hi, wr_lo, router_bias.reshape(-1, 1),
                                             seq=seq, tr=tr)

    m = DISPATCH_BLOCK
    counts = cnt[:, 0].astype(jnp.int32)
    padded = (counts + m - 1) // m * m
    pends = jnp.cumsum(padded)
    pstarts = pends - padded
    n_blocks = -(-(n * TOP_K + N_EXPERTS * (m - 1)) // m)
    n_used = (pends[-1:] // m).astype(jnp.int32)
    dest_t = _dest(eidx_t, rank_t, pstarts, tt=min(n, 2048))

    xs = _scatter_rows_sc(h2p, dest_t.reshape(TOP_K * n), n_blocks * m)
    ys = _experts(pstarts // m, padded // m, counts, n_used, xs, w_gate, w_up, w_down)

    shared_w = (ws_gate.astype(BF16), ws_up.astype(BF16), ws_down.astype(BF16))
    wts = wts_t.T
    part = n // COMBINE_PARTS
    out = None
    for p in range(COMBINE_PARTS):
        idx = dest_t[:, p * part:(p + 1) * part].reshape(TOP_K * part)
        picked = _gather_rows_sc(ys, idx).reshape(TOP_K, part, d // 2)
        out = _combine(x1, scale2, shift2, gate2, wts, picked, *shared_w, row(ln2_g), row(ln2_b), out,
                       seq=seq, tc=tc, first_token=p * part)
    return out.reshape(bsz, seq, d)


def kernel(x, c, w_ada, b_ada, w_in, pool_w, pool_scale, lambda_q1, lambda_k1, lambda_q2, lambda_k2,
           subln_g, w_out, ln1_g, ln1_b, w_router, router_bias, w_gate, w_up, w_down,
           ws_gate, ws_up, ws_down, ln2_g, ln2_b, rel_table):
    per_layer = (w_ada, b_ada, w_in, pool_w, pool_scale, lambda_q1, lambda_k1, lambda_q2, lambda_k2,
                 subln_g, w_out, ln1_g, ln1_b, w_router, router_bias, w_gate, w_up, w_down,
                 ws_gate, ws_up, ws_down, ln2_g, ln2_b)
    assert all(a.shape[0] == DEPTH == 1 for a in per_layer)
    return _layer(x, c, *[a.reshape(a.shape[1:]) for a in per_layer], rel_table)
```

```python
import functools
import math

import jax
import jax.numpy as jnp
from jax import lax
from jax.experimental import pallas as pl
from jax.experimental.pallas import tpu as pltpu
from jax.experimental.pallas import tpu_sc as plsc

F32 = jnp.float32
BF16 = jnp.bfloat16
U32 = jnp.uint32
LANES = 128

CHUNK = 64
ATT_TILE = 256
LOG2_E = math.log2(math.e)
POOL_DIM = 512
POOL_WINDOWS = (2, 4, 8, 16)
POOL_GROUP_DIM = 128
MAX_WINDOW = max(POOL_WINDOWS)
ATTN_HEADS = 4
ATTN_HEAD_DIM = 64
QK_DIM = 512
V_DIM = 512
NUM_BUCKETS = 32
MAX_DISTANCE = 128
N_EXPERTS = 256
TOP_K = 8
N_GROUPS = 8
GROUP_SIZE = N_EXPERTS // N_GROUPS
TOP_K_GROUPS = 4
ROUTED_SCALE = 2.5
DISPATCH_BLOCK = 512
EXPERT_SLOTS = 8
COMBINE_PARTS = 8
SC_CORES, SC_SUBCORES = 2, 16
SC_ROWS = 64
DEPTH = 1
ALPHA = (2.0 * DEPTH) ** 0.25
LN_EPS = 1e-5
LAMBDA_INIT = 0.8 - 0.6 * math.exp(-0.3 * 0)

VMEM_LIMIT = 48 * 1024 * 1024


def _sigmoid(x):
    return 1.0 / (1.0 + jnp.exp(-x))


def _silu(x):
    return x * _sigmoid(x)


def _layer_norm(z, g, b):
    mu = jnp.mean(z, axis=-1, keepdims=True)
    zc = z - mu
    var = jnp.mean(zc * zc, axis=-1, keepdims=True)
    return zc * lax.rsqrt(var + LN_EPS) * g + b


def _params(sem=None):
    return pltpu.CompilerParams(dimension_semantics=sem, vmem_limit_bytes=VMEM_LIMIT)


def _mod_kernel(c_ref, w_ref, b_ref, o_ref):
    ca = _silu(c_ref[...])
    o_ref[...] = jnp.dot(ca, w_ref[...], preferred_element_type=F32,
                         precision=lax.Precision.HIGHEST) + b_ref[...]


def _modulation(c, w_ada, b_ada):
    bsz, d = c.shape
    n_out = w_ada.shape[1]
    return pl.pallas_call(
        _mod_kernel,
        grid=(n_out // d,),
        in_specs=[pl.BlockSpec((bsz, d), lambda j: (0, 0)),
                  pl.BlockSpec((d, d), lambda j: (0, j)),
                  pl.BlockSpec((1, d), lambda j: (0, j))],
        out_specs=pl.BlockSpec((bsz, d), lambda j: (0, j)),
        out_shape=jax.ShapeDtypeStruct((bsz, n_out), F32),
        compiler_params=_params(("arbitrary",)),
    )(c, w_ada, b_ada.reshape(1, n_out))


def _inproj_kernel(x_ref, sc_ref, sh_ref, w_ref, wvt_ref, pw_ref, ps_ref,
                   yp_ref, q_ref, k_ref, vt_ref, ext_ref, *, tm, seq):
    i = pl.program_id(0)
    tiles_per_seq = seq // tm
    it = i % tiles_per_seq
    h = x_ref[...] * (1.0 + sc_ref[0]) + sh_ref[0]
    hb = h.astype(BF16)
    proj = jnp.dot(hb, w_ref[...], preferred_element_type=F32)
    u = proj[:, :POOL_DIM]
    q_ref[...] = (proj[:, POOL_DIM:POOL_DIM + QK_DIM] * (ATTN_HEAD_DIM ** -0.5 * LOG2_E)).astype(BF16)
    k_ref[...] = proj[:, POOL_DIM + QK_DIM:POOL_DIM + 2 * QK_DIM].astype(BF16)
    vt = lax.dot_general(wvt_ref[...], hb, (((1,), (1,)), ((), ())), preferred_element_type=F32)
    for j in range(tm // ATT_TILE):
        vt_ref[0, j] = vt[:, j * ATT_TILE:(j + 1) * ATT_TILE].astype(BF16)

    @pl.when(it == 0)
    def _():
        ext_ref[0:MAX_WINDOW, :] = jnp.zeros((MAX_WINDOW, POOL_DIM), F32)

    ext_ref[MAX_WINDOW:MAX_WINDOW + tm, :] = u
    pos = (it * tm + lax.broadcasted_iota(jnp.int32, (tm, 1), 0) + 1).astype(F32)
    for g, w in enumerate(POOL_WINDOWS):
        c0, c1 = g * POOL_GROUP_DIM, (g + 1) * POOL_GROUP_DIM
        s = ext_ref[MAX_WINDOW:MAX_WINDOW + tm, c0:c1]
        for j in range(1, w):
            s = s + ext_ref[MAX_WINDOW - j:MAX_WINDOW - j + tm, c0:c1]
        pooled = s / jnp.minimum(pos, float(w)) - u[:, c0:c1]
        y = jnp.dot(pooled.astype(BF16), pw_ref[g], preferred_element_type=F32)
        yp_ref[:, c0:c1] = (y * ps_ref[:, c0:c1]).astype(BF16)
    ext_ref[0:MAX_WINDOW, :] = ext_ref[tm:tm + MAX_WINDOW, :]


def _inproj(x2, scale1, shift1, w_main, w_vt, pool_w, pool_scale, *, seq, tm):
    n, d = x2.shape
    assert n % tm == 0 and seq % tm == 0 and tm >= 2 * MAX_WINDOW and tm % ATT_TILE == 0
    tps = seq // tm
    tpt = tm // ATT_TILE
    mod_spec = pl.BlockSpec((1, 1, d), lambda i: (i // tps, 0, 0))
    row = lambda w: pl.BlockSpec((tm, w), lambda i: (i, 0))
    full = lambda a: pl.BlockSpec(a.shape, lambda i: (0,) * a.ndim)
    return pl.pallas_call(
        functools.partial(_inproj_kernel, tm=tm, seq=seq),
        grid=(n // tm,),
        in_specs=[row(d), mod_spec, mod_spec, full(w_main), full(w_vt), full(pool_w), full(pool_scale)],
        out_specs=[row(POOL_DIM), row(QK_DIM), row(QK_DIM),
                   pl.BlockSpec((1, tpt, V_DIM, ATT_TILE), lambda i: (i // tps, i % tps, 0, 0))],
        out_shape=[jax.ShapeDtypeStruct((n, POOL_DIM), BF16),
                   jax.ShapeDtypeStruct((n, QK_DIM), BF16),
                   jax.ShapeDtypeStruct((n, QK_DIM), BF16),
                   jax.ShapeDtypeStruct((n // seq, seq // ATT_TILE, V_DIM, ATT_TILE), BF16)],
        scratch_shapes=[pltpu.VMEM((tm + MAX_WINDOW, POOL_DIM), F32)],
        compiler_params=_params(("arbitrary",)),
    )(x2, scale1, shift1, w_main, w_vt, pool_w, pool_scale)


def _bias_kernel(tab_ref, o_ref):
    delta = pl.program_id(0)
    r = lax.broadcasted_iota(jnp.int32, (ATT_TILE, ATT_TILE), 0)
    c = lax.broadcasted_iota(jnp.int32, (ATT_TILE, ATT_TILE), 1)
    rel = r - c - delta * ATT_TILE
    half = NUM_BUCKETS // 2
    max_exact = half // 2
    ret = jnp.where(rel > 0, half, 0)
    n = jnp.abs(rel)
    nf = jnp.maximum(n, 1).astype(F32)
    large = max_exact + (jnp.log(nf / max_exact) / math.log(MAX_DISTANCE / max_exact)
                         * (half - max_exact)).astype(jnp.int32)
    large = jnp.minimum(large, half - 1)
    bucket = ret + jnp.where(n < max_exact, n, large)
    for h in range(ATTN_HEADS):
        acc = jnp.zeros((ATT_TILE, ATT_TILE), F32)
        for b in range(NUM_BUCKETS):
            acc = jnp.where(bucket == b, tab_ref[b, h], acc)
        o_ref[h, 0] = acc * LOG2_E


def _bias_tiles(rel_table, n_tiles):
    return pl.pallas_call(
        _bias_kernel,
        grid=(n_tiles,),
        in_specs=[pl.BlockSpec(memory_space=pltpu.SMEM)],
        out_specs=pl.BlockSpec((ATTN_HEADS, 1, ATT_TILE, ATT_TILE), lambda dlt: (0, dlt, 0, 0)),
        out_shape=jax.ShapeDtypeStruct((ATTN_HEADS, n_tiles, ATT_TILE, ATT_TILE), F32),
        compiler_params=_params(("arbitrary",)),
    )(rel_table)


def _attn_kernel(q_ref, k_ref, vt_ref, bias_ref, lq1_ref, lk1_ref, lq2_ref, lk2_ref, g_ref, o_ref, *acc_refs):
    qt = pl.program_id(1)
    t = ATT_TILE
    n_maps = 2 * ATTN_HEADS
    lam = (jnp.exp(jnp.sum(lq1_ref[...] * lk1_ref[...], axis=-1, keepdims=True))
           - jnp.exp(jnp.sum(lq2_ref[...] * lk2_ref[...], axis=-1, keepdims=True))
           + LAMBDA_INIT)
    r = lax.broadcasted_iota(jnp.int32, (t, t), 0)
    c = lax.broadcasted_iota(jnp.int32, (t, t), 1)
    allowed = (r // CHUNK) <= (c // CHUNK)
    hd2 = 2 * ATTN_HEAD_DIM

    ahead = 2

    def scores(kt, hm):
        col = hm * ATTN_HEAD_DIM
        qh = q_ref[:, col:col + ATTN_HEAD_DIM]
        kh = k_ref[pl.ds(pl.multiple_of(kt * t, t), t), col:col + ATTN_HEAD_DIM]
        return lax.dot_general(kh, qh, (((1,), (1,)), ((), ())),
                               preferred_element_type=F32) + bias_ref[hm // 2, qt - kt]

    def block(kt, carry, diagonal):
        stats, early = carry[:2 * n_maps], carry[2 * n_maps:]

        def softmax(hm, s):
            if diagonal:
                s = jnp.where(allowed, s, -jnp.inf)
            m_old, l_old = stats[2 * hm:2 * hm + 2]
            m_new = jnp.maximum(m_old, jnp.max(s, axis=0, keepdims=True))
            alpha = jnp.exp2(m_old - m_new)
            p = jnp.exp2(s - m_new)
            return m_new, alpha * l_old + jnp.sum(p, axis=0, keepdims=True), alpha, p.astype(BF16)

        def accumulate(hm, alpha, p):
            h = hm // 2
            vth = vt_ref[0, kt, h * hd2:(h + 1) * hd2, :]
            acc_refs[hm][...] = alpha * acc_refs[hm][...] + jnp.dot(vth, p, preferred_element_type=F32)

        s_vals = dict(enumerate(early))
        sm_vals, out, nxt = {}, [None] * (2 * n_maps), []
        for step in range(1, n_maps + ahead):
            if ahead <= step < n_maps:
                s_vals[step] = scores(kt, step)
            elif step >= n_maps and not diagonal:
                nxt.append(scores(kt + 1, step - n_maps))
            hm = step - 1
            if hm < n_maps:
                m_new, l_new, alpha, p = softmax(hm, s_vals.pop(hm))
                out[2 * hm], out[2 * hm + 1] = m_new, l_new
                sm_vals[hm] = (alpha, p)
            if step - 2 >= 0:
                accumulate(step - 2, *sm_vals.pop(step - 2))
        return tuple(out) + tuple(nxt)

    for acc in acc_refs:
        acc[...] = jnp.zeros_like(acc)
    one = (jnp.full((1, t), -jnp.inf, F32), jnp.zeros((1, t), F32))
    first = tuple(scores(0, hm) for hm in range(ahead))
    carry = lax.fori_loop(0, qt, lambda kt, cr: block(kt, cr, False), one * n_maps + first)
    carry = block(qt, carry, True)
    for h in range(ATTN_HEADS):
        l0, l1 = carry[4 * h + 1], carry[4 * h + 3]
        o = acc_refs[2 * h][...] / l0 - lam * (acc_refs[2 * h + 1][...] / l1)
        y = o * lax.rsqrt(jnp.mean(o * o, axis=0, keepdims=True) + LN_EPS) * g_ref[...]
        o_ref[:, h * hd2:(h + 1) * hd2] = (y * (1.0 - LAMBDA_INIT)).T.astype(BF16)


def _attention(q, k, vt, bias_tiles, lq1, lk1, lq2, lk2, subln_g, *, bsz, seq):
    t = ATT_TILE
    nt = seq // t
    full = lambda a: pl.BlockSpec(a.shape, lambda b, j: (0,) * a.ndim)
    return pl.pallas_call(
        _attn_kernel,
        grid=(bsz, nt),
        in_specs=[pl.BlockSpec((t, QK_DIM), lambda b, j: (b * nt + j, 0)),
                  pl.BlockSpec((seq, QK_DIM), lambda b, j: (b, 0)),
                  pl.BlockSpec((1, nt, V_DIM, t), lambda b, j: (b, 0, 0, 0)),
                  full(bias_tiles), full(lq1), full(lk1), full(lq2), full(lk2), full(subln_g)],
        out_specs=pl.BlockSpec((t, V_DIM), lambda b, j: (b * nt + j, 0)),
        out_shape=jax.ShapeDtypeStruct((bsz * seq, V_DIM), BF16),
        scratch_shapes=[pltpu.VMEM((2 * ATTN_HEAD_DIM, t), F32) for _ in range(2 * ATTN_HEADS)],
        compiler_params=_params(("arbitrary", "arbitrary")),
    )(q, k, vt, bias_tiles, lq1, lk1, lq2, lk2, subln_g)


def _outproj_kernel(x_ref, yp_ref, ya_ref, w_ref, gate_ref, g_ref, b_ref, o_ref):
    mix = (jnp.dot(yp_ref[...], w_ref[0:POOL_DIM, :], preferred_element_type=F32)
           + jnp.dot(ya_ref[...], w_ref[POOL_DIM:, :], preferred_element_type=F32))
    z = ALPHA * x_ref[...] + gate_ref[0] * mix
    o_ref[...] = _layer_norm(z, g_ref[...], b_ref[...])


def _outproj(x2, yp, ya, w_out, gate1, ln_g, ln_b, *, seq, tm):
    n, d = x2.shape
    tps = seq // tm
    row = lambda w: pl.BlockSpec((tm, w), lambda i: (i, 0))
    full = lambda a: pl.BlockSpec(a.shape, lambda i: (0,) * a.ndim)
    return pl.pallas_call(
        _outproj_kernel,
        grid=(n // tm,),
        in_specs=[row(d), row(POOL_DIM), row(V_DIM), full(w_out),
                  pl.BlockSpec((1, 1, d), lambda i: (i // tps, 0, 0)), full(ln_g), full(ln_b)],
        out_specs=row(d),
        out_shape=jax.ShapeDtypeStruct((n, d), F32),
        compiler_params=_params(("arbitrary",)),
    )(x2, yp, ya, w_out, gate1, ln_g, ln_b)


def _route_kernel(x_ref, sc_ref, sh_ref, whi_ref, wlo_ref, rb_ref,
                  eidx_ref, rank_ref, wts_ref, cnt_ref, h2p_ref, carry_ref, *, tr):
    i = pl.program_id(0)

    @pl.when(i == 0)
    def _():
        carry_ref[...] = jnp.zeros_like(carry_ref)

    h2 = x_ref[...] * (1.0 + sc_ref[0]) + sh_ref[0]
    h2p_ref[...] = _pack_bf16_halves(h2)
    hi = h2.astype(BF16)
    lo = (h2 - hi.astype(F32)).astype(BF16)
    nt = (((1,), (1,)), ((), ()))
    logits = (lax.dot_general(whi_ref[...], hi, nt, preferred_element_type=F32)
              + lax.dot_general(wlo_ref[...], hi, nt, preferred_element_type=F32)
              + lax.dot_general(whi_ref[...], lo, nt, preferred_element_type=F32))
    scores = _sigmoid(logits)
    sel = scores + rb_ref[...]
    erow = lax.broadcasted_iota(jnp.int32, (N_EXPERTS, tr), 0).astype(F32)

    g3 = sel.reshape(N_GROUPS, GROUP_SIZE, tr)
    r3 = lax.broadcasted_iota(jnp.int32, (N_GROUPS, GROUP_SIZE, tr), 1).astype(F32)
    m1 = jnp.max(g3, axis=1, keepdims=True)
    first = jnp.min(jnp.where(g3 == m1, r3, float(GROUP_SIZE)), axis=1, keepdims=True)
    m2 = jnp.max(jnp.where(r3 == first, -jnp.inf, g3), axis=1, keepdims=True)
    gscore = (m1 + m2).reshape(N_GROUPS, tr)

    gidx = lax.broadcasted_iota(jnp.int32, (N_GROUPS, tr), 0)
    beaten_by = jnp.zeros((N_GROUPS, tr), jnp.int32)
    for g in range(N_GROUPS):
        other = gscore[g:g + 1, :]
        wins = (other > gscore) | ((other == gscore) & (g < gidx))
        beaten_by = beaten_by + wins.astype(jnp.int32)
    dropped = jnp.where(beaten_by < TOP_K_GROUPS, 0.0, -jnp.inf)
    cur = (g3 + dropped.reshape(N_GROUPS, 1, tr)).reshape(N_EXPERTS, tr)

    picks, weights = [], []
    selmask = jnp.zeros((N_EXPERTS, tr), F32)
    for _ in range(TOP_K):
        mx = jnp.max(cur, axis=0, keepdims=True)
        pick = jnp.min(jnp.where(cur == mx, erow, float(N_EXPERTS)), axis=0, keepdims=True)
        onehot = erow == pick
        weights.append(jnp.sum(jnp.where(onehot, scores, 0.0), axis=0, keepdims=True))
        cur = jnp.where(onehot, -jnp.inf, cur)
        selmask = jnp.where(onehot, 1.0, selmask)
        picks.append(pick)

    t_from = lax.broadcasted_iota(jnp.int32, (tr, tr), 0)
    t_to = lax.broadcasted_iota(jnp.int32, (tr, tr), 1)
    earlier = jnp.where(t_from < t_to, 1.0, 0.0).astype(BF16)
    chosen = selmask.astype(BF16)
    carry = carry_ref[...]
    rankmat = (jnp.dot(chosen, earlier, preferred_element_type=F32)
               + jnp.concatenate([carry] * (tr // LANES), axis=1))
    carry_ref[...] = carry + jnp.dot(chosen, jnp.ones((tr, LANES), BF16), preferred_element_type=F32)
    cnt_ref[...] = carry_ref[...]

    wsum = weights[0]
    for wj in weights[1:]:
        wsum = wsum + wj
    row8 = lax.broadcasted_iota(jnp.int32, (TOP_K, tr), 0)
    eidx = jnp.zeros((TOP_K, tr), jnp.int32)
    rank = jnp.zeros((TOP_K, tr), jnp.int32)
    wts = jnp.zeros((TOP_K, tr), F32)
    for j in range(TOP_K):
        rk = jnp.sum(jnp.where(erow == picks[j], rankmat, 0.0), axis=0, keepdims=True)
        eidx = jnp.where(row8 == j, picks[j].astype(jnp.int32), eidx)
        rank = jnp.where(row8 == j, rk.astype(jnp.int32), rank)
        wts = jnp.where(row8 == j, weights[j] / wsum * ROUTED_SCALE, wts)
    eidx_ref[...] = eidx
    rank_ref[...] = rank
    wts_ref[...] = wts


def _route(x1, scale2, shift2, wr_hi, wr_lo, router_bias, *, seq, tr):
    n, d = x1.shape
    tps = seq // tr
    mod_spec = pl.BlockSpec((1, 1, d), lambda i: (i // tps, 0, 0))
    full = lambda a: pl.BlockSpec(a.shape, lambda i: (0,) * a.ndim)
    assert tr % LANES == 0
    k8 = pl.BlockSpec((TOP_K, tr), lambda i: (0, i))
    return pl.pallas_call(
        functools.partial(_route_kernel, tr=tr),
        grid=(n // tr,),
        in_specs=[pl.BlockSpec((tr, d), lambda i: (i, 0)), mod_spec, mod_spec,
                  full(wr_hi), full(wr_lo), full(router_bias)],
        out_specs=[k8, k8, k8, pl.BlockSpec((N_EXPERTS, LANES), lambda i: (0, 0)),
                   pl.BlockSpec((tr, d // 2), lambda i: (i, 0))],
        out_shape=[jax.ShapeDtypeStruct((TOP_K, n), jnp.int32),
                   jax.ShapeDtypeStruct((TOP_K, n), jnp.int32),
                   jax.ShapeDtypeStruct((TOP_K, n), F32),
                   jax.ShapeDtypeStruct((N_EXPERTS, LANES), F32),
                   jax.ShapeDtypeStruct((n, d // 2), U32)],
        scratch_shapes=[pltpu.VMEM((N_EXPERTS, LANES), F32)],
        compiler_params=_params(("arbitrary",)),
    )(x1, scale2, shift2, wr_hi, wr_lo, router_bias)


def _pack_bf16_halves(x):
    w = x.shape[1] // 2
    lo = pltpu.bitcast(x[:, :w].astype(BF16).astype(F32), U32) >> 16
    hi = pltpu.bitcast(x[:, w:].astype(BF16).astype(F32), U32) & jnp.uint32(0xFFFF0000)
    return lo | hi


def _unpack_bf16_halves(p):
    return pltpu.bitcast(p << 16, F32), pltpu.bitcast(p & jnp.uint32(0xFFFF0000), F32)


def _sc_worker():
    return lax.axis_index("subcore") * SC_CORES + lax.axis_index("core")


def _scatter_rows_sc(rows, idx, n_slots):
    n, width = rows.shape
    k = idx.shape[0] // n
    n_workers = SC_CORES * SC_SUBCORES
    w = SC_ROWS
    per_worker = n // n_workers
    n_chunks = per_worker // w
    assert n == n_workers * n_chunks * w and n_chunks % 2 == 0 and idx.shape[0] == k * n
    mesh = plsc.VectorSubcoreMesh(core_axis_name="core", subcore_axis_name="subcore")

    @functools.partial(
        pl.kernel, mesh=mesh,
        out_type=jax.ShapeDtypeStruct((n_slots, width), rows.dtype),
        scratch_types=[pltpu.VMEM((k * per_worker,), jnp.int32), pltpu.VMEM((2, w, width), rows.dtype),
                       pltpu.SemaphoreType.DMA((2,)), pltpu.SemaphoreType.DMA((2,))])
    def scatter(rows_hbm, idx_hbm, out_hbm, idx_v, rows_v, in_sem, out_sem):
        t0 = _sc_worker() * per_worker
        for j in range(k):
            pltpu.sync_copy(idx_hbm.at[pl.ds(j * n + t0, per_worker)], idx_v.at[pl.ds(j * per_worker, per_worker)])

        def load(i, s):
            return pltpu.make_async_copy(rows_hbm.at[pl.ds(t0 + i * w, w)], rows_v.at[s], in_sem.at[s])

        def send(i, s, j):
            slots = idx_v.at[pl.ds(j * per_worker + i * w, w)]
            return pltpu.make_async_copy(rows_v.at[s], out_hbm.at[slots], out_sem.at[s])

        load(0, 0).start()

        def pair(ii, carry):
            for s in range(2):
                i = ii * 2 + s
                load(i, s).wait()

                @pl.when(i >= 1)
                def _():
                    for j in range(k):
                        send(i - 1, 1 - s, j).wait()

                @pl.when(i + 1 < n_chunks)
                def _():
                    load(i + 1, 1 - s).start()

                for j in range(k):
                    send(i, s, j).start()
            return carry

        lax.fori_loop(0, n_chunks // 2, pair, 0)
        for j in range(k):
            send(n_chunks - 1, 1, j).wait()

    return scatter(rows, idx)


def _dest_kernel(eidx_ref, rank_ref, first_ref, dest_ref):
    tt = eidx_ref.shape[1]
    pieces = [jnp.broadcast_to(first_ref[:, p * LANES:(p + 1) * LANES], (TOP_K, LANES))
              for p in range(N_EXPERTS // LANES)]
    for c in range(tt // LANES):
        cols = slice(c * LANES, (c + 1) * LANES)
        e = eidx_ref[:, cols]
        within = e & (LANES - 1)
        start = jnp.take_along_axis(pieces[0], within, axis=1)
        for p in range(1, len(pieces)):
            start = jnp.where(e // LANES == p, jnp.take_along_axis(pieces[p], within, axis=1), start)
        dest_ref[:, cols] = start + rank_ref[:, cols]


def _dest(eidx_t, rank_t, pstarts, *, tt):
    n = eidx_t.shape[1]
    first = pstarts.reshape(1, N_EXPERTS)
    k8 = pl.BlockSpec((TOP_K, tt), lambda i: (0, i))
    return pl.pallas_call(
        _dest_kernel,
        grid=(n // tt,),
        in_specs=[k8, k8, pl.BlockSpec(first.shape, lambda i: (0, 0))],
        out_specs=k8,
        out_shape=jax.ShapeDtypeStruct((TOP_K, n), jnp.int32),
        compiler_params=_params(("arbitrary",)),
    )(eidx_t, rank_t, first)


def _expert_kernel(fb_ref, nb_ref, cnt_ref, nused_ref, wg_ref, wu_ref, wd_ref, xs_ref, ys_ref,
                   wgb, wub, wdb, xbuf, ybuf, in_sem, out_sem):
    e = pl.program_id(0)
    m = DISPATCH_BLOCK
    ns = EXPERT_SLOTS
    n_used = nused_ref[0]

    def rows(g):
        return pl.ds(pl.multiple_of(g * m, m), m)

    def fetch(g):
        slot = g & (ns - 1)
        return pltpu.make_async_copy(xs_ref.at[rows(g), :], xbuf.at[slot], in_sem.at[slot])

    def put(g):
        slot = g & (ns - 1)
        return pltpu.make_async_copy(ybuf.at[slot], ys_ref.at[rows(g), :], out_sem.at[slot])

    @pl.when(e == 0)
    def _():
        for g0 in range(ns - 1):
            @pl.when(g0 < n_used)
            def _(g0=g0):
                fetch(g0).start()

    wgb[...] = wg_ref[0].astype(BF16)
    wub[...] = wu_ref[0].astype(BF16)
    wdb[...] = wd_ref[0].astype(BF16)

    def block(i, carry):
        g = fb_ref[e] + i
        slot = g & (ns - 1)
        fetch(g).wait()

        @pl.when(g + ns - 1 < n_used)
        def _():
            fetch(g + ns - 1).start()

        @pl.when(g >= ns)
        def _():
            put(g - ns).wait()

        row = lax.broadcasted_iota(jnp.int32, (m, 1), 0)
        packed = jnp.where(row < cnt_ref[e] - i * m, xbuf[slot], jnp.uint32(0))
        x_lo, x_hi = [h.astype(BF16) for h in _unpack_bf16_halves(packed)]
        half = x_lo.shape[1]

        def up_proj(w):
            return (jnp.dot(x_lo, w[0:half, :], preferred_element_type=F32)
                    + jnp.dot(x_hi, w[half:, :], preferred_element_type=F32))

        a = (_silu(up_proj(wgb)) * up_proj(wub)).astype(BF16)
        ybuf[slot] = _pack_bf16_halves(jnp.dot(a, wdb[...], preferred_element_type=F32))
        put(g).start()
        return carry

    lax.fori_loop(0, nb_ref[e], block, 0)

    @pl.when(e == pl.num_programs(0) - 1)
    def _():
        for back in range(ns, 0, -1):
            @pl.when(n_used >= back)
            def _(back=back):
                put(n_used - back).wait()

        ybuf[0] = jnp.zeros(ybuf.shape[1:], ybuf.dtype)
        n_blocks = ys_ref.shape[0] // m

        def tail(wait):
            def body(g, carry):
                cp = pltpu.make_async_copy(ybuf.at[0], ys_ref.at[rows(g), :], out_sem.at[0])
                cp.wait() if wait else cp.start()
                return carry
            return body

        lax.fori_loop(n_used, n_blocks, tail(False), 0)
        lax.fori_loop(n_used, n_blocks, tail(True), 0)


def _experts(first_block, n_blocks_e, counts, n_used, xs, w_gate, w_up, w_down):
    p, dp = xs.shape
    m = DISPATCH_BLOCK
    n_e, d, f = w_gate.shape
    assert d == 2 * dp and p % m == 0
    grid_spec = pltpu.PrefetchScalarGridSpec(
        num_scalar_prefetch=4,
        grid=(n_e,),
        in_specs=[pl.BlockSpec((1, d, f), lambda e, *_: (e, 0, 0)),
                  pl.BlockSpec((1, d, f), lambda e, *_: (e, 0, 0)),
                  pl.BlockSpec((1, f, d), lambda e, *_: (e, 0, 0)),
                  pl.BlockSpec(memory_space=pl.ANY)],
        out_specs=pl.BlockSpec(memory_space=pl.ANY),
        scratch_shapes=[pltpu.VMEM((d, f), BF16), pltpu.VMEM((d, f), BF16), pltpu.VMEM((f, d), BF16),
                        pltpu.VMEM((EXPERT_SLOTS, m, dp), U32), pltpu.VMEM((EXPERT_SLOTS, m, dp), U32),
                        pltpu.SemaphoreType.DMA((EXPERT_SLOTS,)), pltpu.SemaphoreType.DMA((EXPERT_SLOTS,))],
    )
    return pl.pallas_call(
        _expert_kernel,
        grid_spec=grid_spec,
        out_shape=jax.ShapeDtypeStruct((p, dp), U32),
        compiler_params=_params(("arbitrary",)),
    )(first_block, n_blocks_e, counts, n_used, w_gate, w_up, w_down, xs)


def _gather_rows_sc(table, idx):
    n_idx, (_, width) = idx.shape[0], table.shape
    n_workers = SC_CORES * SC_SUBCORES
    w = SC_ROWS
    per_worker = n_idx // n_workers
    n_chunks = per_worker // w
    assert n_idx == n_workers * n_chunks * w and n_chunks % 2 == 0
    mesh = plsc.VectorSubcoreMesh(core_axis_name="core", subcore_axis_name="subcore")

    @functools.partial(
        pl.kernel, mesh=mesh,
        out_type=jax.ShapeDtypeStruct((n_idx, width), table.dtype),
        scratch_types=[pltpu.VMEM((per_worker,), jnp.int32), pltpu.VMEM((2, w, width), table.dtype),
                       pltpu.SemaphoreType.DMA((2,)), pltpu.SemaphoreType.DMA((2,))])
    def gather(table_hbm, idx_hbm, out_hbm, idx_v, rows_v, in_sem, out_sem):
        base = _sc_worker() * per_worker
        pltpu.sync_copy(idx_hbm.at[pl.ds(base, per_worker)], idx_v)

        def fetch(i, s):
            return pltpu.make_async_copy(table_hbm.at[idx_v.at[pl.ds(i * w, w)]], rows_v.at[s], in_sem.at[s])

        def put(i, s):
            return pltpu.make_async_copy(rows_v.at[s], out_hbm.at[pl.ds(base + i * w, w)], out_sem.at[s])

        fetch(0, 0).start()

        def pair(ii, carry):
            for s in range(2):
                i = ii * 2 + s
                fetch(i, s).wait()

                @pl.when(i >= 1)
                def _():
                    put(i - 1, 1 - s).wait()

                @pl.when(i + 1 < n_chunks)
                def _():
                    fetch(i + 1, 1 - s).start()

                put(i, s).start()
            return carry

        lax.fori_loop(0, n_chunks // 2, pair, 0)
        put(n_chunks - 1, 1).wait()

    return gather(table, idx)


def _combine_kernel(x_ref, sc_ref, sh_ref, gate_ref, wts_ref, rows_ref,
                    wsg_ref, wsu_ref, wsd_ref, g_ref, b_ref, *out_refs):
    o_ref = out_refs[-1]
    x = x_ref[...]
    hb = (x * (1.0 + sc_ref[0]) + sh_ref[0]).astype(BF16)
    sg = jnp.dot(hb, wsg_ref[...], preferred_element_type=F32)
    su = jnp.dot(hb, wsu_ref[...], preferred_element_type=F32)
    shared = jnp.dot((_silu(sg) * su).astype(BF16), wsd_ref[...], preferred_element_type=F32)

    wts = wts_ref[...]
    half = shared.shape[1] // 2
    lo, hi = shared[:, :half], shared[:, half:]
    for j in range(TOP_K):
        y_lo, y_hi = _unpack_bf16_halves(rows_ref[j])
        lo = lo + wts[:, j:j + 1] * y_lo
        hi = hi + wts[:, j:j + 1] * y_hi
    z = ALPHA * x + gate_ref[0] * jnp.concatenate([lo, hi], axis=1)
    o_ref[...] = _layer_norm(z, g_ref[...], b_ref[...])


def _combine(x1, scale2, shift2, gate2, wts, rows, ws_gate, ws_up, ws_down, ln_g, ln_b, prev_out,
             *, seq, tc, first_token):
    n, d = x1.shape
    tps = seq // tc
    off = first_token // tc
    n_tiles = rows.shape[1] // tc
    mod_spec = pl.BlockSpec((1, 1, d), lambda i: ((i + off) // tps, 0, 0))
    full = lambda a: pl.BlockSpec(a.shape, lambda i: (0,) * a.ndim)
    args = [x1, scale2, shift2, gate2, wts, rows, ws_gate, ws_up, ws_down, ln_g, ln_b]
    in_specs = [pl.BlockSpec((tc, d), lambda i: (i + off, 0)), mod_spec, mod_spec, mod_spec,
                pl.BlockSpec((tc, TOP_K), lambda i: (i + off, 0)),
                pl.BlockSpec((TOP_K, tc, d // 2), lambda i: (0, i, 0)),
                full(ws_gate), full(ws_up), full(ws_down), full(ln_g), full(ln_b)]
    aliases = {}
    if prev_out is not None:
        aliases = {len(args): 0}
        args.append(prev_out)
        in_specs.append(pl.BlockSpec(memory_space=pl.ANY))
    return pl.pallas_call(
        _combine_kernel,
        grid=(n_tiles,),
        in_specs=in_specs,
        out_specs=pl.BlockSpec((tc, d), lambda i: (i + off, 0)),
        out_shape=jax.ShapeDtypeStruct((n, d), F32),
        input_output_aliases=aliases,
        compiler_params=_params(("arbitrary",)),
    )(*args)


def _layer(x, c, w_ada, b_ada, w_in, pool_w, pool_scale, lq1, lk1, lq2, lk2, subln_g, w_out,
           ln1_g, ln1_b, w_router, router_bias, w_gate, w_up, w_down, ws_gate, ws_up, ws_down,
           ln2_g, ln2_b, rel_table, *, tm=1024, tr=256, tc=512):
    bsz, seq, d = x.shape
    n = bsz * seq
    x2 = x.reshape(n, d)
    row = lambda a: a.reshape(1, -1)

    mod = _modulation(c, w_ada, b_ada)
    shift1, scale1, gate1, shift2, scale2, gate2 = [
        mod[:, j * d:(j + 1) * d].reshape(bsz, 1, d) for j in range(6)]

    n_main = POOL_DIM + 2 * QK_DIM
    yp, q, k, vt = _inproj(x2, scale1, shift1, w_in[:, :n_main].astype(BF16), w_in[:, n_main:].T.astype(BF16),
                           pool_w.astype(BF16), row(pool_scale), seq=seq, tm=tm)
    bias_tiles = _bias_tiles(rel_table, seq // ATT_TILE)
    ya = _attention(q, k, vt, bias_tiles, row(lq1), row(lk1), row(lq2), row(lk2), subln_g.reshape(-1, 1),
                    bsz=bsz, seq=seq)
    x1 = _outproj(x2, yp, ya, w_out.astype(BF16), gate1, row(ln1_g), row(ln1_b), seq=seq, tm=tm)

    wr_t = w_router.T
    wr_hi = wr_t.astype(BF16)
    wr_lo = (wr_t - wr_hi.astype(F32)).astype(BF16)
    eidx_t, rank_t, wts_t, cnt, h2p = _route(x1, scale2, shift2, wr_hi, wr_lo, router_bias.reshape(-1, 1),
                                             seq=seq, tr=tr)

    m = DISPATCH_BLOCK
    counts = cnt[:, 0].astype(jnp.int32)
    padded = (counts + m - 1) // m * m
    pends = jnp.cumsum(padded)
    pstarts = pends - padded
    n_blocks = -(-(n * TOP_K + N_EXPERTS * (m - 1)) // m)
    n_used = (pends[-1:] // m).astype(jnp.int32)
    dest_t = _dest(eidx_t, rank_t, pstarts, tt=min(n, 2048))

    xs = _scatter_rows_sc(h2p, dest_t.reshape(TOP_K * n), n_blocks * m)
    ys = _experts(pstarts // m, padded // m, counts, n_used, xs, w_gate, w_up, w_down)

    shared_w = (ws_gate.astype(BF16), ws_up.astype(BF16), ws_down.astype(BF16))
    wts = wts_t.T
    part = n // COMBINE_PARTS
    assert part % tc == 0 and part * COMBINE_PARTS == n
    out = None
    for p in range(COMBINE_PARTS):
        idx = dest_t[:, p * part:(p + 1) * part].reshape(TOP_K * part)
        picked = _gather_rows_sc(ys, idx).reshape(TOP_K, part, d // 2)
        out = _combine(x1, scale2, shift2, gate2, wts, picked, *shared_w, row(ln2_g), row(ln2_b), out,
                       seq=seq, tc=tc, first_token=p * part)
    return out.reshape(bsz, seq, d)


def kernel(x, c, w_ada, b_ada, w_in, pool_w, pool_scale, lambda_q1, lambda_k1, lambda_q2, lambda_k2,
           subln_g, w_out, ln1_g, ln1_b, w_router, router_bias, w_gate, w_up, w_down,
           ws_gate, ws_up, ws_down, ln2_g, ln2_b, rel_table):
    per_layer = (w_ada, b_ada, w_in, pool_w, pool_scale, lambda_q1, lambda_k1, lambda_q2, lambda_k2,
                 subln_g, w_out, ln1_g, ln1_b, w_router, router_bias, w_gate, w_up, w_down,
                 ws_gate, ws_up, ws_down, ln2_g, ln2_b)
    assert all(a.shape[0] == DEPTH == 1 for a in per_layer)
    return _layer(x, c, *[a.reshape(a.shape[1:]) for a in per_layer], rel_table)
```

```python
import functools
import math

import jax
import jax.numpy as jnp
from jax import lax
from jax.experimental import pallas as pl
from jax.experimental.pallas import tpu as pltpu
from jax.experimental.pallas import tpu_sc as plsc

F32 = jnp.float32
BF16 = jnp.bfloat16
U32 = jnp.uint32
LANES = 128

CHUNK = 64
ATT_TILE = 256
LOG2_E = math.log2(math.e)
POOL_DIM = 512
POOL_WINDOWS = (2, 4, 8, 16)
POOL_GROUP_DIM = 128
MAX_WINDOW = max(POOL_WINDOWS)
ATTN_HEADS = 4
ATTN_HEAD_DIM = 64
QK_DIM = 512
V_DIM = 512
NUM_BUCKETS = 32
MAX_DISTANCE = 128
N_EXPERTS = 256
TOP_K = 8
N_GROUPS = 8
GROUP_SIZE = N_EXPERTS // N_GROUPS
TOP_K_GROUPS = 4
ROUTED_SCALE = 2.5
DISPATCH_BLOCK = 512
EXPERT_SLOTS = 8
COMBINE_PARTS = 8
SC_CORES, SC_SUBCORES = 2, 16
SC_ROWS = 64
DEPTH = 1
ALPHA = (2.0 * DEPTH) ** 0.25
LN_EPS = 1e-5
LAMBDA_INIT = 0.8 - 0.6 * math.exp(-0.3 * 0)

VMEM_LIMIT = 48 * 1024 * 1024


def _sigmoid(x):
    return 1.0 / (1.0 + jnp.exp(-x))


def _silu(x):
    return x * _sigmoid(x)


def _layer_norm(z, g, b):
    mu = jnp.mean(z, axis=-1, keepdims=True)
    zc = z - mu
    var = jnp.mean(zc * zc, axis=-1, keepdims=True)
    return zc * lax.rsqrt(var + LN_EPS) * g + b


def _params(sem=None):
    return pltpu.CompilerParams(dimension_semantics=sem, vmem_limit_bytes=VMEM_LIMIT)


def _mod_kernel(c_ref, w_ref, b_ref, o_ref):
    ca = _silu(c_ref[...])
    o_ref[...] = jnp.dot(ca, w_ref[...], preferred_element_type=F32,
                         precision=lax.Precision.HIGHEST) + b_ref[...]


def _modulation(c, w_ada, b_ada):
    bsz, d = c.shape
    n_out = w_ada.shape[1]
    return pl.pallas_call(
        _mod_kernel,
        grid=(n_out // d,),
        in_specs=[pl.BlockSpec((bsz, d), lambda j: (0, 0)),
                  pl.BlockSpec((d, d), lambda j: (0, j)),
                  pl.BlockSpec((1, d), lambda j: (0, j))],
        out_specs=pl.BlockSpec((bsz, d), lambda j: (0, j)),
        out_shape=jax.ShapeDtypeStruct((bsz, n_out), F32),
        compiler_params=_params(("arbitrary",)),
    )(c, w_ada, b_ada.reshape(1, n_out))


def _inproj_kernel(x_ref, sc_ref, sh_ref, w_ref, wvt_ref, pw_ref, ps_ref,
                   yp_ref, q_ref, k_ref, vt_ref, ext_ref, *, tm, seq):
    i = pl.program_id(0)
    tiles_per_seq = seq // tm
    it = i % tiles_per_seq
    h = x_ref[...] * (1.0 + sc_ref[0]) + sh_ref[0]
    hb = h.astype(BF16)
    proj = jnp.dot(hb, w_ref[...], preferred_element_type=F32)
    u = proj[:, :POOL_DIM]
    q_ref[...] = (proj[:, POOL_DIM:POOL_DIM + QK_DIM] * (ATTN_HEAD_DIM ** -0.5 * LOG2_E)).astype(BF16)
    k_ref[...] = proj[:, POOL_DIM + QK_DIM:POOL_DIM + 2 * QK_DIM].astype(BF16)
    vt = lax.dot_general(wvt_ref[...], hb, (((1,), (1,)), ((), ())), preferred_element_type=F32)
    for j in range(tm // ATT_TILE):
        vt_ref[0, j] = vt[:, j * ATT_TILE:(j + 1) * ATT_TILE].astype(BF16)

    @pl.when(it == 0)
    def _():
        ext_ref[0:MAX_WINDOW, :] = jnp.zeros((MAX_WINDOW, POOL_DIM), F32)

    ext_ref[MAX_WINDOW:MAX_WINDOW + tm, :] = u
    pos = (it * tm + lax.broadcasted_iota(jnp.int32, (tm, 1), 0) + 1).astype(F32)
    for g, w in enumerate(POOL_WINDOWS):
        c0, c1 = g * POOL_GROUP_DIM, (g + 1) * POOL_GROUP_DIM
        s = ext_ref[MAX_WINDOW:MAX_WINDOW + tm, c0:c1]
        for j in range(1, w):
            s = s + ext_ref[MAX_WINDOW - j:MAX_WINDOW - j + tm, c0:c1]
        pooled = s / jnp.minimum(pos, float(w)) - u[:, c0:c1]
        y = jnp.dot(pooled.astype(BF16), pw_ref[g], preferred_element_type=F32)
        yp_ref[:, c0:c1] = (y * ps_ref[:, c0:c1]).astype(BF16)
    ext_ref[0:MAX_WINDOW, :] = ext_ref[tm:tm + MAX_WINDOW, :]


def _inproj(x2, scale1, shift1, w_main, w_vt, pool_w, pool_scale, *, seq, tm):
    n, d = x2.shape
    assert n % tm == 0 and seq % tm == 0 and tm >= 2 * MAX_WINDOW and tm % ATT_TILE == 0
    tps = seq // tm
    tpt = tm // ATT_TILE
    mod_spec = pl.BlockSpec((1, 1, d), lambda i: (i // tps, 0, 0))
    row = lambda w: pl.BlockSpec((tm, w), lambda i: (i, 0))
    full = lambda a: pl.BlockSpec(a.shape, lambda i: (0,) * a.ndim)
    return pl.pallas_call(
        functools.partial(_inproj_kernel, tm=tm, seq=seq),
        grid=(n // tm,),
        in_specs=[row(d), mod_spec, mod_spec, full(w_main), full(w_vt), full(pool_w), full(pool_scale)],
        out_specs=[row(POOL_DIM), row(QK_DIM), row(QK_DIM),
                   pl.BlockSpec((1, tpt, V_DIM, ATT_TILE), lambda i: (i // tps, i % tps, 0, 0))],
        out_shape=[jax.ShapeDtypeStruct((n, POOL_DIM), BF16),
                   jax.ShapeDtypeStruct((n, QK_DIM), BF16),
                   jax.ShapeDtypeStruct((n, QK_DIM), BF16),
                   jax.ShapeDtypeStruct((n // seq, seq // ATT_TILE, V_DIM, ATT_TILE), BF16)],
        scratch_shapes=[pltpu.VMEM((tm + MAX_WINDOW, POOL_DIM), F32)],
        compiler_params=_params(("arbitrary",)),
    )(x2, scale1, shift1, w_main, w_vt, pool_w, pool_scale)


def _bias_kernel(tab_ref, o_ref):
    delta = pl.program_id(0)
    r = lax.broadcasted_iota(jnp.int32, (ATT_TILE, ATT_TILE), 0)
    c = lax.broadcasted_iota(jnp.int32, (ATT_TILE, ATT_TILE), 1)
    rel = r - c - delta * ATT_TILE
    half = NUM_BUCKETS // 2
    max_exact = half // 2
    ret = jnp.where(rel > 0, half, 0)
    n = jnp.abs(rel)
    nf = jnp.maximum(n, 1).astype(F32)
    large = max_exact + (jnp.log(nf / max_exact) / math.log(MAX_DISTANCE / max_exact)
                         * (half - max_exact)).astype(jnp.int32)
    large = jnp.minimum(large, half - 1)
    bucket = ret + jnp.where(n < max_exact, n, large)
    for h in range(ATTN_HEADS):
        acc = jnp.zeros((ATT_TILE, ATT_TILE), F32)
        for b in range(NUM_BUCKETS):
            acc = jnp.where(bucket == b, tab_ref[b, h], acc)
        o_ref[h, 0] = acc * LOG2_E


def _bias_tiles(rel_table, n_tiles):
    return pl.pallas_call(
        _bias_kernel,
        grid=(n_tiles,),
        in_specs=[pl.BlockSpec(memory_space=pltpu.SMEM)],
        out_specs=pl.BlockSpec((ATTN_HEADS, 1, ATT_TILE, ATT_TILE), lambda dlt: (0, dlt, 0, 0)),
        out_shape=jax.ShapeDtypeStruct((ATTN_HEADS, n_tiles, ATT_TILE, ATT_TILE), F32),
        compiler_params=_params(("arbitrary",)),
    )(rel_table)


def _attn_kernel(q_ref, k_ref, vt_ref, bias_ref, lq1_ref, lk1_ref, lq2_ref, lk2_ref, g_ref, o_ref, *acc_refs):
    qt = pl.program_id(1)
    t = ATT_TILE
    n_maps = 2 * ATTN_HEADS
    lam = (jnp.exp(jnp.sum(lq1_ref[...] * lk1_ref[...], axis=-1, keepdims=True))
           - jnp.exp(jnp.sum(lq2_ref[...] * lk2_ref[...], axis=-1, keepdims=True))
           + LAMBDA_INIT)
    r = lax.broadcasted_iota(jnp.int32, (t, t), 0)
    c = lax.broadcasted_iota(jnp.int32, (t, t), 1)
    allowed = (r // CHUNK) <= (c // CHUNK)
    hd2 = 2 * ATTN_HEAD_DIM

    ahead = 2

    def scores(kt, hm):
        col = hm * ATTN_HEAD_DIM
        qh = q_ref[:, col:col + ATTN_HEAD_DIM]
        kh = k_ref[pl.ds(pl.multiple_of(kt * t, t), t), col:col + ATTN_HEAD_DIM]
        return lax.dot_general(kh, qh, (((1,), (1,)), ((), ())),
                               preferred_element_type=F32) + bias_ref[hm // 2, qt - kt]

    def block(kt, carry, diagonal):
        stats, early = carry[:2 * n_maps], carry[2 * n_maps:]

        def softmax(hm, s):
            if diagonal:
                s = jnp.where(allowed, s, -jnp.inf)
            m_old, l_old = stats[2 * hm:2 * hm + 2]
            m_new = jnp.maximum(m_old, jnp.max(s, axis=0, keepdims=True))
            alpha = jnp.exp2(m_old - m_new)
            p = jnp.exp2(s - m_new)
            return m_new, alpha * l_old + jnp.sum(p, axis=0, keepdims=True), alpha, p.astype(BF16)

        def accumulate(hm, alpha, p):
            h = hm // 2
            vth = vt_ref[0, kt, h * hd2:(h + 1) * hd2, :]
            acc_refs[hm][...] = alpha * acc_refs[hm][...] + jnp.dot(vth, p, preferred_element_type=F32)

        s_vals = dict(enumerate(early))
        sm_vals, out, nxt = {}, [None] * (2 * n_maps), []
        for step in range(1, n_maps + ahead):
            if ahead <= step < n_maps:
                s_vals[step] = scores(kt, step)
            elif step >= n_maps and not diagonal:
                nxt.append(scores(kt + 1, step - n_maps))
            hm = step - 1
            if hm < n_maps:
                m_new, l_new, alpha, p = softmax(hm, s_vals.pop(hm))
                out[2 * hm], out[2 * hm + 1] = m_new, l_new
                sm_vals[hm] = (alpha, p)
            if 0 <= step - 2 < n_maps:
                accumulate(step - 2, *sm_vals.pop(step - 2))
        return tuple(out) + tuple(nxt)

    for acc in acc_refs:
        acc[...] = jnp.zeros_like(acc)
    one = (jnp.full((1, t), -jnp.inf, F32), jnp.zeros((1, t), F32))
    first = tuple(scores(0, hm) for hm in range(ahead))
    carry = lax.fori_loop(0, qt, lambda kt, cr: block(kt, cr, False), one * n_maps + first)
    carry = block(qt, carry, True)
    for h in range(ATTN_HEADS):
        l0, l1 = carry[4 * h + 1], carry[4 * h + 3]
        o = acc_refs[2 * h][...] / l0 - lam * (acc_refs[2 * h + 1][...] / l1)
        y = o * lax.rsqrt(jnp.mean(o * o, axis=0, keepdims=True) + LN_EPS) * g_ref[...]
        o_ref[:, h * hd2:(h + 1) * hd2] = (y * (1.0 - LAMBDA_INIT)).T.astype(BF16)


def _attention(q, k, vt, bias_tiles, lq1, lk1, lq2, lk2, subln_g, *, bsz, seq):
    t = ATT_TILE
    nt = seq // t
    full = lambda a: pl.BlockSpec(a.shape, lambda b, j: (0,) * a.ndim)
    return pl.pallas_call(
        _attn_kernel,
        grid=(bsz, nt),
        in_specs=[pl.BlockSpec((t, QK_DIM), lambda b, j: (b * nt + j, 0)),
                  pl.BlockSpec((seq, QK_DIM), lambda b, j: (b, 0)),
                  pl.BlockSpec((1, nt, V_DIM, t), lambda b, j: (b, 0, 0, 0)),
                  full(bias_tiles), full(lq1), full(lk1), full(lq2), full(lk2), full(subln_g)],
        out_specs=pl.BlockSpec((t, V_DIM), lambda b, j: (b * nt + j, 0)),
        out_shape=jax.ShapeDtypeStruct((bsz * seq, V_DIM), BF16),
        scratch_shapes=[pltpu.VMEM((2 * ATTN_HEAD_DIM, t), F32) for _ in range(2 * ATTN_HEADS)],
        compiler_params=_params(("arbitrary", "arbitrary")),
    )(q, k, vt, bias_tiles, lq1, lk1, lq2, lk2, subln_g)


def _outproj_kernel(x_ref, yp_ref, ya_ref, w_ref, gate_ref, g_ref, b_ref, o_ref):
    mix = (jnp.dot(yp_ref[...], w_ref[0:POOL_DIM, :], preferred_element_type=F32)
           + jnp.dot(ya_ref[...], w_ref[POOL_DIM:, :], preferred_element_type=F32))
    z = ALPHA * x_ref[...] + gate_ref[0] * mix
    o_ref[...] = _layer_norm(z, g_ref[...], b_ref[...])


def _outproj(x2, yp, ya, w_out, gate1, ln_g, ln_b, *, seq, tm):
    n, d = x2.shape
    tps = seq // tm
    row = lambda w: pl.BlockSpec((tm, w), lambda i: (i, 0))
    full = lambda a: pl.BlockSpec(a.shape, lambda i: (0,) * a.ndim)
    return pl.pallas_call(
        _outproj_kernel,
        grid=(n // tm,),
        in_specs=[row(d), row(POOL_DIM), row(V_DIM), full(w_out),
                  pl.BlockSpec((1, 1, d), lambda i: (i // tps, 0, 0)), full(ln_g), full(ln_b)],
        out_specs=row(d),
        out_shape=jax.ShapeDtypeStruct((n, d), F32),
        compiler_params=_params(("arbitrary",)),
    )(x2, yp, ya, w_out, gate1, ln_g, ln_b)


def _route_kernel(x_ref, sc_ref, sh_ref, whi_ref, wlo_ref, rb_ref,
                  eidx_ref, rank_ref, wts_ref, cnt_ref, h2p_ref, carry_ref, *, tr):
    i = pl.program_id(0)

    @pl.when(i == 0)
    def _():
        carry_ref[...] = jnp.zeros_like(carry_ref)

    h2 = x_ref[...] * (1.0 + sc_ref[0]) + sh_ref[0]
    h2p_ref[...] = _pack_bf16_halves(h2)
    hi = h2.astype(BF16)
    lo = (h2 - hi.astype(F32)).astype(BF16)
    nt = (((1,), (1,)), ((), ()))
    logits = (lax.dot_general(whi_ref[...], hi, nt, preferred_element_type=F32)
              + lax.dot_general(wlo_ref[...], hi, nt, preferred_element_type=F32)
              + lax.dot_general(whi_ref[...], lo, nt, preferred_element_type=F32))
    scores = _sigmoid(logits)
    sel = scores + rb_ref[...]
    erow = lax.broadcasted_iota(jnp.int32, (N_EXPERTS, tr), 0).astype(F32)

    g3 = sel.reshape(N_GROUPS, GROUP_SIZE, tr)
    r3 = lax.broadcasted_iota(jnp.int32, (N_GROUPS, GROUP_SIZE, tr), 1).astype(F32)
    m1 = jnp.max(g3, axis=1, keepdims=True)
    first = jnp.min(jnp.where(g3 == m1, r3, float(GROUP_SIZE)), axis=1, keepdims=True)
    m2 = jnp.max(jnp.where(r3 == first, -jnp.inf, g3), axis=1, keepdims=True)
    gscore = (m1 + m2).reshape(N_GROUPS, tr)

    gidx = lax.broadcasted_iota(jnp.int32, (N_GROUPS, tr), 0)
    beaten_by = jnp.zeros((N_GROUPS, tr), jnp.int32)
    for g in range(N_GROUPS):
        other = gscore[g:g + 1, :]
        wins = (other > gscore) | ((other == gscore) & (g < gidx))
        beaten_by = beaten_by + wins.astype(jnp.int32)
    dropped = jnp.where(beaten_by < TOP_K_GROUPS, 0.0, -jnp.inf)
    cur = (g3 + dropped.reshape(N_GROUPS, 1, tr)).reshape(N_EXPERTS, tr)

    picks, weights = [], []
    selmask = jnp.zeros((N_EXPERTS, tr), F32)
    for _ in range(TOP_K):
        mx = jnp.max(cur, axis=0, keepdims=True)
        pick = jnp.min(jnp.where(cur == mx, erow, float(N_EXPERTS)), axis=0, keepdims=True)
        onehot = erow == pick
        weights.append(jnp.sum(jnp.where(onehot, scores, 0.0), axis=0, keepdims=True))
        cur = jnp.where(onehot, -jnp.inf, cur)
        selmask = jnp.where(onehot, 1.0, selmask)
        picks.append(pick)

    t_from = lax.broadcasted_iota(jnp.int32, (tr, tr), 0)
    t_to = lax.broadcasted_iota(jnp.int32, (tr, tr), 1)
    earlier = jnp.where(t_from < t_to, 1.0, 0.0).astype(BF16)
    chosen = selmask.astype(BF16)
    carry = carry_ref[...]
    rankmat = (jnp.dot(chosen, earlier, preferred_element_type=F32)
               + jnp.concatenate([carry] * (tr // LANES), axis=1))
    carry_ref[...] = carry + jnp.dot(chosen, jnp.ones((tr, LANES), BF16), preferred_element_type=F32)
    cnt_ref[...] = carry_ref[...]

    wsum = weights[0]
    for wj in weights[1:]:
        wsum = wsum + wj
    row8 = lax.broadcasted_iota(jnp.int32, (TOP_K, tr), 0)
    eidx = jnp.zeros((TOP_K, tr), jnp.int32)
    rank = jnp.zeros((TOP_K, tr), jnp.int32)
    wts = jnp.zeros((TOP_K, tr), F32)
    for j in range(TOP_K):
        rk = jnp.sum(jnp.where(erow == picks[j], rankmat, 0.0), axis=0, keepdims=True)
        eidx = jnp.where(row8 == j, picks[j].astype(jnp.int32), eidx)
        rank = jnp.where(row8 == j, rk.astype(jnp.int32), rank)
        wts = jnp.where(row8 == j, weights[j] / wsum * ROUTED_SCALE, wts)
    eidx_ref[...] = eidx
    rank_ref[...] = rank
    wts_ref[...] = wts


def _route(x1, scale2, shift2, wr_hi, wr_lo, router_bias, *, seq, tr):
    n, d = x1.shape
    tps = seq // tr
    mod_spec = pl.BlockSpec((1, 1, d), lambda i: (i // tps, 0, 0))
    full = lambda a: pl.BlockSpec(a.shape, lambda i: (0,) * a.ndim)
    assert tr % LANES == 0
    k8 = pl.BlockSpec((TOP_K, tr), lambda i: (0, i))
    return pl.pallas_call(
        functools.partial(_route_kernel, tr=tr),
        grid=(n // tr,),
        in_specs=[pl.BlockSpec((tr, d), lambda i: (i, 0)), mod_spec, mod_spec,
                  full(wr_hi), full(wr_lo), full(router_bias)],
        out_specs=[k8, k8, k8, pl.BlockSpec((N_EXPERTS, LANES), lambda i: (0, 0)),
                   pl.BlockSpec((tr, d // 2), lambda i: (i, 0))],
        out_shape=[jax.ShapeDtypeStruct((TOP_K, n), jnp.int32),
                   jax.ShapeDtypeStruct((TOP_K, n), jnp.int32),
                   jax.ShapeDtypeStruct((TOP_K, n), F32),
                   jax.ShapeDtypeStruct((N_EXPERTS, LANES), F32),
                   jax.ShapeDtypeStruct((n, d // 2), U32)],
        scratch_shapes=[pltpu.VMEM((N_EXPERTS, LANES), F32)],
        compiler_params=_params(("arbitrary",)),
    )(x1, scale2, shift2, wr_hi, wr_lo, router_bias)


def _pack_bf16_halves(x):
    w = x.shape[1] // 2
    lo = pltpu.bitcast(x[:, :w].astype(BF16).astype(F32), U32) >> 16
    hi = pltpu.bitcast(x[:, w:].astype(BF16).astype(F32), U32) & jnp.uint32(0xFFFF0000)
    return lo | hi


def _unpack_bf16_halves(p):
    return pltpu.bitcast(p << 16, F32), pltpu.bitcast(p & jnp.uint32(0xFFFF0000), F32)


def _sc_worker():
    return lax.axis_index("subcore") * SC_CORES + lax.axis_index("core")


def _scatter_rows_sc(rows, idx, n_slots):
    n, width = rows.shape
    k = idx.shape[0] // n
    n_workers = SC_CORES * SC_SUBCORES
    w = SC_ROWS
    per_worker = n // n_workers
    n_chunks = per_worker // w
    assert n == n_workers * n_chunks * w and n_chunks % 2 == 0 and idx.shape[0] == k * n
    mesh = plsc.VectorSubcoreMesh(core_axis_name="core", subcore_axis_name="subcore")

    @functools.partial(
        pl.kernel, mesh=mesh,
        out_type=jax.ShapeDtypeStruct((n_slots, width), rows.dtype),
        scratch_types=[pltpu.VMEM((k * per_worker,), jnp.int32), pltpu.VMEM((2, w, width), rows.dtype),
                       pltpu.SemaphoreType.DMA((2,)), pltpu.SemaphoreType.DMA((2,))])
    def scatter(rows_hbm, idx_hbm, out_hbm, idx_v, rows_v, in_sem, out_sem):
        t0 = _sc_worker() * per_worker
        for j in range(k):
            pltpu.sync_copy(idx_hbm.at[pl.ds(j * n + t0, per_worker)], idx_v.at[pl.ds(j * per_worker, per_worker)])

        def load(i, s):
            return pltpu.make_async_copy(rows_hbm.at[pl.ds(t0 + i * w, w)], rows_v.at[s], in_sem.at[s])

        def send(i, s, j):
            slots = idx_v.at[pl.ds(j * per_worker + i * w, w)]
            return pltpu.make_async_copy(rows_v.at[s], out_hbm.at[slots], out_sem.at[s])

        load(0, 0).start()

        def pair(ii, carry):
            for s in range(2):
                i = ii * 2 + s
                load(i, s).wait()

                @pl.when(i >= 1)
                def _():
                    for j in range(k):
                        send(i - 1, 1 - s, j).wait()

                @pl.when(i + 1 < n_chunks)
                def _():
                    load(i + 1, 1 - s).start()

                for j in range(k):
                    send(i, s, j).start()
            return carry

        lax.fori_loop(0, n_chunks // 2, pair, 0)
        for j in range(k):
            send(n_chunks - 1, 1, j).wait()

    return scatter(rows, idx)


def _dest_kernel(eidx_ref, rank_ref, first_ref, dest_ref):
    tt = eidx_ref.shape[1]
    pieces = [jnp.broadcast_to(first_ref[:, p * LANES:(p + 1) * LANES], (TOP_K, LANES))
              for p in range(N_EXPERTS // LANES)]
    for c in range(tt // LANES):
        cols = slice(c * LANES, (c + 1) * LANES)
        e = eidx_ref[:, cols]
        within = e & (LANES - 1)
        start = jnp.take_along_axis(pieces[0], within, axis=1)
        for p in range(1, len(pieces)):
            start = jnp.where(e // LANES == p, jnp.take_along_axis(pieces[p], within, axis=1), start)
        dest_ref[:, cols] = start + rank_ref[:, cols]


def _dest(eidx_t, rank_t, pstarts, *, tt):
    n = eidx_t.shape[1]
    first = pstarts.reshape(1, N_EXPERTS)
    k8 = pl.BlockSpec((TOP_K, tt), lambda i: (0, i))
    return pl.pallas_call(
        _dest_kernel,
        grid=(n // tt,),
        in_specs=[k8, k8, pl.BlockSpec(first.shape, lambda i: (0, 0))],
        out_specs=k8,
        out_shape=jax.ShapeDtypeStruct((TOP_K, n), jnp.int32),
        compiler_params=_params(("arbitrary",)),
    )(eidx_t, rank_t, first)


def _expert_kernel(fb_ref, nb_ref, cnt_ref, nused_ref, wg_ref, wu_ref, wd_ref, xs_ref, ys_ref,
                   wgb, wub, wdb, xbuf, ybuf, in_sem, out_sem):
    e = pl.program_id(0)
    m = DISPATCH_BLOCK
    ns = EXPERT_SLOTS
    n_used = nused_ref[0]

    def rows(g):
        return pl.ds(pl.multiple_of(g * m, m), m)

    def fetch(g):
        slot = g & (ns - 1)
        return pltpu.make_async_copy(xs_ref.at[rows(g), :], xbuf.at[slot], in_sem.at[slot])

    def put(g):
        slot = g & (ns - 1)
        return pltpu.make_async_copy(ybuf.at[slot], ys_ref.at[rows(g), :], out_sem.at[slot])

    @pl.when(e == 0)
    def _():
        for g0 in range(ns - 1):
            @pl.when(g0 < n_used)
            def _(g0=g0):
                fetch(g0).start()

    wgb[...] = wg_ref[0].astype(BF16)
    wub[...] = wu_ref[0].astype(BF16)
    wdb[...] = wd_ref[0].astype(BF16)

    def block(i, carry):
        g = fb_ref[e] + i
        slot = g & (ns - 1)
        fetch(g).wait()

        @pl.when(g + ns - 1 < n_used)
        def _():
            fetch(g + ns - 1).start()

        @pl.when(g >= ns)
        def _():
            put(g - ns).wait()

        row = lax.broadcasted_iota(jnp.int32, (m, 1), 0)
        packed = jnp.where(row < cnt_ref[e] - i * m, xbuf[slot], jnp.uint32(0))
        x_lo, x_hi = [h.astype(BF16) for h in _unpack_bf16_halves(packed)]
        half = x_lo.shape[1]

        def up_proj(w):
            return (jnp.dot(x_lo, w[0:half, :], preferred_element_type=F32)
                    + jnp.dot(x_hi, w[half:, :], preferred_element_type=F32))

        a = (_silu(up_proj(wgb)) * up_proj(wub)).astype(BF16)
        ybuf[slot] = _pack_bf16_halves(jnp.dot(a, wdb[...], preferred_element_type=F32))
        put(g).start()
        return carry

    lax.fori_loop(0, nb_ref[e], block, 0)

    @pl.when(e == pl.num_programs(0) - 1)
    def _():
        for back in range(ns, 0, -1):
            @pl.when(n_used >= back)
            def _(back=back):
                put(n_used - back).wait()

        ybuf[0] = jnp.zeros(ybuf.shape[1:], ybuf.dtype)
        n_blocks = ys_ref.shape[0] // m

        def tail(wait):
            def body(g, carry):
                cp = pltpu.make_async_copy(ybuf.at[0], ys_ref.at[rows(g), :], out_sem.at[0])
                cp.wait() if wait else cp.start()
                return carry
            return body

        lax.fori_loop(n_used, n_blocks, tail(False), 0)
        lax.fori_loop(n_used, n_blocks, tail(True), 0)


def _experts(first_block, n_blocks_e, counts, n_used, xs, w_gate, w_up, w_down):
    p, dp = xs.shape
    m = DISPATCH_BLOCK
    n_e, d, f = w_gate.shape
    assert d == 2 * dp and p % m == 0
    grid_spec = pltpu.PrefetchScalarGridSpec(
        num_scalar_prefetch=4,
        grid=(n_e,),
        in_specs=[pl.BlockSpec((1, d, f), lambda e, *_: (e, 0, 0)),
                  pl.BlockSpec((1, d, f), lambda e, *_: (e, 0, 0)),
                  pl.BlockSpec((1, f, d), lambda e, *_: (e, 0, 0)),
                  pl.BlockSpec(memory_space=pl.ANY)],
        out_specs=pl.BlockSpec(memory_space=pl.ANY),
        scratch_shapes=[pltpu.VMEM((d, f), BF16), pltpu.VMEM((d, f), BF16), pltpu.VMEM((f, d), BF16),
                        pltpu.VMEM((EXPERT_SLOTS, m, dp), U32), pltpu.VMEM((EXPERT_SLOTS, m, dp), U32),
                        pltpu.SemaphoreType.DMA((EXPERT_SLOTS,)), pltpu.SemaphoreType.DMA((EXPERT_SLOTS,))],
    )
    return pl.pallas_call(
        _expert_kernel,
        grid_spec=grid_spec,
        out_shape=jax.ShapeDtypeStruct((p, dp), U32),
        compiler_params=_params(("arbitrary",)),
    )(first_block, n_blocks_e, counts, n_used, w_gate, w_up, w_down, xs)


def _gather_rows_sc(table, idx):
    n_idx, (_, width) = idx.shape[0], table.shape
    n_workers = SC_CORES * SC_SUBCORES
    w = SC_ROWS
    per_worker = n_idx // n_workers
    n_chunks = per_worker // w
    assert n_idx == n_workers * n_chunks * w and n_chunks % 2 == 0
    mesh = plsc.VectorSubcoreMesh(core_axis_name="core", subcore_axis_name="subcore")

    @functools.partial(
        pl.kernel, mesh=mesh,
        out_type=jax.ShapeDtypeStruct((n_idx, width), table.dtype),
        scratch_types=[pltpu.VMEM((per_worker,), jnp.int32), pltpu.VMEM((2, w, width), table.dtype),
                       pltpu.SemaphoreType.DMA((2,)), pltpu.SemaphoreType.DMA((2,))])
    def gather(table_hbm, idx_hbm, out_hbm, idx_v, rows_v, in_sem, out_sem):
        base = _sc_worker() * per_worker
        pltpu.sync_copy(idx_hbm.at[pl.ds(base, per_worker)], idx_v)

        def fetch(i, s):
            return pltpu.make_async_copy(table_hbm.at[idx_v.at[pl.ds(i * w, w)]], rows_v.at[s], in_sem.at[s])

        def put(i, s):
            return pltpu.make_async_copy(rows_v.at[s], out_hbm.at[pl.ds(base + i * w, w)], out_sem.at[s])

        fetch(0, 0).start()

        def pair(ii, carry):
            for s in range(2):
                i = ii * 2 + s
                fetch(i, s).wait()

                @pl.when(i >= 1)
                def _():
                    put(i - 1, 1 - s).wait()

                @pl.when(i + 1 < n_chunks)
                def _():
                    fetch(i + 1, 1 - s).start()

                put(i, s).start()
            return carry

        lax.fori_loop(0, n_chunks // 2, pair, 0)
        put(n_chunks - 1, 1).wait()

    return gather(table, idx)


def _combine_kernel(x_ref, sc_ref, sh_ref, gate_ref, wts_ref, rows_ref,
                    wsg_ref, wsu_ref, wsd_ref, g_ref, b_ref, *out_refs):
    o_ref = out_refs[-1]
    x = x_ref[...]
    hb = (x * (1.0 + sc_ref[0]) + sh_ref[0]).astype(BF16)
    sg = jnp.dot(hb, wsg_ref[...], preferred_element_type=F32)
    su = jnp.dot(hb, wsu_ref[...], preferred_element_type=F32)
    shared = jnp.dot((_silu(sg) * su).astype(BF16), wsd_ref[...], preferred_element_type=F32)

    wts = wts_ref[...]
    half = shared.shape[1] // 2
    lo, hi = shared[:, :half], shared[:, half:]
    for j in range(TOP_K):
        y_lo, y_hi = _unpack_bf16_halves(rows_ref[j])
        lo = lo + wts[:, j:j + 1] * y_lo
        hi = hi + wts[:, j:j + 1] * y_hi
    z = ALPHA * x + gate_ref[0] * jnp.concatenate([lo, hi], axis=1)
    o_ref[...] = _layer_norm(z, g_ref[...], b_ref[...])


def _combine(x1, scale2, shift2, gate2, wts, rows, ws_gate, ws_up, ws_down, ln_g, ln_b, prev_out,
             *, seq, tc, first_token):
    n, d = x1.shape
    tps = seq // tc
    off = first_token // tc
    n_tiles = rows.shape[1] // tc
    mod_spec = pl.BlockSpec((1, 1, d), lambda i: ((i + off) // tps, 0, 0))
    full = lambda a: pl.BlockSpec(a.shape, lambda i: (0,) * a.ndim)
    args = [x1, scale2, shift2, gate2, wts, rows, ws_gate, ws_up, ws_down, ln_g, ln_b]
    in_specs = [pl.BlockSpec((tc, d), lambda i: (i + off, 0)), mod_spec, mod_spec, mod_spec,
                pl.BlockSpec((tc, TOP_K), lambda i: (i + off, 0)),
                pl.BlockSpec((TOP_K, tc, d // 2), lambda i: (0, i, 0)),
                full(ws_gate), full(ws_up), full(ws_down), full(ln_g), full(ln_b)]
    aliases = {}
    if prev_out is not None:
        aliases = {len(args): 0}
        args.append(prev_out)
        in_specs.append(pl.BlockSpec(memory_space=pl.ANY))
    return pl.pallas_call(
        _combine_kernel,
        grid=(n_tiles,),
        in_specs=in_specs,
        out_specs=pl.BlockSpec((tc, d), lambda i: (i + off, 0)),
        out_shape=jax.ShapeDtypeStruct((n, d), F32),
        input_output_aliases=aliases,
        compiler_params=_params(("arbitrary",)),
    )(*args)


def _layer(x, c, w_ada, b_ada, w_in, pool_w, pool_scale, lq1, lk1, lq2, lk2, subln_g, w_out,
           ln1_g, ln1_b, w_router, router_bias, w_gate, w_up, w_down, ws_gate, ws_up, ws_down,
           ln2_g, ln2_b, rel_table, *, tm=1024, tr=512, tc=512):
    bsz, seq, d = x.shape
    n = bsz * seq
    x2 = x.reshape(n, d)
    row = lambda a: a.reshape(1, -1)

    mod = _modulation(c, w_ada, b_ada)
    shift1, scale1, gate1, shift2, scale2, gate2 = [
        mod[:, j * d:(j + 1) * d].reshape(bsz, 1, d) for j in range(6)]

    n_main = POOL_DIM + 2 * QK_DIM
    yp, q, k, vt = _inproj(x2, scale1, shift1, w_in[:, :n_main].astype(BF16), w_in[:, n_main:].T.astype(BF16),
                           pool_w.astype(BF16), row(pool_scale), seq=seq, tm=tm)
    bias_tiles = _bias_tiles(rel_table, seq // ATT_TILE)
    ya = _attention(q, k, vt, bias_tiles, row(lq1), row(lk1), row(lq2), row(lk2), subln_g.reshape(-1, 1),
                    bsz=bsz, seq=seq)
    x1 = _outproj(x2, yp, ya, w_out.astype(BF16), gate1, row(ln1_g), row(ln1_b), seq=seq, tm=tm)

    wr_t = w_router.T
    wr_hi = wr_t.astype(BF16)
    wr_lo = (wr_t - wr_hi.astype(F32)).astype(BF16)
    eidx_t, rank_t, wts_t, cnt, h2p = _route(x1, scale2, shift2, wr_hi, wr_lo, router_bias.reshape(-1, 1),
                                             seq=seq, tr=tr)

    m = DISPATCH_BLOCK
    counts = cnt[:, 0].astype(jnp.int32)
    padded = (counts + m - 1) // m * m
    pends = jnp.cumsum(padded)
    pstarts = pends - padded
    n_blocks = -(-(n * TOP_K + N_EXPERTS * (m - 1)) // m)
    n_used = (pends[-1:] // m).astype(jnp.int32)
    dest_t = _dest(eidx_t, rank_t, pstarts, tt=min(n, 2048))

    xs = _scatter_rows_sc(h2p, dest_t.reshape(TOP_K * n), n_blocks * m)
    ys = _experts(pstarts // m, padded // m, counts, n_used, xs, w_gate, w_up, w_down)

    shared_w = (ws_gate.astype(BF16), ws_up.astype(BF16), ws_down.astype(BF16))
    wts = wts_t.T
    part = n // COMBINE_PARTS
    assert part % tc == 0 and part * COMBINE_PARTS == n
    out = None
    for p in range(COMBINE_PARTS):
        idx = dest_t[:, p * part:(p + 1) * part].reshape(TOP_K * part)
        picked = _gather_rows_sc(ys, idx).reshape(TOP_K, part, d // 2)
        out = _combine(x1, scale2, shift2, gate2, wts, picked, *shared_w, row(ln2_g), row(ln2_b), out,
                       seq=seq, tc=tc, first_token=p * part)
    return out.reshape(bsz, seq, d)


def kernel(x, c, w_ada, b_ada, w_in, pool_w, pool_scale, lambda_q1, lambda_k1, lambda_q2, lambda_k2,
           subln_g, w_out, ln1_g, ln1_b, w_router, router_bias, w_gate, w_up, w_down,
           ws_gate, ws_up, ws_down, ln2_g, ln2_b, rel_table):
    per_layer = (w_ada, b_ada, w_in, pool_w, pool_scale, lambda_q1, lambda_k1, lambda_q2, lambda_k2,
                 subln_g, w_out, ln1_g, ln1_b, w_router, router_bias, w_gate, w_up, w_down,
                 ws_gate, ws_up, ws_down, ln2_g, ln2_b)
    assert all(a.shape[0] == DEPTH == 1 for a in per_layer)
    return _layer(x, c, *[a.reshape(a.shape[1:]) for a in per_layer], rel_table)
```

```python
import functools
import math

import jax
import jax.numpy as jnp
from jax import lax
from jax.experimental import pallas as pl
from jax.experimental.pallas import tpu as pltpu
from jax.experimental.pallas import tpu_sc as plsc

F32 = jnp.float32
BF16 = jnp.bfloat16
U32 = jnp.uint32
LANES = 128

CHUNK = 64
ATT_TILE = 256
LOG2_E = math.log2(math.e)
POOL_DIM = 512
POOL_WINDOWS = (2, 4, 8, 16)
POOL_GROUP_DIM = 128
MAX_WINDOW = max(POOL_WINDOWS)
ATTN_HEADS = 4
ATTN_HEAD_DIM = 64
QK_DIM = 512
V_DIM = 512
NUM_BUCKETS = 32
MAX_DISTANCE = 128
N_EXPERTS = 256
TOP_K = 8
N_GROUPS = 8
GROUP_SIZE = N_EXPERTS // N_GROUPS
TOP_K_GROUPS = 4
ROUTED_SCALE = 2.5
DISPATCH_BLOCK = 512
EXPERT_SLOTS = 8
COMBINE_PARTS = 8
SC_CORES, SC_SUBCORES = 2, 16
SC_ROWS = 64
DEPTH = 1
ALPHA = (2.0 * DEPTH) ** 0.25
LN_EPS = 1e-5
LAMBDA_INIT = 0.8 - 0.6 * math.exp(-0.3 * 0)

VMEM_LIMIT = 48 * 1024 * 1024


def _sigmoid(x):
    return 1.0 / (1.0 + jnp.exp(-x))


def _silu(x):
    return x * _sigmoid(x)


def _layer_norm(z, g, b):
    mu = jnp.mean(z, axis=-1, keepdims=True)
    zc = z - mu
    var = jnp.mean(zc * zc, axis=-1, keepdims=True)
    return zc * lax.rsqrt(var + LN_EPS) * g + b


def _params(sem=None):
    return pltpu.CompilerParams(dimension_semantics=sem, vmem_limit_bytes=VMEM_LIMIT)


def _mod_kernel(c_ref, w_ref, b_ref, o_ref):
    ca = _silu(c_ref[...])
    o_ref[...] = jnp.dot(ca, w_ref[...], preferred_element_type=F32,
                         precision=lax.Precision.HIGHEST) + b_ref[...]


def _modulation(c, w_ada, b_ada):
    bsz, d = c.shape
    n_out = w_ada.shape[1]
    return pl.pallas_call(
        _mod_kernel,
        grid=(n_out // d,),
        in_specs=[pl.BlockSpec((bsz, d), lambda j: (0, 0)),
                  pl.BlockSpec((d, d), lambda j: (0, j)),
                  pl.BlockSpec((1, d), lambda j: (0, j))],
        out_specs=pl.BlockSpec((bsz, d), lambda j: (0, j)),
        out_shape=jax.ShapeDtypeStruct((bsz, n_out), F32),
        compiler_params=_params(("arbitrary",)),
    )(c, w_ada, b_ada.reshape(1, n_out))


def _inproj_kernel(x_ref, sc_ref, sh_ref, w_ref, wvt_ref, pw_ref, ps_ref,
                   yp_ref, q_ref, k_ref, vt_ref, ext_ref, *, tm, seq):
    i = pl.program_id(0)
    tiles_per_seq = seq // tm
    it = i % tiles_per_seq
    h = x_ref[...] * (1.0 + sc_ref[0]) + sh_ref[0]
    hb = h.astype(BF16)
    proj = jnp.dot(hb, w_ref[...], preferred_element_type=F32)
    u = proj[:, :POOL_DIM]
    q_ref[...] = (proj[:, POOL_DIM:POOL_DIM + QK_DIM] * (ATTN_HEAD_DIM ** -0.5 * LOG2_E)).astype(BF16)
    k_ref[...] = proj[:, POOL_DIM + QK_DIM:POOL_DIM + 2 * QK_DIM].astype(BF16)
    vt = lax.dot_general(wvt_ref[...], hb, (((1,), (1,)), ((), ())), preferred_element_type=F32)
    for j in range(tm // ATT_TILE):
        vt_ref[0, j] = vt[:, j * ATT_TILE:(j + 1) * ATT_TILE].astype(BF16)

    @pl.when(it == 0)
    def _():
        ext_ref[0:MAX_WINDOW, :] = jnp.zeros((MAX_WINDOW, POOL_DIM), F32)

    ext_ref[MAX_WINDOW:MAX_WINDOW + tm, :] = u
    pos = (it * tm + lax.broadcasted_iota(jnp.int32, (tm, 1), 0) + 1).astype(F32)
    for g, w in enumerate(POOL_WINDOWS):
        c0, c1 = g * POOL_GROUP_DIM, (g + 1) * POOL_GROUP_DIM
        s = ext_ref[MAX_WINDOW:MAX_WINDOW + tm, c0:c1]
        for j in range(1, w):
            s = s + ext_ref[MAX_WINDOW - j:MAX_WINDOW - j + tm, c0:c1]
        pooled = s / jnp.minimum(pos, float(w)) - u[:, c0:c1]
        y = jnp.dot(pooled.astype(BF16), pw_ref[g], preferred_element_type=F32)
        yp_ref[:, c0:c1] = (y * ps_ref[:, c0:c1]).astype(BF16)
    ext_ref[0:MAX_WINDOW, :] = ext_ref[tm:tm + MAX_WINDOW, :]


def _inproj(x2, scale1, shift1, w_main, w_vt, pool_w, pool_scale, *, seq, tm):
    n, d = x2.shape
    assert n % tm == 0 and seq % tm == 0 and tm >= 2 * MAX_WINDOW and tm % ATT_TILE == 0
    tps = seq // tm
    tpt = tm // ATT_TILE
    mod_spec = pl.BlockSpec((1, 1, d), lambda i: (i // tps, 0, 0))
    row = lambda w: pl.BlockSpec((tm, w), lambda i: (i, 0))
    full = lambda a: pl.BlockSpec(a.shape, lambda i: (0,) * a.ndim)
    return pl.pallas_call(
        functools.partial(_inproj_kernel, tm=tm, seq=seq),
        grid=(n // tm,),
        in_specs=[row(d), mod_spec, mod_spec, full(w_main), full(w_vt), full(pool_w), full(pool_scale)],
        out_specs=[row(POOL_DIM), row(QK_DIM), row(QK_DIM),
                   pl.BlockSpec((1, tpt, V_DIM, ATT_TILE), lambda i: (i // tps, i % tps, 0, 0))],
        out_shape=[jax.ShapeDtypeStruct((n, POOL_DIM), BF16),
                   jax.ShapeDtypeStruct((n, QK_DIM), BF16),
                   jax.ShapeDtypeStruct((n, QK_DIM), BF16),
                   jax.ShapeDtypeStruct((n // seq, seq // ATT_TILE, V_DIM, ATT_TILE), BF16)],
        scratch_shapes=[pltpu.VMEM((tm + MAX_WINDOW, POOL_DIM), F32)],
        compiler_params=_params(("arbitrary",)),
    )(x2, scale1, shift1, w_main, w_vt, pool_w, pool_scale)


def _bias_kernel(tab_ref, o_ref):
    delta = pl.program_id(0)
    r = lax.broadcasted_iota(jnp.int32, (ATT_TILE, ATT_TILE), 0)
    c = lax.broadcasted_iota(jnp.int32, (ATT_TILE, ATT_TILE), 1)
    rel = r - c - delta * ATT_TILE
    half = NUM_BUCKETS // 2
    max_exact = half // 2
    ret = jnp.where(rel > 0, half, 0)
    n = jnp.abs(rel)
    nf = jnp.maximum(n, 1).astype(F32)
    large = max_exact + (jnp.log(nf / max_exact) / math.log(MAX_DISTANCE / max_exact)
                         * (half - max_exact)).astype(jnp.int32)
    large = jnp.minimum(large, half - 1)
    bucket = ret + jnp.where(n < max_exact, n, large)
    for h in range(ATTN_HEADS):
        acc = jnp.zeros((ATT_TILE, ATT_TILE), F32)
        for b in range(NUM_BUCKETS):
            acc = jnp.where(bucket == b, tab_ref[b, h], acc)
        o_ref[h, 0] = acc * LOG2_E


def _bias_tiles(rel_table, n_tiles):
    return pl.pallas_call(
        _bias_kernel,
        grid=(n_tiles,),
        in_specs=[pl.BlockSpec(memory_space=pltpu.SMEM)],
        out_specs=pl.BlockSpec((ATTN_HEADS, 1, ATT_TILE, ATT_TILE), lambda dlt: (0, dlt, 0, 0)),
        out_shape=jax.ShapeDtypeStruct((ATTN_HEADS, n_tiles, ATT_TILE, ATT_TILE), F32),
        compiler_params=_params(("arbitrary",)),
    )(rel_table)


def _attn_kernel(q_ref, k_ref, vt_ref, bias_ref, lq1_ref, lk1_ref, lq2_ref, lk2_ref, g_ref, o_ref, *acc_refs):
    qt = pl.program_id(1)
    t = ATT_TILE
    n_maps = 2 * ATTN_HEADS
    lam = (jnp.exp(jnp.sum(lq1_ref[...] * lk1_ref[...], axis=-1, keepdims=True))
           - jnp.exp(jnp.sum(lq2_ref[...] * lk2_ref[...], axis=-1, keepdims=True))
           + LAMBDA_INIT)
    r = lax.broadcasted_iota(jnp.int32, (t, t), 0)
    c = lax.broadcasted_iota(jnp.int32, (t, t), 1)
    allowed = (r // CHUNK) <= (c // CHUNK)
    hd2 = 2 * ATTN_HEAD_DIM

    ahead = 2

    def scores(kt, hm):
        col = hm * ATTN_HEAD_DIM
        qh = q_ref[:, col:col + ATTN_HEAD_DIM]
        kh = k_ref[pl.ds(pl.multiple_of(kt * t, t), t), col:col + ATTN_HEAD_DIM]
        return lax.dot_general(kh, qh, (((1,), (1,)), ((), ())),
                               preferred_element_type=F32) + bias_ref[hm // 2, qt - kt]

    def block(kt, carry, diagonal):
        stats, early = carry[:2 * n_maps], carry[2 * n_maps:]

        def softmax(hm, s):
            if diagonal:
                s = jnp.where(allowed, s, -jnp.inf)
            m_old, l_old = stats[2 * hm:2 * hm + 2]
            m_new = jnp.maximum(m_old, jnp.max(s, axis=0, keepdims=True))
            alpha = jnp.exp2(m_old - m_new)
            p = jnp.exp2(s - m_new)
            return m_new, alpha * l_old + jnp.sum(p, axis=0, keepdims=True), alpha, p.astype(BF16)

        def accumulate(hm, alpha, p):
            h = hm // 2
            vth = vt_ref[0, kt, h * hd2:(h + 1) * hd2, :]
            acc_refs[hm][...] = alpha * acc_refs[hm][...] + jnp.dot(vth, p, preferred_element_type=F32)

        s_vals = dict(enumerate(early))
        sm_vals, out, nxt = {}, [None] * (2 * n_maps), []
        for step in range(1, n_maps + ahead):
            if ahead <= step < n_maps:
                s_vals[step] = scores(kt, step)
            elif step >= n_maps and not diagonal:
                nxt.append(scores(kt + 1, step - n_maps))
            hm = step - 1
            if hm < n_maps:
                m_new, l_new, alpha, p = softmax(hm, s_vals.pop(hm))
                out[2 * hm], out[2 * hm + 1] = m_new, l_new
                sm_vals[hm] = (alpha, p)
            if step - 2 >= 0:
                accumulate(step - 2, *sm_vals.pop(step - 2))
        return tuple(out) + tuple(nxt)

    for acc in acc_refs:
        acc[...] = jnp.zeros_like(acc)
    one = (jnp.full((1, t), -jnp.inf, F32), jnp.zeros((1, t), F32))
    first = tuple(scores(0, hm) for hm in range(ahead))
    carry = lax.fori_loop(0, qt, lambda kt, cr: block(kt, cr, False), one * n_maps + first)
    carry = block(qt, carry, True)
    for h in range(ATTN_HEADS):
        l0, l1 = carry[4 * h + 1], carry[4 * h + 3]
        o = acc_refs[2 * h][...] / l0 - lam * (acc_refs[2 * h + 1][...] / l1)
        y = o * lax.rsqrt(jnp.mean(o * o, axis=0, keepdims=True) + LN_EPS) * g_ref[...]
        o_ref[:, h * hd2:(h + 1) * hd2] = (y * (1.0 - LAMBDA_INIT)).T.astype(BF16)


def _attention(q, k, vt, bias_tiles, lq1, lk1, lq2, lk2, subln_g, *, bsz, seq):
    t = ATT_TILE
    nt = seq // t
    full = lambda a: pl.BlockSpec(a.shape, lambda b, j: (0,) * a.ndim)
    return pl.pallas_call(
        _attn_kernel,
        grid=(bsz, nt),
        in_specs=[pl.BlockSpec((t, QK_DIM), lambda b, j: (b * nt + j, 0)),
                  pl.BlockSpec((seq, QK_DIM), lambda b, j: (b, 0)),
                  pl.BlockSpec((1, nt, V_DIM, t), lambda b, j: (b, 0, 0, 0)),
                  full(bias_tiles), full(lq1), full(lk1), full(lq2), full(lk2), full(subln_g)],
        out_specs=pl.BlockSpec((t, V_DIM), lambda b, j: (b * nt + j, 0)),
        out_shape=jax.ShapeDtypeStruct((bsz * seq, V_DIM), BF16),
        scratch_shapes=[pltpu.VMEM((2 * ATTN_HEAD_DIM, t), F32) for _ in range(2 * ATTN_HEADS)],
        compiler_params=_params(("arbitrary", "arbitrary")),
    )(q, k, vt, bias_tiles, lq1, lk1, lq2, lk2, subln_g)


def _outproj_kernel(x_ref, yp_ref, ya_ref, w_ref, gate_ref, g_ref, b_ref, o_ref):
    mix = (jnp.dot(yp_ref[...], w_ref[0:POOL_DIM, :], preferred_element_type=F32)
           + jnp.dot(ya_ref[...], w_ref[POOL_DIM:, :], preferred_element_type=F32))
    z = ALPHA * x_ref[...] + gate_ref[0] * mix
    o_ref[...] = _layer_norm(z, g_ref[...], b_ref[...])


def _outproj(x2, yp, ya, w_out, gate1, ln_g, ln_b, *, seq, tm):
    n, d = x2.shape
    tps = seq // tm
    row = lambda w: pl.BlockSpec((tm, w), lambda i: (i, 0))
    full = lambda a: pl.BlockSpec(a.shape, lambda i: (0,) * a.ndim)
    return pl.pallas_call(
        _outproj_kernel,
        grid=(n // tm,),
        in_specs=[row(d), row(POOL_DIM), row(V_DIM), full(w_out),
                  pl.BlockSpec((1, 1, d), lambda i: (i // tps, 0, 0)), full(ln_g), full(ln_b)],
        out_specs=row(d),
        out_shape=jax.ShapeDtypeStruct((n, d), F32),
        compiler_params=_params(("arbitrary",)),
    )(x2, yp, ya, w_out, gate1, ln_g, ln_b)


def _route_kernel(x_ref, sc_ref, sh_ref, whi_ref, wlo_ref, rb_ref,
                  eidx_ref, rank_ref, wts_ref, cnt_ref, h2p_ref, carry_ref, *, tr):
    i = pl.program_id(0)

    @pl.when(i == 0)
    def _():
        carry_ref[...] = jnp.zeros_like(carry_ref)

    h2 = x_ref[...] * (1.0 + sc_ref[0]) + sh_ref[0]
    h2p_ref[...] = _pack_bf16_halves(h2)
    hi = h2.astype(BF16)
    lo = (h2 - hi.astype(F32)).astype(BF16)
    nt = (((1,), (1,)), ((), ()))
    logits = (lax.dot_general(whi_ref[...], hi, nt, preferred_element_type=F32)
              + lax.dot_general(wlo_ref[...], hi, nt, preferred_element_type=F32)
              + lax.dot_general(whi_ref[...], lo, nt, preferred_element_type=F32))
    scores = _sigmoid(logits)
    sel = scores + rb_ref[...]
    erow = lax.broadcasted_iota(jnp.int32, (N_EXPERTS, tr), 0).astype(F32)

    g3 = sel.reshape(N_GROUPS, GROUP_SIZE, tr)
    r3 = lax.broadcasted_iota(jnp.int32, (N_GROUPS, GROUP_SIZE, tr), 1).astype(F32)
    m1 = jnp.max(g3, axis=1, keepdims=True)
    first = jnp.min(jnp.where(g3 == m1, r3, float(GROUP_SIZE)), axis=1, keepdims=True)
    m2 = jnp.max(jnp.where(r3 == first, -jnp.inf, g3), axis=1, keepdims=True)
    gscore = (m1 + m2).reshape(N_GROUPS, tr)

    gidx = lax.broadcasted_iota(jnp.int32, (N_GROUPS, tr), 0)
    beaten_by = jnp.zeros((N_GROUPS, tr), jnp.int32)
    for g in range(N_GROUPS):
        other = gscore[g:g + 1, :]
        wins = (other > gscore) | ((other == gscore) & (g < gidx))
        beaten_by = beaten_by + wins.astype(jnp.int32)
    dropped = jnp.where(beaten_by < TOP_K_GROUPS, 0.0, -jnp.inf)
    cur = (g3 + dropped.reshape(N_GROUPS, 1, tr)).reshape(N_EXPERTS, tr)

    picks, weights = [], []
    selmask = jnp.zeros((N_EXPERTS, tr), F32)
    for _ in range(TOP_K):
        mx = jnp.max(cur, axis=0, keepdims=True)
        pick = jnp.min(jnp.where(cur == mx, erow, float(N_EXPERTS)), axis=0, keepdims=True)
        onehot = erow == pick
        weights.append(jnp.sum(jnp.where(onehot, scores, 0.0), axis=0, keepdims=True))
        cur = jnp.where(onehot, -jnp.inf, cur)
        selmask = jnp.where(onehot, 1.0, selmask)
        picks.append(pick)

    t_from = lax.broadcasted_iota(jnp.int32, (tr, tr), 0)
    t_to = lax.broadcasted_iota(jnp.int32, (tr, tr), 1)
    earlier = jnp.where(t_from < t_to, 1.0, 0.0).astype(BF16)
    chosen = selmask.astype(BF16)
    carry = carry_ref[...]
    rankmat = (jnp.dot(chosen, earlier, preferred_element_type=F32)
               + jnp.concatenate([carry] * (tr // LANES), axis=1))
    carry_ref[...] = carry + jnp.dot(chosen, jnp.ones((tr, LANES), BF16), preferred_element_type=F32)
    cnt_ref[...] = carry_ref[...]

    wsum = weights[0]
    for wj in weights[1:]:
        wsum = wsum + wj
    row8 = lax.broadcasted_iota(jnp.int32, (TOP_K, tr), 0)
    eidx = jnp.zeros((TOP_K, tr), jnp.int32)
    rank = jnp.zeros((TOP_K, tr), jnp.int32)
    wts = jnp.zeros((TOP_K, tr), F32)
    for j in range(TOP_K):
        rk = jnp.sum(jnp.where(erow == picks[j], rankmat, 0.0), axis=0, keepdims=True)
        eidx = jnp.where(row8 == j, picks[j].astype(jnp.int32), eidx)
        rank = jnp.where(row8 == j, rk.astype(jnp.int32), rank)
        wts = jnp.where(row8 == j, weights[j] / wsum * ROUTED_SCALE, wts)
    eidx_ref[...] = eidx
    rank_ref[...] = rank
    wts_ref[...] = wts


def _route(x1, scale2, shift2, wr_hi, wr_lo, router_bias, *, seq, tr):
    n, d = x1.shape
    tps = seq // tr
    mod_spec = pl.BlockSpec((1, 1, d), lambda i: (i // tps, 0, 0))
    full = lambda a: pl.BlockSpec(a.shape, lambda i: (0,) * a.ndim)
    assert tr % LANES == 0
    k8 = pl.BlockSpec((TOP_K, tr), lambda i: (0, i))
    return pl.pallas_call(
        functools.partial(_route_kernel, tr=tr),
        grid=(n // tr,),
        in_specs=[pl.BlockSpec((tr, d), lambda i: (i, 0)), mod_spec, mod_spec,
                  full(wr_hi), full(wr_lo), full(router_bias)],
        out_specs=[k8, k8, k8, pl.BlockSpec((N_EXPERTS, LANES), lambda i: (0, 0)),
                   pl.BlockSpec((tr, d // 2), lambda i: (i, 0))],
        out_shape=[jax.ShapeDtypeStruct((TOP_K, n), jnp.int32),
                   jax.ShapeDtypeStruct((TOP_K, n), jnp.int32),
                   jax.ShapeDtypeStruct((TOP_K, n), F32),
                   jax.ShapeDtypeStruct((N_EXPERTS, LANES), F32),
                   jax.ShapeDtypeStruct((n, d // 2), U32)],
        scratch_shapes=[pltpu.VMEM((N_EXPERTS, LANES), F32)],
        compiler_params=_params(("arbitrary",)),
    )(x1, scale2, shift2, wr_hi, wr_lo, router_bias)


def _pack_bf16_halves(x):
    w = x.shape[1] // 2
    lo = pltpu.bitcast(x[:, :w].astype(BF16).astype(F32), U32) >> 16
    hi = pltpu.bitcast(x[:, w:].astype(BF16).astype(F32), U32) & jnp.uint32(0xFFFF0000)
    return lo | hi


def _unpack_bf16_halves(p):
    return pltpu.bitcast(p << 16, F32), pltpu.bitcast(p & jnp.uint32(0xFFFF0000), F32)


def _sc_worker():
    return lax.axis_index("subcore") * SC_CORES + lax.axis_index("core")


def _scatter_rows_sc(rows, idx, n_slots):
    n, width = rows.shape
    k = idx.shape[0] // n
    n_workers = SC_CORES * SC_SUBCORES
    w = SC_ROWS
    per_worker = n // n_workers
    n_chunks = per_worker // w
    assert n == n_workers * n_chunks * w and n_chunks % 2 == 0 and idx.shape[0] == k * n
    mesh = plsc.VectorSubcoreMesh(core_axis_name="core", subcore_axis_name="subcore")

    @functools.partial(
        pl.kernel, mesh=mesh,
        out_type=jax.ShapeDtypeStruct((n_slots, width), rows.dtype),
        scratch_types=[pltpu.VMEM((k * per_worker,), jnp.int32), pltpu.VMEM((2, w, width), rows.dtype),
                       pltpu.SemaphoreType.DMA((2,)), pltpu.SemaphoreType.DMA((2,))])
    def scatter(rows_hbm, idx_hbm, out_hbm, idx_v, rows_v, in_sem, out_sem):
        t0 = _sc_worker() * per_worker
        for j in range(k):
            pltpu.sync_copy(idx_hbm.at[pl.ds(j * n + t0, per_worker)], idx_v.at[pl.ds(j * per_worker, per_worker)])

        def load(i, s):
            return pltpu.make_async_copy(rows_hbm.at[pl.ds(t0 + i * w, w)], rows_v.at[s], in_sem.at[s])

        def send(i, s, j):
            slots = idx_v.at[pl.ds(j * per_worker + i * w, w)]
            return pltpu.make_async_copy(rows_v.at[s], out_hbm.at[slots], out_sem.at[s])

        load(0, 0).start()

        def pair(ii, carry):
            for s in range(2):
                i = ii * 2 + s
                load(i, s).wait()

                @pl.when(i >= 1)
                def _():
                    for j in range(k):
                        send(i - 1, 1 - s, j).wait()

                @pl.when(i + 1 < n_chunks)
                def _():
                    load(i + 1, 1 - s).start()

                for j in range(k):
                    send(i, s, j).start()
            return carry

        lax.fori_loop(0, n_chunks // 2, pair, 0)
        for j in range(k):
            send(n_chunks - 1, 1, j).wait()

    return scatter(rows, idx)


def _dest_kernel(eidx_ref, rank_ref, first_ref, dest_ref):
    tt = eidx_ref.shape[1]
    pieces = [jnp.broadcast_to(first_ref[:, p * LANES:(p + 1) * LANES], (TOP_K, LANES))
              for p in range(N_EXPERTS // LANES)]
    for c in range(tt // LANES):
        cols = slice(c * LANES, (c + 1) * LANES)
        e = eidx_ref[:, cols]
        within = e & (LANES - 1)
        start = jnp.take_along_axis(pieces[0], within, axis=1)
        for p in range(1, len(pieces)):
            start = jnp.where(e // LANES == p, jnp.take_along_axis(pieces[p], within, axis=1), start)
        dest_ref[:, cols] = start + rank_ref[:, cols]


def _dest(eidx_t, rank_t, pstarts, *, tt):
    n = eidx_t.shape[1]
    first = pstarts.reshape(1, N_EXPERTS)
    k8 = pl.BlockSpec((TOP_K, tt), lambda i: (0, i))
    return pl.pallas_call(
        _dest_kernel,
        grid=(n // tt,),
        in_specs=[k8, k8, pl.BlockSpec(first.shape, lambda i: (0, 0))],
        out_specs=k8,
        out_shape=jax.ShapeDtypeStruct((TOP_K, n), jnp.int32),
        compiler_params=_params(("arbitrary",)),
    )(eidx_t, rank_t, first)


def _expert_kernel(fb_ref, nb_ref, cnt_ref, nused_ref, wg_ref, wu_ref, wd_ref, xs_ref, ys_ref,
                   wgb, wub, wdb, xbuf, ybuf, in_sem, out_sem):
    e = pl.program_id(0)
    m = DISPATCH_BLOCK
    ns = EXPERT_SLOTS
    n_used = nused_ref[0]

    def rows(g):
        return pl.ds(pl.multiple_of(g * m, m), m)

    def fetch(g):
        slot = g & (ns - 1)
        return pltpu.make_async_copy(xs_ref.at[rows(g), :], xbuf.at[slot], in_sem.at[slot])

    def put(g):
        slot = g & (ns - 1)
        return pltpu.make_async_copy(ybuf.at[slot], ys_ref.at[rows(g), :], out_sem.at[slot])

    @pl.when(e == 0)
    def _():
        for g0 in range(ns - 1):
            @pl.when(g0 < n_used)
            def _(g0=g0):
                fetch(g0).start()

    wgb[...] = wg_ref[0].astype(BF16)
    wub[...] = wu_ref[0].astype(BF16)
    wdb[...] = wd_ref[0].astype(BF16)

    def block(i, carry):
        g = fb_ref[e] + i
        slot = g & (ns - 1)
        fetch(g).wait()

        @pl.when(g + ns - 1 < n_used)
        def _():
            fetch(g + ns - 1).start()

        @pl.when(g >= ns)
        def _():
            put(g - ns).wait()

        n_valid = cnt_ref[e] - i * m

        def swiglu(rows):
            row = lax.broadcasted_iota(jnp.int32, (rows, 1), 0)
            packed = jnp.where(row < n_valid, xbuf[slot, 0:rows], jnp.uint32(0))
            x_lo, x_hi = [h.astype(BF16) for h in _unpack_bf16_halves(packed)]
            half = x_lo.shape[1]

            def up_proj(w):
                return (jnp.dot(x_lo, w[0:half, :], preferred_element_type=F32)
                        + jnp.dot(x_hi, w[half:, :], preferred_element_type=F32))

            a = (_silu(up_proj(wgb)) * up_proj(wub)).astype(BF16)
            ybuf[slot, 0:rows] = _pack_bf16_halves(jnp.dot(a, wdb[...], preferred_element_type=F32))

        pl.when(n_valid > m // 2)(functools.partial(swiglu, m))
        pl.when(n_valid <= m // 2)(functools.partial(swiglu, m // 2))
        put(g).start()
        return carry

    lax.fori_loop(0, nb_ref[e], block, 0)

    @pl.when(e == pl.num_programs(0) - 1)
    def _():
        for back in range(ns, 0, -1):
            @pl.when(n_used >= back)
            def _(back=back):
                put(n_used - back).wait()

        ybuf[0] = jnp.zeros(ybuf.shape[1:], ybuf.dtype)
        n_blocks = ys_ref.shape[0] // m

        def tail(wait):
            def body(g, carry):
                cp = pltpu.make_async_copy(ybuf.at[0], ys_ref.at[rows(g), :], out_sem.at[0])
                cp.wait() if wait else cp.start()
                return carry
            return body

        lax.fori_loop(n_used, n_blocks, tail(False), 0)
        lax.fori_loop(n_used, n_blocks, tail(True), 0)


def _experts(first_block, n_blocks_e, counts, n_used, xs, w_gate, w_up, w_down):
    p, dp = xs.shape
    m = DISPATCH_BLOCK
    n_e, d, f = w_gate.shape
    assert d == 2 * dp and p % m == 0
    grid_spec = pltpu.PrefetchScalarGridSpec(
        num_scalar_prefetch=4,
        grid=(n_e,),
        in_specs=[pl.BlockSpec((1, d, f), lambda e, *_: (e, 0, 0)),
                  pl.BlockSpec((1, d, f), lambda e, *_: (e, 0, 0)),
                  pl.BlockSpec((1, f, d), lambda e, *_: (e, 0, 0)),
                  pl.BlockSpec(memory_space=pl.ANY)],
        out_specs=pl.BlockSpec(memory_space=pl.ANY),
        scratch_shapes=[pltpu.VMEM((d, f), BF16), pltpu.VMEM((d, f), BF16), pltpu.VMEM((f, d), BF16),
                        pltpu.VMEM((EXPERT_SLOTS, m, dp), U32), pltpu.VMEM((EXPERT_SLOTS, m, dp), U32),
                        pltpu.SemaphoreType.DMA((EXPERT_SLOTS,)), pltpu.SemaphoreType.DMA((EXPERT_SLOTS,))],
    )
    return pl.pallas_call(
        _expert_kernel,
        grid_spec=grid_spec,
        out_shape=jax.ShapeDtypeStruct((p, dp), U32),
        compiler_params=_params(("arbitrary",)),
    )(first_block, n_blocks_e, counts, n_used, w_gate, w_up, w_down, xs)


def _gather_rows_sc(table, idx):
    n_idx, (_, width) = idx.shape[0], table.shape
    n_workers = SC_CORES * SC_SUBCORES
    w = SC_ROWS
    per_worker = n_idx // n_workers
    n_chunks = per_worker // w
    assert n_idx == n_workers * n_chunks * w and n_chunks % 2 == 0
    mesh = plsc.VectorSubcoreMesh(core_axis_name="core", subcore_axis_name="subcore")

    @functools.partial(
        pl.kernel, mesh=mesh,
        out_type=jax.ShapeDtypeStruct((n_idx, width), table.dtype),
        scratch_types=[pltpu.VMEM((per_worker,), jnp.int32), pltpu.VMEM((2, w, width), table.dtype),
                       pltpu.SemaphoreType.DMA((2,)), pltpu.SemaphoreType.DMA((2,))])
    def gather(table_hbm, idx_hbm, out_hbm, idx_v, rows_v, in_sem, out_sem):
        base = _sc_worker() * per_worker
        pltpu.sync_copy(idx_hbm.at[pl.ds(base, per_worker)], idx_v)

        def fetch(i, s):
            return pltpu.make_async_copy(table_hbm.at[idx_v.at[pl.ds(i * w, w)]], rows_v.at[s], in_sem.at[s])

        def put(i, s):
            return pltpu.make_async_copy(rows_v.at[s], out_hbm.at[pl.ds(base + i * w, w)], out_sem.at[s])

        fetch(0, 0).start()

        def pair(ii, carry):
            for s in range(2):
                i = ii * 2 + s
                fetch(i, s).wait()

                @pl.when(i >= 1)
                def _():
                    put(i - 1, 1 - s).wait()

                @pl.when(i + 1 < n_chunks)
                def _():
                    fetch(i + 1, 1 - s).start()

                put(i, s).start()
            return carry

        lax.fori_loop(0, n_chunks // 2, pair, 0)
        put(n_chunks - 1, 1).wait()

    return gather(table, idx)


def _combine_kernel(x_ref, sc_ref, sh_ref, gate_ref, wts_ref, rows_ref,
                    wsg_ref, wsu_ref, wsd_ref, g_ref, b_ref, *out_refs):
    o_ref = out_refs[-1]
    x = x_ref[...]
    hb = (x * (1.0 + sc_ref[0]) + sh_ref[0]).astype(BF16)
    sg = jnp.dot(hb, wsg_ref[...], preferred_element_type=F32)
    su = jnp.dot(hb, wsu_ref[...], preferred_element_type=F32)
    shared = jnp.dot((_silu(sg) * su).astype(BF16), wsd_ref[...], preferred_element_type=F32)

    wts = wts_ref[...]
    half = shared.shape[1] // 2
    lo, hi = shared[:, :half], shared[:, half:]
    for j in range(TOP_K):
        y_lo, y_hi = _unpack_bf16_halves(rows_ref[j])
        lo = lo + wts[:, j:j + 1] * y_lo
        hi = hi + wts[:, j:j + 1] * y_hi
    z = ALPHA * x + gate_ref[0] * jnp.concatenate([lo, hi], axis=1)
    o_ref[...] = _layer_norm(z, g_ref[...], b_ref[...])


def _combine(x1, scale2, shift2, gate2, wts, rows, ws_gate, ws_up, ws_down, ln_g, ln_b, prev_out,
             *, seq, tc, first_token):
    n, d = x1.shape
    tps = seq // tc
    off = first_token // tc
    n_tiles = rows.shape[1] // tc
    mod_spec = pl.BlockSpec((1, 1, d), lambda i: ((i + off) // tps, 0, 0))
    full = lambda a: pl.BlockSpec(a.shape, lambda i: (0,) * a.ndim)
    args = [x1, scale2, shift2, gate2, wts, rows, ws_gate, ws_up, ws_down, ln_g, ln_b]
    in_specs = [pl.BlockSpec((tc, d), lambda i: (i + off, 0)), mod_spec, mod_spec, mod_spec,
                pl.BlockSpec((tc, TOP_K), lambda i: (i + off, 0)),
                pl.BlockSpec((TOP_K, tc, d // 2), lambda i: (0, i, 0)),
                full(ws_gate), full(ws_up), full(ws_down), full(ln_g), full(ln_b)]
    aliases = {}
    if prev_out is not None:
        aliases = {len(args): 0}
        args.append(prev_out)
        in_specs.append(pl.BlockSpec(memory_space=pl.ANY))
    return pl.pallas_call(
        _combine_kernel,
        grid=(n_tiles,),
        in_specs=in_specs,
        out_specs=pl.BlockSpec((tc, d), lambda i: (i + off, 0)),
        out_shape=jax.ShapeDtypeStruct((n, d), F32),
        input_output_aliases=aliases,
        compiler_params=_params(("arbitrary",)),
    )(*args)


def _layer(x, c, w_ada, b_ada, w_in, pool_w, pool_scale, lq1, lk1, lq2, lk2, subln_g, w_out,
           ln1_g, ln1_b, w_router, router_bias, w_gate, w_up, w_down, ws_gate, ws_up, ws_down,
           ln2_g, ln2_b, rel_table, *, tm=1024, tr=256, tc=512):
    bsz, seq, d = x.shape
    n = bsz * seq
    x2 = x.reshape(n, d)
    row = lambda a: a.reshape(1, -1)

    mod = _modulation(c, w_ada, b_ada)
    shift1, scale1, gate1, shift2, scale2, gate2 = [
        mod[:, j * d:(j + 1) * d].reshape(bsz, 1, d) for j in range(6)]

    n_main = POOL_DIM + 2 * QK_DIM
    yp, q, k, vt = _inproj(x2, scale1, shift1, w_in[:, :n_main].astype(BF16), w_in[:, n_main:].T.astype(BF16),
                           pool_w.astype(BF16), row(pool_scale), seq=seq, tm=tm)
    bias_tiles = _bias_tiles(rel_table, seq // ATT_TILE)
    ya = _attention(q, k, vt, bias_tiles, row(lq1), row(lk1), row(lq2), row(lk2), subln_g.reshape(-1, 1),
                    bsz=bsz, seq=seq)
    x1 = _outproj(x2, yp, ya, w_out.astype(BF16), gate1, row(ln1_g), row(ln1_b), seq=seq, tm=tm)

    wr_t = w_router.T
    wr_hi = wr_t.astype(BF16)
    wr_lo = (wr_t - wr_hi.astype(F32)).astype(BF16)
    eidx_t, rank_t, wts_t, cnt, h2p = _route(x1, scale2, shift2, wr_hi, wr_lo, router_bias.reshape(-1, 1),
                                             seq=seq, tr=tr)

    m = DISPATCH_BLOCK
    counts = cnt[:, 0].astype(jnp.int32)
    padded = (counts + m - 1) // m * m
    pends = jnp.cumsum(padded)
    pstarts = pends - padded
    n_blocks = -(-(n * TOP_K + N_EXPERTS * (m - 1)) // m)
    n_used = (pends[-1:] // m).astype(jnp.int32)
    dest_t = _dest(eidx_t, rank_t, pstarts, tt=min(n, 2048))

    xs = _scatter_rows_sc(h2p, dest_t.reshape(TOP_K * n), n_blocks * m)
    ys = _experts(pstarts // m, padded // m, counts, n_used, xs, w_gate, w_up, w_down)

    shared_w = (ws_gate.astype(BF16), ws_up.astype(BF16), ws_down.astype(BF16))
    wts = wts_t.T
    part = n // COMBINE_PARTS
    assert part % tc == 0 and part * COMBINE_PARTS == n
    out = None
    for p in range(COMBINE_PARTS):
        idx = dest_t[:, p * part:(p + 1) * part].reshape(TOP_K * part)
        picked = _gather_rows_sc(ys, idx).reshape(TOP_K, part, d // 2)
        out = _combine(x1, scale2, shift2, gate2, wts, picked, *shared_w, row(ln2_g), row(ln2_b), out,
                       seq=seq, tc=tc, first_token=p * part)
    return out.reshape(bsz, seq, d)


def kernel(x, c, w_ada, b_ada, w_in, pool_w, pool_scale, lambda_q1, lambda_k1, lambda_q2, lambda_k2,
           subln_g, w_out, ln1_g, ln1_b, w_router, router_bias, w_gate, w_up, w_down,
           ws_gate, ws_up, ws_down, ln2_g, ln2_b, rel_table):
    per_layer = (w_ada, b_ada, w_in, pool_w, pool_scale, lambda_q1, lambda_k1, lambda_q2, lambda_k2,
                 subln_g, w_out, ln1_g, ln1_b, w_router, router_bias, w_gate, w_up, w_down,
                 ws_gate, ws_up, ws_down, ln2_g, ln2_b)
    assert all(a.shape[0] == DEPTH == 1 for a in per_layer)
    return _layer(x, c, *[a.reshape(a.shape[1:]) for a in per_layer], rel_table)
```

```python
import functools
import math

import jax
import jax.numpy as jnp
from jax import lax
from jax.experimental import pallas as pl
from jax.experimental.pallas import tpu as pltpu
from jax.experimental.pallas import tpu_sc as plsc

F32 = jnp.float32
BF16 = jnp.bfloat16
U32 = jnp.uint32
LANES = 128

CHUNK = 64
ATT_TILE = 256
LOG2_E = math.log2(math.e)
POOL_DIM = 512
POOL_WINDOWS = (2, 4, 8, 16)
POOL_GROUP_DIM = 128
MAX_WINDOW = max(POOL_WINDOWS)
ATTN_HEADS = 4
ATTN_HEAD_DIM = 64
QK_DIM = 512
V_DIM = 512
NUM_BUCKETS = 32
MAX_DISTANCE = 128
N_EXPERTS = 256
TOP_K = 8
N_GROUPS = 8
GROUP_SIZE = N_EXPERTS // N_GROUPS
TOP_K_GROUPS = 4
ROUTED_SCALE = 2.5
DISPATCH_BLOCK = 512
EXPERT_SLOTS = 8
COMBINE_PARTS = 8
SC_CORES, SC_SUBCORES = 2, 16
SC_ROWS = 64
DEPTH = 1
ALPHA = (2.0 * DEPTH) ** 0.25
LN_EPS = 1e-5
LAMBDA_INIT = 0.8 - 0.6 * math.exp(-0.3 * 0)

VMEM_LIMIT = 48 * 1024 * 1024


def _sigmoid(x):
    return 1.0 / (1.0 + jnp.exp(-x))


def _silu(x):
    return x * _sigmoid(x)


def _layer_norm(z, g, b):
    mu = jnp.mean(z, axis=-1, keepdims=True)
    zc = z - mu
    var = jnp.mean(zc * zc, axis=-1, keepdims=True)
    return zc * lax.rsqrt(var + LN_EPS) * g + b


def _params(sem=None):
    return pltpu.CompilerParams(dimension_semantics=sem, vmem_limit_bytes=VMEM_LIMIT)


def _mod_kernel(c_ref, w_ref, b_ref, o_ref):
    ca = _silu(c_ref[...])
    o_ref[...] = jnp.dot(ca, w_ref[...], preferred_element_type=F32,
                         precision=lax.Precision.HIGHEST) + b_ref[...]


def _modulation(c, w_ada, b_ada):
    bsz, d = c.shape
    n_out = w_ada.shape[1]
    return pl.pallas_call(
        _mod_kernel,
        grid=(n_out // d,),
        in_specs=[pl.BlockSpec((bsz, d), lambda j: (0, 0)),
                  pl.BlockSpec((d, d), lambda j: (0, j)),
                  pl.BlockSpec((1, d), lambda j: (0, j))],
        out_specs=pl.BlockSpec((bsz, d), lambda j: (0, j)),
        out_shape=jax.ShapeDtypeStruct((bsz, n_out), F32),
        compiler_params=_params(("arbitrary",)),
    )(c, w_ada, b_ada.reshape(1, n_out))


def _inproj_kernel(x_ref, sc_ref, sh_ref, w_ref, wvt_ref, pw_ref, ps_ref,
                   yp_ref, q_ref, k_ref, vt_ref, ext_ref, *, tm, seq):
    i = pl.program_id(0)
    tiles_per_seq = seq // tm
    it = i % tiles_per_seq
    h = x_ref[...] * (1.0 + sc_ref[0]) + sh_ref[0]
    hb = h.astype(BF16)
    proj = jnp.dot(hb, w_ref[...], preferred_element_type=F32)
    u = proj[:, :POOL_DIM]
    q_ref[...] = (proj[:, POOL_DIM:POOL_DIM + QK_DIM] * (ATTN_HEAD_DIM ** -0.5 * LOG2_E)).astype(BF16)
    k_ref[...] = proj[:, POOL_DIM + QK_DIM:POOL_DIM + 2 * QK_DIM].astype(BF16)
    vt = lax.dot_general(wvt_ref[...], hb, (((1,), (1,)), ((), ())), preferred_element_type=F32)
    for j in range(tm // ATT_TILE):
        vt_ref[0, j] = vt[:, j * ATT_TILE:(j + 1) * ATT_TILE].astype(BF16)

    @pl.when(it == 0)
    def _():
        ext_ref[0:MAX_WINDOW, :] = jnp.zeros((MAX_WINDOW, POOL_DIM), F32)

    ext_ref[MAX_WINDOW:MAX_WINDOW + tm, :] = u
    pos = (it * tm + lax.broadcasted_iota(jnp.int32, (tm, 1), 0) + 1).astype(F32)
    for g, w in enumerate(POOL_WINDOWS):
        c0, c1 = g * POOL_GROUP_DIM, (g + 1) * POOL_GROUP_DIM
        s = ext_ref[MAX_WINDOW:MAX_WINDOW + tm, c0:c1]
        for j in range(1, w):
            s = s + ext_ref[MAX_WINDOW - j:MAX_WINDOW - j + tm, c0:c1]
        pooled = s / jnp.minimum(pos, float(w)) - u[:, c0:c1]
        y = jnp.dot(pooled.astype(BF16), pw_ref[g], preferred_element_type=F32)
        yp_ref[:, c0:c1] = (y * ps_ref[:, c0:c1]).astype(BF16)
    ext_ref[0:MAX_WINDOW, :] = ext_ref[tm:tm + MAX_WINDOW, :]


def _inproj(x2, scale1, shift1, w_main, w_vt, pool_w, pool_scale, *, seq, tm):
    n, d = x2.shape
    assert n % tm == 0 and seq % tm == 0 and tm >= 2 * MAX_WINDOW and tm % ATT_TILE == 0
    tps = seq // tm
    tpt = tm // ATT_TILE
    mod_spec = pl.BlockSpec((1, 1, d), lambda i: (i // tps, 0, 0))
    row = lambda w: pl.BlockSpec((tm, w), lambda i: (i, 0))
    full = lambda a: pl.BlockSpec(a.shape, lambda i: (0,) * a.ndim)
    return pl.pallas_call(
        functools.partial(_inproj_kernel, tm=tm, seq=seq),
        grid=(n // tm,),
        in_specs=[row(d), mod_spec, mod_spec, full(w_main), full(w_vt), full(pool_w), full(pool_scale)],
        out_specs=[row(POOL_DIM), row(QK_DIM), row(QK_DIM),
                   pl.BlockSpec((1, tpt, V_DIM, ATT_TILE), lambda i: (i // tps, i % tps, 0, 0))],
        out_shape=[jax.ShapeDtypeStruct((n, POOL_DIM), BF16),
                   jax.ShapeDtypeStruct((n, QK_DIM), BF16),
                   jax.ShapeDtypeStruct((n, QK_DIM), BF16),
                   jax.ShapeDtypeStruct((n // seq, seq // ATT_TILE, V_DIM, ATT_TILE), BF16)],
        scratch_shapes=[pltpu.VMEM((tm + MAX_WINDOW, POOL_DIM), F32)],
        compiler_params=_params(("arbitrary",)),
    )(x2, scale1, shift1, w_main, w_vt, pool_w, pool_scale)


def _bias_kernel(tab_ref, o_ref):
    delta = pl.program_id(0)
    r = lax.broadcasted_iota(jnp.int32, (ATT_TILE, ATT_TILE), 0)
    c = lax.broadcasted_iota(jnp.int32, (ATT_TILE, ATT_TILE), 1)
    rel = r - c - delta * ATT_TILE
    half = NUM_BUCKETS // 2
    max_exact = half // 2
    ret = jnp.where(rel > 0, half, 0)
    n = jnp.abs(rel)
    nf = jnp.maximum(n, 1).astype(F32)
    large = max_exact + (jnp.log(nf / max_exact) / math.log(MAX_DISTANCE / max_exact)
                         * (half - max_exact)).astype(jnp.int32)
    large = jnp.minimum(large, half - 1)
    bucket = ret + jnp.where(n < max_exact, n, large)
    for h in range(ATTN_HEADS):
        acc = jnp.zeros((ATT_TILE, ATT_TILE), F32)
        for b in range(NUM_BUCKETS):
            acc = jnp.where(bucket == b, tab_ref[b, h], acc)
        o_ref[h, 0] = acc * LOG2_E


def _bias_tiles(rel_table, n_tiles):
    return pl.pallas_call(
        _bias_kernel,
        grid=(n_tiles,),
        in_specs=[pl.BlockSpec(memory_space=pltpu.SMEM)],
        out_specs=pl.BlockSpec((ATTN_HEADS, 1, ATT_TILE, ATT_TILE), lambda dlt: (0, dlt, 0, 0)),
        out_shape=jax.ShapeDtypeStruct((ATTN_HEADS, n_tiles, ATT_TILE, ATT_TILE), F32),
        compiler_params=_params(("arbitrary",)),
    )(rel_table)


def _attn_kernel(q_ref, k_ref, vt_ref, bias_ref, lq1_ref, lk1_ref, lq2_ref, lk2_ref, g_ref, o_ref, *acc_refs):
    qt = pl.program_id(1)
    t = ATT_TILE
    n_maps = 2 * ATTN_HEADS
    lam = (jnp.exp(jnp.sum(lq1_ref[...] * lk1_ref[...], axis=-1, keepdims=True))
           - jnp.exp(jnp.sum(lq2_ref[...] * lk2_ref[...], axis=-1, keepdims=True))
           + LAMBDA_INIT)
    r = lax.broadcasted_iota(jnp.int32, (t, t), 0)
    c = lax.broadcasted_iota(jnp.int32, (t, t), 1)
    allowed = (r // CHUNK) <= (c // CHUNK)
    hd2 = 2 * ATTN_HEAD_DIM

    ahead = 2

    def scores(kt, hm):
        col = hm * ATTN_HEAD_DIM
        qh = q_ref[:, col:col + ATTN_HEAD_DIM]
        kh = k_ref[pl.ds(pl.multiple_of(kt * t, t), t), col:col + ATTN_HEAD_DIM]
        return lax.dot_general(kh, qh, (((1,), (1,)), ((), ())),
                               preferred_element_type=F32) + bias_ref[hm // 2, qt - kt]

    def block(kt, carry, diagonal):
        stats, early = carry[:2 * n_maps], carry[2 * n_maps:]

        def softmax(hm, s):
            if diagonal:
                s = jnp.where(allowed, s, -jnp.inf)
            m_old, l_old = stats[2 * hm:2 * hm + 2]
            m_new = jnp.maximum(m_old, jnp.max(s, axis=0, keepdims=True))
            alpha = jnp.exp2(m_old - m_new)
            p = jnp.exp2(s - m_new)
            return m_new, alpha * l_old + jnp.sum(p, axis=0, keepdims=True), alpha, p.astype(BF16)

        def accumulate(hm, alpha, p):
            h = hm // 2
            vth = vt_ref[0, kt, h * hd2:(h + 1) * hd2, :]
            acc_refs[hm][...] = alpha * acc_refs[hm][...] + jnp.dot(vth, p, preferred_element_type=F32)

        s_vals = dict(enumerate(early))
        sm_vals, out, nxt = {}, [None] * (2 * n_maps), []
        for step in range(1, n_maps + ahead):
            if ahead <= step < n_maps:
                s_vals[step] = scores(kt, step)
            elif step >= n_maps and not diagonal:
                nxt.append(scores(kt + 1, step - n_maps))
            hm = step - 1
            if hm < n_maps:
                m_new, l_new, alpha, p = softmax(hm, s_vals.pop(hm))
                out[2 * hm], out[2 * hm + 1] = m_new, l_new
                sm_vals[hm] = (alpha, p)
            if step - 2 >= 0:
                accumulate(step - 2, *sm_vals.pop(step - 2))
        return tuple(out) + tuple(nxt)

    for acc in acc_refs:
        acc[...] = jnp.zeros_like(acc)
    one = (jnp.full((1, t), -jnp.inf, F32), jnp.zeros((1, t), F32))
    first = tuple(scores(0, hm) for hm in range(ahead))
    carry = lax.fori_loop(0, qt, lambda kt, cr: block(kt, cr, False), one * n_maps + first)
    carry = block(qt, carry, True)
    for h in range(ATTN_HEADS):
        l0, l1 = carry[4 * h + 1], carry[4 * h + 3]
        o = acc_refs[2 * h][...] / l0 - lam * (acc_refs[2 * h + 1][...] / l1)
        y = o * lax.rsqrt(jnp.mean(o * o, axis=0, keepdims=True) + LN_EPS) * g_ref[...]
        o_ref[:, h * hd2:(h + 1) * hd2] = (y * (1.0 - LAMBDA_INIT)).T.astype(BF16)


def _attention(q, k, vt, bias_tiles, lq1, lk1, lq2, lk2, subln_g, *, bsz, seq):
    t = ATT_TILE
    nt = seq // t
    full = lambda a: pl.BlockSpec(a.shape, lambda b, j: (0,) * a.ndim)
    return pl.pallas_call(
        _attn_kernel,
        grid=(bsz, nt),
        in_specs=[pl.BlockSpec((t, QK_DIM), lambda b, j: (b * nt + j, 0)),
                  pl.BlockSpec((seq, QK_DIM), lambda b, j: (b, 0)),
                  pl.BlockSpec((1, nt, V_DIM, t), lambda b, j: (b, 0, 0, 0)),
                  full(bias_tiles), full(lq1), full(lk1), full(lq2), full(lk2), full(subln_g)],
        out_specs=pl.BlockSpec((t, V_DIM), lambda b, j: (b * nt + j, 0)),
        out_shape=jax.ShapeDtypeStruct((bsz * seq, V_DIM), BF16),
        scratch_shapes=[pltpu.VMEM((2 * ATTN_HEAD_DIM, t), F32) for _ in range(2 * ATTN_HEADS)],
        compiler_params=_params(("arbitrary", "arbitrary")),
    )(q, k, vt, bias_tiles, lq1, lk1, lq2, lk2, subln_g)


def _outproj_kernel(x_ref, yp_ref, ya_ref, w_ref, gate_ref, g_ref, b_ref, o_ref):
    mix = (jnp.dot(yp_ref[...], w_ref[0:POOL_DIM, :], preferred_element_type=F32)
           + jnp.dot(ya_ref[...], w_ref[POOL_DIM:, :], preferred_element_type=F32))
    z = ALPHA * x_ref[...] + gate_ref[0] * mix
    o_ref[...] = _layer_norm(z, g_ref[...], b_ref[...])


def _outproj(x2, yp, ya, w_out, gate1, ln_g, ln_b, *, seq, tm):
    n, d = x2.shape
    tps = seq // tm
    row = lambda w: pl.BlockSpec((tm, w), lambda i: (i, 0))
    full = lambda a: pl.BlockSpec(a.shape, lambda i: (0,) * a.ndim)
    return pl.pallas_call(
        _outproj_kernel,
        grid=(n // tm,),
        in_specs=[row(d), row(POOL_DIM), row(V_DIM), full(w_out),
                  pl.BlockSpec((1, 1, d), lambda i: (i // tps, 0, 0)), full(ln_g), full(ln_b)],
        out_specs=row(d),
        out_shape=jax.ShapeDtypeStruct((n, d), F32),
        compiler_params=_params(("arbitrary",)),
    )(x2, yp, ya, w_out, gate1, ln_g, ln_b)


def _route_kernel(x_ref, sc_ref, sh_ref, whi_ref, wlo_ref, rb_ref,
                  eidx_ref, rank_ref, wts_ref, cnt_ref, h2p_ref, carry_ref, *, tr):
    i = pl.program_id(0)

    @pl.when(i == 0)
    def _():
        carry_ref[...] = jnp.zeros_like(carry_ref)

    h2 = x_ref[...] * (1.0 + sc_ref[0]) + sh_ref[0]
    h2p_ref[...] = _pack_bf16_halves(h2)
    hi = h2.astype(BF16)
    lo = (h2 - hi.astype(F32)).astype(BF16)
    nt = (((1,), (1,)), ((), ()))
    logits = (lax.dot_general(whi_ref[...], hi, nt, preferred_element_type=F32)
              + lax.dot_general(wlo_ref[...], hi, nt, preferred_element_type=F32)
              + lax.dot_general(whi_ref[...], lo, nt, preferred_element_type=F32))
    scores = _sigmoid(logits)
    sel = scores + rb_ref[...]
    erow = lax.broadcasted_iota(jnp.int32, (N_EXPERTS, tr), 0).astype(F32)

    g3 = sel.reshape(N_GROUPS, GROUP_SIZE, tr)
    r3 = lax.broadcasted_iota(jnp.int32, (N_GROUPS, GROUP_SIZE, tr), 1).astype(F32)
    m1 = jnp.max(g3, axis=1, keepdims=True)
    first = jnp.min(jnp.where(g3 == m1, r3, float(GROUP_SIZE)), axis=1, keepdims=True)
    m2 = jnp.max(jnp.where(r3 == first, -jnp.inf, g3), axis=1, keepdims=True)
    gscore = (m1 + m2).reshape(N_GROUPS, tr)

    gidx = lax.broadcasted_iota(jnp.int32, (N_GROUPS, tr), 0)
    beaten_by = jnp.zeros((N_GROUPS, tr), jnp.int32)
    for g in range(N_GROUPS):
        other = gscore[g:g + 1, :]
        wins = (other > gscore) | ((other == gscore) & (g < gidx))
        beaten_by = beaten_by + wins.astype(jnp.int32)
    dropped = jnp.where(beaten_by < TOP_K_GROUPS, 0.0, -jnp.inf)
    cur = (g3 + dropped.reshape(N_GROUPS, 1, tr)).reshape(N_EXPERTS, tr)

    picks, weights = [], []
    selmask = jnp.zeros((N_EXPERTS, tr), F32)
    for _ in range(TOP_K):
        mx = jnp.max(cur, axis=0, keepdims=True)
        pick = jnp.min(jnp.where(cur == mx, erow, float(N_EXPERTS)), axis=0, keepdims=True)
        onehot = erow == pick
        weights.append(jnp.sum(jnp.where(onehot, scores, 0.0), axis=0, keepdims=True))
        cur = jnp.where(onehot, -jnp.inf, cur)
        selmask = jnp.where(onehot, 1.0, selmask)
        picks.append(pick)

    t_from = lax.broadcasted_iota(jnp.int32, (tr, tr), 0)
    t_to = lax.broadcasted_iota(jnp.int32, (tr, tr), 1)
    earlier = jnp.where(t_from < t_to, 1.0, 0.0).astype(BF16)
    chosen = selmask.astype(BF16)
    carry = carry_ref[...]
    rankmat = (jnp.dot(chosen, earlier, preferred_element_type=F32)
               + jnp.concatenate([carry] * (tr // LANES), axis=1))
    carry_ref[...] = carry + jnp.dot(chosen, jnp.ones((tr, LANES), BF16), preferred_element_type=F32)
    cnt_ref[...] = carry_ref[...]

    wsum = weights[0]
    for wj in weights[1:]:
        wsum = wsum + wj
    row8 = lax.broadcasted_iota(jnp.int32, (TOP_K, tr), 0)
    eidx = jnp.zeros((TOP_K, tr), jnp.int32)
    rank = jnp.zeros((TOP_K, tr), jnp.int32)
    wts = jnp.zeros((TOP_K, tr), F32)
    for j in range(TOP_K):
        rk = jnp.sum(jnp.where(erow == picks[j], rankmat, 0.0), axis=0, keepdims=True)
        eidx = jnp.where(row8 == j, picks[j].astype(jnp.int32), eidx)
        rank = jnp.where(row8 == j, rk.astype(jnp.int32), rank)
        wts = jnp.where(row8 == j, weights[j] / wsum * ROUTED_SCALE, wts)
    eidx_ref[...] = eidx
    rank_ref[...] = rank
    wts_ref[...] = wts


def _route(x1, scale2, shift2, wr_hi, wr_lo, router_bias, *, seq, tr):
    n, d = x1.shape
    tps = seq // tr
    mod_spec = pl.BlockSpec((1, 1, d), lambda i: (i // tps, 0, 0))
    full = lambda a: pl.BlockSpec(a.shape, lambda i: (0,) * a.ndim)
    assert tr % LANES == 0
    k8 = pl.BlockSpec((TOP_K, tr), lambda i: (0, i))
    return pl.pallas_call(
        functools.partial(_route_kernel, tr=tr),
        grid=(n // tr,),
        in_specs=[pl.BlockSpec((tr, d), lambda i: (i, 0)), mod_spec, mod_spec,
                  full(wr_hi), full(wr_lo), full(router_bias)],
        out_specs=[k8, k8, k8, pl.BlockSpec((N_EXPERTS, LANES), lambda i: (0, 0)),
                   pl.BlockSpec((tr, d // 2), lambda i: (i, 0))],
        out_shape=[jax.ShapeDtypeStruct((TOP_K, n), jnp.int32),
                   jax.ShapeDtypeStruct((TOP_K, n), jnp.int32),
                   jax.ShapeDtypeStruct((TOP_K, n), F32),
                   jax.ShapeDtypeStruct((N_EXPERTS, LANES), F32),
                   jax.ShapeDtypeStruct((n, d // 2), U32)],
        scratch_shapes=[pltpu.VMEM((N_EXPERTS, LANES), F32)],
        compiler_params=_params(("arbitrary",)),
    )(x1, scale2, shift2, wr_hi, wr_lo, router_bias)


def _pack_bf16_halves(x):
    w = x.shape[1] // 2
    lo = pltpu.bitcast(x[:, :w].astype(BF16).astype(F32), U32) >> 16
    hi = pltpu.bitcast(x[:, w:].astype(BF16).astype(F32), U32) & jnp.uint32(0xFFFF0000)
    return lo | hi


def _unpack_bf16_halves(p):
    return pltpu.bitcast(p << 16, F32), pltpu.bitcast(p & jnp.uint32(0xFFFF0000), F32)


def _sc_worker():
    return lax.axis_index("subcore") * SC_CORES + lax.axis_index("core")


def _scatter_rows_sc(rows, idx, n_slots):
    n, width = rows.shape
    k = idx.shape[0] // n
    n_workers = SC_CORES * SC_SUBCORES
    w = SC_ROWS
    per_worker = n // n_workers
    n_chunks = per_worker // w
    assert n == n_workers * n_chunks * w and n_chunks % 2 == 0 and idx.shape[0] == k * n
    mesh = plsc.VectorSubcoreMesh(core_axis_name="core", subcore_axis_name="subcore")

    @functools.partial(
        pl.kernel, mesh=mesh,
        out_type=jax.ShapeDtypeStruct((n_slots, width), rows.dtype),
        scratch_types=[pltpu.VMEM((k * per_worker,), jnp.int32), pltpu.VMEM((2, w, width), rows.dtype),
                       pltpu.SemaphoreType.DMA((2,)), pltpu.SemaphoreType.DMA((2,))])
    def scatter(rows_hbm, idx_hbm, out_hbm, idx_v, rows_v, in_sem, out_sem):
        t0 = _sc_worker() * per_worker
        for j in range(k):
            pltpu.sync_copy(idx_hbm.at[pl.ds(j * n + t0, per_worker)], idx_v.at[pl.ds(j * per_worker, per_worker)])

        def load(i, s):
            return pltpu.make_async_copy(rows_hbm.at[pl.ds(t0 + i * w, w)], rows_v.at[s], in_sem.at[s])

        def send(i, s, j):
            slots = idx_v.at[pl.ds(j * per_worker + i * w, w)]
            return pltpu.make_async_copy(rows_v.at[s], out_hbm.at[slots], out_sem.at[s])

        load(0, 0).start()

        def pair(ii, carry):
            for s in range(2):
                i = ii * 2 + s
                load(i, s).wait()

                @pl.when(i >= 1)
                def _():
                    for j in range(k):
                        send(i - 1, 1 - s, j).wait()

                @pl.when(i + 1 < n_chunks)
                def _():
                    load(i + 1, 1 - s).start()

                for j in range(k):
                    send(i, s, j).start()
            return carry

        lax.fori_loop(0, n_chunks // 2, pair, 0)
        for j in range(k):
            send(n_chunks - 1, 1, j).wait()

    return scatter(rows, idx)


def _dest_kernel(eidx_ref, rank_ref, first_ref, dest_ref):
    tt = eidx_ref.shape[1]
    pieces = [jnp.broadcast_to(first_ref[:, p * LANES:(p + 1) * LANES], (TOP_K, LANES))
              for p in range(N_EXPERTS // LANES)]
    for c in range(tt // LANES):
        cols = slice(c * LANES, (c + 1) * LANES)
        e = eidx_ref[:, cols]
        within = e & (LANES - 1)
        start = jnp.take_along_axis(pieces[0], within, axis=1)
        for p in range(1, len(pieces)):
            start = jnp.where(e // LANES == p, jnp.take_along_axis(pieces[p], within, axis=1), start)
        dest_ref[:, cols] = start + rank_ref[:, cols]


def _dest(eidx_t, rank_t, pstarts, *, tt):
    n = eidx_t.shape[1]
    first = pstarts.reshape(1, N_EXPERTS)
    k8 = pl.BlockSpec((TOP_K, tt), lambda i: (0, i))
    return pl.pallas_call(
        _dest_kernel,
        grid=(n // tt,),
        in_specs=[k8, k8, pl.BlockSpec(first.shape, lambda i: (0, 0))],
        out_specs=k8,
        out_shape=jax.ShapeDtypeStruct((TOP_K, n), jnp.int32),
        compiler_params=_params(("arbitrary",)),
    )(eidx_t, rank_t, first)


def _expert_kernel(fb_ref, nb_ref, cnt_ref, nused_ref, wg_ref, wu_ref, wd_ref, xs_ref, ys_ref,
                   wgb, wub, wdb, xbuf, ybuf, in_sem, out_sem):
    e = pl.program_id(0)
    m = DISPATCH_BLOCK
    ns = EXPERT_SLOTS
    n_used = nused_ref[0]

    def rows(g):
        return pl.ds(pl.multiple_of(g * m, m), m)

    def fetch(g):
        slot = g & (ns - 1)
        return pltpu.make_async_copy(xs_ref.at[rows(g), :], xbuf.at[slot], in_sem.at[slot])

    def put(g):
        slot = g & (ns - 1)
        return pltpu.make_async_copy(ybuf.at[slot], ys_ref.at[rows(g), :], out_sem.at[slot])

    @pl.when(e == 0)
    def _():
        for g0 in range(ns - 1):
            @pl.when(g0 < n_used)
            def _(g0=g0):
                fetch(g0).start()

    wgb[...] = wg_ref[0].astype(BF16)
    wub[...] = wu_ref[0].astype(BF16)
    wdb[...] = wd_ref[0].astype(BF16)

    def block(i, carry):
        g = fb_ref[e] + i
        slot = g & (ns - 1)
        fetch(g).wait()

        @pl.when(g + ns - 1 < n_used)
        def _():
            fetch(g + ns - 1).start()

        @pl.when(g >= ns)
        def _():
            put(g - ns).wait()

        n_valid = cnt_ref[e] - i * m

        def swiglu(rows):
            row = lax.broadcasted_iota(jnp.int32, (rows, 1), 0)
            packed = jnp.where(row < n_valid, xbuf[slot, 0:rows], jnp.uint32(0))
            x_lo, x_hi = [h.astype(BF16) for h in _unpack_bf16_halves(packed)]
            half = x_lo.shape[1]

            def up_proj(w):
                return (jnp.dot(x_lo, w[0:half, :], preferred_element_type=F32)
                        + jnp.dot(x_hi, w[half:, :], preferred_element_type=F32))

            a = (_silu(up_proj(wgb)) * up_proj(wub)).astype(BF16)
            ybuf[slot, 0:rows] = _pack_bf16_halves(jnp.dot(a, wdb[...], preferred_element_type=F32))

        pl.when(n_valid > m // 2)(functools.partial(swiglu, m))
        pl.when(n_valid <= m // 2)(functools.partial(swiglu, m // 2))
        put(g).start()
        return carry

    lax.fori_loop(0, nb_ref[e], block, 0)

    @pl.when(e == pl.num_programs(0) - 1)
    def _():
        for back in range(ns, 0, -1):
            @pl.when(n_used >= back)
            def _(back=back):
                put(n_used - back).wait()

        ybuf[0] = jnp.zeros(ybuf.shape[1:], ybuf.dtype)
        n_blocks = ys_ref.shape[0] // m

        def tail(wait):
            def body(g, carry):
                cp = pltpu.make_async_copy(ybuf.at[0], ys_ref.at[rows(g), :], out_sem.at[0])
                cp.wait() if wait else cp.start()
                return carry
            return body

        lax.fori_loop(n_used, n_blocks, tail(False), 0)
        lax.fori_loop(n_used, n_blocks, tail(True), 0)


def _experts(first_block, n_blocks_e, counts, n_used, xs, w_gate, w_up, w_down):
    p, dp = xs.shape
    m = DISPATCH_BLOCK
    n_e, d, f = w_gate.shape
    assert d == 2 * dp and p % m == 0
    grid_spec = pltpu.PrefetchScalarGridSpec(
        num_scalar_prefetch=4,
        grid=(n_e,),
        in_specs=[pl.BlockSpec((1, d, f), lambda e, *_: (e, 0, 0)),
                  pl.BlockSpec((1, d, f), lambda e, *_: (e, 0, 0)),
                  pl.BlockSpec((1, f, d), lambda e, *_: (e, 0, 0)),
                  pl.BlockSpec(memory_space=pl.ANY)],
        out_specs=pl.BlockSpec(memory_space=pl.ANY),
        scratch_shapes=[pltpu.VMEM((d, f), BF16), pltpu.VMEM((d, f), BF16), pltpu.VMEM((f, d), BF16),
                        pltpu.VMEM((EXPERT_SLOTS, m, dp), U32), pltpu.VMEM((EXPERT_SLOTS, m, dp), U32),
                        pltpu.SemaphoreType.DMA((EXPERT_SLOTS,)), pltpu.SemaphoreType.DMA((EXPERT_SLOTS,))],
    )
    return pl.pallas_call(
        _expert_kernel,
        grid_spec=grid_spec,
        out_shape=jax.ShapeDtypeStruct((p, dp), U32),
        compiler_params=_params(("arbitrary",)),
    )(first_block, n_blocks_e, counts, n_used, w_gate, w_up, w_down, xs)


def _gather_rows_sc(table, idx):
    n_idx, (_, width) = idx.shape[0], table.shape
    n_workers = SC_CORES * SC_SUBCORES
    w = SC_ROWS
    per_worker = n_idx // n_workers
    n_chunks = per_worker // w
    assert n_idx == n_workers * n_chunks * w and n_chunks % 2 == 0
    mesh = plsc.VectorSubcoreMesh(core_axis_name="core", subcore_axis_name="subcore")

    @functools.partial(
        pl.kernel, mesh=mesh,
        out_type=jax.ShapeDtypeStruct((n_idx, width), table.dtype),
        scratch_types=[pltpu.VMEM((per_worker,), jnp.int32), pltpu.VMEM((2, w, width), table.dtype),
                       pltpu.SemaphoreType.DMA((2,)), pltpu.SemaphoreType.DMA((2,))])
    def gather(table_hbm, idx_hbm, out_hbm, idx_v, rows_v, in_sem, out_sem):
        base = _sc_worker() * per_worker
        pltpu.sync_copy(idx_hbm.at[pl.ds(base, per_worker)], idx_v)

        def fetch(i, s):
            return pltpu.make_async_copy(table_hbm.at[idx_v.at[pl.ds(i * w, w)]], rows_v.at[s], in_sem.at[s])

        def put(i, s):
            return pltpu.make_async_copy(rows_v.at[s], out_hbm.at[pl.ds(base + i * w, w)], out_sem.at[s])

        fetch(0, 0).start()

        def pair(ii, carry):
            for s in range(2):
                i = ii * 2 + s
                fetch(i, s).wait()

                @pl.when(i >= 1)
                def _():
                    put(i - 1, 1 - s).wait()

                @pl.when(i + 1 < n_chunks)
                def _():
                    fetch(i + 1, 1 - s).start()

                put(i, s).start()
            return carry

        lax.fori_loop(0, n_chunks // 2, pair, 0)
        put(n_chunks - 1, 1).wait()

    return gather(table, idx)


def _combine_kernel(x_ref, sc_ref, sh_ref, gate_ref, wts_ref, rows_ref,
                    wsg_ref, wsu_ref, wsd_ref, g_ref, b_ref, *out_refs):
    o_ref = out_refs[-1]
    x = x_ref[...]
    hb = (x * (1.0 + sc_ref[0]) + sh_ref[0]).astype(BF16)
    sg = jnp.dot(hb, wsg_ref[...], preferred_element_type=F32)
    su = jnp.dot(hb, wsu_ref[...], preferred_element_type=F32)
    shared = jnp.dot((_silu(sg) * su).astype(BF16), wsd_ref[...], preferred_element_type=F32)

    wts = wts_ref[...]
    half = shared.shape[1] // 2
    lo, hi = shared[:, :half], shared[:, half:]
    for j in range(TOP_K):
        y_lo, y_hi = _unpack_bf16_halves(rows_ref[j])
        lo = lo + wts[:, j:j + 1] * y_lo
        hi = hi + wts[:, j:j + 1] * y_hi
    z = ALPHA * x + gate_ref[0] * jnp.concatenate([lo, hi], axis=1)
    o_ref[...] = _layer_norm(z, g_ref[...], b_ref[...])


def _combine(x1, scale2, shift2, gate2, wts, rows, ws_gate, ws_up, ws_down, ln_g, ln_b, prev_out,
             *, seq, tc, first_token):
    n, d = x1.shape
    tps = seq // tc
    off = first_token // tc
    n_tiles = rows.shape[1] // tc
    mod_spec = pl.BlockSpec((1, 1, d), lambda i: ((i + off) // tps, 0, 0))
    full = lambda a: pl.BlockSpec(a.shape, lambda i: (0,) * a.ndim)
    args = [x1, scale2, shift2, gate2, wts, rows, ws_gate, ws_up, ws_down, ln_g, ln_b]
    in_specs = [pl.BlockSpec((tc, d), lambda i: (i + off, 0)), mod_spec, mod_spec, mod_spec,
                pl.BlockSpec((tc, TOP_K), lambda i: (i + off, 0)),
                pl.BlockSpec((TOP_K, tc, d // 2), lambda i: (0, i, 0)),
                full(ws_gate), full(ws_up), full(ws_down), full(ln_g), full(ln_b)]
    aliases = {}
    if prev_out is not None:
        aliases = {len(args): 0}
        args.append(prev_out)
        in_specs.append(pl.BlockSpec(memory_space=pl.ANY))
    return pl.pallas_call(
        _combine_kernel,
        grid=(n_tiles,),
        in_specs=in_specs,
        out_specs=pl.BlockSpec((tc, d), lambda i: (i + off, 0)),
        out_shape=jax.ShapeDtypeStruct((n, d), F32),
        input_output_aliases=aliases,
        compiler_params=_params(("arbitrary",)),
    )(*args)


def _layer(x, c, w_ada, b_ada, w_in, pool_w, pool_scale, lq1, lk1, lq2, lk2, subln_g, w_out,
           ln1_g, ln1_b, w_router, router_bias, w_gate, w_up, w_down, ws_gate, ws_up, ws_down,
           ln2_g, ln2_b, rel_table, *, tm=1024, tr=512, tc=512):
    bsz, seq, d = x.shape
    n = bsz * seq
    x2 = x.reshape(n, d)
    row = lambda a: a.reshape(1, -1)

    mod = _modulation(c, w_ada, b_ada)
    shift1, scale1, gate1, shift2, scale2, gate2 = [
        mod[:, j * d:(j + 1) * d].reshape(bsz, 1, d) for j in range(6)]

    n_main = POOL_DIM + 2 * QK_DIM
    yp, q, k, vt = _inproj(x2, scale1, shift1, w_in[:, :n_main].astype(BF16), w_in[:, n_main:].T.astype(BF16),
                           pool_w.astype(BF16), row(pool_scale), seq=seq, tm=tm)
    bias_tiles = _bias_tiles(rel_table, seq // ATT_TILE)
    ya = _attention(q, k, vt, bias_tiles, row(lq1), row(lk1), row(lq2), row(lk2), subln_g.reshape(-1, 1),
                    bsz=bsz, seq=seq)
    x1 = _outproj(x2, yp, ya, w_out.astype(BF16), gate1, row(ln1_g), row(ln1_b), seq=seq, tm=tm)

    wr_t = w_router.T
    wr_hi = wr_t.astype(BF16)
    wr_lo = (wr_t - wr_hi.astype(F32)).astype(BF16)
    eidx_t, rank_t, wts_t, cnt, h2p = _route(x1, scale2, shift2, wr_hi, wr_lo, router_bias.reshape(-1, 1),
                                             seq=seq, tr=tr)

    m = DISPATCH_BLOCK
    counts = cnt[:, 0].astype(jnp.int32)
    padded = (counts + m - 1) // m * m
    pends = jnp.cumsum(padded)
    pstarts = pends - padded
    n_blocks = -(-(n * TOP_K + N_EXPERTS * (m - 1)) // m)
    n_used = (pends[-1:] // m).astype(jnp.int32)
    dest_t = _dest(eidx_t, rank_t, pstarts, tt=min(n, 2048))

    xs = _scatter_rows_sc(h2p, dest_t.reshape(TOP_K * n), n_blocks * m)
    ys = _experts(pstarts // m, padded // m, counts, n_used, xs, w_gate, w_up, w_down)

    shared_w = (ws_gate.astype(BF16), ws_up.astype(BF16), ws_down.astype(BF16))
    wts = wts_t.T
    part = n // COMBINE_PARTS
    assert part % tc == 0 and part * COMBINE_PARTS == n
    out = None
    for p in range(COMBINE_PARTS):
        idx = dest_t[:, p * part:(p + 1) * part].reshape(TOP_K * part)
        picked = _gather_rows_sc(ys, idx).reshape(TOP_K, part, d // 2)
        out = _combine(x1, scale2, shift2, gate2, wts, picked, *shared_w, row(ln2_g), row(ln2_b), out,
                       seq=seq, tc=tc, first_token=p * part)
    return out.reshape(bsz, seq, d)


def kernel(x, c, w_ada, b_ada, w_in, pool_w, pool_scale, lambda_q1, lambda_k1, lambda_q2, lambda_k2,
           subln_g, w_out, ln1_g, ln1_b, w_router, router_bias, w_gate, w_up, w_down,
           ws_gate, ws_up, ws_down, ln2_g, ln2_b, rel_table):
    per_layer = (w_ada, b_ada, w_in, pool_w, pool_scale, lambda_q1, lambda_k1, lambda_q2, lambda_k2,
                 subln_g, w_out, ln1_g, ln1_b, w_router, router_bias, w_gate, w_up, w_down,
                 ws_gate, ws_up, ws_down, ln2_g, ln2_b)
    assert all(a.shape[0] == DEPTH == 1 for a in per_layer)
    return _layer(x, c, *[a.reshape(a.shape[1:]) for a in per_layer], rel_table)
```

```python
import functools
import math

import jax
import jax.numpy as jnp
from jax import lax
from jax.experimental import pallas as pl
from jax.experimental.pallas import tpu as pltpu
from jax.experimental.pallas import tpu_sc as plsc

F32 = jnp.float32
BF16 = jnp.bfloat16
U32 = jnp.uint32
LANES = 128

CHUNK = 64
ATT_TILE = 256
LOG2_E = math.log2(math.e)
POOL_DIM = 512
POOL_WINDOWS = (2, 4, 8, 16)
POOL_GROUP_DIM = 128
MAX_WINDOW = max(POOL_WINDOWS)
ATTN_HEADS = 4
ATTN_HEAD_DIM = 64
QK_DIM = 512
V_DIM = 512
NUM_BUCKETS = 32
MAX_DISTANCE = 128
N_EXPERTS = 256
TOP_K = 8
N_GROUPS = 8
GROUP_SIZE = N_EXPERTS // N_GROUPS
TOP_K_GROUPS = 4
ROUTED_SCALE = 2.5
DISPATCH_BLOCK = 512
EXPERT_SLOTS = 4
COMBINE_PARTS = 8
SC_CORES, SC_SUBCORES = 2, 16
SC_ROWS = 64
DEPTH = 1
ALPHA = (2.0 * DEPTH) ** 0.25
LN_EPS = 1e-5
LAMBDA_INIT = 0.8 - 0.6 * math.exp(-0.3 * 0)

VMEM_LIMIT = 48 * 1024 * 1024


def _sigmoid(x):
    return 1.0 / (1.0 + jnp.exp(-x))


def _silu(x):
    return x * _sigmoid(x)


def _layer_norm(z, g, b):
    mu = jnp.mean(z, axis=-1, keepdims=True)
    zc = z - mu
    var = jnp.mean(zc * zc, axis=-1, keepdims=True)
    return zc * lax.rsqrt(var + LN_EPS) * g + b


def _params(sem=None):
    return pltpu.CompilerParams(dimension_semantics=sem, vmem_limit_bytes=VMEM_LIMIT)


def _mod_kernel(c_ref, w_ref, b_ref, o_ref):
    ca = _silu(c_ref[...])
    o_ref[...] = jnp.dot(ca, w_ref[...], preferred_element_type=F32,
                         precision=lax.Precision.HIGHEST) + b_ref[...]


def _modulation(c, w_ada, b_ada):
    bsz, d = c.shape
    n_out = w_ada.shape[1]
    return pl.pallas_call(
        _mod_kernel,
        grid=(n_out // d,),
        in_specs=[pl.BlockSpec((bsz, d), lambda j: (0, 0)),
                  pl.BlockSpec((d, d), lambda j: (0, j)),
                  pl.BlockSpec((1, d), lambda j: (0, j))],
        out_specs=pl.BlockSpec((bsz, d), lambda j: (0, j)),
        out_shape=jax.ShapeDtypeStruct((bsz, n_out), F32),
        compiler_params=_params(("arbitrary",)),
    )(c, w_ada, b_ada.reshape(1, n_out))


def _inproj_kernel(x_ref, sc_ref, sh_ref, w_ref, wvt_ref, pw_ref, ps_ref,
                   yp_ref, q_ref, k_ref, vt_ref, ext_ref, *, tm, seq):
    i = pl.program_id(0)
    tiles_per_seq = seq // tm
    it = i % tiles_per_seq
    h = x_ref[...] * (1.0 + sc_ref[0]) + sh_ref[0]
    hb = h.astype(BF16)
    proj = jnp.dot(hb, w_ref[...], preferred_element_type=F32)
    u = proj[:, :POOL_DIM]
    q_ref[...] = (proj[:, POOL_DIM:POOL_DIM + QK_DIM] * (ATTN_HEAD_DIM ** -0.5 * LOG2_E)).astype(BF16)
    k_ref[...] = proj[:, POOL_DIM + QK_DIM:POOL_DIM + 2 * QK_DIM].astype(BF16)
    vt = lax.dot_general(wvt_ref[...], hb, (((1,), (1,)), ((), ())), preferred_element_type=F32)
    for j in range(tm // ATT_TILE):
        vt_ref[0, j] = vt[:, j * ATT_TILE:(j + 1) * ATT_TILE].astype(BF16)

    @pl.when(it == 0)
    def _():
        ext_ref[0:MAX_WINDOW, :] = jnp.zeros((MAX_WINDOW, POOL_DIM), F32)

    ext_ref[MAX_WINDOW:MAX_WINDOW + tm, :] = u
    pos = (it * tm + lax.broadcasted_iota(jnp.int32, (tm, 1), 0) + 1).astype(F32)
    for g, w in enumerate(POOL_WINDOWS):
        c0, c1 = g * POOL_GROUP_DIM, (g + 1) * POOL_GROUP_DIM
        s = ext_ref[MAX_WINDOW:MAX_WINDOW + tm, c0:c1]
        for j in range(1, w):
            s = s + ext_ref[MAX_WINDOW - j:MAX_WINDOW - j + tm, c0:c1]
        pooled = s / jnp.minimum(pos, float(w)) - u[:, c0:c1]
        y = jnp.dot(pooled.astype(BF16), pw_ref[g], preferred_element_type=F32)
        yp_ref[:, c0:c1] = (y * ps_ref[:, c0:c1]).astype(BF16)
    ext_ref[0:MAX_WINDOW, :] = ext_ref[tm:tm + MAX_WINDOW, :]


def _inproj(x2, scale1, shift1, w_main, w_vt, pool_w, pool_scale, *, seq, tm):
    n, d = x2.shape
    assert n % tm == 0 and seq % tm == 0 and tm >= 2 * MAX_WINDOW and tm % ATT_TILE == 0
    tps = seq // tm
    tpt = tm // ATT_TILE
    mod_spec = pl.BlockSpec((1, 1, d), lambda i: (i // tps, 0, 0))
    row = lambda w: pl.BlockSpec((tm, w), lambda i: (i, 0))
    full = lambda a: pl.BlockSpec(a.shape, lambda i: (0,) * a.ndim)
    return pl.pallas_call(
        functools.partial(_inproj_kernel, tm=tm, seq=seq),
        grid=(n // tm,),
        in_specs=[row(d), mod_spec, mod_spec, full(w_main), full(w_vt), full(pool_w), full(pool_scale)],
        out_specs=[row(POOL_DIM), row(QK_DIM), row(QK_DIM),
                   pl.BlockSpec((1, tpt, V_DIM, ATT_TILE), lambda i: (i // tps, i % tps, 0, 0))],
        out_shape=[jax.ShapeDtypeStruct((n, POOL_DIM), BF16),
                   jax.ShapeDtypeStruct((n, QK_DIM), BF16),
                   jax.ShapeDtypeStruct((n, QK_DIM), BF16),
                   jax.ShapeDtypeStruct((n // seq, seq // ATT_TILE, V_DIM, ATT_TILE), BF16)],
        scratch_shapes=[pltpu.VMEM((tm + MAX_WINDOW, POOL_DIM), F32)],
        compiler_params=_params(("arbitrary",)),
    )(x2, scale1, shift1, w_main, w_vt, pool_w, pool_scale)


def _bias_kernel(tab_ref, o_ref):
    delta = pl.program_id(0)
    r = lax.broadcasted_iota(jnp.int32, (ATT_TILE, ATT_TILE), 0)
    c = lax.broadcasted_iota(jnp.int32, (ATT_TILE, ATT_TILE), 1)
    rel = r - c - delta * ATT_TILE
    half = NUM_BUCKETS // 2
    max_exact = half // 2
    ret = jnp.where(rel > 0, half, 0)
    n = jnp.abs(rel)
    nf = jnp.maximum(n, 1).astype(F32)
    large = max_exact + (jnp.log(nf / max_exact) / math.log(MAX_DISTANCE / max_exact)
                         * (half - max_exact)).astype(jnp.int32)
    large = jnp.minimum(large, half - 1)
    bucket = ret + jnp.where(n < max_exact, n, large)
    for h in range(ATTN_HEADS):
        acc = jnp.zeros((ATT_TILE, ATT_TILE), F32)
        for b in range(NUM_BUCKETS):
            acc = jnp.where(bucket == b, tab_ref[b, h], acc)
        o_ref[h, 0] = acc * LOG2_E


def _bias_tiles(rel_table, n_tiles):
    return pl.pallas_call(
        _bias_kernel,
        grid=(n_tiles,),
        in_specs=[pl.BlockSpec(memory_space=pltpu.SMEM)],
        out_specs=pl.BlockSpec((ATTN_HEADS, 1, ATT_TILE, ATT_TILE), lambda dlt: (0, dlt, 0, 0)),
        out_shape=jax.ShapeDtypeStruct((ATTN_HEADS, n_tiles, ATT_TILE, ATT_TILE), F32),
        compiler_params=_params(("arbitrary",)),
    )(rel_table)


def _attn_kernel(q_ref, k_ref, vt_ref, bias_ref, lq1_ref, lk1_ref, lq2_ref, lk2_ref, g_ref, o_ref, *acc_refs):
    qt = pl.program_id(1)
    t = ATT_TILE
    n_maps = 2 * ATTN_HEADS
    lam = (jnp.exp(jnp.sum(lq1_ref[...] * lk1_ref[...], axis=-1, keepdims=True))
           - jnp.exp(jnp.sum(lq2_ref[...] * lk2_ref[...], axis=-1, keepdims=True))
           + LAMBDA_INIT)
    r = lax.broadcasted_iota(jnp.int32, (t, t), 0)
    c = lax.broadcasted_iota(jnp.int32, (t, t), 1)
    allowed = (r // CHUNK) <= (c // CHUNK)
    hd2 = 2 * ATTN_HEAD_DIM

    ahead = 2

    def scores(kt, hm):
        col = hm * ATTN_HEAD_DIM
        qh = q_ref[:, col:col + ATTN_HEAD_DIM]
        kh = k_ref[pl.ds(pl.multiple_of(kt * t, t), t), col:col + ATTN_HEAD_DIM]
        return lax.dot_general(kh, qh, (((1,), (1,)), ((), ())),
                               preferred_element_type=F32) + bias_ref[hm // 2, qt - kt]

    def block(kt, carry, diagonal):
        stats, early = carry[:2 * n_maps], carry[2 * n_maps:]

        def softmax(hm, s):
            if diagonal:
                s = jnp.where(allowed, s, -jnp.inf)
            m_old, l_old = stats[2 * hm:2 * hm + 2]
            m_new = jnp.maximum(m_old, jnp.max(s, axis=0, keepdims=True))
            alpha = jnp.exp2(m_old - m_new)
            p = jnp.exp2(s - m_new)
            return m_new, alpha * l_old + jnp.sum(p, axis=0, keepdims=True), alpha, p.astype(BF16)

        def accumulate(hm, alpha, p):
            h = hm // 2
            vth = vt_ref[0, kt, h * hd2:(h + 1) * hd2, :]
            acc_refs[hm][...] = alpha * acc_refs[hm][...] + jnp.dot(vth, p, preferred_element_type=F32)

        s_vals = dict(enumerate(early))
        sm_vals, out, nxt = {}, [None] * (2 * n_maps), []
        for step in range(1, n_maps + ahead):
            if ahead <= step < n_maps:
                s_vals[step] = scores(kt, step)
            elif step >= n_maps and not diagonal:
                nxt.append(scores(kt + 1, step - n_maps))
            hm = step - 1
            if hm < n_maps:
                m_new, l_new, alpha, p = softmax(hm, s_vals.pop(hm))
                out[2 * hm], out[2 * hm + 1] = m_new, l_new
                sm_vals[hm] = (alpha, p)
            if step - 2 >= 0:
                accumulate(step - 2, *sm_vals.pop(step - 2))
        return tuple(out) + tuple(nxt)

    for acc in acc_refs:
        acc[...] = jnp.zeros_like(acc)
    one = (jnp.full((1, t), -jnp.inf, F32), jnp.zeros((1, t), F32))
    first = tuple(scores(0, hm) for hm in range(ahead))
    carry = lax.fori_loop(0, qt, lambda kt, cr: block(kt, cr, False), one * n_maps + first)
    carry = block(qt, carry, True)
    for h in range(ATTN_HEADS):
        l0, l1 = carry[4 * h + 1], carry[4 * h + 3]
        o = acc_refs[2 * h][...] / l0 - lam * (acc_refs[2 * h + 1][...] / l1)
        y = o * lax.rsqrt(jnp.mean(o * o, axis=0, keepdims=True) + LN_EPS) * g_ref[...]
        o_ref[:, h * hd2:(h + 1) * hd2] = (y * (1.0 - LAMBDA_INIT)).T.astype(BF16)


def _attention(q, k, vt, bias_tiles, lq1, lk1, lq2, lk2, subln_g, *, bsz, seq):
    t = ATT_TILE
    nt = seq // t
    full = lambda a: pl.BlockSpec(a.shape, lambda b, j: (0,) * a.ndim)
    return pl.pallas_call(
        _attn_kernel,
        grid=(bsz, nt),
        in_specs=[pl.BlockSpec((t, QK_DIM), lambda b, j: (b * nt + j, 0)),
                  pl.BlockSpec((seq, QK_DIM), lambda b, j: (b, 0)),
                  pl.BlockSpec((1, nt, V_DIM, t), lambda b, j: (b, 0, 0, 0)),
                  full(bias_tiles), full(lq1), full(lk1), full(lq2), full(lk2), full(subln_g)],
        out_specs=pl.BlockSpec((t, V_DIM), lambda b, j: (b * nt + j, 0)),
        out_shape=jax.ShapeDtypeStruct((bsz * seq, V_DIM), BF16),
        scratch_shapes=[pltpu.VMEM((2 * ATTN_HEAD_DIM, t), F32) for _ in range(2 * ATTN_HEADS)],
        compiler_params=_params(("arbitrary", "arbitrary")),
    )(q, k, vt, bias_tiles, lq1, lk1, lq2, lk2, subln_g)


def _outproj_kernel(x_ref, yp_ref, ya_ref, w_ref, gate_ref, g_ref, b_ref, o_ref):
    mix = (jnp.dot(yp_ref[...], w_ref[0:POOL_DIM, :], preferred_element_type=F32)
           + jnp.dot(ya_ref[...], w_ref[POOL_DIM:, :], preferred_element_type=F32))
    z = ALPHA * x_ref[...] + gate_ref[0] * mix
    o_ref[...] = _layer_norm(z, g_ref[...], b_ref[...])


def _outproj(x2, yp, ya, w_out, gate1, ln_g, ln_b, *, seq, tm):
    n, d = x2.shape
    tps = seq // tm
    row = lambda w: pl.BlockSpec((tm, w), lambda i: (i, 0))
    full = lambda a: pl.BlockSpec(a.shape, lambda i: (0,) * a.ndim)
    return pl.pallas_call(
        _outproj_kernel,
        grid=(n // tm,),
        in_specs=[row(d), row(POOL_DIM), row(V_DIM), full(w_out),
                  pl.BlockSpec((1, 1, d), lambda i: (i // tps, 0, 0)), full(ln_g), full(ln_b)],
        out_specs=row(d),
        out_shape=jax.ShapeDtypeStruct((n, d), F32),
        compiler_params=_params(("arbitrary",)),
    )(x2, yp, ya, w_out, gate1, ln_g, ln_b)


def _route_kernel(x_ref, sc_ref, sh_ref, whi_ref, wlo_ref, rb_ref,
                  eidx_ref, rank_ref, wts_ref, cnt_ref, h2p_ref, carry_ref, *, tr):
    i = pl.program_id(0)

    @pl.when(i == 0)
    def _():
        carry_ref[...] = jnp.zeros_like(carry_ref)

    h2 = x_ref[...] * (1.0 + sc_ref[0]) + sh_ref[0]
    h2p_ref[...] = _pack_bf16_halves(h2)
    hi = h2.astype(BF16)
    lo = (h2 - hi.astype(F32)).astype(BF16)
    nt = (((1,), (1,)), ((), ()))
    logits = (lax.dot_general(whi_ref[...], hi, nt, preferred_element_type=F32)
              + lax.dot_general(wlo_ref[...], hi, nt, preferred_element_type=F32)
              + lax.dot_general(whi_ref[...], lo, nt, preferred_element_type=F32))
    scores = _sigmoid(logits)
    sel = scores + rb_ref[...]
    erow = lax.broadcasted_iota(jnp.int32, (N_EXPERTS, tr), 0).astype(F32)

    g3 = sel.reshape(N_GROUPS, GROUP_SIZE, tr)
    r3 = lax.broadcasted_iota(jnp.int32, (N_GROUPS, GROUP_SIZE, tr), 1).astype(F32)
    m1 = jnp.max(g3, axis=1, keepdims=True)
    first = jnp.min(jnp.where(g3 == m1, r3, float(GROUP_SIZE)), axis=1, keepdims=True)
    m2 = jnp.max(jnp.where(r3 == first, -jnp.inf, g3), axis=1, keepdims=True)
    gscore = (m1 + m2).reshape(N_GROUPS, tr)

    gidx = lax.broadcasted_iota(jnp.int32, (N_GROUPS, tr), 0)
    beaten_by = jnp.zeros((N_GROUPS, tr), jnp.int32)
    for g in range(N_GROUPS):
        other = gscore[g:g + 1, :]
        wins = (other > gscore) | ((other == gscore) & (g < gidx))
        beaten_by = beaten_by + wins.astype(jnp.int32)
    dropped = jnp.where(beaten_by < TOP_K_GROUPS, 0.0, -jnp.inf)
    cur = (g3 + dropped.reshape(N_GROUPS, 1, tr)).reshape(N_EXPERTS, tr)

    picks, weights = [], []
    selmask = jnp.zeros((N_EXPERTS, tr), F32)
    for _ in range(TOP_K):
        mx = jnp.max(cur, axis=0, keepdims=True)
        pick = jnp.min(jnp.where(cur == mx, erow, float(N_EXPERTS)), axis=0, keepdims=True)
        onehot = erow == pick
        weights.append(jnp.sum(jnp.where(onehot, scores, 0.0), axis=0, keepdims=True))
        cur = jnp.where(onehot, -jnp.inf, cur)
        selmask = jnp.where(onehot, 1.0, selmask)
        picks.append(pick)

    t_from = lax.broadcasted_iota(jnp.int32, (tr, tr), 0)
    t_to = lax.broadcasted_iota(jnp.int32, (tr, tr), 1)
    earlier = jnp.where(t_from < t_to, 1.0, 0.0).astype(BF16)
    chosen = selmask.astype(BF16)
    carry = carry_ref[...]
    rankmat = (jnp.dot(chosen, earlier, preferred_element_type=F32)
               + jnp.concatenate([carry] * (tr // LANES), axis=1))
    carry_ref[...] = carry + jnp.dot(chosen, jnp.ones((tr, LANES), BF16), preferred_element_type=F32)
    cnt_ref[...] = carry_ref[...]

    wsum = weights[0]
    for wj in weights[1:]:
        wsum = wsum + wj
    row8 = lax.broadcasted_iota(jnp.int32, (TOP_K, tr), 0)
    eidx = jnp.zeros((TOP_K, tr), jnp.int32)
    rank = jnp.zeros((TOP_K, tr), jnp.int32)
    wts = jnp.zeros((TOP_K, tr), F32)
    for j in range(TOP_K):
        rk = jnp.sum(jnp.where(erow == picks[j], rankmat, 0.0), axis=0, keepdims=True)
        eidx = jnp.where(row8 == j, picks[j].astype(jnp.int32), eidx)
        rank = jnp.where(row8 == j, rk.astype(jnp.int32), rank)
        wts = jnp.where(row8 == j, weights[j] / wsum * ROUTED_SCALE, wts)
    eidx_ref[...] = eidx
    rank_ref[...] = rank
    wts_ref[...] = wts


def _route(x1, scale2, shift2, wr_hi, wr_lo, router_bias, *, seq, tr):
    n, d = x1.shape
    tps = seq // tr
    mod_spec = pl.BlockSpec((1, 1, d), lambda i: (i // tps, 0, 0))
    full = lambda a: pl.BlockSpec(a.shape, lambda i: (0,) * a.ndim)
    assert tr % LANES == 0
    k8 = pl.BlockSpec((TOP_K, tr), lambda i: (0, i))
    return pl.pallas_call(
        functools.partial(_route_kernel, tr=tr),
        grid=(n // tr,),
        in_specs=[pl.BlockSpec((tr, d), lambda i: (i, 0)), mod_spec, mod_spec,
                  full(wr_hi), full(wr_lo), full(router_bias)],
        out_specs=[k8, k8, k8, pl.BlockSpec((N_EXPERTS, LANES), lambda i: (0, 0)),
                   pl.BlockSpec((tr, d // 2), lambda i: (i, 0))],
        out_shape=[jax.ShapeDtypeStruct((TOP_K, n), jnp.int32),
                   jax.ShapeDtypeStruct((TOP_K, n), jnp.int32),
                   jax.ShapeDtypeStruct((TOP_K, n), F32),
                   jax.ShapeDtypeStruct((N_EXPERTS, LANES), F32),
                   jax.ShapeDtypeStruct((n, d // 2), U32)],
        scratch_shapes=[pltpu.VMEM((N_EXPERTS, LANES), F32)],
        compiler_params=_params(("arbitrary",)),
    )(x1, scale2, shift2, wr_hi, wr_lo, router_bias)


def _pack_bf16_halves(x):
    w = x.shape[1] // 2
    lo = pltpu.bitcast(x[:, :w].astype(BF16).astype(F32), U32) >> 16
    hi = pltpu.bitcast(x[:, w:].astype(BF16).astype(F32), U32) & jnp.uint32(0xFFFF0000)
    return lo | hi


def _unpack_bf16_halves(p):
    return pltpu.bitcast(p << 16, F32), pltpu.bitcast(p & jnp.uint32(0xFFFF0000), F32)


def _sc_worker():
    return lax.axis_index("subcore") * SC_CORES + lax.axis_index("core")


def _scatter_rows_sc(rows, idx, n_slots):
    n, width = rows.shape
    k = idx.shape[0] // n
    n_workers = SC_CORES * SC_SUBCORES
    w = SC_ROWS
    per_worker = n // n_workers
    n_chunks = per_worker // w
    assert n == n_workers * n_chunks * w and n_chunks % 2 == 0 and idx.shape[0] == k * n
    mesh = plsc.VectorSubcoreMesh(core_axis_name="core", subcore_axis_name="subcore")

    @functools.partial(
        pl.kernel, mesh=mesh,
        out_type=jax.ShapeDtypeStruct((n_slots, width), rows.dtype),
        scratch_types=[pltpu.VMEM((k * per_worker,), jnp.int32), pltpu.VMEM((2, w, width), rows.dtype),
                       pltpu.SemaphoreType.DMA((2,)), pltpu.SemaphoreType.DMA((2,))])
    def scatter(rows_hbm, idx_hbm, out_hbm, idx_v, rows_v, in_sem, out_sem):
        t0 = _sc_worker() * per_worker
        for j in range(k):
            pltpu.sync_copy(idx_hbm.at[pl.ds(j * n + t0, per_worker)], idx_v.at[pl.ds(j * per_worker, per_worker)])

        def load(i, s):
            return pltpu.make_async_copy(rows_hbm.at[pl.ds(t0 + i * w, w)], rows_v.at[s], in_sem.at[s])

        def send(i, s, j):
            slots = idx_v.at[pl.ds(j * per_worker + i * w, w)]
            return pltpu.make_async_copy(rows_v.at[s], out_hbm.at[slots], out_sem.at[s])

        load(0, 0).start()

        def pair(ii, carry):
            for s in range(2):
                i = ii * 2 + s
                load(i, s).wait()

                @pl.when(i >= 1)
                def _():
                    for j in range(k):
                        send(i - 1, 1 - s, j).wait()

                @pl.when(i + 1 < n_chunks)
                def _():
                    load(i + 1, 1 - s).start()

                for j in range(k):
                    send(i, s, j).start()
            return carry

        lax.fori_loop(0, n_chunks // 2, pair, 0)
        for j in range(k):
            send(n_chunks - 1, 1, j).wait()

    return scatter(rows, idx)


def _dest_kernel(eidx_ref, rank_ref, first_ref, dest_ref):
    tt = eidx_ref.shape[1]
    pieces = [jnp.broadcast_to(first_ref[:, p * LANES:(p + 1) * LANES], (TOP_K, LANES))
              for p in range(N_EXPERTS // LANES)]
    for c in range(tt // LANES):
        cols = slice(c * LANES, (c + 1) * LANES)
        e = eidx_ref[:, cols]
        within = e & (LANES - 1)
        start = jnp.take_along_axis(pieces[0], within, axis=1)
        for p in range(1, len(pieces)):
            start = jnp.where(e // LANES == p, jnp.take_along_axis(pieces[p], within, axis=1), start)
        dest_ref[:, cols] = start + rank_ref[:, cols]


def _dest(eidx_t, rank_t, pstarts, *, tt):
    n = eidx_t.shape[1]
    first = pstarts.reshape(1, N_EXPERTS)
    k8 = pl.BlockSpec((TOP_K, tt), lambda i: (0, i))
    return pl.pallas_call(
        _dest_kernel,
        grid=(n // tt,),
        in_specs=[k8, k8, pl.BlockSpec(first.shape, lambda i: (0, 0))],
        out_specs=k8,
        out_shape=jax.ShapeDtypeStruct((TOP_K, n), jnp.int32),
        compiler_params=_params(("arbitrary",)),
    )(eidx_t, rank_t, first)


def _expert_kernel(fb_ref, nb_ref, cnt_ref, nused_ref, wg_ref, wu_ref, wd_ref, xs_ref, ys_ref,
                   wgb, wub, wdb, xbuf, ybuf, in_sem, out_sem):
    e = pl.program_id(0)
    m = DISPATCH_BLOCK
    ns = EXPERT_SLOTS
    n_used = nused_ref[0]

    def rows(g):
        return pl.ds(pl.multiple_of(g * m, m), m)

    def fetch(g):
        slot = g & (ns - 1)
        return pltpu.make_async_copy(xs_ref.at[rows(g), :], xbuf.at[slot], in_sem.at[slot])

    def put(g):
        slot = g & (ns - 1)
        return pltpu.make_async_copy(ybuf.at[slot], ys_ref.at[rows(g), :], out_sem.at[slot])

    @pl.when(e == 0)
    def _():
        for g0 in range(ns - 1):
            @pl.when(g0 < n_used)
            def _(g0=g0):
                fetch(g0).start()

    wgb[...] = wg_ref[0].astype(BF16)
    wub[...] = wu_ref[0].astype(BF16)
    wdb[...] = wd_ref[0].astype(BF16)

    def block(i, carry):
        g = fb_ref[e] + i
        slot = g & (ns - 1)
        fetch(g).wait()

        @pl.when(g + ns - 1 < n_used)
        def _():
            fetch(g + ns - 1).start()

        @pl.when(g >= ns)
        def _():
            put(g - ns).wait()

        n_valid = cnt_ref[e] - i * m

        def swiglu(rows):
            row = lax.broadcasted_iota(jnp.int32, (rows, 1), 0)
            packed = jnp.where(row < n_valid, xbuf[slot, 0:rows], jnp.uint32(0))
            x_lo, x_hi = [h.astype(BF16) for h in _unpack_bf16_halves(packed)]
            half = x_lo.shape[1]

            def up_proj(w):
                return (jnp.dot(x_lo, w[0:half, :], preferred_element_type=F32)
                        + jnp.dot(x_hi, w[half:, :], preferred_element_type=F32))

            a = (_silu(up_proj(wgb)) * up_proj(wub)).astype(BF16)
            ybuf[slot, 0:rows] = _pack_bf16_halves(jnp.dot(a, wdb[...], preferred_element_type=F32))

        pl.when(n_valid > m // 2)(functools.partial(swiglu, m))
        pl.when(n_valid <= m // 2)(functools.partial(swiglu, m // 2))
        put(g).start()
        return carry

    lax.fori_loop(0, nb_ref[e], block, 0)

    @pl.when(e == pl.num_programs(0) - 1)
    def _():
        for back in range(ns, 0, -1):
            @pl.when(n_used >= back)
            def _(back=back):
                put(n_used - back).wait()

        ybuf[0] = jnp.zeros(ybuf.shape[1:], ybuf.dtype)
        n_blocks = ys_ref.shape[0] // m

        def tail(wait):
            def body(g, carry):
                cp = pltpu.make_async_copy(ybuf.at[0], ys_ref.at[rows(g), :], out_sem.at[0])
                cp.wait() if wait else cp.start()
                return carry
            return body

        lax.fori_loop(n_used, n_blocks, tail(False), 0)
        lax.fori_loop(n_used, n_blocks, tail(True), 0)


def _experts(first_block, n_blocks_e, counts, n_used, xs, w_gate, w_up, w_down):
    p, dp = xs.shape
    m = DISPATCH_BLOCK
    n_e, d, f = w_gate.shape
    assert d == 2 * dp and p % m == 0
    grid_spec = pltpu.PrefetchScalarGridSpec(
        num_scalar_prefetch=4,
        grid=(n_e,),
        in_specs=[pl.BlockSpec((1, d, f), lambda e, *_: (e, 0, 0)),
                  pl.BlockSpec((1, d, f), lambda e, *_: (e, 0, 0)),
                  pl.BlockSpec((1, f, d), lambda e, *_: (e, 0, 0)),
                  pl.BlockSpec(memory_space=pl.ANY)],
        out_specs=pl.BlockSpec(memory_space=pl.ANY),
        scratch_shapes=[pltpu.VMEM((d, f), BF16), pltpu.VMEM((d, f), BF16), pltpu.VMEM((f, d), BF16),
                        pltpu.VMEM((EXPERT_SLOTS, m, dp), U32), pltpu.VMEM((EXPERT_SLOTS, m, dp), U32),
                        pltpu.SemaphoreType.DMA((EXPERT_SLOTS,)), pltpu.SemaphoreType.DMA((EXPERT_SLOTS,))],
    )
    return pl.pallas_call(
        _expert_kernel,
        grid_spec=grid_spec,
        out_shape=jax.ShapeDtypeStruct((p, dp), U32),
        compiler_params=_params(("arbitrary",)),
    )(first_block, n_blocks_e, counts, n_used, w_gate, w_up, w_down, xs)


def _gather_rows_sc(table, idx):
    n_idx, (_, width) = idx.shape[0], table.shape
    n_workers = SC_CORES * SC_SUBCORES
    w = SC_ROWS
    per_worker = n_idx // n_workers
    n_chunks = per_worker // w
    assert n_idx == n_workers * n_chunks * w and n_chunks % 2 == 0
    mesh = plsc.VectorSubcoreMesh(core_axis_name="core", subcore_axis_name="subcore")

    @functools.partial(
        pl.kernel, mesh=mesh,
        out_type=jax.ShapeDtypeStruct((n_idx, width), table.dtype),
        scratch_types=[pltpu.VMEM((per_worker,), jnp.int32), pltpu.VMEM((2, w, width), table.dtype),
                       pltpu.SemaphoreType.DMA((2,)), pltpu.SemaphoreType.DMA((2,))])
    def gather(table_hbm, idx_hbm, out_hbm, idx_v, rows_v, in_sem, out_sem):
        base = _sc_worker() * per_worker
        pltpu.sync_copy(idx_hbm.at[pl.ds(base, per_worker)], idx_v)

        def fetch(i, s):
            return pltpu.make_async_copy(table_hbm.at[idx_v.at[pl.ds(i * w, w)]], rows_v.at[s], in_sem.at[s])

        def put(i, s):
            return pltpu.make_async_copy(rows_v.at[s], out_hbm.at[pl.ds(base + i * w, w)], out_sem.at[s])

        fetch(0, 0).start()

        def pair(ii, carry):
            for s in range(2):
                i = ii * 2 + s
                fetch(i, s).wait()

                @pl.when(i >= 1)
                def _():
                    put(i - 1, 1 - s).wait()

                @pl.when(i + 1 < n_chunks)
                def _():
                    fetch(i + 1, 1 - s).start()

                put(i, s).start()
            return carry

        lax.fori_loop(0, n_chunks // 2, pair, 0)
        put(n_chunks - 1, 1).wait()

    return gather(table, idx)


def _combine_kernel(x_ref, sc_ref, sh_ref, gate_ref, wts_ref, rows_ref,
                    wsg_ref, wsu_ref, wsd_ref, g_ref, b_ref, *out_refs):
    o_ref = out_refs[-1]
    x = x_ref[...]
    hb = (x * (1.0 + sc_ref[0]) + sh_ref[0]).astype(BF16)
    sg = jnp.dot(hb, wsg_ref[...], preferred_element_type=F32)
    su = jnp.dot(hb, wsu_ref[...], preferred_element_type=F32)
    shared = jnp.dot((_silu(sg) * su).astype(BF16), wsd_ref[...], preferred_element_type=F32)

    wts = wts_ref[...]
    half = shared.shape[1] // 2
    lo, hi = shared[:, :half], shared[:, half:]
    for j in range(TOP_K):
        y_lo, y_hi = _unpack_bf16_halves(rows_ref[j])
        lo = lo + wts[:, j:j + 1] * y_lo
        hi = hi + wts[:, j:j + 1] * y_hi
    z = ALPHA * x + gate_ref[0] * jnp.concatenate([lo, hi], axis=1)
    o_ref[...] = _layer_norm(z, g_ref[...], b_ref[...])


def _combine(x1, scale2, shift2, gate2, wts, rows, ws_gate, ws_up, ws_down, ln_g, ln_b, prev_out,
             *, seq, tc, first_token):
    n, d = x1.shape
    tps = seq // tc
    off = first_token // tc
    n_tiles = rows.shape[1] // tc
    mod_spec = pl.BlockSpec((1, 1, d), lambda i: ((i + off) // tps, 0, 0))
    full = lambda a: pl.BlockSpec(a.shape, lambda i: (0,) * a.ndim)
    args = [x1, scale2, shift2, gate2, wts, rows, ws_gate, ws_up, ws_down, ln_g, ln_b]
    in_specs = [pl.BlockSpec((tc, d), lambda i: (i + off, 0)), mod_spec, mod_spec, mod_spec,
                pl.BlockSpec((tc, TOP_K), lambda i: (i + off, 0)),
                pl.BlockSpec((TOP_K, tc, d // 2), lambda i: (0, i, 0)),
                full(ws_gate), full(ws_up), full(ws_down), full(ln_g), full(ln_b)]
    aliases = {}
    if prev_out is not None:
        aliases = {len(args): 0}
        args.append(prev_out)
        in_specs.append(pl.BlockSpec(memory_space=pl.ANY))
    return pl.pallas_call(
        _combine_kernel,
        grid=(n_tiles,),
        in_specs=in_specs,
        out_specs=pl.BlockSpec((tc, d), lambda i: (i + off, 0)),
        out_shape=jax.ShapeDtypeStruct((n, d), F32),
        input_output_aliases=aliases,
        compiler_params=_params(("arbitrary",)),
    )(*args)


def _layer(x, c, w_ada, b_ada, w_in, pool_w, pool_scale, lq1, lk1, lq2, lk2, subln_g, w_out,
           ln1_g, ln1_b, w_router, router_bias, w_gate, w_up, w_down, ws_gate, ws_up, ws_down,
           ln2_g, ln2_b, rel_table, *, tm=1024, tr=256, tc=512):
    bsz, seq, d = x.shape
    n = bsz * seq
    x2 = x.reshape(n, d)
    row = lambda a: a.reshape(1, -1)

    mod = _modulation(c, w_ada, b_ada)
    shift1, scale1, gate1, shift2, scale2, gate2 = [
        mod[:, j * d:(j + 1) * d].reshape(bsz, 1, d) for j in range(6)]

    n_main = POOL_DIM + 2 * QK_DIM
    yp, q, k, vt = _inproj(x2, scale1, shift1, w_in[:, :n_main].astype(BF16), w_in[:, n_main:].T.astype(BF16),
                           pool_w.astype(BF16), row(pool_scale), seq=seq, tm=tm)
    bias_tiles = _bias_tiles(rel_table, seq // ATT_TILE)
    ya = _attention(q, k, vt, bias_tiles, row(lq1), row(lk1), row(lq2), row(lk2), subln_g.reshape(-1, 1),
                    bsz=bsz, seq=seq)
    x1 = _outproj(x2, yp, ya, w_out.astype(BF16), gate1, row(ln1_g), row(ln1_b), seq=seq, tm=tm)

    wr_t = w_router.T
    wr_hi = wr_t.astype(BF16)
    wr_lo = (wr_t - wr_hi.astype(F32)).astype(BF16)
    eidx_t, rank_t, wts_t, cnt, h2p = _route(x1, scale2, shift2, wr_hi, wr_lo, router_bias.reshape(-1, 1),
                                             seq=seq, tr=tr)

    m = DISPATCH_BLOCK
    counts = cnt[:, 0].astype(jnp.int32)
    padded = (counts + m - 1) // m * m
    pends = jnp.cumsum(padded)
    pstarts = pends - padded
    n_blocks = -(-(n * TOP_K + N_EXPERTS * (m - 1)) // m)
    n_used = (pends[-1:] // m).astype(jnp.int32)
    dest_t = _dest(eidx_t, rank_t, pstarts, tt=min(n, 2048))

    xs = _scatter_rows_sc(h2p, dest_t.reshape(TOP_K * n), n_blocks * m)
    ys = _experts(pstarts // m, padded // m, counts, n_used, xs, w_gate, w_up, w_down)

    shared_w = (ws_gate.astype(BF16), ws_up.astype(BF16), ws_down.astype(BF16))
    wts = wts_t.T
    part = n // COMBINE_PARTS
    assert part % tc == 0 and part * COMBINE_PARTS == n
    out = None
    for p in range(COMBINE_PARTS):
        idx = dest_t[:, p * part:(p + 1) * part].reshape(TOP_K * part)
        picked = _gather_rows_sc(ys, idx).reshape(TOP_K, part, d // 2)
        out = _combine(x1, scale2, shift2, gate2, wts, picked, *shared_w, row(ln2_g), row(ln2_b), out,
                       seq=seq, tc=tc, first_token=p * part)
    return out.reshape(bsz, seq, d)


def kernel(x, c, w_ada, b_ada, w_in, pool_w, pool_scale, lambda_q1, lambda_k1, lambda_q2, lambda_k2,
           subln_g, w_out, ln1_g, ln1_b, w_router, router_bias, w_gate, w_up, w_down,
           ws_gate, ws_up, ws_down, ln2_g, ln2_b, rel_table):
    per_layer = (w_ada, b_ada, w_in, pool_w, pool_scale, lambda_q1, lambda_k1, lambda_q2, lambda_k2,
                 subln_g, w_out, ln1_g, ln1_b, w_router, router_bias, w_gate, w_up, w_down,
                 ws_gate, ws_up, ws_down, ln2_g, ln2_b)
    assert all(a.shape[0] == DEPTH == 1 for a in per_layer)
    return _layer(x, c, *[a.reshape(a.shape[1:]) for a in per_layer], rel_table)
```

```python
import functools
import math

import jax
import jax.numpy as jnp
from jax import lax
from jax.experimental import pallas as pl
from jax.experimental.pallas import tpu as pltpu
from jax.experimental.pallas import tpu_sc as plsc

F32 = jnp.float32
BF16 = jnp.bfloat16
U32 = jnp.uint32
LANES = 128

CHUNK = 64
ATT_TILE = 256
LOG2_E = math.log2(math.e)
POOL_DIM = 512
POOL_WINDOWS = (2, 4, 8, 16)
POOL_GROUP_DIM = 128
MAX_WINDOW = max(POOL_WINDOWS)
ATTN_HEADS = 4
ATTN_HEAD_DIM = 64
QK_DIM = 512
V_DIM = 512
NUM_BUCKETS = 32
MAX_DISTANCE = 128
N_EXPERTS = 256
TOP_K = 8
N_GROUPS = 8
GROUP_SIZE = N_EXPERTS // N_GROUPS
TOP_K_GROUPS = 4
ROUTED_SCALE = 2.5
DISPATCH_BLOCK = 512
EXPERT_SLOTS = 4
COMBINE_PARTS = 8
COMBINE_RING = 3
SC_CORES, SC_SUBCORES = 2, 16
SC_ROWS = 64
DEPTH = 1
ALPHA = (2.0 * DEPTH) ** 0.25
LN_EPS = 1e-5
LAMBDA_INIT = 0.8 - 0.6 * math.exp(-0.3 * 0)

VMEM_LIMIT = 48 * 1024 * 1024


def _sigmoid(x):
    return 1.0 / (1.0 + jnp.exp(-x))


def _silu(x):
    return x * _sigmoid(x)


def _layer_norm(z, g, b):
    mu = jnp.mean(z, axis=-1, keepdims=True)
    zc = z - mu
    var = jnp.mean(zc * zc, axis=-1, keepdims=True)
    return zc * lax.rsqrt(var + LN_EPS) * g + b


def _params(sem=None):
    return pltpu.CompilerParams(dimension_semantics=sem, vmem_limit_bytes=VMEM_LIMIT)


def _mod_kernel(c_ref, w_ref, b_ref, o_ref):
    ca = _silu(c_ref[...])
    o_ref[...] = jnp.dot(ca, w_ref[...], preferred_element_type=F32,
                         precision=lax.Precision.HIGHEST) + b_ref[...]


def _modulation(c, w_ada, b_ada):
    bsz, d = c.shape
    n_out = w_ada.shape[1]
    return pl.pallas_call(
        _mod_kernel,
        grid=(n_out // d,),
        in_specs=[pl.BlockSpec((bsz, d), lambda j: (0, 0)),
                  pl.BlockSpec((d, d), lambda j: (0, j)),
                  pl.BlockSpec((1, d), lambda j: (0, j))],
        out_specs=pl.BlockSpec((bsz, d), lambda j: (0, j)),
        out_shape=jax.ShapeDtypeStruct((bsz, n_out), F32),
        compiler_params=_params(("arbitrary",)),
    )(c, w_ada, b_ada.reshape(1, n_out))


def _inproj_kernel(x_ref, sc_ref, sh_ref, w_ref, wvt_ref, pw_ref, ps_ref,
                   yp_ref, q_ref, k_ref, vt_ref, ext_ref, *, tm, seq):
    i = pl.program_id(0)
    tiles_per_seq = seq // tm
    it = i % tiles_per_seq
    h = x_ref[...] * (1.0 + sc_ref[0]) + sh_ref[0]
    hb = h.astype(BF16)
    proj = jnp.dot(hb, w_ref[...], preferred_element_type=F32)
    u = proj[:, :POOL_DIM]
    q_ref[...] = (proj[:, POOL_DIM:POOL_DIM + QK_DIM] * (ATTN_HEAD_DIM ** -0.5 * LOG2_E)).astype(BF16)
    k_ref[...] = proj[:, POOL_DIM + QK_DIM:POOL_DIM + 2 * QK_DIM].astype(BF16)
    vt = lax.dot_general(wvt_ref[...], hb, (((1,), (1,)), ((), ())), preferred_element_type=F32)
    for j in range(tm // ATT_TILE):
        vt_ref[0, j] = vt[:, j * ATT_TILE:(j + 1) * ATT_TILE].astype(BF16)

    @pl.when(it == 0)
    def _():
        ext_ref[0:MAX_WINDOW, :] = jnp.zeros((MAX_WINDOW, POOL_DIM), F32)

    ext_ref[MAX_WINDOW:MAX_WINDOW + tm, :] = u
    pos = (it * tm + lax.broadcasted_iota(jnp.int32, (tm, 1), 0) + 1).astype(F32)
    for g, w in enumerate(POOL_WINDOWS):
        c0, c1 = g * POOL_GROUP_DIM, (g + 1) * POOL_GROUP_DIM
        s = ext_ref[MAX_WINDOW:MAX_WINDOW + tm, c0:c1]
        for j in range(1, w):
            s = s + ext_ref[MAX_WINDOW - j:MAX_WINDOW - j + tm, c0:c1]
        pooled = s / jnp.minimum(pos, float(w)) - u[:, c0:c1]
        y = jnp.dot(pooled.astype(BF16), pw_ref[g], preferred_element_type=F32)
        yp_ref[:, c0:c1] = (y * ps_ref[:, c0:c1]).astype(BF16)
    ext_ref[0:MAX_WINDOW, :] = ext_ref[tm:tm + MAX_WINDOW, :]


def _inproj(x2, scale1, shift1, w_main, w_vt, pool_w, pool_scale, *, seq, tm):
    n, d = x2.shape
    assert n % tm == 0 and seq % tm == 0 and tm >= 2 * MAX_WINDOW and tm % ATT_TILE == 0
    tps = seq // tm
    tpt = tm // ATT_TILE
    mod_spec = pl.BlockSpec((1, 1, d), lambda i: (i // tps, 0, 0))
    row = lambda w: pl.BlockSpec((tm, w), lambda i: (i, 0))
    full = lambda a: pl.BlockSpec(a.shape, lambda i: (0,) * a.ndim)
    return pl.pallas_call(
        functools.partial(_inproj_kernel, tm=tm, seq=seq),
        grid=(n // tm,),
        in_specs=[row(d), mod_spec, mod_spec, full(w_main), full(w_vt), full(pool_w), full(pool_scale)],
        out_specs=[row(POOL_DIM), row(QK_DIM), row(QK_DIM),
                   pl.BlockSpec((1, tpt, V_DIM, ATT_TILE), lambda i: (i // tps, i % tps, 0, 0))],
        out_shape=[jax.ShapeDtypeStruct((n, POOL_DIM), BF16),
                   jax.ShapeDtypeStruct((n, QK_DIM), BF16),
                   jax.ShapeDtypeStruct((n, QK_DIM), BF16),
                   jax.ShapeDtypeStruct((n // seq, seq // ATT_TILE, V_DIM, ATT_TILE), BF16)],
        scratch_shapes=[pltpu.VMEM((tm + MAX_WINDOW, POOL_DIM), F32)],
        compiler_params=_params(("arbitrary",)),
    )(x2, scale1, shift1, w_main, w_vt, pool_w, pool_scale)


def _bias_kernel(tab_ref, o_ref):
    delta = pl.program_id(0)
    r = lax.broadcasted_iota(jnp.int32, (ATT_TILE, ATT_TILE), 0)
    c = lax.broadcasted_iota(jnp.int32, (ATT_TILE, ATT_TILE), 1)
    rel = r - c - delta * ATT_TILE
    half = NUM_BUCKETS // 2
    max_exact = half // 2
    ret = jnp.where(rel > 0, half, 0)
    n = jnp.abs(rel)
    nf = jnp.maximum(n, 1).astype(F32)
    large = max_exact + (jnp.log(nf / max_exact) / math.log(MAX_DISTANCE / max_exact)
                         * (half - max_exact)).astype(jnp.int32)
    large = jnp.minimum(large, half - 1)
    bucket = ret + jnp.where(n < max_exact, n, large)
    for h in range(ATTN_HEADS):
        acc = jnp.zeros((ATT_TILE, ATT_TILE), F32)
        for b in range(NUM_BUCKETS):
            acc = jnp.where(bucket == b, tab_ref[b, h], acc)
        o_ref[h, 0] = acc * LOG2_E


def _bias_tiles(rel_table, n_tiles):
    return pl.pallas_call(
        _bias_kernel,
        grid=(n_tiles,),
        in_specs=[pl.BlockSpec(memory_space=pltpu.SMEM)],
        out_specs=pl.BlockSpec((ATTN_HEADS, 1, ATT_TILE, ATT_TILE), lambda dlt: (0, dlt, 0, 0)),
        out_shape=jax.ShapeDtypeStruct((ATTN_HEADS, n_tiles, ATT_TILE, ATT_TILE), F32),
        compiler_params=_params(("arbitrary",)),
    )(rel_table)


def _attn_kernel(q_ref, k_ref, vt_ref, bias_ref, lq1_ref, lk1_ref, lq2_ref, lk2_ref, g_ref, o_ref, *acc_refs):
    qt = pl.program_id(1)
    t = ATT_TILE
    n_maps = 2 * ATTN_HEADS
    lam = (jnp.exp(jnp.sum(lq1_ref[...] * lk1_ref[...], axis=-1, keepdims=True))
           - jnp.exp(jnp.sum(lq2_ref[...] * lk2_ref[...], axis=-1, keepdims=True))
           + LAMBDA_INIT)
    r = lax.broadcasted_iota(jnp.int32, (t, t), 0)
    c = lax.broadcasted_iota(jnp.int32, (t, t), 1)
    allowed = (r // CHUNK) <= (c // CHUNK)
    hd2 = 2 * ATTN_HEAD_DIM

    ahead = 2

    def scores(kt, hm):
        col = hm * ATTN_HEAD_DIM
        qh = q_ref[:, col:col + ATTN_HEAD_DIM]
        kh = k_ref[pl.ds(pl.multiple_of(kt * t, t), t), col:col + ATTN_HEAD_DIM]
        return lax.dot_general(kh, qh, (((1,), (1,)), ((), ())),
                               preferred_element_type=F32) + bias_ref[hm // 2, qt - kt]

    def block(kt, carry, diagonal):
        stats, early = carry[:2 * n_maps], carry[2 * n_maps:]

        def softmax(hm, s):
            if diagonal:
                s = jnp.where(allowed, s, -jnp.inf)
            m_old, l_old = stats[2 * hm:2 * hm + 2]
            m_new = jnp.maximum(m_old, jnp.max(s, axis=0, keepdims=True))
            alpha = jnp.exp2(m_old - m_new)
            p = jnp.exp2(s - m_new)
            return m_new, alpha * l_old + jnp.sum(p, axis=0, keepdims=True), alpha, p.astype(BF16)

        def accumulate(hm, alpha, p):
            h = hm // 2
            vth = vt_ref[0, kt, h * hd2:(h + 1) * hd2, :]
            acc_refs[hm][...] = alpha * acc_refs[hm][...] + jnp.dot(vth, p, preferred_element_type=F32)

        s_vals = dict(enumerate(early))
        sm_vals, out, nxt = {}, [None] * (2 * n_maps), []
        for step in range(1, n_maps + ahead):
            if ahead <= step < n_maps:
                s_vals[step] = scores(kt, step)
            elif step >= n_maps and not diagonal:
                nxt.append(scores(kt + 1, step - n_maps))
            hm = step - 1
            if hm < n_maps:
                m_new, l_new, alpha, p = softmax(hm, s_vals.pop(hm))
                out[2 * hm], out[2 * hm + 1] = m_new, l_new
                sm_vals[hm] = (alpha, p)
            if step - 2 >= 0:
                accumulate(step - 2, *sm_vals.pop(step - 2))
        return tuple(out) + tuple(nxt)

    for acc in acc_refs:
        acc[...] = jnp.zeros_like(acc)
    one = (jnp.full((1, t), -jnp.inf, F32), jnp.zeros((1, t), F32))
    first = tuple(scores(0, hm) for hm in range(ahead))
    carry = lax.fori_loop(0, qt, lambda kt, cr: block(kt, cr, False), one * n_maps + first)
    carry = block(qt, carry, True)
    for h in range(ATTN_HEADS):
        l0, l1 = carry[4 * h + 1], carry[4 * h + 3]
        o = acc_refs[2 * h][...] / l0 - lam * (acc_refs[2 * h + 1][...] / l1)
        y = o * lax.rsqrt(jnp.mean(o * o, axis=0, keepdims=True) + LN_EPS) * g_ref[...]
        o_ref[:, h * hd2:(h + 1) * hd2] = (y * (1.0 - LAMBDA_INIT)).T.astype(BF16)


def _attention(q, k, vt, bias_tiles, lq1, lk1, lq2, lk2, subln_g, *, bsz, seq):
    t = ATT_TILE
    nt = seq // t
    full = lambda a: pl.BlockSpec(a.shape, lambda b, j: (0,) * a.ndim)
    return pl.pallas_call(
        _attn_kernel,
        grid=(bsz, nt),
        in_specs=[pl.BlockSpec((t, QK_DIM), lambda b, j: (b * nt + j, 0)),
                  pl.BlockSpec((seq, QK_DIM), lambda b, j: (b, 0)),
                  pl.BlockSpec((1, nt, V_DIM, t), lambda b, j: (b, 0, 0, 0)),
                  full(bias_tiles), full(lq1), full(lk1), full(lq2), full(lk2), full(subln_g)],
        out_specs=pl.BlockSpec((t, V_DIM), lambda b, j: (b * nt + j, 0)),
        out_shape=jax.ShapeDtypeStruct((bsz * seq, V_DIM), BF16),
        scratch_shapes=[pltpu.VMEM((2 * ATTN_HEAD_DIM, t), F32) for _ in range(2 * ATTN_HEADS)],
        compiler_params=_params(("arbitrary", "arbitrary")),
    )(q, k, vt, bias_tiles, lq1, lk1, lq2, lk2, subln_g)


def _outproj_kernel(x_ref, yp_ref, ya_ref, w_ref, gate_ref, g_ref, b_ref, o_ref):
    mix = (jnp.dot(yp_ref[...], w_ref[0:POOL_DIM, :], preferred_element_type=F32)
           + jnp.dot(ya_ref[...], w_ref[POOL_DIM:, :], preferred_element_type=F32))
    z = ALPHA * x_ref[...] + gate_ref[0] * mix
    o_ref[...] = _layer_norm(z, g_ref[...], b_ref[...])


def _outproj(x2, yp, ya, w_out, gate1, ln_g, ln_b, *, seq, tm):
    n, d = x2.shape
    tps = seq // tm
    row = lambda w: pl.BlockSpec((tm, w), lambda i: (i, 0))
    full = lambda a: pl.BlockSpec(a.shape, lambda i: (0,) * a.ndim)
    return pl.pallas_call(
        _outproj_kernel,
        grid=(n // tm,),
        in_specs=[row(d), row(POOL_DIM), row(V_DIM), full(w_out),
                  pl.BlockSpec((1, 1, d), lambda i: (i // tps, 0, 0)), full(ln_g), full(ln_b)],
        out_specs=row(d),
        out_shape=jax.ShapeDtypeStruct((n, d), F32),
        compiler_params=_params(("arbitrary",)),
    )(x2, yp, ya, w_out, gate1, ln_g, ln_b)


def _route_kernel(x_ref, sc_ref, sh_ref, whi_ref, wlo_ref, rb_ref,
                  eidx_ref, rank_ref, wts_ref, cnt_ref, h2p_ref, carry_ref, *, tr):
    i = pl.program_id(0)

    @pl.when(i == 0)
    def _():
        carry_ref[...] = jnp.zeros_like(carry_ref)

    h2 = x_ref[...] * (1.0 + sc_ref[0]) + sh_ref[0]
    h2p_ref[...] = _pack_bf16_halves(h2)
    hi = h2.astype(BF16)
    lo = (h2 - hi.astype(F32)).astype(BF16)
    nt = (((1,), (1,)), ((), ()))
    logits = (lax.dot_general(whi_ref[...], hi, nt, preferred_element_type=F32)
              + lax.dot_general(wlo_ref[...], hi, nt, preferred_element_type=F32)
              + lax.dot_general(whi_ref[...], lo, nt, preferred_element_type=F32))
    scores = _sigmoid(logits)
    sel = scores + rb_ref[...]
    erow = lax.broadcasted_iota(jnp.int32, (N_EXPERTS, tr), 0).astype(F32)

    g3 = sel.reshape(N_GROUPS, GROUP_SIZE, tr)
    r3 = lax.broadcasted_iota(jnp.int32, (N_GROUPS, GROUP_SIZE, tr), 1).astype(F32)
    m1 = jnp.max(g3, axis=1, keepdims=True)
    first = jnp.min(jnp.where(g3 == m1, r3, float(GROUP_SIZE)), axis=1, keepdims=True)
    m2 = jnp.max(jnp.where(r3 == first, -jnp.inf, g3), axis=1, keepdims=True)
    gscore = (m1 + m2).reshape(N_GROUPS, tr)

    gidx = lax.broadcasted_iota(jnp.int32, (N_GROUPS, tr), 0)
    beaten_by = jnp.zeros((N_GROUPS, tr), jnp.int32)
    for g in range(N_GROUPS):
        other = gscore[g:g + 1, :]
        wins = (other > gscore) | ((other == gscore) & (g < gidx))
        beaten_by = beaten_by + wins.astype(jnp.int32)
    dropped = jnp.where(beaten_by < TOP_K_GROUPS, 0.0, -jnp.inf)
    cur = (g3 + dropped.reshape(N_GROUPS, 1, tr)).reshape(N_EXPERTS, tr)

    picks, weights = [], []
    selmask = jnp.zeros((N_EXPERTS, tr), F32)
    for _ in range(TOP_K):
        mx = jnp.max(cur, axis=0, keepdims=True)
        pick = jnp.min(jnp.where(cur == mx, erow, float(N_EXPERTS)), axis=0, keepdims=True)
        onehot = erow == pick
        weights.append(jnp.sum(jnp.where(onehot, scores, 0.0), axis=0, keepdims=True))
        cur = jnp.where(onehot, -jnp.inf, cur)
        selmask = jnp.where(onehot, 1.0, selmask)
        picks.append(pick)

    t_from = lax.broadcasted_iota(jnp.int32, (tr, tr), 0)
    t_to = lax.broadcasted_iota(jnp.int32, (tr, tr), 1)
    earlier = jnp.where(t_from < t_to, 1.0, 0.0).astype(BF16)
    chosen = selmask.astype(BF16)
    carry = carry_ref[...]
    rankmat = (jnp.dot(chosen, earlier, preferred_element_type=F32)
               + jnp.concatenate([carry] * (tr // LANES), axis=1))
    carry_ref[...] = carry + jnp.dot(chosen, jnp.ones((tr, LANES), BF16), preferred_element_type=F32)
    cnt_ref[...] = carry_ref[...]

    wsum = weights[0]
    for wj in weights[1:]:
        wsum = wsum + wj
    row8 = lax.broadcasted_iota(jnp.int32, (TOP_K, tr), 0)
    eidx = jnp.zeros((TOP_K, tr), jnp.int32)
    rank = jnp.zeros((TOP_K, tr), jnp.int32)
    wts = jnp.zeros((TOP_K, tr), F32)
    for j in range(TOP_K):
        rk = jnp.sum(jnp.where(erow == picks[j], rankmat, 0.0), axis=0, keepdims=True)
        eidx = jnp.where(row8 == j, picks[j].astype(jnp.int32), eidx)
        rank = jnp.where(row8 == j, rk.astype(jnp.int32), rank)
        wts = jnp.where(row8 == j, weights[j] / wsum * ROUTED_SCALE, wts)
    eidx_ref[...] = eidx
    rank_ref[...] = rank
    wts_ref[...] = wts


def _route(x1, scale2, shift2, wr_hi, wr_lo, router_bias, *, seq, tr):
    n, d = x1.shape
    tps = seq // tr
    mod_spec = pl.BlockSpec((1, 1, d), lambda i: (i // tps, 0, 0))
    full = lambda a: pl.BlockSpec(a.shape, lambda i: (0,) * a.ndim)
    assert tr % LANES == 0
    k8 = pl.BlockSpec((TOP_K, tr), lambda i: (0, i))
    return pl.pallas_call(
        functools.partial(_route_kernel, tr=tr),
        grid=(n // tr,),
        in_specs=[pl.BlockSpec((tr, d), lambda i: (i, 0)), mod_spec, mod_spec,
                  full(wr_hi), full(wr_lo), full(router_bias)],
        out_specs=[k8, k8, k8, pl.BlockSpec((N_EXPERTS, LANES), lambda i: (0, 0)),
                   pl.BlockSpec((tr, d // 2), lambda i: (i, 0))],
        out_shape=[jax.ShapeDtypeStruct((TOP_K, n), jnp.int32),
                   jax.ShapeDtypeStruct((TOP_K, n), jnp.int32),
                   jax.ShapeDtypeStruct((TOP_K, n), F32),
                   jax.ShapeDtypeStruct((N_EXPERTS, LANES), F32),
                   jax.ShapeDtypeStruct((n, d // 2), U32)],
        scratch_shapes=[pltpu.VMEM((N_EXPERTS, LANES), F32)],
        compiler_params=_params(("arbitrary",)),
    )(x1, scale2, shift2, wr_hi, wr_lo, router_bias)


def _pack_bf16_halves(x):
    w = x.shape[1] // 2
    lo = pltpu.bitcast(x[:, :w].astype(BF16).astype(F32), U32) >> 16
    hi = pltpu.bitcast(x[:, w:].astype(BF16).astype(F32), U32) & jnp.uint32(0xFFFF0000)
    return lo | hi


def _unpack_bf16_halves(p):
    return pltpu.bitcast(p << 16, F32), pltpu.bitcast(p & jnp.uint32(0xFFFF0000), F32)


def _sc_worker():
    return lax.axis_index("subcore") * SC_CORES + lax.axis_index("core")


def _scatter_rows_sc(rows, idx, n_slots):
    n, width = rows.shape
    k = idx.shape[0] // n
    n_workers = SC_CORES * SC_SUBCORES
    w = SC_ROWS
    per_worker = n // n_workers
    n_chunks = per_worker // w
    assert n == n_workers * n_chunks * w and n_chunks % 2 == 0 and idx.shape[0] == k * n
    mesh = plsc.VectorSubcoreMesh(core_axis_name="core", subcore_axis_name="subcore")

    @functools.partial(
        pl.kernel, mesh=mesh,
        out_type=jax.ShapeDtypeStruct((n_slots, width), rows.dtype),
        scratch_types=[pltpu.VMEM((k * per_worker,), jnp.int32), pltpu.VMEM((2, w, width), rows.dtype),
                       pltpu.SemaphoreType.DMA((2,)), pltpu.SemaphoreType.DMA((2,))])
    def scatter(rows_hbm, idx_hbm, out_hbm, idx_v, rows_v, in_sem, out_sem):
        t0 = _sc_worker() * per_worker
        for j in range(k):
            pltpu.sync_copy(idx_hbm.at[pl.ds(j * n + t0, per_worker)], idx_v.at[pl.ds(j * per_worker, per_worker)])

        def load(i, s):
            return pltpu.make_async_copy(rows_hbm.at[pl.ds(t0 + i * w, w)], rows_v.at[s], in_sem.at[s])

        def send(i, s, j):
            slots = idx_v.at[pl.ds(j * per_worker + i * w, w)]
            return pltpu.make_async_copy(rows_v.at[s], out_hbm.at[slots], out_sem.at[s])

        load(0, 0).start()

        def pair(ii, carry):
            for s in range(2):
                i = ii * 2 + s
                load(i, s).wait()

                @pl.when(i >= 1)
                def _():
                    for j in range(k):
                        send(i - 1, 1 - s, j).wait()

                @pl.when(i + 1 < n_chunks)
                def _():
                    load(i + 1, 1 - s).start()

                for j in range(k):
                    send(i, s, j).start()
            return carry

        lax.fori_loop(0, n_chunks // 2, pair, 0)
        for j in range(k):
            send(n_chunks - 1, 1, j).wait()

    return scatter(rows, idx)


def _dest_kernel(eidx_ref, rank_ref, first_ref, dest_ref):
    tt = eidx_ref.shape[1]
    pieces = [jnp.broadcast_to(first_ref[:, p * LANES:(p + 1) * LANES], (TOP_K, LANES))
              for p in range(N_EXPERTS // LANES)]
    for c in range(tt // LANES):
        cols = slice(c * LANES, (c + 1) * LANES)
        e = eidx_ref[:, cols]
        within = e & (LANES - 1)
        start = jnp.take_along_axis(pieces[0], within, axis=1)
        for p in range(1, len(pieces)):
            start = jnp.where(e // LANES == p, jnp.take_along_axis(pieces[p], within, axis=1), start)
        dest_ref[:, cols] = start + rank_ref[:, cols]


def _dest(eidx_t, rank_t, pstarts, *, tt):
    n = eidx_t.shape[1]
    first = pstarts.reshape(1, N_EXPERTS)
    k8 = pl.BlockSpec((TOP_K, tt), lambda i: (0, i))
    return pl.pallas_call(
        _dest_kernel,
        grid=(n // tt,),
        in_specs=[k8, k8, pl.BlockSpec(first.shape, lambda i: (0, 0))],
        out_specs=k8,
        out_shape=jax.ShapeDtypeStruct((TOP_K, n), jnp.int32),
        compiler_params=_params(("arbitrary",)),
    )(eidx_t, rank_t, first)


def _expert_kernel(fb_ref, nb_ref, cnt_ref, nused_ref, wg_ref, wu_ref, wd_ref, xs_ref, ys_ref,
                   wgb, wub, wdb, xbuf, ybuf, in_sem, out_sem):
    e = pl.program_id(0)
    m = DISPATCH_BLOCK
    ns = EXPERT_SLOTS
    n_used = nused_ref[0]

    def rows(g):
        return pl.ds(pl.multiple_of(g * m, m), m)

    def fetch(g):
        slot = g & (ns - 1)
        return pltpu.make_async_copy(xs_ref.at[rows(g), :], xbuf.at[slot], in_sem.at[slot])

    def put(g):
        slot = g & (ns - 1)
        return pltpu.make_async_copy(ybuf.at[slot], ys_ref.at[rows(g), :], out_sem.at[slot])

    @pl.when(e == 0)
    def _():
        for g0 in range(ns - 1):
            @pl.when(g0 < n_used)
            def _(g0=g0):
                fetch(g0).start()

    wgb[...] = wg_ref[0].astype(BF16)
    wub[...] = wu_ref[0].astype(BF16)
    wdb[...] = wd_ref[0].astype(BF16)

    def block(i, carry):
        g = fb_ref[e] + i
        slot = g & (ns - 1)
        fetch(g).wait()

        @pl.when(g + ns - 1 < n_used)
        def _():
            fetch(g + ns - 1).start()

        @pl.when(g >= ns)
        def _():
            put(g - ns).wait()

        n_valid = cnt_ref[e] - i * m

        def swiglu(rows):
            row = lax.broadcasted_iota(jnp.int32, (rows, 1), 0)
            packed = jnp.where(row < n_valid, xbuf[slot, 0:rows], jnp.uint32(0))
            x_lo, x_hi = [h.astype(BF16) for h in _unpack_bf16_halves(packed)]
            half = x_lo.shape[1]

            def up_proj(w):
                return (jnp.dot(x_lo, w[0:half, :], preferred_element_type=F32)
                        + jnp.dot(x_hi, w[half:, :], preferred_element_type=F32))

            a = (_silu(up_proj(wgb)) * up_proj(wub)).astype(BF16)
            ybuf[slot, 0:rows] = _pack_bf16_halves(jnp.dot(a, wdb[...], preferred_element_type=F32))

        pl.when(n_valid > m // 2)(functools.partial(swiglu, m))
        pl.when(n_valid <= m // 2)(functools.partial(swiglu, m // 2))
        put(g).start()
        return carry

    lax.fori_loop(0, nb_ref[e], block, 0)

    @pl.when(e == pl.num_programs(0) - 1)
    def _():
        for back in range(ns, 0, -1):
            @pl.when(n_used >= back)
            def _(back=back):
                put(n_used - back).wait()

        ybuf[0] = jnp.zeros(ybuf.shape[1:], ybuf.dtype)
        n_blocks = ys_ref.shape[0] // m

        def tail(wait):
            def body(g, carry):
                cp = pltpu.make_async_copy(ybuf.at[0], ys_ref.at[rows(g), :], out_sem.at[0])
                cp.wait() if wait else cp.start()
                return carry
            return body

        lax.fori_loop(n_used, n_blocks, tail(False), 0)
        lax.fori_loop(n_used, n_blocks, tail(True), 0)


def _experts(first_block, n_blocks_e, counts, n_used, xs, w_gate, w_up, w_down):
    p, dp = xs.shape
    m = DISPATCH_BLOCK
    n_e, d, f = w_gate.shape
    assert d == 2 * dp and p % m == 0
    grid_spec = pltpu.PrefetchScalarGridSpec(
        num_scalar_prefetch=4,
        grid=(n_e,),
        in_specs=[pl.BlockSpec((1, d, f), lambda e, *_: (e, 0, 0)),
                  pl.BlockSpec((1, d, f), lambda e, *_: (e, 0, 0)),
                  pl.BlockSpec((1, f, d), lambda e, *_: (e, 0, 0)),
                  pl.BlockSpec(memory_space=pl.ANY)],
        out_specs=pl.BlockSpec(memory_space=pl.ANY),
        scratch_shapes=[pltpu.VMEM((d, f), BF16), pltpu.VMEM((d, f), BF16), pltpu.VMEM((f, d), BF16),
                        pltpu.VMEM((EXPERT_SLOTS, m, dp), U32), pltpu.VMEM((EXPERT_SLOTS, m, dp), U32),
                        pltpu.SemaphoreType.DMA((EXPERT_SLOTS,)), pltpu.SemaphoreType.DMA((EXPERT_SLOTS,))],
    )
    return pl.pallas_call(
        _expert_kernel,
        grid_spec=grid_spec,
        out_shape=jax.ShapeDtypeStruct((p, dp), U32),
        compiler_params=_params(("arbitrary",)),
    )(first_block, n_blocks_e, counts, n_used, w_gate, w_up, w_down, xs)


def _gather_rows_sc(table, idx):
    n_idx, (_, width) = idx.shape[0], table.shape
    n_workers = SC_CORES * SC_SUBCORES
    w = SC_ROWS
    per_worker = n_idx // n_workers
    n_chunks = per_worker // w
    assert n_idx == n_workers * n_chunks * w and n_chunks % 2 == 0
    mesh = plsc.VectorSubcoreMesh(core_axis_name="core", subcore_axis_name="subcore")

    @functools.partial(
        pl.kernel, mesh=mesh,
        out_type=jax.ShapeDtypeStruct((n_idx, width), table.dtype),
        scratch_types=[pltpu.VMEM((per_worker,), jnp.int32), pltpu.VMEM((2, w, width), table.dtype),
                       pltpu.SemaphoreType.DMA((2,)), pltpu.SemaphoreType.DMA((2,))])
    def gather(table_hbm, idx_hbm, out_hbm, idx_v, rows_v, in_sem, out_sem):
        base = _sc_worker() * per_worker
        pltpu.sync_copy(idx_hbm.at[pl.ds(base, per_worker)], idx_v)

        def fetch(i, s):
            return pltpu.make_async_copy(table_hbm.at[idx_v.at[pl.ds(i * w, w)]], rows_v.at[s], in_sem.at[s])

        def put(i, s):
            return pltpu.make_async_copy(rows_v.at[s], out_hbm.at[pl.ds(base + i * w, w)], out_sem.at[s])

        fetch(0, 0).start()

        def pair(ii, carry):
            for s in range(2):
                i = ii * 2 + s
                fetch(i, s).wait()

                @pl.when(i >= 1)
                def _():
                    put(i - 1, 1 - s).wait()

                @pl.when(i + 1 < n_chunks)
                def _():
                    fetch(i + 1, 1 - s).start()

                put(i, s).start()
            return carry

        lax.fori_loop(0, n_chunks // 2, pair, 0)
        put(n_chunks - 1, 1).wait()

    return gather(table, idx)


def _combine_kernel(x_ref, sc_ref, sh_ref, gate_ref, wts_ref, rows_ref,
                    wsg_ref, wsu_ref, wsd_ref, g_ref, b_ref, *rest, tc):
    o_ref, ring, sems = rest[-3:]
    i = pl.program_id(0)
    n_tiles = pl.num_programs(0)

    def fetch(tile):
        slot = lax.rem(tile, COMBINE_RING)
        return pltpu.make_async_copy(rows_ref.at[:, pl.ds(pl.multiple_of(tile * tc, tc), tc), :],
                                     ring.at[slot], sems.at[slot])

    @pl.when(i == 0)
    def _():
        for first in range(COMBINE_RING):
            pl.when(first < n_tiles)(lambda first=first: fetch(first).start())

    @pl.when((i > 0) & (i + COMBINE_RING - 1 < n_tiles))
    def _():
        fetch(i + COMBINE_RING - 1).start()

    x = x_ref[...]
    hb = (x * (1.0 + sc_ref[0]) + sh_ref[0]).astype(BF16)
    sg = jnp.dot(hb, wsg_ref[...], preferred_element_type=F32)
    su = jnp.dot(hb, wsu_ref[...], preferred_element_type=F32)
    shared = jnp.dot((_silu(sg) * su).astype(BF16), wsd_ref[...], preferred_element_type=F32)

    fetch(i).wait()
    tile = ring.at[lax.rem(i, COMBINE_RING)]
    wts = wts_ref[...]
    half = shared.shape[1] // 2
    lo, hi = shared[:, :half], shared[:, half:]
    for j in range(TOP_K):
        y_lo, y_hi = _unpack_bf16_halves(tile[j])
        lo = lo + wts[:, j:j + 1] * y_lo
        hi = hi + wts[:, j:j + 1] * y_hi
    z = ALPHA * x + gate_ref[0] * jnp.concatenate([lo, hi], axis=1)
    o_ref[...] = _layer_norm(z, g_ref[...], b_ref[...])


def _combine(x1, scale2, shift2, gate2, wts, rows, ws_gate, ws_up, ws_down, ln_g, ln_b, prev_out,
             *, seq, tc, first_token):
    n, d = x1.shape
    tps = seq // tc
    off = first_token // tc
    n_tiles = rows.shape[1] // tc
    mod_spec = pl.BlockSpec((1, 1, d), lambda i: ((i + off) // tps, 0, 0))
    full = lambda a: pl.BlockSpec(a.shape, lambda i: (0,) * a.ndim)
    args = [x1, scale2, shift2, gate2, wts, rows, ws_gate, ws_up, ws_down, ln_g, ln_b]
    in_specs = [pl.BlockSpec((tc, d), lambda i: (i + off, 0)), mod_spec, mod_spec, mod_spec,
                pl.BlockSpec((tc, TOP_K), lambda i: (i + off, 0)),
                pl.BlockSpec(memory_space=pl.ANY),
                full(ws_gate), full(ws_up), full(ws_down), full(ln_g), full(ln_b)]
    aliases = {}
    if prev_out is not None:
        aliases = {len(args): 0}
        args.append(prev_out)
        in_specs.append(pl.BlockSpec(memory_space=pl.ANY))
    return pl.pallas_call(
        functools.partial(_combine_kernel, tc=tc),
        grid=(n_tiles,),
        in_specs=in_specs,
        out_specs=pl.BlockSpec((tc, d), lambda i: (i + off, 0)),
        out_shape=jax.ShapeDtypeStruct((n, d), F32),
        scratch_shapes=[pltpu.VMEM((COMBINE_RING, TOP_K, tc, d // 2), U32),
                        pltpu.SemaphoreType.DMA((COMBINE_RING,))],
        input_output_aliases=aliases,
        compiler_params=_params(("arbitrary",)),
    )(*args)


def _layer(x, c, w_ada, b_ada, w_in, pool_w, pool_scale, lq1, lk1, lq2, lk2, subln_g, w_out,
           ln1_g, ln1_b, w_router, router_bias, w_gate, w_up, w_down, ws_gate, ws_up, ws_down,
           ln2_g, ln2_b, rel_table, *, tm=1024, tr=256, tc=512):
    bsz, seq, d = x.shape
    n = bsz * seq
    x2 = x.reshape(n, d)
    row = lambda a: a.reshape(1, -1)

    mod = _modulation(c, w_ada, b_ada)
    shift1, scale1, gate1, shift2, scale2, gate2 = [
        mod[:, j * d:(j + 1) * d].reshape(bsz, 1, d) for j in range(6)]

    n_main = POOL_DIM + 2 * QK_DIM
    yp, q, k, vt = _inproj(x2, scale1, shift1, w_in[:, :n_main].astype(BF16), w_in[:, n_main:].T.astype(BF16),
                           pool_w.astype(BF16), row(pool_scale), seq=seq, tm=tm)
    bias_tiles = _bias_tiles(rel_table, seq // ATT_TILE)
    ya = _attention(q, k, vt, bias_tiles, row(lq1), row(lk1), row(lq2), row(lk2), subln_g.reshape(-1, 1),
                    bsz=bsz, seq=seq)
    x1 = _outproj(x2, yp, ya, w_out.astype(BF16), gate1, row(ln1_g), row(ln1_b), seq=seq, tm=tm)

    wr_t = w_router.T
    wr_hi = wr_t.astype(BF16)
    wr_lo = (wr_t - wr_hi.astype(F32)).astype(BF16)
    eidx_t, rank_t, wts_t, cnt, h2p = _route(x1, scale2, shift2, wr_hi, wr_lo, router_bias.reshape(-1, 1),
                                             seq=seq, tr=tr)

    m = DISPATCH_BLOCK
    counts = cnt[:, 0].astype(jnp.int32)
    padded = (counts + m - 1) // m * m
    pends = jnp.cumsum(padded)
    pstarts = pends - padded
    n_blocks = -(-(n * TOP_K + N_EXPERTS * (m - 1)) // m)
    n_used = (pends[-1:] // m).astype(jnp.int32)
    dest_t = _dest(eidx_t, rank_t, pstarts, tt=min(n, 2048))

    xs = _scatter_rows_sc(h2p, dest_t.reshape(TOP_K * n), n_blocks * m)
    ys = _experts(pstarts // m, padded // m, counts, n_used, xs, w_gate, w_up, w_down)

    shared_w = (ws_gate.astype(BF16), ws_up.astype(BF16), ws_down.astype(BF16))
    wts = wts_t.T
    part = n // COMBINE_PARTS
    assert part % tc == 0 and part * COMBINE_PARTS == n
    out = None
    for p in range(COMBINE_PARTS):
        idx = dest_t[:, p * part:(p + 1) * part].reshape(TOP_K * part)
        picked = _gather_rows_sc(ys, idx).reshape(TOP_K, part, d // 2)
        out = _combine(x1, scale2, shift2, gate2, wts, picked, *shared_w, row(ln2_g), row(ln2_b), out,
                       seq=seq, tc=tc, first_token=p * part)
    return out.reshape(bsz, seq, d)


def kernel(x, c, w_ada, b_ada, w_in, pool_w, pool_scale, lambda_q1, lambda_k1, lambda_q2, lambda_k2,
           subln_g, w_out, ln1_g, ln1_b, w_router, router_bias, w_gate, w_up, w_down,
           ws_gate, ws_up, ws_down, ln2_g, ln2_b, rel_table):
    per_layer = (w_ada, b_ada, w_in, pool_w, pool_scale, lambda_q1, lambda_k1, lambda_q2, lambda_k2,
                 subln_g, w_out, ln1_g, ln1_b, w_router, router_bias, w_gate, w_up, w_down,
                 ws_gate, ws_up, ws_down, ln2_g, ln2_b)
    assert all(a.shape[0] == DEPTH == 1 for a in per_layer)
    return _layer(x, c, *[a.reshape(a.shape[1:]) for a in per_layer], rel_table)
```

```python
import functools
import math

import jax
import jax.numpy as jnp
from jax import lax
from jax.experimental import pallas as pl
from jax.experimental.pallas import tpu as pltpu
from jax.experimental.pallas import tpu_sc as plsc

F32 = jnp.float32
BF16 = jnp.bfloat16
U32 = jnp.uint32
LANES = 128

CHUNK = 64
ATT_TILE = 256
LOG2_E = math.log2(math.e)
POOL_DIM = 512
POOL_WINDOWS = (2, 4, 8, 16)
POOL_GROUP_DIM = 128
MAX_WINDOW = max(POOL_WINDOWS)
ATTN_HEADS = 4
ATTN_HEAD_DIM = 64
QK_DIM = 512
V_DIM = 512
NUM_BUCKETS = 32
MAX_DISTANCE = 128
N_EXPERTS = 256
TOP_K = 8
N_GROUPS = 8
GROUP_SIZE = N_EXPERTS // N_GROUPS
TOP_K_GROUPS = 4
ROUTED_SCALE = 2.5
DISPATCH_BLOCK = 512
EXPERT_SLOTS = 4
COMBINE_PARTS = 8
SC_CORES, SC_SUBCORES = 2, 16
SC_ROWS = 64
DEPTH = 1
ALPHA = (2.0 * DEPTH) ** 0.25
LN_EPS = 1e-5
LAMBDA_INIT = 0.8 - 0.6 * math.exp(-0.3 * 0)


def _sigmoid(x):
    return 1.0 / (1.0 + jnp.exp(-x))


def _silu(x):
    return x * _sigmoid(x)


def _layer_norm(z, g, b):
    mu = jnp.mean(z, axis=-1, keepdims=True)
    zc = z - mu
    var = jnp.mean(zc * zc, axis=-1, keepdims=True)
    return zc * lax.rsqrt(var + LN_EPS) * g + b


def _params(sem, vmem_mib):
    return pltpu.CompilerParams(dimension_semantics=sem, vmem_limit_bytes=vmem_mib * 1024 * 1024)


def _mod_kernel(c_ref, w_ref, b_ref, o_ref):
    ca = _silu(c_ref[...])
    o_ref[...] = jnp.dot(ca, w_ref[...], preferred_element_type=F32,
                         precision=lax.Precision.HIGHEST) + b_ref[...]


def _modulation(c, w_ada, b_ada):
    bsz, d = c.shape
    n_out = w_ada.shape[1]
    return pl.pallas_call(
        _mod_kernel,
        grid=(n_out // d,),
        in_specs=[pl.BlockSpec((bsz, d), lambda j: (0, 0)),
                  pl.BlockSpec((d, d), lambda j: (0, j)),
                  pl.BlockSpec((1, d), lambda j: (0, j))],
        out_specs=pl.BlockSpec((bsz, d), lambda j: (0, j)),
        out_shape=jax.ShapeDtypeStruct((bsz, n_out), F32),
        compiler_params=_params(("arbitrary",), 12),
    )(c, w_ada, b_ada.reshape(1, n_out))


def _inproj_kernel(x_ref, sc_ref, sh_ref, w_ref, wvt_ref, pw_ref, ps_ref,
                   yp_ref, q_ref, k_ref, vt_ref, ext_ref, *, tm, seq):
    i = pl.program_id(0)
    tiles_per_seq = seq // tm
    it = i % tiles_per_seq
    h = x_ref[...] * (1.0 + sc_ref[0]) + sh_ref[0]
    hb = h.astype(BF16)
    proj = jnp.dot(hb, w_ref[...], preferred_element_type=F32)
    u = proj[:, :POOL_DIM]
    q_ref[...] = (proj[:, POOL_DIM:POOL_DIM + QK_DIM] * (ATTN_HEAD_DIM ** -0.5 * LOG2_E)).astype(BF16)
    k_ref[...] = proj[:, POOL_DIM + QK_DIM:POOL_DIM + 2 * QK_DIM].astype(BF16)
    vt = lax.dot_general(wvt_ref[...], hb, (((1,), (1,)), ((), ())), preferred_element_type=F32)
    for j in range(tm // ATT_TILE):
        vt_ref[0, j] = vt[:, j * ATT_TILE:(j + 1) * ATT_TILE].astype(BF16)

    @pl.when(it == 0)
    def _():
        ext_ref[0:MAX_WINDOW, :] = jnp.zeros((MAX_WINDOW, POOL_DIM), F32)

    ext_ref[MAX_WINDOW:MAX_WINDOW + tm, :] = u
    pos = (it * tm + lax.broadcasted_iota(jnp.int32, (tm, 1), 0) + 1).astype(F32)
    for g, w in enumerate(POOL_WINDOWS):
        c0, c1 = g * POOL_GROUP_DIM, (g + 1) * POOL_GROUP_DIM
        s = ext_ref[MAX_WINDOW:MAX_WINDOW + tm, c0:c1]
        for j in range(1, w):
            s = s + ext_ref[MAX_WINDOW - j:MAX_WINDOW - j + tm, c0:c1]
        pooled = s / jnp.minimum(pos, float(w)) - u[:, c0:c1]
        y = jnp.dot(pooled.astype(BF16), pw_ref[g], preferred_element_type=F32)
        yp_ref[:, c0:c1] = (y * ps_ref[:, c0:c1]).astype(BF16)
    ext_ref[0:MAX_WINDOW, :] = ext_ref[tm:tm + MAX_WINDOW, :]


def _inproj(x2, scale1, shift1, w_main, w_vt, pool_w, pool_scale, *, seq, tm):
    n, d = x2.shape
    assert n % tm == 0 and seq % tm == 0 and tm >= 2 * MAX_WINDOW and tm % ATT_TILE == 0
    tps = seq // tm
    tpt = tm // ATT_TILE
    mod_spec = pl.BlockSpec((1, 1, d), lambda i: (i // tps, 0, 0))
    row = lambda w: pl.BlockSpec((tm, w), lambda i: (i, 0))
    full = lambda a: pl.BlockSpec(a.shape, lambda i: (0,) * a.ndim)
    return pl.pallas_call(
        functools.partial(_inproj_kernel, tm=tm, seq=seq),
        grid=(n // tm,),
        in_specs=[row(d), mod_spec, mod_spec, full(w_main), full(w_vt), full(pool_w), full(pool_scale)],
        out_specs=[row(POOL_DIM), row(QK_DIM), row(QK_DIM),
                   pl.BlockSpec((1, tpt, V_DIM, ATT_TILE), lambda i: (i // tps, i % tps, 0, 0))],
        out_shape=[jax.ShapeDtypeStruct((n, POOL_DIM), BF16),
                   jax.ShapeDtypeStruct((n, QK_DIM), BF16),
                   jax.ShapeDtypeStruct((n, QK_DIM), BF16),
                   jax.ShapeDtypeStruct((n // seq, seq // ATT_TILE, V_DIM, ATT_TILE), BF16)],
        scratch_shapes=[pltpu.VMEM((tm + MAX_WINDOW, POOL_DIM), F32)],
        compiler_params=_params(("arbitrary",), 28),
    )(x2, scale1, shift1, w_main, w_vt, pool_w, pool_scale)


def _bias_kernel(tab_ref, o_ref):
    delta = pl.program_id(0)
    r = lax.broadcasted_iota(jnp.int32, (ATT_TILE, ATT_TILE), 0)
    c = lax.broadcasted_iota(jnp.int32, (ATT_TILE, ATT_TILE), 1)
    rel = r - c - delta * ATT_TILE
    half = NUM_BUCKETS // 2
    max_exact = half // 2
    ret = jnp.where(rel > 0, half, 0)
    n = jnp.abs(rel)
    nf = jnp.maximum(n, 1).astype(F32)
    large = max_exact + (jnp.log(nf / max_exact) / math.log(MAX_DISTANCE / max_exact)
                         * (half - max_exact)).astype(jnp.int32)
    large = jnp.minimum(large, half - 1)
    bucket = ret + jnp.where(n < max_exact, n, large)
    for h in range(ATTN_HEADS):
        acc = jnp.zeros((ATT_TILE, ATT_TILE), F32)
        for b in range(NUM_BUCKETS):
            acc = jnp.where(bucket == b, tab_ref[b, h], acc)
        o_ref[h, 0] = acc * LOG2_E


def _bias_tiles(rel_table, n_tiles):
    return pl.pallas_call(
        _bias_kernel,
        grid=(n_tiles,),
        in_specs=[pl.BlockSpec(memory_space=pltpu.SMEM)],
        out_specs=pl.BlockSpec((ATTN_HEADS, 1, ATT_TILE, ATT_TILE), lambda dlt: (0, dlt, 0, 0)),
        out_shape=jax.ShapeDtypeStruct((ATTN_HEADS, n_tiles, ATT_TILE, ATT_TILE), F32),
        compiler_params=_params(("arbitrary",), 4),
    )(rel_table)


def _attn_kernel(q_ref, k_ref, vt_ref, bias_ref, lq1_ref, lk1_ref, lq2_ref, lk2_ref, g_ref, o_ref, *acc_refs):
    qt = pl.program_id(1)
    t = ATT_TILE
    n_maps = 2 * ATTN_HEADS
    lam = (jnp.exp(jnp.sum(lq1_ref[...] * lk1_ref[...], axis=-1, keepdims=True))
           - jnp.exp(jnp.sum(lq2_ref[...] * lk2_ref[...], axis=-1, keepdims=True))
           + LAMBDA_INIT)
    r = lax.broadcasted_iota(jnp.int32, (t, t), 0)
    c = lax.broadcasted_iota(jnp.int32, (t, t), 1)
    allowed = (r // CHUNK) <= (c // CHUNK)
    hd2 = 2 * ATTN_HEAD_DIM

    ahead = 2

    def scores(kt, hm):
        col = hm * ATTN_HEAD_DIM
        qh = q_ref[:, col:col + ATTN_HEAD_DIM]
        kh = k_ref[pl.ds(pl.multiple_of(kt * t, t), t), col:col + ATTN_HEAD_DIM]
        return lax.dot_general(kh, qh, (((1,), (1,)), ((), ())),
                               preferred_element_type=F32) + bias_ref[hm // 2, qt - kt]

    def block(kt, carry, diagonal):
        stats, early = carry[:2 * n_maps], carry[2 * n_maps:]

        def softmax(hm, s):
            if diagonal:
                s = jnp.where(allowed, s, -jnp.inf)
            m_old, l_old = stats[2 * hm:2 * hm + 2]
            m_new = jnp.maximum(m_old, jnp.max(s, axis=0, keepdims=True))
            alpha = jnp.exp2(m_old - m_new)
            p = jnp.exp2(s - m_new)
            return m_new, alpha * l_old + jnp.sum(p, axis=0, keepdims=True), alpha, p.astype(BF16)

        def accumulate(hm, alpha, p):
            h = hm // 2
            vth = vt_ref[0, kt, h * hd2:(h + 1) * hd2, :]
            acc_refs[hm][...] = alpha * acc_refs[hm][...] + jnp.dot(vth, p, preferred_element_type=F32)

        s_vals = dict(enumerate(early))
        sm_vals, out, nxt = {}, [None] * (2 * n_maps), []
        for step in range(1, n_maps + ahead):
            if ahead <= step < n_maps:
                s_vals[step] = scores(kt, step)
            elif step >= n_maps and not diagonal:
                nxt.append(scores(kt + 1, step - n_maps))
            hm = step - 1
            if hm < n_maps:
                m_new, l_new, alpha, p = softmax(hm, s_vals.pop(hm))
                out[2 * hm], out[2 * hm + 1] = m_new, l_new
                sm_vals[hm] = (alpha, p)
            if step - 2 >= 0:
                accumulate(step - 2, *sm_vals.pop(step - 2))
        return tuple(out) + tuple(nxt)

    for acc in acc_refs:
        acc[...] = jnp.zeros_like(acc)
    one = (jnp.full((1, t), -jnp.inf, F32), jnp.zeros((1, t), F32))
    first = tuple(scores(0, hm) for hm in range(ahead))
    carry = lax.fori_loop(0, qt, lambda kt, cr: block(kt, cr, False), one * n_maps + first)
    carry = block(qt, carry, True)
    for h in range(ATTN_HEADS):
        l0, l1 = carry[4 * h + 1], carry[4 * h + 3]
        o = acc_refs[2 * h][...] / l0 - lam * (acc_refs[2 * h + 1][...] / l1)
        y = o * lax.rsqrt(jnp.mean(o * o, axis=0, keepdims=True) + LN_EPS) * g_ref[...]
        o_ref[:, h * hd2:(h + 1) * hd2] = (y * (1.0 - LAMBDA_INIT)).T.astype(BF16)


def _attention(q, k, vt, bias_tiles, lq1, lk1, lq2, lk2, subln_g, *, bsz, seq):
    t = ATT_TILE
    nt = seq // t
    full = lambda a: pl.BlockSpec(a.shape, lambda b, j: (0,) * a.ndim)
    return pl.pallas_call(
        _attn_kernel,
        grid=(bsz, nt),
        in_specs=[pl.BlockSpec((t, QK_DIM), lambda b, j: (b * nt + j, 0)),
                  pl.BlockSpec((seq, QK_DIM), lambda b, j: (b, 0)),
                  pl.BlockSpec((1, nt, V_DIM, t), lambda b, j: (b, 0, 0, 0)),
                  full(bias_tiles), full(lq1), full(lk1), full(lq2), full(lk2), full(subln_g)],
        out_specs=pl.BlockSpec((t, V_DIM), lambda b, j: (b * nt + j, 0)),
        out_shape=jax.ShapeDtypeStruct((bsz * seq, V_DIM), BF16),
        scratch_shapes=[pltpu.VMEM((2 * ATTN_HEAD_DIM, t), F32) for _ in range(2 * ATTN_HEADS)],
        compiler_params=_params(("arbitrary", "arbitrary"), 26),
    )(q, k, vt, bias_tiles, lq1, lk1, lq2, lk2, subln_g)


def _outproj_kernel(x_ref, yp_ref, ya_ref, w_ref, gate_ref, g_ref, b_ref, o_ref):
    mix = (jnp.dot(yp_ref[...], w_ref[0:POOL_DIM, :], preferred_element_type=F32)
           + jnp.dot(ya_ref[...], w_ref[POOL_DIM:, :], preferred_element_type=F32))
    z = ALPHA * x_ref[...] + gate_ref[0] * mix
    o_ref[...] = _layer_norm(z, g_ref[...], b_ref[...])


def _outproj(x2, yp, ya, w_out, gate1, ln_g, ln_b, *, seq, tm):
    n, d = x2.shape
    tps = seq // tm
    row = lambda w: pl.BlockSpec((tm, w), lambda i: (i, 0))
    full = lambda a: pl.BlockSpec(a.shape, lambda i: (0,) * a.ndim)
    return pl.pallas_call(
        _outproj_kernel,
        grid=(n // tm,),
        in_specs=[row(d), row(POOL_DIM), row(V_DIM), full(w_out),
                  pl.BlockSpec((1, 1, d), lambda i: (i // tps, 0, 0)), full(ln_g), full(ln_b)],
        out_specs=row(d),
        out_shape=jax.ShapeDtypeStruct((n, d), F32),
        compiler_params=_params(("arbitrary",), 32),
    )(x2, yp, ya, w_out, gate1, ln_g, ln_b)


def _route_kernel(x_ref, sc_ref, sh_ref, whi_ref, wlo_ref, rb_ref,
                  eidx_ref, rank_ref, wts_ref, cnt_ref, h2p_ref, carry_ref, *, tr):
    i = pl.program_id(0)

    @pl.when(i == 0)
    def _():
        carry_ref[...] = jnp.zeros_like(carry_ref)

    h2 = x_ref[...] * (1.0 + sc_ref[0]) + sh_ref[0]
    h2p_ref[...] = _pack_bf16_halves(h2)
    hi = h2.astype(BF16)
    lo = (h2 - hi.astype(F32)).astype(BF16)
    nt = (((1,), (1,)), ((), ()))
    logits = (lax.dot_general(whi_ref[...], hi, nt, preferred_element_type=F32)
              + lax.dot_general(wlo_ref[...], hi, nt, preferred_element_type=F32)
              + lax.dot_general(whi_ref[...], lo, nt, preferred_element_type=F32))
    scores = _sigmoid(logits)
    sel = scores + rb_ref[...]
    erow = lax.broadcasted_iota(jnp.int32, (N_EXPERTS, tr), 0).astype(F32)

    g3 = sel.reshape(N_GROUPS, GROUP_SIZE, tr)
    r3 = lax.broadcasted_iota(jnp.int32, (N_GROUPS, GROUP_SIZE, tr), 1).astype(F32)
    m1 = jnp.max(g3, axis=1, keepdims=True)
    first = jnp.min(jnp.where(g3 == m1, r3, float(GROUP_SIZE)), axis=1, keepdims=True)
    m2 = jnp.max(jnp.where(r3 == first, -jnp.inf, g3), axis=1, keepdims=True)
    gscore = (m1 + m2).reshape(N_GROUPS, tr)

    gidx = lax.broadcasted_iota(jnp.int32, (N_GROUPS, tr), 0)
    beaten_by = jnp.zeros((N_GROUPS, tr), jnp.int32)
    for g in range(N_GROUPS):
        other = gscore[g:g + 1, :]
        wins = (other > gscore) | ((other == gscore) & (g < gidx))
        beaten_by = beaten_by + wins.astype(jnp.int32)
    dropped = jnp.where(beaten_by < TOP_K_GROUPS, 0.0, -jnp.inf)
    cur = (g3 + dropped.reshape(N_GROUPS, 1, tr)).reshape(N_EXPERTS, tr)

    picks, weights = [], []
    selmask = jnp.zeros((N_EXPERTS, tr), F32)
    for _ in range(TOP_K):
        mx = jnp.max(cur, axis=0, keepdims=True)
        pick = jnp.min(jnp.where(cur == mx, erow, float(N_EXPERTS)), axis=0, keepdims=True)
        onehot = erow == pick
        weights.append(jnp.sum(jnp.where(onehot, scores, 0.0), axis=0, keepdims=True))
        cur = jnp.where(onehot, -jnp.inf, cur)
        selmask = jnp.where(onehot, 1.0, selmask)
        picks.append(pick)

    t_from = lax.broadcasted_iota(jnp.int32, (tr, tr), 0)
    t_to = lax.broadcasted_iota(jnp.int32, (tr, tr), 1)
    earlier = jnp.where(t_from < t_to, 1.0, 0.0).astype(BF16)
    chosen = selmask.astype(BF16)
    carry = carry_ref[...]
    rankmat = (jnp.dot(chosen, earlier, preferred_element_type=F32)
               + jnp.concatenate([carry] * (tr // LANES), axis=1))
    carry_ref[...] = carry + jnp.dot(chosen, jnp.ones((tr, LANES), BF16), preferred_element_type=F32)
    cnt_ref[...] = carry_ref[...]

    wsum = weights[0]
    for wj in weights[1:]:
        wsum = wsum + wj
    row8 = lax.broadcasted_iota(jnp.int32, (TOP_K, tr), 0)
    eidx = jnp.zeros((TOP_K, tr), jnp.int32)
    rank = jnp.zeros((TOP_K, tr), jnp.int32)
    wts = jnp.zeros((TOP_K, tr), F32)
    for j in range(TOP_K):
        rk = jnp.sum(jnp.where(erow == picks[j], rankmat, 0.0), axis=0, keepdims=True)
        eidx = jnp.where(row8 == j, picks[j].astype(jnp.int32), eidx)
        rank = jnp.where(row8 == j, rk.astype(jnp.int32), rank)
        wts = jnp.where(row8 == j, weights[j] / wsum * ROUTED_SCALE, wts)
    eidx_ref[...] = eidx
    rank_ref[...] = rank
    wts_ref[...] = wts


def _route(x1, scale2, shift2, wr_hi, wr_lo, router_bias, *, seq, tr):
    n, d = x1.shape
    tps = seq // tr
    mod_spec = pl.BlockSpec((1, 1, d), lambda i: (i // tps, 0, 0))
    full = lambda a: pl.BlockSpec(a.shape, lambda i: (0,) * a.ndim)
    assert tr % LANES == 0
    k8 = pl.BlockSpec((TOP_K, tr), lambda i: (0, i))
    return pl.pallas_call(
        functools.partial(_route_kernel, tr=tr),
        grid=(n // tr,),
        in_specs=[pl.BlockSpec((tr, d), lambda i: (i, 0)), mod_spec, mod_spec,
                  full(wr_hi), full(wr_lo), full(router_bias)],
        out_specs=[k8, k8, k8, pl.BlockSpec((N_EXPERTS, LANES), lambda i: (0, 0)),
                   pl.BlockSpec((tr, d // 2), lambda i: (i, 0))],
        out_shape=[jax.ShapeDtypeStruct((TOP_K, n), jnp.int32),
                   jax.ShapeDtypeStruct((TOP_K, n), jnp.int32),
                   jax.ShapeDtypeStruct((TOP_K, n), F32),
                   jax.ShapeDtypeStruct((N_EXPERTS, LANES), F32),
                   jax.ShapeDtypeStruct((n, d // 2), U32)],
        scratch_shapes=[pltpu.VMEM((N_EXPERTS, LANES), F32)],
        compiler_params=_params(("arbitrary",), 8),
    )(x1, scale2, shift2, wr_hi, wr_lo, router_bias)


def _pack_bf16_halves(x):
    w = x.shape[1] // 2
    lo = pltpu.bitcast(x[:, :w].astype(BF16).astype(F32), U32) >> 16
    hi = pltpu.bitcast(x[:, w:].astype(BF16).astype(F32), U32) & jnp.uint32(0xFFFF0000)
    return lo | hi


def _unpack_bf16_halves(p):
    return pltpu.bitcast(p << 16, F32), pltpu.bitcast(p & jnp.uint32(0xFFFF0000), F32)


def _sc_worker():
    return lax.axis_index("subcore") * SC_CORES + lax.axis_index("core")


def _scatter_rows_sc(rows, idx, n_slots):
    n, width = rows.shape
    k = idx.shape[0] // n
    n_workers = SC_CORES * SC_SUBCORES
    w = SC_ROWS
    per_worker = n // n_workers
    n_chunks = per_worker // w
    assert n == n_workers * n_chunks * w and n_chunks % 2 == 0 and idx.shape[0] == k * n
    mesh = plsc.VectorSubcoreMesh(core_axis_name="core", subcore_axis_name="subcore")

    @functools.partial(
        pl.kernel, mesh=mesh,
        out_type=jax.ShapeDtypeStruct((n_slots, width), rows.dtype),
        scratch_types=[pltpu.VMEM((k * per_worker,), jnp.int32), pltpu.VMEM((2, w, width), rows.dtype),
                       pltpu.SemaphoreType.DMA((2,)), pltpu.SemaphoreType.DMA((2,))])
    def scatter(rows_hbm, idx_hbm, out_hbm, idx_v, rows_v, in_sem, out_sem):
        t0 = _sc_worker() * per_worker
        for j in range(k):
            pltpu.sync_copy(idx_hbm.at[pl.ds(j * n + t0, per_worker)], idx_v.at[pl.ds(j * per_worker, per_worker)])

        def load(i, s):
            return pltpu.make_async_copy(rows_hbm.at[pl.ds(t0 + i * w, w)], rows_v.at[s], in_sem.at[s])

        def send(i, s, j):
            slots = idx_v.at[pl.ds(j * per_worker + i * w, w)]
            return pltpu.make_async_copy(rows_v.at[s], out_hbm.at[slots], out_sem.at[s])

        load(0, 0).start()

        def pair(ii, carry):
            for s in range(2):
                i = ii * 2 + s
                load(i, s).wait()

                @pl.when(i >= 1)
                def _():
                    for j in range(k):
                        send(i - 1, 1 - s, j).wait()

                @pl.when(i + 1 < n_chunks)
                def _():
                    load(i + 1, 1 - s).start()

                for j in range(k):
                    send(i, s, j).start()
            return carry

        lax.fori_loop(0, n_chunks // 2, pair, 0)
        for j in range(k):
            send(n_chunks - 1, 1, j).wait()

    return scatter(rows, idx)


def _dest_kernel(eidx_ref, rank_ref, first_ref, dest_ref):
    tt = eidx_ref.shape[1]
    pieces = [jnp.broadcast_to(first_ref[:, p * LANES:(p + 1) * LANES], (TOP_K, LANES))
              for p in range(N_EXPERTS // LANES)]
    for c in range(tt // LANES):
        cols = slice(c * LANES, (c + 1) * LANES)
        e = eidx_ref[:, cols]
        within = e & (LANES - 1)
        start = jnp.take_along_axis(pieces[0], within, axis=1)
        for p in range(1, len(pieces)):
            start = jnp.where(e // LANES == p, jnp.take_along_axis(pieces[p], within, axis=1), start)
        dest_ref[:, cols] = start + rank_ref[:, cols]


def _dest(eidx_t, rank_t, pstarts, *, tt):
    n = eidx_t.shape[1]
    first = pstarts.reshape(1, N_EXPERTS)
    k8 = pl.BlockSpec((TOP_K, tt), lambda i: (0, i))
    return pl.pallas_call(
        _dest_kernel,
        grid=(n // tt,),
        in_specs=[k8, k8, pl.BlockSpec(first.shape, lambda i: (0, 0))],
        out_specs=k8,
        out_shape=jax.ShapeDtypeStruct((TOP_K, n), jnp.int32),
        compiler_params=_params(("arbitrary",), 2),
    )(eidx_t, rank_t, first)


def _expert_kernel(fb_ref, nb_ref, cnt_ref, nused_ref, wg_ref, wu_ref, wd_ref, xs_ref, ys_ref,
                   wgb, wub, wdb, xbuf, ybuf, in_sem, out_sem):
    e = pl.program_id(0)
    m = DISPATCH_BLOCK
    ns = EXPERT_SLOTS
    n_used = nused_ref[0]

    def rows(g):
        return pl.ds(pl.multiple_of(g * m, m), m)

    def fetch(g):
        slot = g & (ns - 1)
        return pltpu.make_async_copy(xs_ref.at[rows(g), :], xbuf.at[slot], in_sem.at[slot])

    def put(g):
        slot = g & (ns - 1)
        return pltpu.make_async_copy(ybuf.at[slot], ys_ref.at[rows(g), :], out_sem.at[slot])

    @pl.when(e == 0)
    def _():
        for g0 in range(ns - 1):
            @pl.when(g0 < n_used)
            def _(g0=g0):
                fetch(g0).start()

    wgb[...] = wg_ref[0].astype(BF16)
    wub[...] = wu_ref[0].astype(BF16)
    wdb[...] = wd_ref[0].astype(BF16)

    def block(i, carry):
        g = fb_ref[e] + i
        slot = g & (ns - 1)
        fetch(g).wait()

        @pl.when(g + ns - 1 < n_used)
        def _():
            fetch(g + ns - 1).start()

        @pl.when(g >= ns)
        def _():
            put(g - ns).wait()

        n_valid = cnt_ref[e] - i * m

        def swiglu(rows):
            row = lax.broadcasted_iota(jnp.int32, (rows, 1), 0)
            packed = jnp.where(row < n_valid, xbuf[slot, 0:rows], jnp.uint32(0))
            x_lo, x_hi = [h.astype(BF16) for h in _unpack_bf16_halves(packed)]
            half = x_lo.shape[1]

            def up_proj(w):
                return (jnp.dot(x_lo, w[0:half, :], preferred_element_type=F32)
                        + jnp.dot(x_hi, w[half:, :], preferred_element_type=F32))

            a = (_silu(up_proj(wgb)) * up_proj(wub)).astype(BF16)
            ybuf[slot, 0:rows] = _pack_bf16_halves(jnp.dot(a, wdb[...], preferred_element_type=F32))

        pl.when(n_valid > m // 2)(functools.partial(swiglu, m))
        pl.when(n_valid <= m // 2)(functools.partial(swiglu, m // 2))
        put(g).start()
        return carry

    lax.fori_loop(0, nb_ref[e], block, 0)

    @pl.when(e == pl.num_programs(0) - 1)
    def _():
        for back in range(ns, 0, -1):
            @pl.when(n_used >= back)
            def _(back=back):
                put(n_used - back).wait()

        ybuf[0] = jnp.zeros(ybuf.shape[1:], ybuf.dtype)
        n_blocks = ys_ref.shape[0] // m

        def tail(wait):
            def body(g, carry):
                cp = pltpu.make_async_copy(ybuf.at[0], ys_ref.at[rows(g), :], out_sem.at[0])
                cp.wait() if wait else cp.start()
                return carry
            return body

        lax.fori_loop(n_used, n_blocks, tail(False), 0)
        lax.fori_loop(n_used, n_blocks, tail(True), 0)


def _experts(first_block, n_blocks_e, counts, n_used, xs, w_gate, w_up, w_down):
    p, dp = xs.shape
    m = DISPATCH_BLOCK
    n_e, d, f = w_gate.shape
    assert d == 2 * dp and p % m == 0
    grid_spec = pltpu.PrefetchScalarGridSpec(
        num_scalar_prefetch=4,
        grid=(n_e,),
        in_specs=[pl.BlockSpec((1, d, f), lambda e, *_: (e, 0, 0)),
                  pl.BlockSpec((1, d, f), lambda e, *_: (e, 0, 0)),
                  pl.BlockSpec((1, f, d), lambda e, *_: (e, 0, 0)),
                  pl.BlockSpec(memory_space=pl.ANY)],
        out_specs=pl.BlockSpec(memory_space=pl.ANY),
        scratch_shapes=[pltpu.VMEM((d, f), BF16), pltpu.VMEM((d, f), BF16), pltpu.VMEM((f, d), BF16),
                        pltpu.VMEM((EXPERT_SLOTS, m, dp), U32), pltpu.VMEM((EXPERT_SLOTS, m, dp), U32),
                        pltpu.SemaphoreType.DMA((EXPERT_SLOTS,)), pltpu.SemaphoreType.DMA((EXPERT_SLOTS,))],
    )
    return pl.pallas_call(
        _expert_kernel,
        grid_spec=grid_spec,
        out_shape=jax.ShapeDtypeStruct((p, dp), U32),
        compiler_params=_params(("arbitrary",), 22),
    )(first_block, n_blocks_e, counts, n_used, w_gate, w_up, w_down, xs)


def _gather_rows_sc(table, idx):
    n_idx, (_, width) = idx.shape[0], table.shape
    n_workers = SC_CORES * SC_SUBCORES
    w = SC_ROWS
    per_worker = n_idx // n_workers
    n_chunks = per_worker // w
    assert n_idx == n_workers * n_chunks * w and n_chunks % 2 == 0
    mesh = plsc.VectorSubcoreMesh(core_axis_name="core", subcore_axis_name="subcore")

    @functools.partial(
        pl.kernel, mesh=mesh,
        out_type=jax.ShapeDtypeStruct((n_idx, width), table.dtype),
        scratch_types=[pltpu.VMEM((per_worker,), jnp.int32), pltpu.VMEM((2, w, width), table.dtype),
                       pltpu.SemaphoreType.DMA((2,)), pltpu.SemaphoreType.DMA((2,))])
    def gather(table_hbm, idx_hbm, out_hbm, idx_v, rows_v, in_sem, out_sem):
        base = _sc_worker() * per_worker
        pltpu.sync_copy(idx_hbm.at[pl.ds(base, per_worker)], idx_v)

        def fetch(i, s):
            return pltpu.make_async_copy(table_hbm.at[idx_v.at[pl.ds(i * w, w)]], rows_v.at[s], in_sem.at[s])

        def put(i, s):
            return pltpu.make_async_copy(rows_v.at[s], out_hbm.at[pl.ds(base + i * w, w)], out_sem.at[s])

        fetch(0, 0).start()

        def pair(ii, carry):
            for s in range(2):
                i = ii * 2 + s
                fetch(i, s).wait()

                @pl.when(i >= 1)
                def _():
                    put(i - 1, 1 - s).wait()

                @pl.when(i + 1 < n_chunks)
                def _():
                    fetch(i + 1, 1 - s).start()

                put(i, s).start()
            return carry

        lax.fori_loop(0, n_chunks // 2, pair, 0)
        put(n_chunks - 1, 1).wait()

    return gather(table, idx)


def _combine_kernel(x_ref, sc_ref, sh_ref, gate_ref, wts_ref, rows_ref,
                    wsg_ref, wsu_ref, wsd_ref, g_ref, b_ref, *out_refs):
    o_ref = out_refs[-1]
    x = x_ref[...]
    hb = (x * (1.0 + sc_ref[0]) + sh_ref[0]).astype(BF16)
    sg = jnp.dot(hb, wsg_ref[...], preferred_element_type=F32)
    su = jnp.dot(hb, wsu_ref[...], preferred_element_type=F32)
    shared = jnp.dot((_silu(sg) * su).astype(BF16), wsd_ref[...], preferred_element_type=F32)

    wts = wts_ref[...]
    half = shared.shape[1] // 2
    lo, hi = shared[:, :half], shared[:, half:]
    for j in range(TOP_K):
        y_lo, y_hi = _unpack_bf16_halves(rows_ref[j])
        lo = lo + wts[:, j:j + 1] * y_lo
        hi = hi + wts[:, j:j + 1] * y_hi
    z = ALPHA * x + gate_ref[0] * jnp.concatenate([lo, hi], axis=1)
    o_ref[...] = _layer_norm(z, g_ref[...], b_ref[...])


def _combine(x1, scale2, shift2, gate2, wts, rows, ws_gate, ws_up, ws_down, ln_g, ln_b, prev_out,
             *, seq, tc, first_token):
    n, d = x1.shape
    tps = seq // tc
    off = first_token // tc
    n_tiles = rows.shape[1] // tc
    mod_spec = pl.BlockSpec((1, 1, d), lambda i: ((i + off) // tps, 0, 0))
    full = lambda a: pl.BlockSpec(a.shape, lambda i: (0,) * a.ndim)
    args = [x1, scale2, shift2, gate2, wts, rows, ws_gate, ws_up, ws_down, ln_g, ln_b]
    in_specs = [pl.BlockSpec((tc, d), lambda i: (i + off, 0)), mod_spec, mod_spec, mod_spec,
                pl.BlockSpec((tc, TOP_K), lambda i: (i + off, 0)),
                pl.BlockSpec((TOP_K, tc, d // 2), lambda i: (0, i, 0)),
                full(ws_gate), full(ws_up), full(ws_down), full(ln_g), full(ln_b)]
    aliases = {}
    if prev_out is not None:
        aliases = {len(args): 0}
        args.append(prev_out)
        in_specs.append(pl.BlockSpec(memory_space=pl.ANY))
    return pl.pallas_call(
        _combine_kernel,
        grid=(n_tiles,),
        in_specs=in_specs,
        out_specs=pl.BlockSpec((tc, d), lambda i: (i + off, 0)),
        out_shape=jax.ShapeDtypeStruct((n, d), F32),
        input_output_aliases=aliases,
        compiler_params=_params(("arbitrary",), 36),
    )(*args)


def _layer(x, c, w_ada, b_ada, w_in, pool_w, pool_scale, lq1, lk1, lq2, lk2, subln_g, w_out,
           ln1_g, ln1_b, w_router, router_bias, w_gate, w_up, w_down, ws_gate, ws_up, ws_down,
           ln2_g, ln2_b, rel_table, *, tm=1024, tr=256, tc=512):
    bsz, seq, d = x.shape
    n = bsz * seq
    x2 = x.reshape(n, d)
    row = lambda a: a.reshape(1, -1)

    mod = _modulation(c, w_ada, b_ada)
    shift1, scale1, gate1, shift2, scale2, gate2 = [
        mod[:, j * d:(j + 1) * d].reshape(bsz, 1, d) for j in range(6)]

    n_main = POOL_DIM + 2 * QK_DIM
    yp, q, k, vt = _inproj(x2, scale1, shift1, w_in[:, :n_main].astype(BF16), w_in[:, n_main:].T.astype(BF16),
                           pool_w.astype(BF16), row(pool_scale), seq=seq, tm=tm)
    bias_tiles = _bias_tiles(rel_table, seq // ATT_TILE)
    ya = _attention(q, k, vt, bias_tiles, row(lq1), row(lk1), row(lq2), row(lk2), subln_g.reshape(-1, 1),
                    bsz=bsz, seq=seq)
    x1 = _outproj(x2, yp, ya, w_out.astype(BF16), gate1, row(ln1_g), row(ln1_b), seq=seq, tm=tm)

    wr_t = w_router.T
    wr_hi = wr_t.astype(BF16)
    wr_lo = (wr_t - wr_hi.astype(F32)).astype(BF16)
    eidx_t, rank_t, wts_t, cnt, h2p = _route(x1, scale2, shift2, wr_hi, wr_lo, router_bias.reshape(-1, 1),
                                             seq=seq, tr=tr)

    m = DISPATCH_BLOCK
    counts = cnt[:, 0].astype(jnp.int32)
    padded = (counts + m - 1) // m * m
    pends = jnp.cumsum(padded)
    pstarts = pends - padded
    n_blocks = -(-(n * TOP_K + N_EXPERTS * (m - 1)) // m)
    n_used = (pends[-1:] // m).astype(jnp.int32)
    dest_t = _dest(eidx_t, rank_t, pstarts, tt=min(n, 2048))

    xs = _scatter_rows_sc(h2p, dest_t.reshape(TOP_K * n), n_blocks * m)
    ys = _experts(pstarts // m, padded // m, counts, n_used, xs, w_gate, w_up, w_down)

    shared_w = (ws_gate.astype(BF16), ws_up.astype(BF16), ws_down.astype(BF16))
    wts = wts_t.T
    part = n // COMBINE_PARTS
    assert part % tc == 0 and part * COMBINE_PARTS == n
    out = None
    for p in range(COMBINE_PARTS):
        idx = dest_t[:, p * part:(p + 1) * part].reshape(TOP_K * part)
        picked = _gather_rows_sc(ys, idx).reshape(TOP_K, part, d // 2)
        out = _combine(x1, scale2, shift2, gate2, wts, picked, *shared_w, row(ln2_g), row(ln2_b), out,
                       seq=seq, tc=tc, first_token=p * part)
    return out.reshape(bsz, seq, d)


def kernel(x, c, w_ada, b_ada, w_in, pool_w, pool_scale, lambda_q1, lambda_k1, lambda_q2, lambda_k2,
           subln_g, w_out, ln1_g, ln1_b, w_router, router_bias, w_gate, w_up, w_down,
           ws_gate, ws_up, ws_down, ln2_g, ln2_b, rel_table):
    per_layer = (w_ada, b_ada, w_in, pool_w, pool_scale, lambda_q1, lambda_k1, lambda_q2, lambda_k2,
                 subln_g, w_out, ln1_g, ln1_b, w_router, router_bias, w_gate, w_up, w_down,
                 ws_gate, ws_up, ws_down, ln2_g, ln2_b)
    assert all(a.shape[0] == DEPTH == 1 for a in per_layer)
    return _layer(x, c, *[a.reshape(a.shape[1:]) for a in per_layer], rel_table)
```

```python
import functools
import math

import jax
import jax.numpy as jnp
from jax import lax
from jax.experimental import pallas as pl
from jax.experimental.pallas import tpu as pltpu
from jax.experimental.pallas import tpu_sc as plsc

F32 = jnp.float32
BF16 = jnp.bfloat16
U32 = jnp.uint32
LANES = 128

CHUNK = 64
ATT_TILE = 256
LOG2_E = math.log2(math.e)
POOL_DIM = 512
POOL_WINDOWS = (2, 4, 8, 16)
POOL_GROUP_DIM = 128
MAX_WINDOW = max(POOL_WINDOWS)
ATTN_HEADS = 4
ATTN_HEAD_DIM = 64
QK_DIM = 512
V_DIM = 512
NUM_BUCKETS = 32
MAX_DISTANCE = 128
N_EXPERTS = 256
TOP_K = 8
N_GROUPS = 8
GROUP_SIZE = N_EXPERTS // N_GROUPS
TOP_K_GROUPS = 4
ROUTED_SCALE = 2.5
DISPATCH_BLOCK = 512
EXPERT_SLOTS = 4
COMBINE_PARTS = 8
SC_CORES, SC_SUBCORES = 2, 16
SC_ROWS = 64
DEPTH = 1
ALPHA = (2.0 * DEPTH) ** 0.25
LN_EPS = 1e-5
LAMBDA_INIT = 0.8 - 0.6 * math.exp(-0.3 * 0)


def _sigmoid(x):
    return 1.0 / (1.0 + jnp.exp(-x))


def _silu(x):
    return x * _sigmoid(x)


def _layer_norm(z, g, b):
    mu = jnp.mean(z, axis=-1, keepdims=True)
    zc = z - mu
    var = jnp.mean(zc * zc, axis=-1, keepdims=True)
    return zc * lax.rsqrt(var + LN_EPS) * g + b


def _params(sem, vmem_mib):
    return pltpu.CompilerParams(dimension_semantics=sem, vmem_limit_bytes=vmem_mib * 1024 * 1024)


def _mod_kernel(c_ref, w_ref, b_ref, o_ref):
    ca = _silu(c_ref[...])
    o_ref[...] = jnp.dot(ca, w_ref[...], preferred_element_type=F32,
                         precision=lax.Precision.HIGHEST) + b_ref[...]


def _modulation(c, w_ada, b_ada):
    bsz, d = c.shape
    n_out = w_ada.shape[1]
    return pl.pallas_call(
        _mod_kernel,
        grid=(n_out // d,),
        in_specs=[pl.BlockSpec((bsz, d), lambda j: (0, 0)),
                  pl.BlockSpec((d, d), lambda j: (0, j)),
                  pl.BlockSpec((1, d), lambda j: (0, j))],
        out_specs=pl.BlockSpec((bsz, d), lambda j: (0, j)),
        out_shape=jax.ShapeDtypeStruct((bsz, n_out), F32),
        compiler_params=_params(("arbitrary",), 12),
    )(c, w_ada, b_ada.reshape(1, n_out))


def _inproj_kernel(x_ref, sc_ref, sh_ref, w_ref, wvt_ref, pw_ref, ps_ref,
                   yp_ref, q_ref, k_ref, vt_ref, ext_ref, *, tm, seq):
    i = pl.program_id(0)
    tiles_per_seq = seq // tm
    it = i % tiles_per_seq
    h = x_ref[...] * (1.0 + sc_ref[0]) + sh_ref[0]
    hb = h.astype(BF16)
    proj = jnp.dot(hb, w_ref[...], preferred_element_type=F32)
    u = proj[:, :POOL_DIM]
    q_ref[...] = (proj[:, POOL_DIM:POOL_DIM + QK_DIM] * (ATTN_HEAD_DIM ** -0.5 * LOG2_E)).astype(BF16)
    k_ref[...] = proj[:, POOL_DIM + QK_DIM:POOL_DIM + 2 * QK_DIM].astype(BF16)
    vt = lax.dot_general(wvt_ref[...], hb, (((1,), (1,)), ((), ())), preferred_element_type=F32)
    for j in range(tm // ATT_TILE):
        vt_ref[0, j] = vt[:, j * ATT_TILE:(j + 1) * ATT_TILE].astype(BF16)

    @pl.when(it == 0)
    def _():
        ext_ref[0:MAX_WINDOW, :] = jnp.zeros((MAX_WINDOW, POOL_DIM), F32)

    ext_ref[MAX_WINDOW:MAX_WINDOW + tm, :] = u
    pos = (it * tm + lax.broadcasted_iota(jnp.int32, (tm, 1), 0) + 1).astype(F32)
    for g, w in enumerate(POOL_WINDOWS):
        c0, c1 = g * POOL_GROUP_DIM, (g + 1) * POOL_GROUP_DIM
        s = ext_ref[MAX_WINDOW:MAX_WINDOW + tm, c0:c1]
        for j in range(1, w):
            s = s + ext_ref[MAX_WINDOW - j:MAX_WINDOW - j + tm, c0:c1]
        pooled = s / jnp.minimum(pos, float(w)) - u[:, c0:c1]
        y = jnp.dot(pooled.astype(BF16), pw_ref[g], preferred_element_type=F32)
        yp_ref[:, c0:c1] = (y * ps_ref[:, c0:c1]).astype(BF16)
    ext_ref[0:MAX_WINDOW, :] = ext_ref[tm:tm + MAX_WINDOW, :]


def _inproj(x2, scale1, shift1, w_main, w_vt, pool_w, pool_scale, *, seq, tm):
    n, d = x2.shape
    assert n % tm == 0 and seq % tm == 0 and tm >= 2 * MAX_WINDOW and tm % ATT_TILE == 0
    tps = seq // tm
    tpt = tm // ATT_TILE
    mod_spec = pl.BlockSpec((1, 1, d), lambda i: (i // tps, 0, 0))
    row = lambda w: pl.BlockSpec((tm, w), lambda i: (i, 0))
    full = lambda a: pl.BlockSpec(a.shape, lambda i: (0,) * a.ndim)
    return pl.pallas_call(
        functools.partial(_inproj_kernel, tm=tm, seq=seq),
        grid=(n // tm,),
        in_specs=[row(d), mod_spec, mod_spec, full(w_main), full(w_vt), full(pool_w), full(pool_scale)],
        out_specs=[row(POOL_DIM), row(QK_DIM), row(QK_DIM),
                   pl.BlockSpec((1, tpt, V_DIM, ATT_TILE), lambda i: (i // tps, i % tps, 0, 0))],
        out_shape=[jax.ShapeDtypeStruct((n, POOL_DIM), BF16),
                   jax.ShapeDtypeStruct((n, QK_DIM), BF16),
                   jax.ShapeDtypeStruct((n, QK_DIM), BF16),
                   jax.ShapeDtypeStruct((n // seq, seq // ATT_TILE, V_DIM, ATT_TILE), BF16)],
        scratch_shapes=[pltpu.VMEM((tm + MAX_WINDOW, POOL_DIM), F32)],
        compiler_params=_params(("arbitrary",), 28),
    )(x2, scale1, shift1, w_main, w_vt, pool_w, pool_scale)


def _bias_kernel(tab_ref, o_ref):
    delta = pl.program_id(0)
    r = lax.broadcasted_iota(jnp.int32, (ATT_TILE, ATT_TILE), 0)
    c = lax.broadcasted_iota(jnp.int32, (ATT_TILE, ATT_TILE), 1)
    rel = r - c - delta * ATT_TILE
    half = NUM_BUCKETS // 2
    max_exact = half // 2
    ret = jnp.where(rel > 0, half, 0)
    n = jnp.abs(rel)
    nf = jnp.maximum(n, 1).astype(F32)
    large = max_exact + (jnp.log(nf / max_exact) / math.log(MAX_DISTANCE / max_exact)
                         * (half - max_exact)).astype(jnp.int32)
    large = jnp.minimum(large, half - 1)
    bucket = ret + jnp.where(n < max_exact, n, large)
    for h in range(ATTN_HEADS):
        acc = jnp.zeros((ATT_TILE, ATT_TILE), F32)
        for b in range(NUM_BUCKETS):
            acc = jnp.where(bucket == b, tab_ref[b, h], acc)
        o_ref[h, 0] = acc * LOG2_E


def _bias_tiles(rel_table, n_tiles):
    return pl.pallas_call(
        _bias_kernel,
        grid=(n_tiles,),
        in_specs=[pl.BlockSpec(memory_space=pltpu.SMEM)],
        out_specs=pl.BlockSpec((ATTN_HEADS, 1, ATT_TILE, ATT_TILE), lambda dlt: (0, dlt, 0, 0)),
        out_shape=jax.ShapeDtypeStruct((ATTN_HEADS, n_tiles, ATT_TILE, ATT_TILE), F32),
        compiler_params=_params(("arbitrary",), 4),
    )(rel_table)


def _attn_kernel(q_ref, k_ref, vt_ref, bias_ref, lq1_ref, lk1_ref, lq2_ref, lk2_ref, g_ref, o_ref, *acc_refs):
    qt = pl.program_id(1)
    t = ATT_TILE
    n_maps = 2 * ATTN_HEADS
    lam = (jnp.exp(jnp.sum(lq1_ref[...] * lk1_ref[...], axis=-1, keepdims=True))
           - jnp.exp(jnp.sum(lq2_ref[...] * lk2_ref[...], axis=-1, keepdims=True))
           + LAMBDA_INIT)
    r = lax.broadcasted_iota(jnp.int32, (t, t), 0)
    c = lax.broadcasted_iota(jnp.int32, (t, t), 1)
    allowed = (r // CHUNK) <= (c // CHUNK)
    hd2 = 2 * ATTN_HEAD_DIM

    ahead = 2

    def scores(kt, hm):
        col = hm * ATTN_HEAD_DIM
        qh = q_ref[:, col:col + ATTN_HEAD_DIM]
        kh = k_ref[pl.ds(pl.multiple_of(kt * t, t), t), col:col + ATTN_HEAD_DIM]
        return lax.dot_general(kh, qh, (((1,), (1,)), ((), ())),
                               preferred_element_type=F32) + bias_ref[hm // 2, qt - kt]

    def block(kt, carry, diagonal):
        stats, early = carry[:2 * n_maps], carry[2 * n_maps:]

        def softmax(hm, s):
            if diagonal:
                s = jnp.where(allowed, s, -jnp.inf)
            m_old, l_old = stats[2 * hm:2 * hm + 2]
            m_new = jnp.maximum(m_old, jnp.max(s, axis=0, keepdims=True))
            alpha = jnp.exp2(m_old - m_new)
            p = jnp.exp2(s - m_new)
            return m_new, alpha * l_old + jnp.sum(p, axis=0, keepdims=True), alpha, p.astype(BF16)

        def accumulate(hm, alpha, p):
            h = hm // 2
            vth = vt_ref[0, kt, h * hd2:(h + 1) * hd2, :]
            acc_refs[hm][...] = alpha * acc_refs[hm][...] + jnp.dot(vth, p, preferred_element_type=F32)

        s_vals = dict(enumerate(early))
        sm_vals, out, nxt = {}, [None] * (2 * n_maps), []
        for step in range(1, n_maps + ahead):
            if ahead <= step < n_maps:
                s_vals[step] = scores(kt, step)
            elif step >= n_maps and not diagonal:
                nxt.append(scores(kt + 1, step - n_maps))
            hm = step - 1
            if hm < n_maps:
                m_new, l_new, alpha, p = softmax(hm, s_vals.pop(hm))
                out[2 * hm], out[2 * hm + 1] = m_new, l_new
                sm_vals[hm] = (alpha, p)
            if step - 2 >= 0:
                accumulate(step - 2, *sm_vals.pop(step - 2))
        return tuple(out) + tuple(nxt)

    for acc in acc_refs:
        acc[...] = jnp.zeros_like(acc)
    one = (jnp.full((1, t), -jnp.inf, F32), jnp.zeros((1, t), F32))
    first = tuple(scores(0, hm) for hm in range(ahead))
    carry = lax.fori_loop(0, qt, lambda kt, cr: block(kt, cr, False), one * n_maps + first)
    carry = block(qt, carry, True)
    for h in range(ATTN_HEADS):
        l0, l1 = carry[4 * h + 1], carry[4 * h + 3]
        o = acc_refs[2 * h][...] / l0 - lam * (acc_refs[2 * h + 1][...] / l1)
        y = o * lax.rsqrt(jnp.mean(o * o, axis=0, keepdims=True) + LN_EPS) * g_ref[...]
        o_ref[:, h * hd2:(h + 1) * hd2] = (y * (1.0 - LAMBDA_INIT)).T.astype(BF16)


def _attention(q, k, vt, bias_tiles, lq1, lk1, lq2, lk2, subln_g, *, bsz, seq):
    t = ATT_TILE
    nt = seq // t
    full = lambda a: pl.BlockSpec(a.shape, lambda b, j: (0,) * a.ndim)
    return pl.pallas_call(
        _attn_kernel,
        grid=(bsz, nt),
        in_specs=[pl.BlockSpec((t, QK_DIM), lambda b, j: (b * nt + j, 0)),
                  pl.BlockSpec((seq, QK_DIM), lambda b, j: (b, 0)),
                  pl.BlockSpec((1, nt, V_DIM, t), lambda b, j: (b, 0, 0, 0)),
                  full(bias_tiles), full(lq1), full(lk1), full(lq2), full(lk2), full(subln_g)],
        out_specs=pl.BlockSpec((t, V_DIM), lambda b, j: (b * nt + j, 0)),
        out_shape=jax.ShapeDtypeStruct((bsz * seq, V_DIM), BF16),
        scratch_shapes=[pltpu.VMEM((2 * ATTN_HEAD_DIM, t), F32) for _ in range(2 * ATTN_HEADS)],
        compiler_params=_params(("arbitrary", "arbitrary"), 26),
    )(q, k, vt, bias_tiles, lq1, lk1, lq2, lk2, subln_g)


def _outproj_kernel(x_ref, yp_ref, ya_ref, w_ref, gate_ref, g_ref, b_ref, o_ref):
    mix = (jnp.dot(yp_ref[...], w_ref[0:POOL_DIM, :], preferred_element_type=F32)
           + jnp.dot(ya_ref[...], w_ref[POOL_DIM:, :], preferred_element_type=F32))
    z = ALPHA * x_ref[...] + gate_ref[0] * mix
    o_ref[...] = _layer_norm(z, g_ref[...], b_ref[...])


def _outproj(x2, yp, ya, w_out, gate1, ln_g, ln_b, *, seq, tm):
    n, d = x2.shape
    tps = seq // tm
    row = lambda w: pl.BlockSpec((tm, w), lambda i: (i, 0))
    full = lambda a: pl.BlockSpec(a.shape, lambda i: (0,) * a.ndim)
    return pl.pallas_call(
        _outproj_kernel,
        grid=(n // tm,),
        in_specs=[row(d), row(POOL_DIM), row(V_DIM), full(w_out),
                  pl.BlockSpec((1, 1, d), lambda i: (i // tps, 0, 0)), full(ln_g), full(ln_b)],
        out_specs=row(d),
        out_shape=jax.ShapeDtypeStruct((n, d), F32),
        compiler_params=_params(("arbitrary",), 32),
    )(x2, yp, ya, w_out, gate1, ln_g, ln_b)


def _route_kernel(x_ref, sc_ref, sh_ref, whi_ref, wlo_ref, rb_ref,
                  eidx_ref, rank_ref, wts_ref, cnt_ref, h2p_ref, carry_ref, *, tr):
    i = pl.program_id(0)

    @pl.when(i == 0)
    def _():
        carry_ref[...] = jnp.zeros_like(carry_ref)

    h2 = x_ref[...] * (1.0 + sc_ref[0]) + sh_ref[0]
    h2p_ref[...] = _pack_bf16_halves(h2)
    hi = h2.astype(BF16)
    lo = (h2 - hi.astype(F32)).astype(BF16)
    nt = (((1,), (1,)), ((), ()))
    logits = (lax.dot_general(whi_ref[...], hi, nt, preferred_element_type=F32)
              + lax.dot_general(wlo_ref[...], hi, nt, preferred_element_type=F32)
              + lax.dot_general(whi_ref[...], lo, nt, preferred_element_type=F32))
    scores = _sigmoid(logits)
    sel = scores + rb_ref[...]
    erow = lax.broadcasted_iota(jnp.int32, (N_EXPERTS, tr), 0).astype(F32)

    g3 = sel.reshape(N_GROUPS, GROUP_SIZE, tr)
    r3 = lax.broadcasted_iota(jnp.int32, (N_GROUPS, GROUP_SIZE, tr), 1).astype(F32)
    m1 = jnp.max(g3, axis=1, keepdims=True)
    first = jnp.min(jnp.where(g3 == m1, r3, float(GROUP_SIZE)), axis=1, keepdims=True)
    m2 = jnp.max(jnp.where(r3 == first, -jnp.inf, g3), axis=1, keepdims=True)
    gscore = (m1 + m2).reshape(N_GROUPS, tr)

    gidx = lax.broadcasted_iota(jnp.int32, (N_GROUPS, tr), 0)
    beaten_by = jnp.zeros((N_GROUPS, tr), jnp.int32)
    for g in range(N_GROUPS):
        other = gscore[g:g + 1, :]
        wins = (other > gscore) | ((other == gscore) & (g < gidx))
        beaten_by = beaten_by + wins.astype(jnp.int32)
    dropped = jnp.where(beaten_by < TOP_K_GROUPS, 0.0, -jnp.inf)
    cur = (g3 + dropped.reshape(N_GROUPS, 1, tr)).reshape(N_EXPERTS, tr)

    picks, weights = [], []
    selmask = jnp.zeros((N_EXPERTS, tr), F32)
    for _ in range(TOP_K):
        mx = jnp.max(cur, axis=0, keepdims=True)
        pick = jnp.min(jnp.where(cur == mx, erow, float(N_EXPERTS)), axis=0, keepdims=True)
        onehot = erow == pick
        weights.append(jnp.sum(jnp.where(onehot, scores, 0.0), axis=0, keepdims=True))
        cur = jnp.where(onehot, -jnp.inf, cur)
        selmask = jnp.where(onehot, 1.0, selmask)
        picks.append(pick)

    t_from = lax.broadcasted_iota(jnp.int32, (tr, tr), 0)
    t_to = lax.broadcasted_iota(jnp.int32, (tr, tr), 1)
    earlier = jnp.where(t_from < t_to, 1.0, 0.0).astype(BF16)
    chosen = selmask.astype(BF16)
    carry = carry_ref[...]
    rankmat = (jnp.dot(chosen, earlier, preferred_element_type=F32)
               + jnp.concatenate([carry] * (tr // LANES), axis=1))
    carry_ref[...] = carry + jnp.dot(chosen, jnp.ones((tr, LANES), BF16), preferred_element_type=F32)
    cnt_ref[...] = carry_ref[...]

    wsum = weights[0]
    for wj in weights[1:]:
        wsum = wsum + wj
    row8 = lax.broadcasted_iota(jnp.int32, (TOP_K, tr), 0)
    eidx = jnp.zeros((TOP_K, tr), jnp.int32)
    rank = jnp.zeros((TOP_K, tr), jnp.int32)
    wts = jnp.zeros((TOP_K, tr), F32)
    for j in range(TOP_K):
        rk = jnp.sum(jnp.where(erow == picks[j], rankmat, 0.0), axis=0, keepdims=True)
        eidx = jnp.where(row8 == j, picks[j].astype(jnp.int32), eidx)
        rank = jnp.where(row8 == j, rk.astype(jnp.int32), rank)
        wts = jnp.where(row8 == j, weights[j] / wsum * ROUTED_SCALE, wts)
    eidx_ref[...] = eidx
    rank_ref[...] = rank
    wts_ref[...] = wts


def _route(x1, scale2, shift2, wr_hi, wr_lo, router_bias, *, seq, tr):
    n, d = x1.shape
    tps = seq // tr
    mod_spec = pl.BlockSpec((1, 1, d), lambda i: (i // tps, 0, 0))
    full = lambda a: pl.BlockSpec(a.shape, lambda i: (0,) * a.ndim)
    assert tr % LANES == 0
    k8 = pl.BlockSpec((TOP_K, tr), lambda i: (0, i))
    return pl.pallas_call(
        functools.partial(_route_kernel, tr=tr),
        grid=(n // tr,),
        in_specs=[pl.BlockSpec((tr, d), lambda i: (i, 0)), mod_spec, mod_spec,
                  full(wr_hi), full(wr_lo), full(router_bias)],
        out_specs=[k8, k8, k8, pl.BlockSpec((N_EXPERTS, LANES), lambda i: (0, 0)),
                   pl.BlockSpec((tr, d // 2), lambda i: (i, 0))],
        out_shape=[jax.ShapeDtypeStruct((TOP_K, n), jnp.int32),
                   jax.ShapeDtypeStruct((TOP_K, n), jnp.int32),
                   jax.ShapeDtypeStruct((TOP_K, n), F32),
                   jax.ShapeDtypeStruct((N_EXPERTS, LANES), F32),
                   jax.ShapeDtypeStruct((n, d // 2), U32)],
        scratch_shapes=[pltpu.VMEM((N_EXPERTS, LANES), F32)],
        compiler_params=_params(("arbitrary",), 8),
    )(x1, scale2, shift2, wr_hi, wr_lo, router_bias)


def _pack_bf16_halves(x):
    w = x.shape[1] // 2
    lo = pltpu.bitcast(x[:, :w].astype(BF16).astype(F32), U32) >> 16
    hi = pltpu.bitcast(x[:, w:].astype(BF16).astype(F32), U32) & jnp.uint32(0xFFFF0000)
    return lo | hi


def _unpack_bf16_halves(p):
    return pltpu.bitcast(p << 16, F32), pltpu.bitcast(p & jnp.uint32(0xFFFF0000), F32)


def _sc_worker():
    return lax.axis_index("subcore") * SC_CORES + lax.axis_index("core")


def _scatter_rows_sc(rows, idx, n_slots):
    n, width = rows.shape
    k = idx.shape[0] // n
    n_workers = SC_CORES * SC_SUBCORES
    w = SC_ROWS
    per_worker = n // n_workers
    n_chunks = per_worker // w
    assert n == n_workers * n_chunks * w and n_chunks % 2 == 0 and idx.shape[0] == k * n
    mesh = plsc.VectorSubcoreMesh(core_axis_name="core", subcore_axis_name="subcore")

    @functools.partial(
        pl.kernel, mesh=mesh,
        out_type=jax.ShapeDtypeStruct((n_slots, width), rows.dtype),
        scratch_types=[pltpu.VMEM((k * per_worker,), jnp.int32), pltpu.VMEM((2, w, width), rows.dtype),
                       pltpu.SemaphoreType.DMA((2,)), pltpu.SemaphoreType.DMA((2,))])
    def scatter(rows_hbm, idx_hbm, out_hbm, idx_v, rows_v, in_sem, out_sem):
        t0 = _sc_worker() * per_worker
        for j in range(k):
            pltpu.sync_copy(idx_hbm.at[pl.ds(j * n + t0, per_worker)], idx_v.at[pl.ds(j * per_worker, per_worker)])

        def load(i, s):
            return pltpu.make_async_copy(rows_hbm.at[pl.ds(t0 + i * w, w)], rows_v.at[s], in_sem.at[s])

        def send(i, s, j):
            slots = idx_v.at[pl.ds(j * per_worker + i * w, w)]
            return pltpu.make_async_copy(rows_v.at[s], out_hbm.at[slots], out_sem.at[s])

        load(0, 0).start()

        def pair(ii, carry):
            for s in range(2):
                i = ii * 2 + s
                load(i, s).wait()

                @pl.when(i >= 1)
                def _():
                    for j in range(k):
                        send(i - 1, 1 - s, j).wait()

                @pl.when(i + 1 < n_chunks)
                def _():
                    load(i + 1, 1 - s).start()

                for j in range(k):
                    send(i, s, j).start()
            return carry

        lax.fori_loop(0, n_chunks // 2, pair, 0)
        for j in range(k):
            send(n_chunks - 1, 1, j).wait()

    return scatter(rows, idx)


def _dest_kernel(eidx_ref, rank_ref, first_ref, dest_ref):
    tt = eidx_ref.shape[1]
    pieces = [jnp.broadcast_to(first_ref[:, p * LANES:(p + 1) * LANES], (TOP_K, LANES))
              for p in range(N_EXPERTS // LANES)]
    for c in range(tt // LANES):
        cols = slice(c * LANES, (c + 1) * LANES)
        e = eidx_ref[:, cols]
        within = e & (LANES - 1)
        start = jnp.take_along_axis(pieces[0], within, axis=1)
        for p in range(1, len(pieces)):
            start = jnp.where(e // LANES == p, jnp.take_along_axis(pieces[p], within, axis=1), start)
        dest_ref[:, cols] = start + rank_ref[:, cols]


def _dest(eidx_t, rank_t, pstarts, *, tt):
    n = eidx_t.shape[1]
    first = pstarts.reshape(1, N_EXPERTS)
    k8 = pl.BlockSpec((TOP_K, tt), lambda i: (0, i))
    return pl.pallas_call(
        _dest_kernel,
        grid=(n // tt,),
        in_specs=[k8, k8, pl.BlockSpec(first.shape, lambda i: (0, 0))],
        out_specs=k8,
        out_shape=jax.ShapeDtypeStruct((TOP_K, n), jnp.int32),
        compiler_params=_params(("arbitrary",), 2),
    )(eidx_t, rank_t, first)


def _expert_kernel(fb_ref, nb_ref, cnt_ref, nused_ref, wg_ref, wu_ref, wd_ref, xs_ref, ys_ref,
                   wgb, wub, wdb, xbuf, ybuf, in_sem, out_sem):
    e = pl.program_id(0)
    m = DISPATCH_BLOCK
    ns = EXPERT_SLOTS
    n_used = nused_ref[0]

    def rows(g):
        return pl.ds(pl.multiple_of(g * m, m), m)

    def fetch(g):
        slot = g & (ns - 1)
        return pltpu.make_async_copy(xs_ref.at[rows(g), :], xbuf.at[slot], in_sem.at[slot])

    def put(g):
        slot = g & (ns - 1)
        return pltpu.make_async_copy(ybuf.at[slot], ys_ref.at[rows(g), :], out_sem.at[slot])

    @pl.when(e == 0)
    def _():
        for g0 in range(ns - 1):
            @pl.when(g0 < n_used)
            def _(g0=g0):
                fetch(g0).start()

    wgb[...] = wg_ref[0].astype(BF16)
    wub[...] = wu_ref[0].astype(BF16)
    wdb[...] = wd_ref[0].astype(BF16)

    def block(i, carry):
        g = fb_ref[e] + i
        slot = g & (ns - 1)
        fetch(g).wait()

        @pl.when(g + ns - 1 < n_used)
        def _():
            fetch(g + ns - 1).start()

        @pl.when(g >= ns)
        def _():
            put(g - ns).wait()

        n_valid = cnt_ref[e] - i * m

        def swiglu(rows):
            row = lax.broadcasted_iota(jnp.int32, (rows, 1), 0)
            packed = jnp.where(row < n_valid, xbuf[slot, 0:rows], jnp.uint32(0))
            x_lo, x_hi = [h.astype(BF16) for h in _unpack_bf16_halves(packed)]
            half = x_lo.shape[1]

            def up_proj(w):
                return (jnp.dot(x_lo, w[0:half, :], preferred_element_type=F32)
                        + jnp.dot(x_hi, w[half:, :], preferred_element_type=F32))

            a = (_silu(up_proj(wgb)) * up_proj(wub)).astype(BF16)
            ybuf[slot, 0:rows] = _pack_bf16_halves(jnp.dot(a, wdb[...], preferred_element_type=F32))

        pl.when(n_valid > m // 2)(functools.partial(swiglu, m))
        pl.when(n_valid <= m // 2)(functools.partial(swiglu, m // 2))
        put(g).start()
        return carry

    lax.fori_loop(0, nb_ref[e], block, 0)

    @pl.when(e == pl.num_programs(0) - 1)
    def _():
        for back in range(ns, 0, -1):
            @pl.when(n_used >= back)
            def _(back=back):
                put(n_used - back).wait()

        ybuf[0] = jnp.zeros(ybuf.shape[1:], ybuf.dtype)
        n_blocks = ys_ref.shape[0] // m

        def tail(wait):
            def body(g, carry):
                cp = pltpu.make_async_copy(ybuf.at[0], ys_ref.at[rows(g), :], out_sem.at[0])
                cp.wait() if wait else cp.start()
                return carry
            return body

        lax.fori_loop(n_used, n_blocks, tail(False), 0)
        lax.fori_loop(n_used, n_blocks, tail(True), 0)


def _experts(first_block, n_blocks_e, counts, n_used, xs, w_gate, w_up, w_down):
    p, dp = xs.shape
    m = DISPATCH_BLOCK
    n_e, d, f = w_gate.shape
    assert d == 2 * dp and p % m == 0
    grid_spec = pltpu.PrefetchScalarGridSpec(
        num_scalar_prefetch=4,
        grid=(n_e,),
        in_specs=[pl.BlockSpec((1, d, f), lambda e, *_: (e, 0, 0)),
                  pl.BlockSpec((1, d, f), lambda e, *_: (e, 0, 0)),
                  pl.BlockSpec((1, f, d), lambda e, *_: (e, 0, 0)),
                  pl.BlockSpec(memory_space=pl.ANY)],
        out_specs=pl.BlockSpec(memory_space=pl.ANY),
        scratch_shapes=[pltpu.VMEM((d, f), BF16), pltpu.VMEM((d, f), BF16), pltpu.VMEM((f, d), BF16),
                        pltpu.VMEM((EXPERT_SLOTS, m, dp), U32), pltpu.VMEM((EXPERT_SLOTS, m, dp), U32),
                        pltpu.SemaphoreType.DMA((EXPERT_SLOTS,)), pltpu.SemaphoreType.DMA((EXPERT_SLOTS,))],
    )
    return pl.pallas_call(
        _expert_kernel,
        grid_spec=grid_spec,
        out_shape=jax.ShapeDtypeStruct((p, dp), U32),
        compiler_params=_params(("arbitrary",), 22),
    )(first_block, n_blocks_e, counts, n_used, w_gate, w_up, w_down, xs)


def _gather_rows_sc(table, idx):
    n_idx, (_, width) = idx.shape[0], table.shape
    n_workers = SC_CORES * SC_SUBCORES
    w = SC_ROWS
    per_worker = n_idx // n_workers
    n_chunks = per_worker // w
    assert n_idx == n_workers * n_chunks * w and n_chunks % 2 == 0
    mesh = plsc.VectorSubcoreMesh(core_axis_name="core", subcore_axis_name="subcore")

    @functools.partial(
        pl.kernel, mesh=mesh,
        out_type=jax.ShapeDtypeStruct((n_idx, width), table.dtype),
        scratch_types=[pltpu.VMEM((per_worker,), jnp.int32), pltpu.VMEM((2, w, width), table.dtype),
                       pltpu.SemaphoreType.DMA((2,)), pltpu.SemaphoreType.DMA((2,))])
    def gather(table_hbm, idx_hbm, out_hbm, idx_v, rows_v, in_sem, out_sem):
        base = _sc_worker() * per_worker
        pltpu.sync_copy(idx_hbm.at[pl.ds(base, per_worker)], idx_v)

        def fetch(i, s):
            return pltpu.make_async_copy(table_hbm.at[idx_v.at[pl.ds(i * w, w)]], rows_v.at[s], in_sem.at[s])

        def put(i, s):
            return pltpu.make_async_copy(rows_v.at[s], out_hbm.at[pl.ds(base + i * w, w)], out_sem.at[s])

        fetch(0, 0).start()

        def pair(ii, carry):
            for s in range(2):
                i = ii * 2 + s
                fetch(i, s).wait()

                @pl.when(i >= 1)
                def _():
                    put(i - 1, 1 - s).wait()

                @pl.when(i + 1 < n_chunks)
                def _():
                    fetch(i + 1, 1 - s).start()

                put(i, s).start()
            return carry

        lax.fori_loop(0, n_chunks // 2, pair, 0)
        put(n_chunks - 1, 1).wait()

    return gather(table, idx)


def _combine_kernel(x_ref, sc_ref, sh_ref, gate_ref, wts_ref, rows_ref,
                    wsg_ref, wsu_ref, wsd_ref, g_ref, b_ref, *out_refs):
    o_ref = out_refs[-1]
    x = x_ref[...]
    hb = (x * (1.0 + sc_ref[0]) + sh_ref[0]).astype(BF16)
    sg = jnp.dot(hb, wsg_ref[...], preferred_element_type=F32)
    su = jnp.dot(hb, wsu_ref[...], preferred_element_type=F32)
    shared = jnp.dot((_silu(sg) * su).astype(BF16), wsd_ref[...], preferred_element_type=F32)

    wts = wts_ref[...].T
    half = shared.shape[1] // 2
    lo, hi = shared[:, :half], shared[:, half:]
    for j in range(TOP_K):
        y_lo, y_hi = _unpack_bf16_halves(rows_ref[j])
        lo = lo + wts[:, j:j + 1] * y_lo
        hi = hi + wts[:, j:j + 1] * y_hi
    z = ALPHA * x + gate_ref[0] * jnp.concatenate([lo, hi], axis=1)
    o_ref[...] = _layer_norm(z, g_ref[...], b_ref[...])


def _combine(x1, scale2, shift2, gate2, wts, rows, ws_gate, ws_up, ws_down, ln_g, ln_b, prev_out,
             *, seq, tc, first_token):
    n, d = x1.shape
    tps = seq // tc
    off = first_token // tc
    n_tiles = rows.shape[1] // tc
    mod_spec = pl.BlockSpec((1, 1, d), lambda i: ((i + off) // tps, 0, 0))
    full = lambda a: pl.BlockSpec(a.shape, lambda i: (0,) * a.ndim)
    args = [x1, scale2, shift2, gate2, wts, rows, ws_gate, ws_up, ws_down, ln_g, ln_b]
    in_specs = [pl.BlockSpec((tc, d), lambda i: (i + off, 0)), mod_spec, mod_spec, mod_spec,
                pl.BlockSpec((TOP_K, tc), lambda i: (0, i + off)),
                pl.BlockSpec((TOP_K, tc, d // 2), lambda i: (0, i, 0)),
                full(ws_gate), full(ws_up), full(ws_down), full(ln_g), full(ln_b)]
    aliases = {}
    if prev_out is not None:
        aliases = {len(args): 0}
        args.append(prev_out)
        in_specs.append(pl.BlockSpec(memory_space=pl.ANY))
    return pl.pallas_call(
        _combine_kernel,
        grid=(n_tiles,),
        in_specs=in_specs,
        out_specs=pl.BlockSpec((tc, d), lambda i: (i + off, 0)),
        out_shape=jax.ShapeDtypeStruct((n, d), F32),
        input_output_aliases=aliases,
        compiler_params=_params(("arbitrary",), 36),
    )(*args)


def _layer(x, c, w_ada, b_ada, w_in, pool_w, pool_scale, lq1, lk1, lq2, lk2, subln_g, w_out,
           ln1_g, ln1_b, w_router, router_bias, w_gate, w_up, w_down, ws_gate, ws_up, ws_down,
           ln2_g, ln2_b, rel_table, *, tm=1024, tr=256, tc=512):
    bsz, seq, d = x.shape
    n = bsz * seq
    x2 = x.reshape(n, d)
    row = lambda a: a.reshape(1, -1)

    mod = _modulation(c, w_ada, b_ada)
    shift1, scale1, gate1, shift2, scale2, gate2 = [
        mod[:, j * d:(j + 1) * d].reshape(bsz, 1, d) for j in range(6)]

    n_main = POOL_DIM + 2 * QK_DIM
    yp, q, k, vt = _inproj(x2, scale1, shift1, w_in[:, :n_main].astype(BF16), w_in[:, n_main:].T.astype(BF16),
                           pool_w.astype(BF16), row(pool_scale), seq=seq, tm=tm)
    bias_tiles = _bias_tiles(rel_table, seq // ATT_TILE)
    ya = _attention(q, k, vt, bias_tiles, row(lq1), row(lk1), row(lq2), row(lk2), subln_g.reshape(-1, 1),
                    bsz=bsz, seq=seq)
    x1 = _outproj(x2, yp, ya, w_out.astype(BF16), gate1, row(ln1_g), row(ln1_b), seq=seq, tm=tm)

    wr_t = w_router.T
    wr_hi = wr_t.astype(BF16)
    wr_lo = (wr_t - wr_hi.astype(F32)).astype(BF16)
    eidx_t, rank_t, wts_t, cnt, h2p = _route(x1, scale2, shift2, wr_hi, wr_lo, router_bias.reshape(-1, 1),
                                             seq=seq, tr=tr)

    m = DISPATCH_BLOCK
    counts = cnt[:, 0].astype(jnp.int32)
    padded = (counts + m - 1) // m * m
    pends = jnp.cumsum(padded)
    pstarts = pends - padded
    n_blocks = -(-(n * TOP_K + N_EXPERTS * (m - 1)) // m)
    n_used = (pends[-1:] // m).astype(jnp.int32)
    dest_t = _dest(eidx_t, rank_t, pstarts, tt=min(n, 2048))

    xs = _scatter_rows_sc(h2p, dest_t.reshape(TOP_K * n), n_blocks * m)
    ys = _experts(pstarts // m, padded // m, counts, n_used, xs, w_gate, w_up, w_down)

    shared_w = (ws_gate.astype(BF16), ws_up.astype(BF16), ws_down.astype(BF16))
    wts = wts_t
    part = n // COMBINE_PARTS
    assert part % tc == 0 and part * COMBINE_PARTS == n
    out = None
    for p in range(COMBINE_PARTS):
        idx = dest_t[:, p * part:(p + 1) * part].reshape(TOP_K * part)
        picked = _gather_rows_sc(ys, idx).reshape(TOP_K, part, d // 2)
        out = _combine(x1, scale2, shift2, gate2, wts, picked, *shared_w, row(ln2_g), row(ln2_b), out,
                       seq=seq, tc=tc, first_token=p * part)
    return out.reshape(bsz, seq, d)


def kernel(x, c, w_ada, b_ada, w_in, pool_w, pool_scale, lambda_q1, lambda_k1, lambda_q2, lambda_k2,
           subln_g, w_out, ln1_g, ln1_b, w_router, router_bias, w_gate, w_up, w_down,
           ws_gate, ws_up, ws_down, ln2_g, ln2_b, rel_table):
    per_layer = (w_ada, b_ada, w_in, pool_w, pool_scale, lambda_q1, lambda_k1, lambda_q2, lambda_k2,
                 subln_g, w_out, ln1_g, ln1_b, w_router, router_bias, w_gate, w_up, w_down,
                 ws_gate, ws_up, ws_down, ln2_g, ln2_b)
    assert all(a.shape[0] == DEPTH == 1 for a in per_layer)
    return _layer(x, c, *[a.reshape(a.shape[1:]) for a in per_layer], rel_table)
```
